```python
import math
import jax, jax.numpy as jnp
from jax import lax
import numpy as np

D_MODEL = 1024
BATCH = 8
SEQ = 4096
DEPTH = 1

RET_HEADS = 4
RET_DK = D_MODEL // (2 * RET_HEADS)
RET_DV = D_MODEL // RET_HEADS
RET_CHUNK = 128
DIFF_HEADS = 8
DIFF_DH = D_MODEL // (2 * DIFF_HEADS)
DIFF_DV = 2 * DIFF_DH
Q_BLOCK = 128
N_EXPERTS = 32
TOP_K = 4
D_EXPERT = D_MODEL
SWIGLU_LIMIT = 7.0
SWIGLU_ALPHA = 1.702
MOE_BLOCK = 128
EPS = 1e-6
RET_QK_W = RET_HEADS * RET_DK
RET_V_W = RET_HEADS * RET_DV
DIFF_QK_W = DIFF_HEADS * 2 * DIFF_DH
DIFF_V_W = DIFF_HEADS * DIFF_DV
IN_SPLITS = (RET_QK_W, RET_QK_W, RET_V_W, RET_V_W, DIFF_QK_W, DIFF_QK_W, DIFF_V_W, 2 * D_MODEL)
IN_W = sum(IN_SPLITS)

kernel_name = "hybrid_retention_diffattn_moe_block"


def rms_norm(x, g):
    xf = x.astype(jnp.float32)
    y = xf * lax.rsqrt(jnp.mean(xf * xf, axis=-1, keepdims=True) + EPS)
    return (y * g.astype(jnp.float32)).astype(x.dtype)


def group_norm_heads(y, g, b):
    yf = y.astype(jnp.float32)
    mu = jnp.mean(yf, axis=-1, keepdims=True)
    var = jnp.mean(jnp.square(yf - mu), axis=-1, keepdims=True)
    yn = (yf - mu) * lax.rsqrt(var + EPS)
    Bn, Sn, H, dv = y.shape
    yn = yn.reshape(Bn, Sn, H * dv) * g.astype(jnp.float32) + b.astype(jnp.float32)
    return yn.astype(y.dtype)


def split_cols(t, sizes):
    idx = np.cumsum(np.array(sizes))[:-1].tolist()
    return jnp.split(t, idx, axis=-1)


def retention(q, k, v):
    Bn, Sn = q.shape[0], q.shape[1]
    C = RET_CHUNK
    n = Sn // C
    f32 = jnp.float32
    q = q.astype(f32)
    k = k.astype(f32) * (RET_DK ** -0.5)
    v = v.astype(f32)
    log_g = jnp.log(1.0 - 2.0 ** (-5.0 - jnp.arange(RET_HEADS, dtype=f32)))
    pos = jnp.arange(C, dtype=f32)
    rel = pos[:, None] - pos[None, :]
    decay_in = jnp.where(rel >= 0, jnp.exp(log_g[:, None, None] * jnp.maximum(rel, 0.0)), 0.0)
    xi = jnp.exp(log_g[:, None] * (pos + 1.0))
    zeta = jnp.exp(log_g[:, None] * (C - 1.0 - pos))
    g_chunk = jnp.exp(log_g * C)

    def to_chunks(t):
        return t.reshape(Bn, n, C, RET_HEADS, t.shape[-1]).transpose(1, 0, 3, 2, 4)

    def step(R, qkv):
        qc, kc, vc = qkv
        inner = jnp.einsum('bhnd,bhmd->bhnm', qc, kc) * decay_in
        y = jnp.einsum('bhnm,bhmv->bhnv', inner, vc) \
            + jnp.einsum('bhnd,bhdv->bhnv', qc, R) * xi[None, :, :, None]
        R = R * g_chunk[None, :, None, None] \
            + jnp.einsum('bhmd,bhmv->bhdv', kc, vc * zeta[None, :, :, None])
        return R, y

    R0 = jnp.zeros((Bn, RET_HEADS, RET_DK, RET_DV), f32)
    _, ys = lax.scan(step, R0, (to_chunks(q), to_chunks(k), to_chunks(v)))
    return ys.transpose(1, 0, 3, 2, 4).reshape(Bn, Sn, RET_HEADS, RET_DV)


def diff_attention(q, k, v, g_q, g_k, lq1, lk1, lq2, lk2, g_sub, lambda_init):
    Bn, Sn = q.shape[0], q.shape[1]
    f32 = jnp.float32
    q = rms_norm(q.reshape(Bn, Sn, DIFF_HEADS, 2, DIFF_DH), g_q).transpose(0, 2, 3, 1, 4)
    k = rms_norm(k.reshape(Bn, Sn, DIFF_HEADS, 2, DIFF_DH), g_k).transpose(0, 2, 3, 1, 4)
    v = v.reshape(Bn, Sn, DIFF_HEADS, DIFF_DV).transpose(0, 2, 1, 3)
    lam = jnp.exp(jnp.sum(lq1.astype(f32) * lk1.astype(f32))) \
        - jnp.exp(jnp.sum(lq2.astype(f32) * lk2.astype(f32))) + lambda_init
    slopes = 2.0 ** (-8.0 * (jnp.arange(DIFF_HEADS, dtype=f32) + 1.0) / DIFF_HEADS)
    key_pos = jnp.arange(Sn, dtype=f32)
    scale = DIFF_DH ** -0.5

    def block(i):
        qb = lax.dynamic_slice_in_dim(q, i * Q_BLOCK, Q_BLOCK, axis=3)
        s = jnp.einsum('bhcqd,bhckd->bhcqk', qb, k).astype(f32) * scale
        dist = (i * Q_BLOCK + jnp.arange(Q_BLOCK)).astype(f32)[:, None] - key_pos[None, :]
        s = jnp.where(dist >= 0, s - slopes[:, None, None, None] * dist, -jnp.inf)
        p = jax.nn.softmax(s, axis=-1)
        a = p[:, :, 0] - lam * p[:, :, 1]
        return jnp.einsum('bhqk,bhkv->bhqv', a.astype(v.dtype), v)

    o = lax.map(block, jnp.arange(Sn // Q_BLOCK))
    o = o.transpose(1, 0, 3, 2, 4).reshape(Bn, Sn, DIFF_HEADS, DIFF_DV)
    o = rms_norm(o, g_sub) * (1.0 - lambda_init)
    return o.reshape(Bn, Sn, DIFF_HEADS * DIFF_DV)


def moe_ffn(h, w_router, b_router, w1, b1, w2, b2):
    Bn, Sn, D = h.shape
    T = Bn * Sn
    ht = h.reshape(T, D)
    logits = (ht @ w_router + b_router).astype(jnp.float32)
    top_v, top_i = lax.top_k(logits, TOP_K)
    top_w = jax.nn.softmax(top_v, axis=-1)
    n_assign = T * TOP_K
    flat_e = top_i.reshape(-1)
    flat_tok = jnp.arange(n_assign, dtype=jnp.int32) // TOP_K
    flat_w = top_w.reshape(-1)
    order = jnp.argsort(flat_e)
    sorted_e = flat_e[order]
    counts = jnp.bincount(flat_e, length=N_EXPERTS)
    padded = ((counts + MOE_BLOCK - 1) // MOE_BLOCK) * MOE_BLOCK
    pad_end = jnp.cumsum(padded)
    pad_start = pad_end - padded
    start = jnp.cumsum(counts) - counts
    dest = pad_start[sorted_e] + (jnp.arange(n_assign, dtype=jnp.int32) - start[sorted_e])
    n_rows = ((n_assign + MOE_BLOCK - 1) // MOE_BLOCK) * MOE_BLOCK + N_EXPERTS * MOE_BLOCK
    n_blocks = n_rows // MOE_BLOCK
    row_tok = jnp.zeros((n_rows,), jnp.int32).at[dest].set(flat_tok[order])
    row_w = jnp.zeros((n_rows,), jnp.float32).at[dest].set(flat_w[order])
    blk_e = jnp.minimum(jnp.searchsorted(pad_end, jnp.arange(n_blocks, dtype=jnp.int32) * MOE_BLOCK,
                                         side='right'), N_EXPERTS - 1).astype(jnp.int32)

    def expert_block(args):
        tok, e = args
        xb = ht[tok]
        gu = xb @ w1[e] + b1[e]
        g, u = gu[:, :D_EXPERT], gu[:, D_EXPERT:]
        g = jnp.minimum(g, SWIGLU_LIMIT)
        u = jnp.clip(u, -SWIGLU_LIMIT, SWIGLU_LIMIT)
        a = g * jax.nn.sigmoid(SWIGLU_ALPHA * g) * (u + 1.0)
        return a @ w2[e] + b2[e]

    out = lax.map(expert_block, (row_tok.reshape(n_blocks, MOE_BLOCK), blk_e))
    out = out.reshape(n_rows, D) * row_w[:, None].astype(h.dtype)
    y = jnp.zeros((T, D), h.dtype).at[row_tok].add(out)
    return y.reshape(Bn, Sn, D)


def setup_inputs(seed: int = 0) -> dict:
    key = jax.random.key(seed)
    ks = jax.random.split(key, 24)
    L, D, E, F = DEPTH, D_MODEL, N_EXPERTS, D_EXPERT
    f32 = jnp.float32
    nrm = lambda k, s: jax.random.normal(k, s, f32)
    return {
        "x": nrm(ks[0], (BATCH, SEQ, D)),
        "c": nrm(ks[1], (BATCH, D)),
        "w_ada": nrm(ks[2], (L, D, 6 * D)) * D ** -0.5,
        "b_ada": nrm(ks[3], (L, 6 * D)) * 0.02,
        "g_norm1": 1.0 + 0.02 * nrm(ks[4], (L, D)),
        "w_in": nrm(ks[5], (L, D, IN_W)) * D ** -0.5,
        "g_ret_gn": 1.0 + 0.02 * nrm(ks[6], (L, RET_V_W)),
        "b_ret_gn": 0.02 * nrm(ks[7], (L, RET_V_W)),
        "g_qnorm": 1.0 + 0.02 * nrm(ks[8], (L, DIFF_DH)),
        "g_knorm": 1.0 + 0.02 * nrm(ks[9], (L, DIFF_DH)),
        "lambda_q1": 0.1 * nrm(ks[10], (L, DIFF_DH)),
        "lambda_k1": 0.1 * nrm(ks[11], (L, DIFF_DH)),
        "lambda_q2": 0.1 * nrm(ks[12], (L, DIFF_DH)),
        "lambda_k2": 0.1 * nrm(ks[13], (L, DIFF_DH)),
        "g_diff_subln": 1.0 + 0.02 * nrm(ks[14], (L, DIFF_DV)),
        "w_out": nrm(ks[15], (L, D, D)) * D ** -0.5,
        "g_norm2": 1.0 + 0.02 * nrm(ks[16], (L, D)),
        "w_router": nrm(ks[17], (L, D, E)) * D ** -0.5,
        "b_router": 0.01 * nrm(ks[18], (L, E)),
        "w_expert_in": nrm(ks[19], (L, E, D, 2 * F)) * D ** -0.5,
        "b_expert_in": 0.01 * nrm(ks[20], (L, E, 2 * F)),
        "w_expert_out": nrm(ks[21], (L, E, F, D)) * F ** -0.5,
        "b_expert_out": 0.01 * nrm(ks[22], (L, E, D)),
    }


def reference(x, c, w_ada, b_ada, g_norm1, w_in, g_ret_gn, b_ret_gn, g_qnorm, g_knorm,
              lambda_q1, lambda_k1, lambda_q2, lambda_k2, g_diff_subln, w_out, g_norm2,
              w_router, b_router, w_expert_in, b_expert_in, w_expert_out, b_expert_out):
    Bn, Sn, D = x.shape
    for l in range(DEPTH):
        lambda_init = 0.8 - 0.6 * math.exp(-0.3 * l)
        mod = jax.nn.silu(c) @ w_ada[l] + b_ada[l]
        sh1, sc1, gt1, sh2, sc2, gt2 = [m[:, None, :] for m in jnp.split(mod, 6, axis=-1)]

        h = rms_norm(x, g_norm1[l]) * (1.0 + sc1) + sh1
        proj = h @ w_in[l]
        rq, rk, rv, rg, dq, dk, dv, mg = split_cols(proj, IN_SPLITS)
        y_ret = retention(rq.reshape(Bn, Sn, RET_HEADS, RET_DK),
                          rk.reshape(Bn, Sn, RET_HEADS, RET_DK),
                          rv.reshape(Bn, Sn, RET_HEADS, RET_DV)).astype(x.dtype)
        o_ret = jax.nn.silu(rg) * group_norm_heads(y_ret, g_ret_gn[l], b_ret_gn[l])
        o_diff = diff_attention(dq, dk, dv, g_qnorm[l], g_knorm[l], lambda_q1[l], lambda_k1[l],
                                lambda_q2[l], lambda_k2[l], g_diff_subln[l], lambda_init)
        gate_ret, gate_diff = jnp.split(jax.nn.sigmoid(mg), 2, axis=-1)
        merged = gate_ret * o_ret + gate_diff * o_diff
        x = x + gt1 * (merged @ w_out[l])

        h2 = rms_norm(x, g_norm2[l]) * (1.0 + sc2) + sh2
        x = x + gt2 * moe_ffn(h2, w_router[l], b_router[l], w_expert_in[l], b_expert_in[l],
                              w_expert_out[l], b_expert_out[l])
    return x
```

```python
import functools
import math

import jax
import jax.numpy as jnp
from jax import lax
from jax.experimental import pallas as pl
from jax.experimental.pallas import tpu as pltpu

F32 = jnp.float32
BF16 = jnp.bfloat16
I32 = jnp.int32

EPS = 1e-6
MASK_VALUE = -1e30
LANES = 128
VMEM_LIMIT_BYTES = 48 * 1024 * 1024

RET_HEADS = 4
DIFF_HEADS = 8
TOP_K = 4
SWIGLU_LIMIT = 7.0
SWIGLU_ALPHA = 1.702
LAMBDA_INIT = 0.8 - 0.6 * math.exp(-0.3 * 0)

IN_TM, IN_TN = 1024, 1024
RET_TC = 256
ATT_T = 512
OUT_TM = 512
DISP_TM = 512
EXP_BLK = 256
COMB_TM = 256
ROW_UNROLL = 8


def _params(*sem):
    return pltpu.CompilerParams(dimension_semantics=sem, vmem_limit_bytes=VMEM_LIMIT_BYTES)


def _sigmoid(x):
    return 1.0 / (1.0 + jnp.exp(-x))


def _rms(x, axis=-1):
    return x * lax.rsqrt(jnp.mean(x * x, axis=axis, keepdims=True) + EPS)


def _mod_kernel(c_ref, w_ref, b_ref, o_ref):
    c = c_ref[...]
    s = c * _sigmoid(c)
    o_ref[...] = jnp.dot(s, w_ref[...], preferred_element_type=F32,
                         precision=lax.Precision.HIGHEST) + b_ref[...]


def _adaln_mod(c, w, b):
    bn, d = c.shape
    n = w.shape[1]
    tn = d
    return pl.pallas_call(
        _mod_kernel,
        out_shape=jax.ShapeDtypeStruct((bn, n), F32),
        grid=(n // tn,),
        in_specs=[pl.BlockSpec((bn, d), lambda j: (0, 0)),
                  pl.BlockSpec((d, tn), lambda j: (0, j)),
                  pl.BlockSpec((1, tn), lambda j: (0, j))],
        out_specs=pl.BlockSpec((bn, tn), lambda j: (0, j)),
        compiler_params=_params("arbitrary"),
        name="adaln_mod",
    )(c, w, b)


def _inproj_kernel(x_ref, mod_ref, g_ref, w_ref, o_ref, h_ref):
    @pl.when(pl.program_id(1) == 0)
    def _():
        m = mod_ref[0]
        h = _rms(x_ref[...]) * g_ref[...] * (1.0 + m[1:2]) + m[0:1]
        h_ref[...] = h.astype(BF16)

    o_ref[...] = jnp.dot(h_ref[...], w_ref[...], preferred_element_type=F32).astype(o_ref.dtype)


def _in_proj(x2, mod3, g1, w_in_bf16, seq):
    t, d = x2.shape
    n = w_in_bf16.shape[1]
    tm, tn = IN_TM, IN_TN
    return pl.pallas_call(
        _inproj_kernel,
        out_shape=jax.ShapeDtypeStruct((t, n), BF16),
        grid=(t // tm, n // tn),
        in_specs=[pl.BlockSpec((tm, d), lambda i, j: (i, 0)),
                  pl.BlockSpec((1, 6, d), lambda i, j: (i * tm // seq, 0, 0)),
                  pl.BlockSpec((1, d), lambda i, j: (0, 0)),
                  pl.BlockSpec((d, tn), lambda i, j: (0, j))],
        out_specs=pl.BlockSpec((tm, tn), lambda i, j: (i, j)),
        scratch_shapes=[pltpu.VMEM((tm, d), BF16)],
        compiler_params=_params("arbitrary", "arbitrary"),
        name="in_proj",
    )(x2, mod3, g1, w_in_bf16)


def _ret_kernel(q_ref, k_ref, v_ref, rg_ref, mg_ref, gng_ref, gnb_ref, o_ref, r_ref, *, dk, dv, tc):
    @pl.when(pl.program_id(1) == 0)
    def _():
        r_ref[...] = jnp.zeros_like(r_ref)

    row = lax.broadcasted_iota(I32, (tc, tc), 0)
    col = lax.broadcasted_iota(I32, (tc, tc), 1)
    rel = (row - col).astype(F32)
    pos = lax.broadcasted_iota(I32, (tc, 1), 0).astype(F32)
    scale = dk ** -0.5
    for h in range(RET_HEADS):
        log_g = math.log(1.0 - 2.0 ** (-5.0 - h))
        decay = jnp.where(rel >= 0, jnp.exp(log_g * jnp.maximum(rel, 0.0)), 0.0) * scale
        xi = jnp.exp(log_g * (pos + 1.0))
        zeta = jnp.exp(log_g * (tc - 1.0 - pos))
        g_chunk = math.exp(log_g * tc)
        q = q_ref[:, h * dk:(h + 1) * dk]
        k = k_ref[:, h * dk:(h + 1) * dk]
        v = v_ref[:, h * dv:(h + 1) * dv]
        s = lax.dot_general(q, k, (((1,), (1,)), ((), ())), preferred_element_type=F32)
        y = jnp.dot((s * decay).astype(BF16), v, preferred_element_type=F32)
        state = r_ref[h]
        y = y + jnp.dot(q, state.astype(BF16), preferred_element_type=F32) * xi
        vz = (v.astype(F32) * zeta).astype(BF16)
        kv = lax.dot_general(k, vz, (((0,), (0,)), ((), ())), preferred_element_type=F32)
        r_ref[h] = state * g_chunk + kv * scale

        mu = jnp.mean(y, axis=-1, keepdims=True)
        yc = y - mu
        var = jnp.mean(yc * yc, axis=-1, keepdims=True)
        sl = slice(h * dv, (h + 1) * dv)
        yn = yc * lax.rsqrt(var + EPS) * gng_ref[:, sl] + gnb_ref[:, sl]
        rg = rg_ref[:, sl].astype(F32)
        gate = _sigmoid(mg_ref[:, sl].astype(F32))
        o_ref[:, sl] = (rg * _sigmoid(rg) * yn * gate).astype(o_ref.dtype)


def _retention(proj, gn_g, gn_b, batch, seq, d):
    t = proj.shape[0]
    tc = RET_TC
    nc = seq // tc
    dk = d // (2 * RET_HEADS)
    dv = d // RET_HEADS
    qk_w = RET_HEADS * dk
    row = lambda b, c: b * nc + c
    kernel = functools.partial(_ret_kernel, dk=dk, dv=dv, tc=tc)
    return pl.pallas_call(
        kernel,
        out_shape=jax.ShapeDtypeStruct((t, d), BF16),
        grid=(batch, nc),
        in_specs=[pl.BlockSpec((tc, qk_w), lambda b, c: (row(b, c), 0)),
                  pl.BlockSpec((tc, qk_w), lambda b, c: (row(b, c), 1)),
                  pl.BlockSpec((tc, d), lambda b, c: (row(b, c), 1)),
                  pl.BlockSpec((tc, d), lambda b, c: (row(b, c), 2)),
                  pl.BlockSpec((tc, d), lambda b, c: (row(b, c), 6)),
                  pl.BlockSpec((1, d), lambda b, c: (0, 0)),
                  pl.BlockSpec((1, d), lambda b, c: (0, 0))],
        out_specs=pl.BlockSpec((tc, d), lambda b, c: (row(b, c), 0)),
        scratch_shapes=[pltpu.VMEM((RET_HEADS, dk, dv), F32)],
        compiler_params=_params("arbitrary", "arbitrary"),
        name="retention",
    )(proj, proj, proj, proj, proj, gn_g, gn_b)


def _halves_rms(x, lo, dh):
    sq = x * x
    s_lo = jnp.sum(jnp.where(lo, sq, 0.0), axis=-1, keepdims=True)
    s_hi = jnp.sum(jnp.where(lo, 0.0, sq), axis=-1, keepdims=True)
    inv = jnp.where(lo, lax.rsqrt(s_lo / dh + EPS), lax.rsqrt(s_hi / dh + EPS))
    return x * inv


def _attn_kernel(q_ref, k_ref, v_ref, gate_ref, slope_ref, gq_ref, gk_ref, gsub_ref,
                 lq1_ref, lk1_ref, lq2_ref, lk2_ref,
                 o_ref, ka_ref, kb_ref, vt_ref, acc_ref, *, tile, dh, nk):
    qi = pl.program_id(2)
    lane = lax.broadcasted_iota(I32, (tile, 2 * dh), 1)
    lo = lane < dh

    @pl.when(qi == 0)
    def _prepare_keys_values():
        slope = slope_ref[0]
        sub = lax.broadcasted_iota(I32, (tile, 2 * dh), 0)

        def body(c, carry):
            r0 = pl.multiple_of(c * tile, tile)
            kn = _halves_rms(k_ref[pl.ds(r0, tile), :].astype(F32), lo, dh) * gk_ref[...]
            j = r0 + sub
            j_hi = ((j >> 8) << 8).astype(F32) * slope
            j_lo = (j & 255).astype(F32) * slope
            bias_a = jnp.where(lane == dh, j_hi, jnp.where(lane == dh + 1, j_lo, 0.0))
            bias_b = jnp.where(lane == 0, j_hi, jnp.where(lane == 1, j_lo, 0.0))
            ka_ref[c] = jnp.where(lo, kn, bias_a).astype(BF16)
            kb_ref[c] = jnp.where(lo, bias_b, kn).astype(BF16)
            vt_ref[c] = v_ref[pl.ds(r0, tile), :].astype(F32).T.astype(BF16)
            return carry

        lax.fori_loop(0, nk, body, 0)

    qn = _halves_rms(q_ref[...].astype(F32), lo, dh) * gq_ref[...] * (dh ** -0.5)
    ones_a = jnp.where((lane == dh) | (lane == dh + 1), 1.0, 0.0)
    ones_b = jnp.where((lane == 0) | (lane == 1), 1.0, 0.0)
    qa_t = jnp.where(lo, qn, ones_a).T.astype(BF16)
    qb_t = jnp.where(lo, ones_b, qn).T.astype(BF16)

    acc_ref[...] = jnp.zeros_like(acc_ref)
    krow = lax.broadcasted_iota(I32, (tile, tile), 0)
    qcol = lax.broadcasted_iota(I32, (tile, tile), 1)
    future = krow > qcol

    def half(s, m, l, idx, vt, masked):
        if masked:
            s = jnp.where(future, MASK_VALUE, s)
        m_new = jnp.maximum(m, jnp.max(s, axis=0, keepdims=True))
        alpha = jnp.exp(m - m_new)
        p = jnp.exp(s - m_new)
        l_new = alpha * l + jnp.sum(p, axis=0, keepdims=True)
        acc_ref[idx] = acc_ref[idx] * alpha + jnp.dot(vt, p.astype(BF16), preferred_element_type=F32)
        return m_new, l_new

    def step(jb, carry, masked):
        m1, l1, m2, l2 = carry
        vt = vt_ref[jb]
        s1 = jnp.dot(ka_ref[jb], qa_t, preferred_element_type=F32)
        s2 = jnp.dot(kb_ref[jb], qb_t, preferred_element_type=F32)
        m1, l1 = half(s1, m1, l1, 0, vt, masked)
        m2, l2 = half(s2, m2, l2, 1, vt, masked)
        return m1, l1, m2, l2

    init = (jnp.full((1, tile), MASK_VALUE, F32), jnp.zeros((1, tile), F32),
            jnp.full((1, tile), MASK_VALUE, F32), jnp.zeros((1, tile), F32))
    carry = lax.fori_loop(0, qi, lambda jb, c: step(jb, c, False), init)
    _, l1, _, l2 = step(qi, carry, True)

    lam = (jnp.exp(jnp.sum(lq1_ref[...] * lk1_ref[...], axis=-1, keepdims=True))
           - jnp.exp(jnp.sum(lq2_ref[...] * lk2_ref[...], axis=-1, keepdims=True)) + LAMBDA_INIT)
    o_t = acc_ref[0] * (1.0 / l1) - lam * (acc_ref[1] * (1.0 / l2))
    o = _rms(o_t, axis=0).T * gsub_ref[...] * (1.0 - LAMBDA_INIT)
    o_ref[...] = (o * _sigmoid(gate_ref[...].astype(F32))).astype(o_ref.dtype)


def _diff_attention(proj, g_q, g_k, g_sub, lq1, lk1, lq2, lk2, batch, seq, d):
    t = proj.shape[0]
    tile = ATT_T
    nq = seq // tile
    dh = d // (2 * DIFF_HEADS)
    w = 2 * dh
    assert w == LANES
    q_col, k_col, v_col, gate_col = 3 * d // w, 4 * d // w, 5 * d // w, 7 * d // w
    slopes = 2.0 ** -(jnp.arange(DIFF_HEADS, dtype=F32) + 1.0)
    slopes = jnp.broadcast_to(slopes[:, None, None], (DIFF_HEADS, 1, LANES))
    tile2 = lambda a: jnp.concatenate([a, a], axis=-1)
    small = lambda n: pl.BlockSpec((1, n), lambda b, h, i: (0, 0))
    kernel = functools.partial(_attn_kernel, tile=tile, dh=dh, nk=nq)
    return pl.pallas_call(
        kernel,
        out_shape=jax.ShapeDtypeStruct((t, d), BF16),
        grid=(batch, DIFF_HEADS, nq),
        in_specs=[pl.BlockSpec((tile, w), lambda b, h, i: (b * nq + i, q_col + h)),
                  pl.BlockSpec((seq, w), lambda b, h, i: (b, k_col + h)),
                  pl.BlockSpec((seq, w), lambda b, h, i: (b, v_col + h)),
                  pl.BlockSpec((tile, w), lambda b, h, i: (b * nq + i, gate_col + h)),
                  pl.BlockSpec((1, 1, LANES), lambda b, h, i: (h, 0, 0)),
                  small(w), small(w), small(w), small(dh), small(dh), small(dh), small(dh)],
        out_specs=pl.BlockSpec((tile, w), lambda b, h, i: (b * nq + i, h)),
        scratch_shapes=[pltpu.VMEM((nq, tile, w), BF16),
                        pltpu.VMEM((nq, tile, w), BF16),
                        pltpu.VMEM((nq, w, tile), BF16),
                        pltpu.VMEM((2, w, tile), F32)],
        compiler_params=_params("arbitrary", "arbitrary", "arbitrary"),
        name="diff_attn",
    )(proj, proj, proj, proj, slopes, tile2(g_q), tile2(g_k), g_sub, lq1, lk1, lq2, lk2)


def _out_kernel(x_ref, ret_ref, att_ref, mod_ref, wo_ref, g2_ref, wr_ref, br_ref,
                x1_ref, h2_ref, ti_ref, tw_ref, rk_ref, cnt_ref, run_ref, *, tm, n_exp):
    @pl.when(pl.program_id(0) == 0)
    def _():
        run_ref[...] = jnp.zeros_like(run_ref)

    m = mod_ref[0]
    merged = (ret_ref[...].astype(F32) + att_ref[...].astype(F32)).astype(BF16)
    x1 = x_ref[...] + m[2:3] * jnp.dot(merged, wo_ref[...], preferred_element_type=F32)
    x1_ref[...] = x1
    h2 = _rms(x1) * g2_ref[...] * (1.0 + m[4:5]) + m[3:4]
    h2_ref[...] = h2

    hi = h2.astype(BF16)
    lo = (h2 - hi.astype(F32)).astype(BF16)
    both = jnp.dot(hi, wr_ref[...], preferred_element_type=F32)
    logits = (both[:, :LANES] + both[:, LANES:]
              + jnp.dot(lo, wr_ref[:, :LANES], preferred_element_type=F32) + br_ref[...])
    lt = logits.T[:n_exp]

    erow = lax.broadcasted_iota(I32, (n_exp, tm), 0)
    vals, sels = [], []
    for k in range(TOP_K):
        mx = jnp.max(lt, axis=0, keepdims=True)
        idx = jnp.min(jnp.where(lt == mx, erow, n_exp), axis=0, keepdims=True)
        sel = erow == idx
        ti_ref[k:k + 1, :] = idx
        vals.append(mx)
        sels.append(sel)
        lt = jnp.where(sel, MASK_VALUE, lt)
    exps = [jnp.exp(v - vals[0]) for v in vals]
    inv = 1.0 / (exps[0] + exps[1] + exps[2] + exps[3])
    for k in range(TOP_K):
        tw_ref[k:k + 1, :] = exps[k] * inv

    chosen = jnp.where(sels[0] | sels[1] | sels[2] | sels[3], 1.0, 0.0)
    before = (lax.broadcasted_iota(I32, (tm, tm), 0) < lax.broadcasted_iota(I32, (tm, tm), 1))
    prefix = jnp.dot(chosen.astype(BF16), jnp.where(before, 1.0, 0.0).astype(BF16),
                     preferred_element_type=F32)
    pos = prefix + run_ref[...]
    for k in range(TOP_K):
        rk_ref[k:k + 1, :] = jnp.sum(jnp.where(sels[k], pos, 0.0), axis=0, keepdims=True).astype(I32)
    run_ref[...] = run_ref[...] + jnp.sum(chosen, axis=1, keepdims=True)
    cnt_ref[...] = run_ref[...]


def _out_router(x2, ret, att, mod3, w_out_bf16, g2, wr_split, br_pad, seq, n_exp):
    t, d = x2.shape
    tm = OUT_TM
    row = lambda i: (i, 0)
    const = lambda i: (0, 0)
    kernel = functools.partial(_out_kernel, tm=tm, n_exp=n_exp)
    return pl.pallas_call(
        kernel,
        out_shape=(jax.ShapeDtypeStruct((t, d), F32),
                   jax.ShapeDtypeStruct((t, d), F32),
                   jax.ShapeDtypeStruct((TOP_K, t), I32),
                   jax.ShapeDtypeStruct((TOP_K, t), F32),
                   jax.ShapeDtypeStruct((TOP_K, t), I32),
                   jax.ShapeDtypeStruct((n_exp, 1), F32)),
        grid=(t // tm,),
        in_specs=[pl.BlockSpec((tm, d), row), pl.BlockSpec((tm, d), row), pl.BlockSpec((tm, d), row),
                  pl.BlockSpec((1, 6, d), lambda i: (i * tm // seq, 0, 0)),
                  pl.BlockSpec((d, d), const), pl.BlockSpec((1, d), const),
                  pl.BlockSpec((d, 2 * LANES), const), pl.BlockSpec((1, LANES), const)],
        out_specs=(pl.BlockSpec((tm, d), row), pl.BlockSpec((tm, d), row),
                   pl.BlockSpec((TOP_K, tm), lambda i: (0, i)),
                   pl.BlockSpec((TOP_K, tm), lambda i: (0, i)),
                   pl.BlockSpec((TOP_K, tm), lambda i: (0, i)),
                   pl.BlockSpec((n_exp, 1), const)),
        scratch_shapes=[pltpu.VMEM((n_exp, 1), F32)],
        compiler_params=_params("arbitrary"),
        name="out_router",
    )(x2, ret, att, mod3, w_out_bf16, g2, wr_split, br_pad)


def _dispatch_kernel(dest_ref, h_ref, xs_in_ref, xs_ref, sem, *, tm):
    del xs_in_ref

    def copy(t, d):
        return pltpu.make_async_copy(h_ref.at[pl.ds(t, 1)], xs_ref.at[pl.ds(d, 1)], sem)

    def start(i, carry):
        for u in range(ROW_UNROLL):
            t = i * ROW_UNROLL + u
            for k in range(TOP_K):
                copy(t, dest_ref[k, t]).start()
        return carry

    def wait(i, carry):
        for _ in range(ROW_UNROLL * TOP_K):
            copy(0, 0).wait()
        return carry

    lax.fori_loop(0, tm // ROW_UNROLL, start, 0)
    lax.fori_loop(0, tm // ROW_UNROLL, wait, 0)


def _dispatch(dest3, h2, xs_zero):
    t, d = h2.shape
    tm = DISP_TM
    kernel = functools.partial(_dispatch_kernel, tm=tm)
    return pl.pallas_call(
        kernel,
        out_shape=jax.ShapeDtypeStruct(xs_zero.shape, xs_zero.dtype),
        grid=(t // tm,),
        in_specs=[pl.BlockSpec((None, TOP_K, tm), lambda i: (i, 0, 0), memory_space=pltpu.SMEM),
                  pl.BlockSpec((tm, d), lambda i: (i, 0)),
                  pl.BlockSpec(memory_space=pl.ANY)],
        out_specs=pl.BlockSpec(memory_space=pl.ANY),
        scratch_shapes=[pltpu.SemaphoreType.DMA],
        input_output_aliases={2: 0},
        compiler_params=_params("arbitrary"),
        name="dispatch",
    )(dest3, h2, xs_zero)


def _expert_kernel(blk_e_ref, n_used_ref, x_ref, w1_ref, b1_ref, w2_ref, b2_ref, o_ref, *, f):
    i = pl.program_id(0)

    @pl.when(i < n_used_ref[0])
    def _():
        gu = jnp.dot(x_ref[...].astype(BF16), w1_ref[0], preferred_element_type=F32) + b1_ref[0]
        g = jnp.minimum(gu[:, :f], SWIGLU_LIMIT)
        u = jnp.clip(gu[:, f:], -SWIGLU_LIMIT, SWIGLU_LIMIT)
        a = g * _sigmoid(SWIGLU_ALPHA * g) * (u + 1.0)
        o_ref[...] = jnp.dot(a.astype(BF16), w2_ref[0], preferred_element_type=F32) + b2_ref[0]

    @pl.when(i >= n_used_ref[0])
    def _():
        o_ref[...] = jnp.zeros_like(o_ref)


def _experts(blk_e, n_used, xs, w1_bf16, b1, w2_bf16, b2):
    n_rows, d = xs.shape
    n_exp, _, f2 = w1_bf16.shape
    f = f2 // 2
    blk = EXP_BLK
    kernel = functools.partial(_expert_kernel, f=f)
    grid_spec = pltpu.PrefetchScalarGridSpec(
        num_scalar_prefetch=2,
        grid=(n_rows // blk,),
        in_specs=[pl.BlockSpec((blk, d), lambda i, e, n: (i, 0)),
                  pl.BlockSpec((1, d, f2), lambda i, e, n: (e[i], 0, 0)),
                  pl.BlockSpec((1, 1, f2), lambda i, e, n: (e[i], 0, 0)),
                  pl.BlockSpec((1, f, d), lambda i, e, n: (e[i], 0, 0)),
                  pl.BlockSpec((1, 1, d), lambda i, e, n: (e[i], 0, 0))],
        out_specs=pl.BlockSpec((blk, d), lambda i, e, n: (i, 0)),
    )
    return pl.pallas_call(
        kernel,
        out_shape=jax.ShapeDtypeStruct((n_rows, d), F32),
        grid_spec=grid_spec,
        compiler_params=_params("arbitrary"),
        name="experts",
    )(blk_e, n_used, xs, w1_bf16, b1, w2_bf16, b2)


def _combine_kernel(dest_ref, eo_ref, w_ref, x1_ref, mod_ref, o_ref, buf_ref, sem, *, tm):
    def copy(k, t, d):
        return pltpu.make_async_copy(eo_ref.at[pl.ds(d, 1)], buf_ref.at[k, pl.ds(t, 1)], sem)

    def start(i, carry):
        for u in range(ROW_UNROLL):
            t = i * ROW_UNROLL + u
            for k in range(TOP_K):
                copy(k, t, dest_ref[k, t]).start()
        return carry

    def wait(i, carry):
        for _ in range(ROW_UNROLL * TOP_K):
            copy(0, 0, 0).wait()
        return carry

    lax.fori_loop(0, tm // ROW_UNROLL, start, 0)
    lax.fori_loop(0, tm // ROW_UNROLL, wait, 0)

    y = w_ref[:, 0:1] * buf_ref[0]
    for k in range(1, TOP_K):
        y = y + w_ref[:, k:k + 1] * buf_ref[k]
    o_ref[...] = x1_ref[...] + mod_ref[0][5:6] * y


def _combine(dest3, eo, w_tok, x1, mod3, seq):
    t, d = x1.shape
    tm = COMB_TM
    kernel = functools.partial(_combine_kernel, tm=tm)
    return pl.pallas_call(
        kernel,
        out_shape=jax.ShapeDtypeStruct((t, d), F32),
        grid=(t // tm,),
        in_specs=[pl.BlockSpec((None, TOP_K, tm), lambda i: (i, 0, 0), memory_space=pltpu.SMEM),
                  pl.BlockSpec(memory_space=pl.ANY),
                  pl.BlockSpec((tm, TOP_K), lambda i: (i, 0)),
                  pl.BlockSpec((tm, d), lambda i: (i, 0)),
                  pl.BlockSpec((1, 6, d), lambda i: (i * tm // seq, 0, 0))],
        out_specs=pl.BlockSpec((tm, d), lambda i: (i, 0)),
        scratch_shapes=[pltpu.VMEM((TOP_K, tm, d), F32), pltpu.SemaphoreType.DMA],
        compiler_params=_params("arbitrary"),
        name="combine",
    )(dest3, eo, w_tok, x1, mod3)


def _routing_tables(top_i, rank, counts, n_tokens, n_exp):
    blk = EXP_BLK
    counts = counts.astype(I32)
    padded = ((counts + blk - 1) // blk) * blk
    pad_end = jnp.cumsum(padded)
    pad_start = pad_end - padded
    dest = pad_start[top_i] + rank
    n_blocks = n_tokens * TOP_K // blk + n_exp
    blk_e = jnp.searchsorted(pad_end, jnp.arange(n_blocks, dtype=I32) * blk, side="right")
    blk_e = jnp.minimum(blk_e, n_exp - 1).astype(I32)
    n_used = (pad_end[-1:] // blk).astype(I32)
    return dest, blk_e, n_used, n_blocks * blk


def _tile_major(a, tm):
    k, t = a.shape
    return a.reshape(k, t // tm, tm).transpose(1, 0, 2)


def kernel(x, c, w_ada, b_ada, g_norm1, w_in, g_ret_gn, b_ret_gn, g_qnorm, g_knorm,
           lambda_q1, lambda_k1, lambda_q2, lambda_k2, g_diff_subln, w_out, g_norm2,
           w_router, b_router, w_expert_in, b_expert_in, w_expert_out, b_expert_out):
    batch, seq, d = x.shape
    depth = w_ada.shape[0]
    assert depth == 1
    t = batch * seq
    n_exp = w_router.shape[-1]
    l = 0

    mod = _adaln_mod(c, w_ada[l], b_ada[l][None, :])
    mod3 = mod.reshape(batch, 6, d)
    x2 = x.reshape(t, d)

    proj = _in_proj(x2, mod3, g_norm1[l][None, :], w_in[l].astype(BF16), seq)
    ret = _retention(proj, g_ret_gn[l][None, :], b_ret_gn[l][None, :], batch, seq, d)
    att = _diff_attention(proj, g_qnorm[l][None, :], g_knorm[l][None, :], g_diff_subln[l][None, :],
                          lambda_q1[l][None, :], lambda_k1[l][None, :],
                          lambda_q2[l][None, :], lambda_k2[l][None, :], batch, seq, d)

    wr = w_router[l]
    wr_hi = wr.astype(BF16)
    wr_lo = (wr - wr_hi.astype(F32)).astype(BF16)
    pad = lambda a: jnp.pad(a, ((0, 0), (0, LANES - n_exp)))
    wr_split = jnp.concatenate([pad(wr_hi), pad(wr_lo)], axis=1)
    br_pad = jnp.pad(b_router[l][None, :], ((0, 0), (0, LANES - n_exp)), constant_values=MASK_VALUE)
    x1, h2, top_i, top_w, rank, counts = _out_router(
        x2, ret, att, mod3, w_out[l].astype(BF16), g_norm2[l][None, :], wr_split, br_pad, seq, n_exp)

    dest, blk_e, n_used, n_rows = _routing_tables(top_i, rank, counts[:, 0], t, n_exp)
    xs = _dispatch(_tile_major(dest, DISP_TM), h2, jnp.zeros((n_rows, d), F32))
    eo = _experts(blk_e, n_used, xs, w_expert_in[l].astype(BF16), b_expert_in[l][:, None, :],
                  w_expert_out[l].astype(BF16), b_expert_out[l][:, None, :])
    out = _combine(_tile_major(dest, COMB_TM), eo, top_w.T, x1, mod3, seq)
    return out.reshape(batch, seq, d)
```

```python
import functools
import math

import jax
import jax.numpy as jnp
from jax import lax
from jax.experimental import pallas as pl
from jax.experimental.pallas import tpu as pltpu

F32 = jnp.float32
BF16 = jnp.bfloat16
I32 = jnp.int32

EPS = 1e-6
LOG2E = 1.4426950216293335
MASK_VALUE = -1e30
LANES = 128
BF16_SUBLANES = 16
VMEM_LIMIT_BYTES = 48 * 1024 * 1024

RET_HEADS = 4
DIFF_HEADS = 8
TOP_K = 4
SWIGLU_LIMIT = 7.0
SWIGLU_ALPHA = 1.702
LAMBDA_INIT = 0.8 - 0.6 * math.exp(-0.3 * 0)

IN_TM, IN_TN = 1024, 1024
RET_TC = 256
ATT_T = 512
TOK_TM = 512
EXP_BLK = 256
RUN_ALIGN = BF16_SUBLANES
GATHER_RC = 256
SCATTER_RC = 128


def _round_up(n, m):
    return (n + m - 1) // m * m


def _stage_rows(n_exp):
    return _round_up(TOK_TM * TOP_K + n_exp * (RUN_ALIGN - 1), max(GATHER_RC, SCATTER_RC))


SLAB_SIZES = tuple(TOK_TM >> s for s in range(TOK_TM.bit_length()) if TOK_TM >> s >= RUN_ALIGN)
TAIL_SIZES = tuple(s for s in SLAB_SIZES if s < EXP_BLK)


def _params(*sem):
    return pltpu.CompilerParams(dimension_semantics=sem, vmem_limit_bytes=VMEM_LIMIT_BYTES)


def _sigmoid(x):
    return 1.0 / (1.0 + jnp.exp(-x))


def _rms(x, axis=-1):
    return x * lax.rsqrt(jnp.mean(x * x, axis=axis, keepdims=True) + EPS)


def _mod_kernel(c_ref, w_ref, b_ref, o_ref):
    c = c_ref[...]
    s = c * _sigmoid(c)
    o_ref[...] = jnp.dot(s, w_ref[...], preferred_element_type=F32,
                         precision=lax.Precision.HIGHEST) + b_ref[...]


def _adaln_mod(c, w, b):
    bn, d = c.shape
    n = w.shape[1]
    tn = d
    return pl.pallas_call(
        _mod_kernel,
        out_shape=jax.ShapeDtypeStruct((bn, n), F32),
        grid=(n // tn,),
        in_specs=[pl.BlockSpec((bn, d), lambda j: (0, 0)),
                  pl.BlockSpec((d, tn), lambda j: (0, j)),
                  pl.BlockSpec((1, tn), lambda j: (0, j))],
        out_specs=pl.BlockSpec((bn, tn), lambda j: (0, j)),
        compiler_params=_params("arbitrary"),
        name="adaln_mod",
    )(c, w, b)


def _inproj_kernel(x_ref, mod_ref, g_ref, w_ref, o_ref, h_ref):
    @pl.when(pl.program_id(1) == 0)
    def _():
        m = mod_ref[0]
        h = _rms(x_ref[...]) * g_ref[...] * (1.0 + m[1:2]) + m[0:1]
        h_ref[...] = h.astype(BF16)

    o_ref[...] = jnp.dot(h_ref[...], w_ref[...], preferred_element_type=F32).astype(o_ref.dtype)


def _in_proj(x2, mod3, g1, w_in_bf16, seq):
    t, d = x2.shape
    n = w_in_bf16.shape[1]
    tm, tn = IN_TM, IN_TN
    return pl.pallas_call(
        _inproj_kernel,
        out_shape=jax.ShapeDtypeStruct((t, n), BF16),
        grid=(t // tm, n // tn),
        in_specs=[pl.BlockSpec((tm, d), lambda i, j: (i, 0)),
                  pl.BlockSpec((1, 6, d), lambda i, j: (i * tm // seq, 0, 0)),
                  pl.BlockSpec((1, d), lambda i, j: (0, 0)),
                  pl.BlockSpec((d, tn), lambda i, j: (0, j))],
        out_specs=pl.BlockSpec((tm, tn), lambda i, j: (i, j)),
        scratch_shapes=[pltpu.VMEM((tm, d), BF16)],
        compiler_params=_params("arbitrary", "arbitrary"),
        name="in_proj",
    )(x2, mod3, g1, w_in_bf16)


def _ret_kernel(q_ref, k_ref, v_ref, rg_ref, mg_ref, gng_ref, gnb_ref, o_ref, r_ref, *, dk, dv, tc):
    @pl.when(pl.program_id(1) == 0)
    def _():
        r_ref[...] = jnp.zeros_like(r_ref)

    row = lax.broadcasted_iota(I32, (tc, tc), 0)
    col = lax.broadcasted_iota(I32, (tc, tc), 1)
    rel = (row - col).astype(F32)
    pos = lax.broadcasted_iota(I32, (tc, 1), 0).astype(F32)
    scale = dk ** -0.5
    for h in range(RET_HEADS):
        log_g = math.log(1.0 - 2.0 ** (-5.0 - h))
        decay = jnp.where(rel >= 0, jnp.exp(log_g * jnp.maximum(rel, 0.0)), 0.0) * scale
        xi = jnp.exp(log_g * (pos + 1.0))
        zeta = jnp.exp(log_g * (tc - 1.0 - pos))
        g_chunk = math.exp(log_g * tc)
        q = q_ref[:, h * dk:(h + 1) * dk]
        k = k_ref[:, h * dk:(h + 1) * dk]
        v = v_ref[:, h * dv:(h + 1) * dv]
        s = lax.dot_general(q, k, (((1,), (1,)), ((), ())), preferred_element_type=F32)
        y = jnp.dot((s * decay).astype(BF16), v, preferred_element_type=F32)
        state = r_ref[h]
        y = y + jnp.dot(q, state.astype(BF16), preferred_element_type=F32) * xi
        vz = (v.astype(F32) * zeta).astype(BF16)
        kv = lax.dot_general(k, vz, (((0,), (0,)), ((), ())), preferred_element_type=F32)
        r_ref[h] = state * g_chunk + kv * scale

        mu = jnp.mean(y, axis=-1, keepdims=True)
        yc = y - mu
        var = jnp.mean(yc * yc, axis=-1, keepdims=True)
        sl = slice(h * dv, (h + 1) * dv)
        yn = yc * lax.rsqrt(var + EPS) * gng_ref[:, sl] + gnb_ref[:, sl]
        rg = rg_ref[:, sl].astype(F32)
        gate = _sigmoid(mg_ref[:, sl].astype(F32))
        o_ref[:, sl] = (rg * _sigmoid(rg) * yn * gate).astype(o_ref.dtype)


def _retention(proj, gn_g, gn_b, batch, seq, d):
    t = proj.shape[0]
    tc = RET_TC
    nc = seq // tc
    dk = d // (2 * RET_HEADS)
    dv = d // RET_HEADS
    qk_w = RET_HEADS * dk
    row = lambda b, c: b * nc + c
    kernel = functools.partial(_ret_kernel, dk=dk, dv=dv, tc=tc)
    return pl.pallas_call(
        kernel,
        out_shape=jax.ShapeDtypeStruct((t, d), BF16),
        grid=(batch, nc),
        in_specs=[pl.BlockSpec((tc, qk_w), lambda b, c: (row(b, c), 0)),
                  pl.BlockSpec((tc, qk_w), lambda b, c: (row(b, c), 1)),
                  pl.BlockSpec((tc, d), lambda b, c: (row(b, c), 1)),
                  pl.BlockSpec((tc, d), lambda b, c: (row(b, c), 2)),
                  pl.BlockSpec((tc, d), lambda b, c: (row(b, c), 6)),
                  pl.BlockSpec((1, d), lambda b, c: (0, 0)),
                  pl.BlockSpec((1, d), lambda b, c: (0, 0))],
        out_specs=pl.BlockSpec((tc, d), lambda b, c: (row(b, c), 0)),
        scratch_shapes=[pltpu.VMEM((RET_HEADS, dk, dv), F32)],
        compiler_params=_params("arbitrary", "arbitrary"),
        name="retention",
    )(proj, proj, proj, proj, proj, gn_g, gn_b)


def _halves_rms(x, lo, dh):
    sq = x * x
    s_lo = jnp.sum(jnp.where(lo, sq, 0.0), axis=-1, keepdims=True)
    s_hi = jnp.sum(jnp.where(lo, 0.0, sq), axis=-1, keepdims=True)
    inv = jnp.where(lo, lax.rsqrt(s_lo / dh + EPS), lax.rsqrt(s_hi / dh + EPS))
    return x * inv


def _attn_kernel(q_ref, k_ref, v_ref, gate_ref, slope_ref, qca_ref, qcb_ref, gq_ref, gk_ref, gsub_ref,
                 lq1_ref, lk1_ref, lq2_ref, lk2_ref,
                 o_ref, ka_ref, kb_ref, vt_ref, sa_ref, sb_ref, acc_ref, m_ref, *, tile, dh, dv, nk):
    qi = pl.program_id(2)
    lane = lax.broadcasted_iota(I32, (tile, 2 * dh), 1)
    lo = lane < dh

    @pl.when(qi == 0)
    def _prepare_keys_values():
        slope = slope_ref[0]
        sub = lax.broadcasted_iota(I32, (tile, 2 * dh), 0)
        ones_rows = jnp.where(lax.broadcasted_iota(I32, (BF16_SUBLANES, tile), 0) == 0, 1.0, 0.0)

        def body(c, carry):
            r0 = pl.multiple_of(c * tile, tile)
            kn = _halves_rms(k_ref[pl.ds(r0, tile), :].astype(F32), lo, dh) * gk_ref[...]
            j = r0 + sub
            j_hi = ((j >> 8) << 8).astype(F32) * slope
            j_lo = (j & 255).astype(F32) * slope
            off = lane & (dh - 1)
            bias = jnp.where(off < 6, jnp.where((off & 1) == 0, j_hi, j_lo), 0.0)
            ka_ref[c] = jnp.where(lo, kn, bias).astype(BF16)
            kb_ref[c] = jnp.where(lo, bias, kn).astype(BF16)
            vt_ref[c, :dv, :] = v_ref[pl.ds(r0, tile), :].astype(F32).T.astype(BF16)
            vt_ref[c, dv:, :] = ones_rows.astype(BF16)
            return carry

        lax.fori_loop(0, nk, body, 0)

    qn = _halves_rms(q_ref[...].astype(F32), lo, dh) * gq_ref[...] * (dh ** -0.5 * LOG2E)
    qa_t = jnp.where(lo, qn, qca_ref[...]).T.astype(BF16)
    qb_t = jnp.where(lo, qcb_ref[...], qn).T.astype(BF16)

    def scores(jb, s_ref):
        s_ref[0] = jnp.dot(ka_ref[jb], qa_t, preferred_element_type=F32)
        s_ref[1] = jnp.dot(kb_ref[jb], qb_t, preferred_element_type=F32)

    def absorb(jb, s_ref, masked):
        vt = vt_ref[jb]
        for idx in range(2):
            s = s_ref[idx]
            if masked:
                krow = lax.broadcasted_iota(I32, (tile, tile), 0)
                qcol = lax.broadcasted_iota(I32, (tile, tile), 1)
                s = jnp.where(krow > qcol, MASK_VALUE, s)
            m_old = m_ref[idx]
            m_new = jnp.maximum(m_old, jnp.max(s, axis=0, keepdims=True))
            m_ref[idx] = m_new
            p = jnp.exp2(s - m_new).astype(BF16)
            acc_ref[idx] = (acc_ref[idx] * jnp.exp2(m_old - m_new)
                            + jnp.dot(vt, p, preferred_element_type=F32))

    acc_ref[...] = jnp.zeros_like(acc_ref)
    m_ref[...] = jnp.full_like(m_ref, MASK_VALUE)
    scores(0, sa_ref)

    def pair(j2, carry):
        jb = 2 * j2
        scores(jb + 1, sb_ref)
        absorb(jb, sa_ref, False)
        scores(jb + 2, sa_ref)
        absorb(jb + 1, sb_ref, False)
        return carry

    lax.fori_loop(0, qi // 2, pair, 0)

    @pl.when(qi % 2 == 1)
    def _():
        scores(qi, sb_ref)
        absorb(qi - 1, sa_ref, False)
        absorb(qi, sb_ref, True)

    @pl.when(qi % 2 == 0)
    def _():
        absorb(qi, sa_ref, True)

    lam =(jnp.exp(jnp.sum(lq1_ref[...] * lk1_ref[...], axis=-1, keepdims=True))
           - jnp.exp(jnp.sum(lq2_ref[...] * lk2_ref[...], axis=-1, keepdims=True)) + LAMBDA_INIT)
    a1, a2 = acc_ref[0], acc_ref[1]
    o_t = a1[:dv] * (1.0 / a1[dv:dv + 1]) - lam * (a2[:dv] * (1.0 / a2[dv:dv + 1]))
    o = _rms(o_t, axis=0).T * gsub_ref[...] * (1.0 - LAMBDA_INIT)
    o_ref[...] = (o * _sigmoid(gate_ref[...].astype(F32))).astype(o_ref.dtype)


def _bf16_terms(x, n):
    terms, rest = [], jnp.asarray(x, F32)
    for _ in range(n):
        term = rest.astype(BF16).astype(F32)
        terms.append(term)
        rest = rest - term
    return terms


def _diff_attention(proj, g_q, g_k, g_sub, lq1, lk1, lq2, lk2, batch, seq, d):
    t = proj.shape[0]
    tile = ATT_T
    nq = seq // tile
    dh = d // (2 * DIFF_HEADS)
    w = 2 * dh
    assert w == LANES
    q_col, k_col, v_col, gate_col = 3 * d // w, 4 * d // w, 5 * d // w, 7 * d // w
    slopes = 2.0 ** -(jnp.arange(DIFF_HEADS, dtype=F32) + 1.0)
    slopes = jnp.broadcast_to(slopes[:, None, None], (DIFF_HEADS, 1, LANES))
    c1, c2, c3 = _bf16_terms(LOG2E, 3)
    qconst = jnp.zeros((dh,), F32).at[:6].set(jnp.stack([c1, c1, c2, c2, c3, c3]))
    zeros = jnp.zeros((dh,), F32)
    qconst_a = jnp.concatenate([zeros, qconst])[None, :]
    qconst_b = jnp.concatenate([qconst, zeros])[None, :]
    tile2 = lambda a: jnp.concatenate([a, a], axis=-1)
    small = lambda n: pl.BlockSpec((1, n), lambda b, h, i: (0, 0))
    kernel = functools.partial(_attn_kernel, tile=tile, dh=dh, dv=w, nk=nq)
    return pl.pallas_call(
        kernel,
        out_shape=jax.ShapeDtypeStruct((t, d), BF16),
        grid=(batch, DIFF_HEADS, nq),
        in_specs=[pl.BlockSpec((tile, w), lambda b, h, i: (b * nq + i, q_col + h)),
                  pl.BlockSpec((seq, w), lambda b, h, i: (b, k_col + h)),
                  pl.BlockSpec((seq, w), lambda b, h, i: (b, v_col + h)),
                  pl.BlockSpec((tile, w), lambda b, h, i: (b * nq + i, gate_col + h)),
                  pl.BlockSpec((1, 1, LANES), lambda b, h, i: (h, 0, 0)),
                  small(w), small(w), small(w), small(w), small(w),
                  small(dh), small(dh), small(dh), small(dh)],
        out_specs=pl.BlockSpec((tile, w), lambda b, h, i: (b * nq + i, h)),
        scratch_shapes=[pltpu.VMEM((nq, tile, w), BF16),
                        pltpu.VMEM((nq, tile, w), BF16),
                        pltpu.VMEM((nq, w + BF16_SUBLANES, tile), BF16),
                        pltpu.VMEM((2, tile, tile), F32),
                        pltpu.VMEM((2, tile, tile), F32),
                        pltpu.VMEM((2, w + BF16_SUBLANES, tile), F32),
                        pltpu.VMEM((2, 1, tile), F32)],
        compiler_params=_params("arbitrary", "arbitrary", "arbitrary"),
        name="diff_attn",
    )(proj, proj, proj, proj, slopes, qconst_a, qconst_b, tile2(g_q), tile2(g_k), g_sub,
      lq1, lk1, lq2, lk2)


def _out_kernel(x_ref, ret_ref, att_ref, mod_ref, wo_ref, g2_ref, wr_ref, br_ref,
                x1_ref, h2_ref, tw_ref, ld_ref, cnt_ref, *, tm, n_exp):
    m = mod_ref[0]
    merged = (ret_ref[...].astype(F32) + att_ref[...].astype(F32)).astype(BF16)
    x1 = x_ref[...] + m[2:3] * jnp.dot(merged, wo_ref[...], preferred_element_type=F32)
    x1_ref[...] = x1
    h2 = _rms(x1) * g2_ref[...] * (1.0 + m[4:5]) + m[3:4]
    hi = h2.astype(BF16)
    h2_ref[...] = hi

    lo = (h2 - hi.astype(F32)).astype(BF16)
    both = jnp.dot(hi, wr_ref[...], preferred_element_type=F32)
    logits = (both[:, :LANES] + both[:, LANES:]
              + jnp.dot(lo, wr_ref[:, :LANES], preferred_element_type=F32) + br_ref[...])
    lt = logits.T[:n_exp]

    erow = lax.broadcasted_iota(I32, (n_exp, tm), 0)
    vals, sels = [], []
    for k in range(TOP_K):
        mx = jnp.max(lt, axis=0, keepdims=True)
        idx = jnp.min(jnp.where(lt == mx, erow, n_exp), axis=0, keepdims=True)
        sel = erow == idx
        vals.append(mx)
        sels.append(sel)
        lt = jnp.where(sel, MASK_VALUE, lt)
    exps = [jnp.exp(v - vals[0]) for v in vals]
    inv = 1.0 / (exps[0] + exps[1] + exps[2] + exps[3])
    for k in range(TOP_K):
        tw_ref[k:k + 1, :] = exps[k] * inv

    chosen = jnp.where(sels[0] | sels[1] | sels[2] | sels[3], 1.0, 0.0)
    before = (lax.broadcasted_iota(I32, (tm, tm), 0) < lax.broadcasted_iota(I32, (tm, tm), 1))
    prefix = jnp.dot(chosen.astype(BF16), jnp.where(before, 1.0, 0.0).astype(BF16),
                     preferred_element_type=F32)
    count = jnp.sum(chosen, axis=1, keepdims=True)
    cnt_ref[...] = count
    padded = jnp.ceil(count * (1.0 / RUN_ALIGN)) * RUN_ALIGN
    below = (lax.broadcasted_iota(I32, (n_exp, n_exp), 1) < lax.broadcasted_iota(I32, (n_exp, n_exp), 0))
    run_start = jnp.dot(jnp.where(below, 1.0, 0.0).astype(BF16),
                        jnp.broadcast_to(padded, (n_exp, LANES)).astype(BF16),
                        preferred_element_type=F32)[:, :1]
    pos = prefix + run_start
    for k in range(TOP_K):
        ld_ref[k:k + 1, :] = jnp.sum(jnp.where(sels[k], pos, 0.0), axis=0, keepdims=True).astype(I32)


def _out_router(x2, ret, att, mod3, w_out_bf16, g2, wr_split, br_pad, seq, n_exp):
    t, d = x2.shape
    tm = TOK_TM
    row = lambda i: (i, 0)
    const = lambda i: (0, 0)
    kernel = functools.partial(_out_kernel, tm=tm, n_exp=n_exp)
    return pl.pallas_call(
        kernel,
        out_shape=(jax.ShapeDtypeStruct((t, d), F32),
                   jax.ShapeDtypeStruct((t, d), BF16),
                   jax.ShapeDtypeStruct((TOP_K, t), F32),
                   jax.ShapeDtypeStruct((TOP_K, t), I32),
                   jax.ShapeDtypeStruct((t // tm, n_exp, 1), F32)),
        grid=(t // tm,),
        in_specs=[pl.BlockSpec((tm, d), row), pl.BlockSpec((tm, d), row), pl.BlockSpec((tm, d), row),
                  pl.BlockSpec((1, 6, d), lambda i: (i * tm // seq, 0, 0)),
                  pl.BlockSpec((d, d), const), pl.BlockSpec((1, d), const),
                  pl.BlockSpec((d, 2 * LANES), const), pl.BlockSpec((1, LANES), const)],
        out_specs=(pl.BlockSpec((tm, d), row), pl.BlockSpec((tm, d), row),
                   pl.BlockSpec((TOP_K, tm), lambda i: (0, i)),
                   pl.BlockSpec((TOP_K, tm), lambda i: (0, i)),
                   pl.BlockSpec((None, n_exp, 1), lambda i: (i, 0, 0))),
        compiler_params=_params("arbitrary"),
        name="out_router",
    )(x2, ret, att, mod3, w_out_bf16, g2, wr_split, br_pad)


def _run_slabs(meta_ref, tile, n_exp, local_ref, global_ref, sem, *, to_global, wait):
    def per_expert(e, carry):
        n = meta_ref[(tile * 3 + 0) * n_exp + e]
        local = meta_ref[(tile * 3 + 1) * n_exp + e]
        glob = meta_ref[(tile * 3 + 2) * n_exp + e]
        off = 0
        for size in SLAB_SIZES:
            take = n & size

            @pl.when(take != 0)
            def _(off=off, size=size):
                loc = local_ref.at[pl.ds(pl.multiple_of(local + off, RUN_ALIGN), size)]
                glo = global_ref.at[pl.ds(pl.multiple_of(glob + off, RUN_ALIGN), size)]
                cp = pltpu.make_async_copy(loc, glo, sem) if to_global else pltpu.make_async_copy(glo, loc, sem)
                if wait:
                    cp.wait()
                else:
                    cp.start()

            off = off + take
        return carry

    lax.fori_loop(0, n_exp, per_expert, 0)


def _dispatch_kernel(meta_ref, tail_ref, ld_ref, h_ref, xs_ref, stage_ref, zero_ref, sems, tail_sem,
                     *, tm, n_exp, n_tiles, n_blocks, stage_rows):
    i = pl.program_id(0)
    slot = i & 1
    stage = stage_ref.at[slot]

    def tails(wait):
        def unused_block(b, carry):
            cp = pltpu.make_async_copy(
                zero_ref, xs_ref.at[pl.ds(pl.multiple_of(b * EXP_BLK, EXP_BLK), EXP_BLK)], tail_sem)
            if wait:
                cp.wait()
            else:
                cp.start()
            return carry

        lax.fori_loop(tail_ref[2 * n_exp] // EXP_BLK, n_blocks, unused_block, 0)

        def per_expert(e, carry):
            start, n = tail_ref[e], tail_ref[n_exp + e]
            off = 0
            for size in TAIL_SIZES:
                take = n & size

                @pl.when(take != 0)
                def _(off=off, size=size):
                    cp = pltpu.make_async_copy(
                        zero_ref.at[pl.ds(0, size)],
                        xs_ref.at[pl.ds(pl.multiple_of(start + off, RUN_ALIGN), size)], tail_sem)
                    if wait:
                        cp.wait()
                    else:
                        cp.start()

                off = off + take
            return carry

        lax.fori_loop(0, n_exp, per_expert, 0)

    @pl.when(i == 0)
    def _():
        zero_ref[...] = jnp.zeros_like(zero_ref)
        tails(False)
        tails(True)

    ld = ld_ref[...]
    h = h_ref[...]
    rows_used = (meta_ref[(i * 3 + 0) * n_exp + n_exp - 1] + meta_ref[(i * 3 + 1) * n_exp + n_exp - 1])
    for c in range(stage_rows // GATHER_RC):
        @pl.when(c * GATHER_RC < rows_used)
        def _(c=c):
            r = lax.broadcasted_iota(I32, (GATHER_RC, tm), 0) + c * GATHER_RC
            hit = jnp.where(r == ld[0:1], 1.0, jnp.where(r == ld[1:2], 1.0,
                            jnp.where(r == ld[2:3], 1.0, jnp.where(r == ld[3:4], 1.0, 0.0))))
            stage[c * GATHER_RC:(c + 1) * GATHER_RC, :] = jnp.dot(
                hit.astype(BF16), h, preferred_element_type=F32).astype(BF16)

    _run_slabs(meta_ref, i, n_exp, stage, xs_ref, sems.at[slot], to_global=True, wait=False)

    @pl.when(i > 0)
    def _():
        _run_slabs(meta_ref, i - 1, n_exp, stage_ref.at[1 - slot], xs_ref, sems.at[1 - slot],
                   to_global=True, wait=True)

    @pl.when(i == n_tiles - 1)
    def _():
        _run_slabs(meta_ref, i, n_exp, stage, xs_ref, sems.at[slot], to_global=True, wait=True)


def _dispatch(meta, tail, ldest, h2, n_rows, n_exp):
    t, d = h2.shape
    tm = TOK_TM
    n_tiles = t // tm
    stage_rows = _stage_rows(n_exp)
    kernel = functools.partial(_dispatch_kernel, tm=tm, n_exp=n_exp, n_tiles=n_tiles,
                               n_blocks=n_rows // EXP_BLK, stage_rows=stage_rows)
    grid_spec = pltpu.PrefetchScalarGridSpec(
        num_scalar_prefetch=2,
        grid=(n_tiles,),
        in_specs=[pl.BlockSpec((TOP_K, tm), lambda i, m, tl: (0, i)),
                  pl.BlockSpec((tm, d), lambda i, m, tl: (i, 0))],
        out_specs=pl.BlockSpec(memory_space=pl.ANY),
        scratch_shapes=[pltpu.VMEM((2, stage_rows, d), BF16),
                        pltpu.VMEM((EXP_BLK, d), BF16),
                        pltpu.SemaphoreType.DMA((2,)),
                        pltpu.SemaphoreType.DMA],
    )
    return pl.pallas_call(
        kernel,
        out_shape=jax.ShapeDtypeStruct((n_rows, d), BF16),
        grid_spec=grid_spec,
        compiler_params=_params("arbitrary"),
        name="dispatch",
    )(meta, tail, ldest, h2)


def _expert_kernel(blk_e_ref, n_used_ref, x_ref, w1_ref, b1_ref, w2_ref, b2_ref, o_ref, *, f):
    i = pl.program_id(0)

    @pl.when(i < n_used_ref[0])
    def _():
        gu = jnp.dot(x_ref[...], w1_ref[0], preferred_element_type=F32) + b1_ref[0]
        g = jnp.minimum(gu[:, :f], SWIGLU_LIMIT)
        u = jnp.clip(gu[:, f:], -SWIGLU_LIMIT, SWIGLU_LIMIT)
        a = g * _sigmoid(SWIGLU_ALPHA * g) * (u + 1.0)
        o_ref[...] = (jnp.dot(a.astype(BF16), w2_ref[0], preferred_element_type=F32)
                      + b2_ref[0]).astype(o_ref.dtype)

    @pl.when(i >= n_used_ref[0])
    def _():
        o_ref[...] = jnp.zeros_like(o_ref)


def _experts(blk_e, n_used, xs, w1_bf16, b1, w2_bf16, b2):
    n_rows, d = xs.shape
    n_exp, _, f2 = w1_bf16.shape
    f = f2 // 2
    blk = EXP_BLK
    kernel = functools.partial(_expert_kernel, f=f)
    grid_spec = pltpu.PrefetchScalarGridSpec(
        num_scalar_prefetch=2,
        grid=(n_rows // blk,),
        in_specs=[pl.BlockSpec((blk, d), lambda i, e, n: (jnp.minimum(i, n[0] - 1), 0)),
                  pl.BlockSpec((1, d, f2), lambda i, e, n: (e[i], 0, 0)),
                  pl.BlockSpec((1, 1, f2), lambda i, e, n: (e[i], 0, 0)),
                  pl.BlockSpec((1, f, d), lambda i, e, n: (e[i], 0, 0)),
                  pl.BlockSpec((1, 1, d), lambda i, e, n: (e[i], 0, 0))],
        out_specs=pl.BlockSpec((blk, d), lambda i, e, n: (i, 0)),
    )
    return pl.pallas_call(
        kernel,
        out_shape=jax.ShapeDtypeStruct((n_rows, d), BF16),
        grid_spec=grid_spec,
        compiler_params=_params("arbitrary"),
        name="experts",
    )(blk_e, n_used, xs, w1_bf16, b1, w2_bf16, b2)


def _combine_kernel(meta_ref, eo_ref, ld_ref, w_ref, x1_ref, mod_ref, o_ref, stage_ref, sel_ref, sems,
                    *, tm, n_exp, n_tiles, stage_rows):
    i = pl.program_id(0)
    slot = i & 1

    @pl.when(i == 0)
    def _():
        stage_ref[...] = jnp.zeros_like(stage_ref)
        _run_slabs(meta_ref, 0, n_exp, stage_ref.at[0], eo_ref, sems.at[0], to_global=False, wait=False)

    @pl.when(i + 1 < n_tiles)
    def _():
        _run_slabs(meta_ref, i + 1, n_exp, stage_ref.at[1 - slot], eo_ref, sems.at[1 - slot],
                   to_global=False, wait=False)

    ld = ld_ref[...]
    w = w_ref[...]
    ldb = [jnp.broadcast_to(ld[:, k:k + 1], (tm, SCATTER_RC)) for k in range(TOP_K)]
    wb = [jnp.broadcast_to(w[:, k:k + 1], (tm, SCATTER_RC)) for k in range(TOP_K)]
    col = lax.broadcasted_iota(I32, (tm, SCATTER_RC), 1)
    for c in range(stage_rows // SCATTER_RC):
        r = col + c * SCATTER_RC
        sel = jnp.where(ldb[0] == r, wb[0], jnp.where(ldb[1] == r, wb[1],
                        jnp.where(ldb[2] == r, wb[2], jnp.where(ldb[3] == r, wb[3], 0.0))))
        sel_ref[:, c * SCATTER_RC:(c + 1) * SCATTER_RC] = sel.astype(BF16)

    _run_slabs(meta_ref, i, n_exp, stage_ref.at[slot], eo_ref, sems.at[slot], to_global=False, wait=True)
    y = jnp.dot(sel_ref[...], stage_ref[slot], preferred_element_type=F32)
    o_ref[...] = x1_ref[...] + mod_ref[0][5:6] * y


def _combine(meta, eo, ld_tok, w_tok, x1, mod3, seq, n_exp):
    t, d = x1.shape
    tm = TOK_TM
    n_tiles = t // tm
    stage_rows = _stage_rows(n_exp)
    kernel = functools.partial(_combine_kernel, tm=tm, n_exp=n_exp, n_tiles=n_tiles, stage_rows=stage_rows)
    grid_spec = pltpu.PrefetchScalarGridSpec(
        num_scalar_prefetch=1,
        grid=(n_tiles,),
        in_specs=[pl.BlockSpec(memory_space=pl.ANY),
                  pl.BlockSpec((tm, TOP_K), lambda i, m: (i, 0)),
                  pl.BlockSpec((tm, TOP_K), lambda i, m: (i, 0)),
                  pl.BlockSpec((tm, d), lambda i, m: (i, 0)),
                  pl.BlockSpec((1, 6, d), lambda i, m: (i * tm // seq, 0, 0))],
        out_specs=pl.BlockSpec((tm, d), lambda i, m: (i, 0)),
        scratch_shapes=[pltpu.VMEM((2, stage_rows, d), BF16),
                        pltpu.VMEM((tm, stage_rows), BF16),
                        pltpu.SemaphoreType.DMA((2,))],
    )
    return pl.pallas_call(
        kernel,
        out_shape=jax.ShapeDtypeStruct((t, d), F32),
        grid_spec=grid_spec,
        compiler_params=_params("arbitrary"),
        name="combine",
    )(meta, eo, ld_tok, w_tok, x1, mod3)


def _routing_tables(tile_cnt, n_tokens):
    n_tiles, n_exp = tile_cnt.shape
    blk = EXP_BLK
    run = _round_up(tile_cnt, RUN_ALIGN)
    local = jnp.cumsum(run, axis=1) - run
    size = jnp.sum(run, axis=0)
    region = _round_up(size, blk)
    region_end = jnp.cumsum(region)
    region_start = region_end - region
    glob = region_start[None, :] + jnp.cumsum(run, axis=0) - run
    meta = jnp.stack([run, local, glob], axis=1).reshape(-1).astype(I32)
    tail = jnp.concatenate([region_start + size, region - size, region_end[-1:]]).astype(I32)
    n_rows = _round_up(n_tokens * TOP_K + n_tiles * n_exp * RUN_ALIGN + n_exp * blk, blk)
    block_row = jnp.arange(n_rows // blk, dtype=I32) * blk
    n_used = (region_end[-1:] // blk).astype(I32)
    blk_e = jnp.sum((region_end[None, :] <= block_row[:, None]).astype(I32), axis=1)
    last_e = jnp.sum((region_end[:-1] < region_end[-1]).astype(I32))
    blk_e = jnp.minimum(blk_e, last_e).astype(I32)
    return meta, tail, blk_e, n_used, n_rows


def kernel(x, c, w_ada, b_ada, g_norm1, w_in, g_ret_gn, b_ret_gn, g_qnorm, g_knorm,
           lambda_q1, lambda_k1, lambda_q2, lambda_k2, g_diff_subln, w_out, g_norm2,
           w_router, b_router, w_expert_in, b_expert_in, w_expert_out, b_expert_out):
    batch, seq, d = x.shape
    depth = w_ada.shape[0]
    assert depth == 1
    t = batch * seq
    n_exp = w_router.shape[-1]
    l = 0

    mod = _adaln_mod(c, w_ada[l], b_ada[l][None, :])
    mod3 = mod.reshape(batch, 6, d)
    x2 = x.reshape(t, d)

    proj = _in_proj(x2, mod3, g_norm1[l][None, :], w_in[l].astype(BF16), seq)
    ret = _retention(proj, g_ret_gn[l][None, :], b_ret_gn[l][None, :], batch, seq, d)
    att = _diff_attention(proj, g_qnorm[l][None, :], g_knorm[l][None, :], g_diff_subln[l][None, :],
                          lambda_q1[l][None, :], lambda_k1[l][None, :],
                          lambda_q2[l][None, :], lambda_k2[l][None, :], batch, seq, d)

    wr = w_router[l]
    wr_hi = wr.astype(BF16)
    wr_lo = (wr - wr_hi.astype(F32)).astype(BF16)
    pad = lambda a: jnp.pad(a, ((0, 0), (0, LANES - n_exp)))
    wr_split = jnp.concatenate([pad(wr_hi), pad(wr_lo)], axis=1)
    br_pad = jnp.pad(b_router[l][None, :], ((0, 0), (0, LANES - n_exp)), constant_values=MASK_VALUE)
    x1, h2, top_w, ldest, tile_cnt = _out_router(
        x2, ret, att, mod3, w_out[l].astype(BF16), g_norm2[l][None, :], wr_split, br_pad, seq, n_exp)

    meta, tail, blk_e, n_used, n_rows = _routing_tables(tile_cnt[:, :, 0].astype(I32), t)
    xs = _dispatch(meta, tail, ldest, h2, n_rows, n_exp)
    eo = _experts(blk_e, n_used, xs, w_expert_in[l].astype(BF16), b_expert_in[l][:, None, :],
                  w_expert_out[l].astype(BF16), b_expert_out[l][:, None, :])
    out = _combine(meta, eo, ldest.T, top_w.T, x1, mod3, seq, n_exp)
    return out.reshape(batch, seq, d)
```

```python
import functools
import math

import jax
import jax.numpy as jnp
from jax import lax
from jax.experimental import pallas as pl
from jax.experimental.pallas import tpu as pltpu

F32 = jnp.float32
BF16 = jnp.bfloat16
I32 = jnp.int32

EPS = 1e-6
LOG2E = 1.4426950216293335
MASK_VALUE = -1e30
LANES = 128
BF16_SUBLANES = 16
VMEM_LIMIT_BYTES = 48 * 1024 * 1024

RET_HEADS = 4
DIFF_HEADS = 8
TOP_K = 4
SWIGLU_LIMIT = 7.0
SWIGLU_ALPHA = 1.702
LAMBDA_INIT = 0.8 - 0.6 * math.exp(-0.3 * 0)

IN_TM, IN_TN = 1024, 2048
RET_TC = 256
ATT_T = 512
TOK_TM = 512
EXP_BLK = 256
RUN_ALIGN = BF16_SUBLANES
GATHER_RC = 256
SCATTER_RC = 128


def _round_up(n, m):
    return (n + m - 1) // m * m


def _stage_rows(n_exp):
    return _round_up(TOK_TM * TOP_K + n_exp * (RUN_ALIGN - 1), max(GATHER_RC, SCATTER_RC))


SLAB_SIZES = tuple(TOK_TM >> s for s in range(TOK_TM.bit_length()) if TOK_TM >> s >= RUN_ALIGN)
TAIL_SIZES = tuple(s for s in SLAB_SIZES if s < EXP_BLK)


def _params(*sem):
    return pltpu.CompilerParams(dimension_semantics=sem, vmem_limit_bytes=VMEM_LIMIT_BYTES)


def _sigmoid(x):
    return 1.0 / (1.0 + jnp.exp(-x))


def _rms(x, axis=-1):
    return x * lax.rsqrt(jnp.mean(x * x, axis=axis, keepdims=True) + EPS)


def _mod_kernel(c_ref, w_ref, b_ref, o_ref):
    c = c_ref[...]
    s = c * _sigmoid(c)
    o_ref[...] = jnp.dot(s, w_ref[...], preferred_element_type=F32,
                         precision=lax.Precision.HIGHEST) + b_ref[...]


def _adaln_mod(c, w, b):
    bn, d = c.shape
    n = w.shape[1]
    tn = d
    return pl.pallas_call(
        _mod_kernel,
        out_shape=jax.ShapeDtypeStruct((bn, n), F32),
        grid=(n // tn,),
        in_specs=[pl.BlockSpec((bn, d), lambda j: (0, 0)),
                  pl.BlockSpec((d, tn), lambda j: (0, j)),
                  pl.BlockSpec((1, tn), lambda j: (0, j))],
        out_specs=pl.BlockSpec((bn, tn), lambda j: (0, j)),
        compiler_params=_params("arbitrary"),
        name="adaln_mod",
    )(c, w, b)


def _inproj_kernel(x_ref, mod_ref, g_ref, w_ref, o_ref, h_ref):
    @pl.when(pl.program_id(1) == 0)
    def _():
        m = mod_ref[0]
        h = _rms(x_ref[...]) * g_ref[...] * (1.0 + m[1:2]) + m[0:1]
        h_ref[...] = h.astype(BF16)

    o_ref[...] = jnp.dot(h_ref[...], w_ref[...], preferred_element_type=F32).astype(o_ref.dtype)


def _in_proj(x2, mod3, g1, w_in_bf16, seq):
    t, d = x2.shape
    n = w_in_bf16.shape[1]
    tm, tn = IN_TM, IN_TN
    return pl.pallas_call(
        _inproj_kernel,
        out_shape=jax.ShapeDtypeStruct((t, n), BF16),
        grid=(t // tm, n // tn),
        in_specs=[pl.BlockSpec((tm, d), lambda i, j: (i, 0)),
                  pl.BlockSpec((1, 6, d), lambda i, j: (i * tm // seq, 0, 0)),
                  pl.BlockSpec((1, d), lambda i, j: (0, 0)),
                  pl.BlockSpec((d, tn), lambda i, j: (0, j))],
        out_specs=pl.BlockSpec((tm, tn), lambda i, j: (i, j)),
        scratch_shapes=[pltpu.VMEM((tm, d), BF16)],
        compiler_params=_params("arbitrary", "arbitrary"),
        name="in_proj",
    )(x2, mod3, g1, w_in_bf16)


def _ret_kernel(q_ref, k_ref, v_ref, rg_ref, mg_ref, gng_ref, gnb_ref, o_ref, r_ref, *, dk, dv, tc):
    @pl.when(pl.program_id(1) == 0)
    def _():
        r_ref[...] = jnp.zeros_like(r_ref)

    row = lax.broadcasted_iota(I32, (tc, tc), 0)
    col = lax.broadcasted_iota(I32, (tc, tc), 1)
    rel = (row - col).astype(F32)
    pos = lax.broadcasted_iota(I32, (tc, 1), 0).astype(F32)
    scale = dk ** -0.5
    for h in range(RET_HEADS):
        log_g = math.log(1.0 - 2.0 ** (-5.0 - h))
        decay = jnp.where(rel >= 0, jnp.exp(log_g * jnp.maximum(rel, 0.0)), 0.0) * scale
        xi = jnp.exp(log_g * (pos + 1.0))
        zeta = jnp.exp(log_g * (tc - 1.0 - pos))
        g_chunk = math.exp(log_g * tc)
        q = q_ref[:, h * dk:(h + 1) * dk]
        k = k_ref[:, h * dk:(h + 1) * dk]
        v = v_ref[:, h * dv:(h + 1) * dv]
        s = lax.dot_general(q, k, (((1,), (1,)), ((), ())), preferred_element_type=F32)
        y = jnp.dot((s * decay).astype(BF16), v, preferred_element_type=F32)
        state = r_ref[h]
        y = y + jnp.dot(q, state.astype(BF16), preferred_element_type=F32) * xi
        vz = (v.astype(F32) * zeta).astype(BF16)
        kv = lax.dot_general(k, vz, (((0,), (0,)), ((), ())), preferred_element_type=F32)
        r_ref[h] = state * g_chunk + kv * scale

        mu = jnp.mean(y, axis=-1, keepdims=True)
        yc = y - mu
        var = jnp.mean(yc * yc, axis=-1, keepdims=True)
        sl = slice(h * dv, (h + 1) * dv)
        yn = yc * lax.rsqrt(var + EPS) * gng_ref[:, sl] + gnb_ref[:, sl]
        rg = rg_ref[:, sl].astype(F32)
        gate = _sigmoid(mg_ref[:, sl].astype(F32))
        o_ref[:, sl] = (rg * _sigmoid(rg) * yn * gate).astype(o_ref.dtype)


def _retention(proj, gn_g, gn_b, batch, seq, d):
    t = proj.shape[0]
    tc = RET_TC
    nc = seq // tc
    dk = d // (2 * RET_HEADS)
    dv = d // RET_HEADS
    qk_w = RET_HEADS * dk
    row = lambda b, c: b * nc + c
    kernel = functools.partial(_ret_kernel, dk=dk, dv=dv, tc=tc)
    return pl.pallas_call(
        kernel,
        out_shape=jax.ShapeDtypeStruct((t, d), BF16),
        grid=(batch, nc),
        in_specs=[pl.BlockSpec((tc, qk_w), lambda b, c: (row(b, c), 0)),
                  pl.BlockSpec((tc, qk_w), lambda b, c: (row(b, c), 1)),
                  pl.BlockSpec((tc, d), lambda b, c: (row(b, c), 1)),
                  pl.BlockSpec((tc, d), lambda b, c: (row(b, c), 2)),
                  pl.BlockSpec((tc, d), lambda b, c: (row(b, c), 6)),
                  pl.BlockSpec((1, d), lambda b, c: (0, 0)),
                  pl.BlockSpec((1, d), lambda b, c: (0, 0))],
        out_specs=pl.BlockSpec((tc, d), lambda b, c: (row(b, c), 0)),
        scratch_shapes=[pltpu.VMEM((RET_HEADS, dk, dv), F32)],
        compiler_params=_params("arbitrary", "arbitrary"),
        name="retention",
    )(proj, proj, proj, proj, proj, gn_g, gn_b)


def _halves_rms(x, lo, dh):
    sq = x * x
    s_lo = jnp.sum(jnp.where(lo, sq, 0.0), axis=-1, keepdims=True)
    s_hi = jnp.sum(jnp.where(lo, 0.0, sq), axis=-1, keepdims=True)
    inv = jnp.where(lo, lax.rsqrt(s_lo / dh + EPS), lax.rsqrt(s_hi / dh + EPS))
    return x * inv


def _attn_kernel(q_ref, k_ref, v_ref, gate_ref, slope_ref, qca_ref, qcb_ref, gq_ref, gk_ref, gsub_ref,
                 lq1_ref, lk1_ref, lq2_ref, lk2_ref,
                 o_ref, ka_ref, kb_ref, vt_ref, qa_ref, qb_ref, sa_ref, sb_ref, acc_ref, m_ref,
                 *, tile, dh, dv, nk):
    lane = lax.broadcasted_iota(I32, (tile, 2 * dh), 1)
    lo = lane < dh

    slope = slope_ref[0]
    sub = lax.broadcasted_iota(I32, (tile, 2 * dh), 0)
    off = lane & (dh - 1)
    hi_lane = (off < 6) & ((off & 1) == 0)
    lo_lane = (off < 6) & ((off & 1) == 1)
    bias0 = jnp.where(hi_lane, ((sub >> 8) << 8).astype(F32),
                      jnp.where(lo_lane, (sub & 255).astype(F32), 0.0)) * slope
    bias_step = jnp.where(hi_lane, float(tile), 0.0) * slope
    ones_rows = jnp.where(lax.broadcasted_iota(I32, (BF16_SUBLANES, tile), 0) == 0, 1.0, 0.0)
    lo_t = lax.broadcasted_iota(I32, (2 * dh, tile), 0) < dh

    def prepare(c, carry):
        r0 = pl.multiple_of(c * tile, tile)
        kn = _halves_rms(k_ref[pl.ds(r0, tile), :].astype(F32), lo, dh) * gk_ref[...]
        bias = bias0 + lax.convert_element_type(c, F32) * bias_step
        ka_ref[c] = jnp.where(lo, kn, bias).astype(BF16)
        kb_ref[c] = jnp.where(lo, bias, kn).astype(BF16)
        vt_ref[c, :dv, :] = v_ref[pl.ds(r0, tile), :].astype(F32).T.astype(BF16)
        vt_ref[c, dv:, :] = ones_rows.astype(BF16)
        qn = (_halves_rms(q_ref[pl.ds(r0, tile), :].astype(F32), lo, dh) * gq_ref[...]
              * (dh ** -0.5 * LOG2E))
        qn_t = qn.T
        qa_ref[c] = jnp.where(lo_t, qn_t, qca_ref[...]).astype(BF16)
        qb_ref[c] = jnp.where(lo_t, qcb_ref[...], qn_t).astype(BF16)
        return carry

    lax.fori_loop(0, nk, prepare, 0)
    lam = (jnp.exp(jnp.sum(lq1_ref[...] * lk1_ref[...], axis=-1, keepdims=True))
           - jnp.exp(jnp.sum(lq2_ref[...] * lk2_ref[...], axis=-1, keepdims=True)) + LAMBDA_INIT)

    def q_tile(qi, carry):
        _attn_q_tile(qi, lam, gate_ref, gsub_ref, o_ref, ka_ref, kb_ref, vt_ref, qa_ref, qb_ref,
                     sa_ref, sb_ref, acc_ref, m_ref, tile=tile, dv=dv)
        return carry

    lax.fori_loop(0, nk, q_tile, 0)


def _attn_q_tile(qi, lam, gate_ref, gsub_ref, o_ref, ka_ref, kb_ref, vt_ref, qa_ref, qb_ref,
                 sa_ref, sb_ref, acc_ref, m_ref, *, tile, dv):
    def scores(jb, s_ref):
        s_ref[0] = jnp.dot(ka_ref[jb], qa_ref[qi], preferred_element_type=F32)
        s_ref[1] = jnp.dot(kb_ref[jb], qb_ref[qi], preferred_element_type=F32)

    def absorb(jb, s_ref, masked):
        vt = vt_ref[jb]
        for idx in range(2):
            s = s_ref[idx]
            if masked:
                krow = lax.broadcasted_iota(I32, (tile, tile), 0)
                qcol = lax.broadcasted_iota(I32, (tile, tile), 1)
                s = jnp.where(krow > qcol, MASK_VALUE, s)
            m_old = m_ref[idx]
            m_new = jnp.maximum(m_old, jnp.max(s, axis=0, keepdims=True))
            m_ref[idx] = m_new
            p = jnp.exp2(s - m_new).astype(BF16)
            acc_ref[idx] = (acc_ref[idx] * jnp.exp2(m_old - m_new)
                            + jnp.dot(vt, p, preferred_element_type=F32))

    acc_ref[...] = jnp.zeros_like(acc_ref)
    m_ref[...] = jnp.full_like(m_ref, MASK_VALUE)
    scores(0, sa_ref)

    def pair(j2, carry):
        jb = 2 * j2
        scores(jb + 1, sb_ref)
        absorb(jb, sa_ref, False)
        scores(jb + 2, sa_ref)
        absorb(jb + 1, sb_ref, False)
        return carry

    lax.fori_loop(0, qi // 2, pair, 0)

    @pl.when(qi % 2 == 1)
    def _():
        scores(qi, sb_ref)
        absorb(qi - 1, sa_ref, False)
        absorb(qi, sb_ref, True)

    @pl.when(qi % 2 == 0)
    def _():
        absorb(qi, sa_ref, True)

    a1, a2 = acc_ref[0], acc_ref[1]
    o_t = a1[:dv] * (1.0 / a1[dv:dv + 1]) - lam * (a2[:dv] * (1.0 / a2[dv:dv + 1]))
    o = _rms(o_t, axis=0).T * gsub_ref[...] * (1.0 - LAMBDA_INIT)
    rows = pl.ds(pl.multiple_of(qi * tile, tile), tile)
    o_ref[rows, :] = (o * _sigmoid(gate_ref[rows, :].astype(F32))).astype(o_ref.dtype)


def _bf16_terms(x, n):
    terms, rest = [], jnp.asarray(x, F32)
    for _ in range(n):
        term = rest.astype(BF16).astype(F32)
        terms.append(term)
        rest = rest - term
    return terms


def _diff_attention(proj, g_q, g_k, g_sub, lq1, lk1, lq2, lk2, batch, seq, d):
    t = proj.shape[0]
    tile = ATT_T
    nq = seq // tile
    dh = d // (2 * DIFF_HEADS)
    w = 2 * dh
    assert w == LANES
    q_col, k_col, v_col, gate_col = 3 * d // w, 4 * d // w, 5 * d // w, 7 * d // w
    slopes = 2.0 ** -(jnp.arange(DIFF_HEADS, dtype=F32) + 1.0)
    slopes = jnp.broadcast_to(slopes[:, None, None], (DIFF_HEADS, 1, LANES))
    c1, c2, c3 = _bf16_terms(LOG2E, 3)
    qconst = jnp.zeros((dh,), F32).at[:6].set(jnp.stack([c1, c1, c2, c2, c3, c3]))
    zeros = jnp.zeros((dh,), F32)
    qconst_a = jnp.concatenate([zeros, qconst])[:, None]
    qconst_b = jnp.concatenate([qconst, zeros])[:, None]
    tile2 = lambda a: jnp.concatenate([a, a], axis=-1)
    assert tile % 256 == 0
    small = lambda n: pl.BlockSpec((1, n), lambda b, h: (0, 0))
    kernel = functools.partial(_attn_kernel, tile=tile, dh=dh, dv=w, nk=nq)
    return pl.pallas_call(
        kernel,
        out_shape=jax.ShapeDtypeStruct((t, d), BF16),
        grid=(batch, DIFF_HEADS),
        in_specs=[pl.BlockSpec((seq, w), lambda b, h: (b, q_col + h)),
                  pl.BlockSpec((seq, w), lambda b, h: (b, k_col + h)),
                  pl.BlockSpec((seq, w), lambda b, h: (b, v_col + h)),
                  pl.BlockSpec((seq, w), lambda b, h: (b, gate_col + h)),
                  pl.BlockSpec((1, 1, LANES), lambda b, h: (h, 0, 0)),
                  pl.BlockSpec((w, 1), lambda b, h: (0, 0)), pl.BlockSpec((w, 1), lambda b, h: (0, 0)),
                  small(w), small(w), small(w),
                  small(dh), small(dh), small(dh), small(dh)],
        out_specs=pl.BlockSpec((seq, w), lambda b, h: (b, h)),
        scratch_shapes=[pltpu.VMEM((nq, tile, w), BF16),
                        pltpu.VMEM((nq, tile, w), BF16),
                        pltpu.VMEM((nq, w + BF16_SUBLANES, tile), BF16),
                        pltpu.VMEM((nq, w, tile), BF16),
                        pltpu.VMEM((nq, w, tile), BF16),
                        pltpu.VMEM((2, tile, tile), F32),
                        pltpu.VMEM((2, tile, tile), F32),
                        pltpu.VMEM((2, w + BF16_SUBLANES, tile), F32),
                        pltpu.VMEM((2, 1, tile), F32)],
        compiler_params=_params("arbitrary", "arbitrary"),
        name="diff_attn",
    )(proj, proj, proj, proj, slopes, qconst_a, qconst_b, tile2(g_q), tile2(g_k), g_sub,
      lq1, lk1, lq2, lk2)


def _out_kernel(x_ref, ret_ref, att_ref, mod_ref, wo_ref, g2_ref, wr_ref, br_ref,
                x1_ref, h2_ref, tw_ref, ld_ref, cnt_ref, *, tm, n_exp):
    m = mod_ref[0]
    merged = (ret_ref[...].astype(F32) + att_ref[...].astype(F32)).astype(BF16)
    x1 = x_ref[...] + m[2:3] * jnp.dot(merged, wo_ref[...], preferred_element_type=F32)
    x1_ref[...] = x1
    h2 = _rms(x1) * g2_ref[...] * (1.0 + m[4:5]) + m[3:4]
    hi = h2.astype(BF16)
    h2_ref[...] = hi

    lo = (h2 - hi.astype(F32)).astype(BF16)
    both = jnp.dot(hi, wr_ref[...], preferred_element_type=F32)
    logits = (both[:, :LANES] + both[:, LANES:]
              + jnp.dot(lo, wr_ref[:, :LANES], preferred_element_type=F32) + br_ref[...])
    lt = logits.T[:n_exp]

    erow = lax.broadcasted_iota(I32, (n_exp, tm), 0)
    vals, sels = [], []
    for k in range(TOP_K):
        mx = jnp.max(lt, axis=0, keepdims=True)
        idx = jnp.min(jnp.where(lt == mx, erow, n_exp), axis=0, keepdims=True)
        sel = erow == idx
        vals.append(mx)
        sels.append(sel)
        lt = jnp.where(sel, MASK_VALUE, lt)
    exps = [jnp.exp(v - vals[0]) for v in vals]
    inv = 1.0 / (exps[0] + exps[1] + exps[2] + exps[3])
    for k in range(TOP_K):
        tw_ref[k:k + 1, :] = exps[k] * inv

    chosen = jnp.where(sels[0] | sels[1] | sels[2] | sels[3], 1.0, 0.0)
    before = (lax.broadcasted_iota(I32, (tm, tm), 0) < lax.broadcasted_iota(I32, (tm, tm), 1))
    prefix = jnp.dot(chosen.astype(BF16), jnp.where(before, 1.0, 0.0).astype(BF16),
                     preferred_element_type=F32)
    count = jnp.sum(chosen, axis=1, keepdims=True)
    cnt_ref[...] = count
    padded = jnp.ceil(count * (1.0 / RUN_ALIGN)) * RUN_ALIGN
    below = (lax.broadcasted_iota(I32, (n_exp, n_exp), 1) < lax.broadcasted_iota(I32, (n_exp, n_exp), 0))
    run_start = jnp.dot(jnp.where(below, 1.0, 0.0).astype(BF16),
                        jnp.broadcast_to(padded, (n_exp, LANES)).astype(BF16),
                        preferred_element_type=F32)[:, :1]
    pos = prefix + run_start
    for k in range(TOP_K):
        ld_ref[k:k + 1, :] = jnp.sum(jnp.where(sels[k], pos, 0.0), axis=0, keepdims=True).astype(I32)


def _out_router(x2, ret, att, mod3, w_out_bf16, g2, wr_split, br_pad, seq, n_exp):
    t, d = x2.shape
    tm = TOK_TM
    row = lambda i: (i, 0)
    const = lambda i: (0, 0)
    kernel = functools.partial(_out_kernel, tm=tm, n_exp=n_exp)
    return pl.pallas_call(
        kernel,
        out_shape=(jax.ShapeDtypeStruct((t, d), F32),
                   jax.ShapeDtypeStruct((t, d), BF16),
                   jax.ShapeDtypeStruct((TOP_K, t), F32),
                   jax.ShapeDtypeStruct((TOP_K, t), I32),
                   jax.ShapeDtypeStruct((t // tm, n_exp, 1), F32)),
        grid=(t // tm,),
        in_specs=[pl.BlockSpec((tm, d), row), pl.BlockSpec((tm, d), row), pl.BlockSpec((tm, d), row),
                  pl.BlockSpec((1, 6, d), lambda i: (i * tm // seq, 0, 0)),
                  pl.BlockSpec((d, d), const), pl.BlockSpec((1, d), const),
                  pl.BlockSpec((d, 2 * LANES), const), pl.BlockSpec((1, LANES), const)],
        out_specs=(pl.BlockSpec((tm, d), row), pl.BlockSpec((tm, d), row),
                   pl.BlockSpec((TOP_K, tm), lambda i: (0, i)),
                   pl.BlockSpec((TOP_K, tm), lambda i: (0, i)),
                   pl.BlockSpec((None, n_exp, 1), lambda i: (i, 0, 0))),
        compiler_params=_params("arbitrary"),
        name="out_router",
    )(x2, ret, att, mod3, w_out_bf16, g2, wr_split, br_pad)


def _run_slabs(meta_ref, tile, n_exp, local_ref, global_ref, sem, *, to_global, wait):
    def per_expert(e, carry):
        n = meta_ref[(tile * 3 + 0) * n_exp + e]
        local = meta_ref[(tile * 3 + 1) * n_exp + e]
        glob = meta_ref[(tile * 3 + 2) * n_exp + e]
        off = 0
        for size in SLAB_SIZES:
            take = n & size

            @pl.when(take != 0)
            def _(off=off, size=size):
                loc = local_ref.at[pl.ds(pl.multiple_of(local + off, RUN_ALIGN), size)]
                glo = global_ref.at[pl.ds(pl.multiple_of(glob + off, RUN_ALIGN), size)]
                cp = pltpu.make_async_copy(loc, glo, sem) if to_global else pltpu.make_async_copy(glo, loc, sem)
                if wait:
                    cp.wait()
                else:
                    cp.start()

            off = off + take
        return carry

    lax.fori_loop(0, n_exp, per_expert, 0)


def _dispatch_kernel(meta_ref, tail_ref, ld_ref, h_ref, xs_ref, stage_ref, zero_ref, sems, tail_sem,
                     *, tm, n_exp, n_tiles, n_blocks, stage_rows):
    i = pl.program_id(0)
    slot = i & 1
    stage = stage_ref.at[slot]

    def tails(wait):
        def unused_block(b, carry):
            cp = pltpu.make_async_copy(
                zero_ref, xs_ref.at[pl.ds(pl.multiple_of(b * EXP_BLK, EXP_BLK), EXP_BLK)], tail_sem)
            if wait:
                cp.wait()
            else:
                cp.start()
            return carry

        lax.fori_loop(tail_ref[2 * n_exp] // EXP_BLK, n_blocks, unused_block, 0)

        def per_expert(e, carry):
            start, n = tail_ref[e], tail_ref[n_exp + e]
            off = 0
            for size in TAIL_SIZES:
                take = n & size

                @pl.when(take != 0)
                def _(off=off, size=size):
                    cp = pltpu.make_async_copy(
                        zero_ref.at[pl.ds(0, size)],
                        xs_ref.at[pl.ds(pl.multiple_of(start + off, RUN_ALIGN), size)], tail_sem)
                    if wait:
                        cp.wait()
                    else:
                        cp.start()

                off = off + take
            return carry

        lax.fori_loop(0, n_exp, per_expert, 0)

    @pl.when(i == 0)
    def _():
        zero_ref[...] = jnp.zeros_like(zero_ref)
        tails(False)
        tails(True)

    ld = ld_ref[...]
    h = h_ref[...]
    rows_used = (meta_ref[(i * 3 + 0) * n_exp + n_exp - 1] + meta_ref[(i * 3 + 1) * n_exp + n_exp - 1])
    for c in range(stage_rows // GATHER_RC):
        @pl.when(c * GATHER_RC < rows_used)
        def _(c=c):
            r = lax.broadcasted_iota(I32, (GATHER_RC, tm), 0) + c * GATHER_RC
            hit = jnp.where(r == ld[0:1], 1.0, jnp.where(r == ld[1:2], 1.0,
                            jnp.where(r == ld[2:3], 1.0, jnp.where(r == ld[3:4], 1.0, 0.0))))
            stage[c * GATHER_RC:(c + 1) * GATHER_RC, :] = jnp.dot(
                hit.astype(BF16), h, preferred_element_type=F32).astype(BF16)

    _run_slabs(meta_ref, i, n_exp, stage, xs_ref, sems.at[slot], to_global=True, wait=False)

    @pl.when(i > 0)
    def _():
        _run_slabs(meta_ref, i - 1, n_exp, stage_ref.at[1 - slot], xs_ref, sems.at[1 - slot],
                   to_global=True, wait=True)

    @pl.when(i == n_tiles - 1)
    def _():
        _run_slabs(meta_ref, i, n_exp, stage, xs_ref, sems.at[slot], to_global=True, wait=True)


def _dispatch(meta, tail, ldest, h2, n_rows, n_exp):
    t, d = h2.shape
    tm = TOK_TM
    n_tiles = t // tm
    stage_rows = _stage_rows(n_exp)
    kernel = functools.partial(_dispatch_kernel, tm=tm, n_exp=n_exp, n_tiles=n_tiles,
                               n_blocks=n_rows // EXP_BLK, stage_rows=stage_rows)
    grid_spec = pltpu.PrefetchScalarGridSpec(
        num_scalar_prefetch=2,
        grid=(n_tiles,),
        in_specs=[pl.BlockSpec((TOP_K, tm), lambda i, m, tl: (0, i)),
                  pl.BlockSpec((tm, d), lambda i, m, tl: (i, 0))],
        out_specs=pl.BlockSpec(memory_space=pl.ANY),
        scratch_shapes=[pltpu.VMEM((2, stage_rows, d), BF16),
                        pltpu.VMEM((EXP_BLK, d), BF16),
                        pltpu.SemaphoreType.DMA((2,)),
                        pltpu.SemaphoreType.DMA],
    )
    return pl.pallas_call(
        kernel,
        out_shape=jax.ShapeDtypeStruct((n_rows, d), BF16),
        grid_spec=grid_spec,
        compiler_params=_params("arbitrary"),
        name="dispatch",
    )(meta, tail, ldest, h2)


def _expert_kernel(blk_e_ref, n_used_ref, x_ref, w1_ref, b1_ref, w2_ref, b2_ref, o_ref, w1b_ref, w2b_ref,
                   *, f):
    i = pl.program_id(0)
    new_expert = jnp.logical_or(i == 0, blk_e_ref[i] != blk_e_ref[jnp.maximum(i - 1, 0)])

    @pl.when(jnp.logical_and(i < n_used_ref[0], new_expert))
    def _():
        w1b_ref[...] = w1_ref[0].astype(BF16)
        w2b_ref[...] = w2_ref[0].astype(BF16)

    @pl.when(i < n_used_ref[0])
    def _():
        gu = jnp.dot(x_ref[...], w1b_ref[...], preferred_element_type=F32) + b1_ref[0]
        g = jnp.minimum(gu[:, :f], SWIGLU_LIMIT)
        u = jnp.clip(gu[:, f:], -SWIGLU_LIMIT, SWIGLU_LIMIT)
        a = g * _sigmoid(SWIGLU_ALPHA * g) * (u + 1.0)
        o_ref[...] = (jnp.dot(a.astype(BF16), w2b_ref[...], preferred_element_type=F32)
                      + b2_ref[0]).astype(o_ref.dtype)

    @pl.when(i >= n_used_ref[0])
    def _():
        o_ref[...] = jnp.zeros_like(o_ref)


def _experts(blk_e, n_used, xs, w1, b1, w2, b2):
    n_rows, d = xs.shape
    n_exp, _, f2 = w1.shape
    f = f2 // 2
    blk = EXP_BLK
    kernel = functools.partial(_expert_kernel, f=f)
    grid_spec = pltpu.PrefetchScalarGridSpec(
        num_scalar_prefetch=2,
        grid=(n_rows // blk,),
        in_specs=[pl.BlockSpec((blk, d), lambda i, e, n: (jnp.minimum(i, n[0] - 1), 0)),
                  pl.BlockSpec((1, d, f2), lambda i, e, n: (e[i], 0, 0)),
                  pl.BlockSpec((1, 1, f2), lambda i, e, n: (e[i], 0, 0)),
                  pl.BlockSpec((1, f, d), lambda i, e, n: (e[i], 0, 0)),
                  pl.BlockSpec((1, 1, d), lambda i, e, n: (e[i], 0, 0))],
        out_specs=pl.BlockSpec((blk, d), lambda i, e, n: (i, 0)),
        scratch_shapes=[pltpu.VMEM((d, f2), BF16), pltpu.VMEM((f, d), BF16)],
    )
    return pl.pallas_call(
        kernel,
        out_shape=jax.ShapeDtypeStruct((n_rows, d), BF16),
        grid_spec=grid_spec,
        compiler_params=_params("arbitrary"),
        name="experts",
    )(blk_e, n_used, xs, w1, b1, w2, b2)


def _combine_kernel(meta_ref, eo_ref, ld_ref, w_ref, x1_ref, mod_ref, o_ref, stage_ref, sel_ref, sems,
                    *, tm, n_exp, n_tiles, stage_rows):
    i = pl.program_id(0)
    slot = i & 1

    @pl.when(i == 0)
    def _():
        stage_ref[...] = jnp.zeros_like(stage_ref)
        _run_slabs(meta_ref, 0, n_exp, stage_ref.at[0], eo_ref, sems.at[0], to_global=False, wait=False)

    @pl.when(i + 1 < n_tiles)
    def _():
        _run_slabs(meta_ref, i + 1, n_exp, stage_ref.at[1 - slot], eo_ref, sems.at[1 - slot],
                   to_global=False, wait=False)

    ld = ld_ref[...]
    w = w_ref[...]
    ldb = [jnp.broadcast_to(ld[:, k:k + 1], (tm, SCATTER_RC)) for k in range(TOP_K)]
    wb = [jnp.broadcast_to(w[:, k:k + 1], (tm, SCATTER_RC)) for k in range(TOP_K)]
    col = lax.broadcasted_iota(I32, (tm, SCATTER_RC), 1)
    for c in range(stage_rows // SCATTER_RC):
        r = col + c * SCATTER_RC
        sel = jnp.where(ldb[0] == r, wb[0], jnp.where(ldb[1] == r, wb[1],
                        jnp.where(ldb[2] == r, wb[2], jnp.where(ldb[3] == r, wb[3], 0.0))))
        sel_ref[:, c * SCATTER_RC:(c + 1) * SCATTER_RC] = sel.astype(BF16)

    _run_slabs(meta_ref, i, n_exp, stage_ref.at[slot], eo_ref, sems.at[slot], to_global=False, wait=True)
    y = jnp.dot(sel_ref[...], stage_ref[slot], preferred_element_type=F32)
    o_ref[...] = x1_ref[...] + mod_ref[0][5:6] * y


def _combine(meta, eo, ld_tok, w_tok, x1, mod3, seq, n_exp):
    t, d = x1.shape
    tm = TOK_TM
    n_tiles = t // tm
    stage_rows = _stage_rows(n_exp)
    kernel = functools.partial(_combine_kernel, tm=tm, n_exp=n_exp, n_tiles=n_tiles, stage_rows=stage_rows)
    grid_spec = pltpu.PrefetchScalarGridSpec(
        num_scalar_prefetch=1,
        grid=(n_tiles,),
        in_specs=[pl.BlockSpec(memory_space=pl.ANY),
                  pl.BlockSpec((tm, TOP_K), lambda i, m: (i, 0)),
                  pl.BlockSpec((tm, TOP_K), lambda i, m: (i, 0)),
                  pl.BlockSpec((tm, d), lambda i, m: (i, 0)),
                  pl.BlockSpec((1, 6, d), lambda i, m: (i * tm // seq, 0, 0))],
        out_specs=pl.BlockSpec((tm, d), lambda i, m: (i, 0)),
        scratch_shapes=[pltpu.VMEM((2, stage_rows, d), BF16),
                        pltpu.VMEM((tm, stage_rows), BF16),
                        pltpu.SemaphoreType.DMA((2,))],
    )
    return pl.pallas_call(
        kernel,
        out_shape=jax.ShapeDtypeStruct((t, d), F32),
        grid_spec=grid_spec,
        compiler_params=_params("arbitrary"),
        name="combine",
    )(meta, eo, ld_tok, w_tok, x1, mod3)


def _routing_tables(tile_cnt, n_tokens):
    n_tiles, n_exp = tile_cnt.shape
    blk = EXP_BLK
    run = _round_up(tile_cnt, RUN_ALIGN)
    local = jnp.cumsum(run, axis=1) - run
    size = jnp.sum(run, axis=0)
    region = _round_up(size, blk)
    region_end = jnp.cumsum(region)
    region_start = region_end - region
    glob = region_start[None, :] + jnp.cumsum(run, axis=0) - run
    meta = jnp.stack([run, local, glob], axis=1).reshape(-1).astype(I32)
    tail = jnp.concatenate([region_start + size, region - size, region_end[-1:]]).astype(I32)
    n_rows = _round_up(n_tokens * TOP_K + n_tiles * n_exp * RUN_ALIGN + n_exp * blk, blk)
    block_row = jnp.arange(n_rows // blk, dtype=I32) * blk
    n_used = (region_end[-1:] // blk).astype(I32)
    blk_e = jnp.sum((region_end[None, :] <= block_row[:, None]).astype(I32), axis=1)
    last_e = jnp.sum((region_end[:-1] < region_end[-1]).astype(I32))
    blk_e = jnp.minimum(blk_e, last_e).astype(I32)
    return meta, tail, blk_e, n_used, n_rows


def kernel(x, c, w_ada, b_ada, g_norm1, w_in, g_ret_gn, b_ret_gn, g_qnorm, g_knorm,
           lambda_q1, lambda_k1, lambda_q2, lambda_k2, g_diff_subln, w_out, g_norm2,
           w_router, b_router, w_expert_in, b_expert_in, w_expert_out, b_expert_out):
    batch, seq, d = x.shape
    depth = w_ada.shape[0]
    assert depth == 1
    t = batch * seq
    n_exp = w_router.shape[-1]
    l = 0

    mod = _adaln_mod(c, w_ada[l], b_ada[l][None, :])
    mod3 = mod.reshape(batch, 6, d)
    x2 = x.reshape(t, d)

    proj = _in_proj(x2, mod3, g_norm1[l][None, :], w_in[l].astype(BF16), seq)
    ret = _retention(proj, g_ret_gn[l][None, :], b_ret_gn[l][None, :], batch, seq, d)
    att = _diff_attention(proj, g_qnorm[l][None, :], g_knorm[l][None, :], g_diff_subln[l][None, :],
                          lambda_q1[l][None, :], lambda_k1[l][None, :],
                          lambda_q2[l][None, :], lambda_k2[l][None, :], batch, seq, d)

    wr = w_router[l]
    wr_hi = wr.astype(BF16)
    wr_lo = (wr - wr_hi.astype(F32)).astype(BF16)
    pad = lambda a: jnp.pad(a, ((0, 0), (0, LANES - n_exp)))
    wr_split = jnp.concatenate([pad(wr_hi), pad(wr_lo)], axis=1)
    br_pad = jnp.pad(b_router[l][None, :], ((0, 0), (0, LANES - n_exp)), constant_values=MASK_VALUE)
    x1, h2, top_w, ldest, tile_cnt = _out_router(
        x2, ret, att, mod3, w_out[l].astype(BF16), g_norm2[l][None, :], wr_split, br_pad, seq, n_exp)

    meta, tail, blk_e, n_used, n_rows = _routing_tables(tile_cnt[:, :, 0].astype(I32), t)
    xs = _dispatch(meta, tail, ldest, h2, n_rows, n_exp)
    eo = _experts(blk_e, n_used, xs, w_expert_in[l], b_expert_in[l][:, None, :],
                  w_expert_out[l], b_expert_out[l][:, None, :])
    out = _combine(meta, eo, ldest.T, top_w.T, x1, mod3, seq, n_exp)
    return out.reshape(batch, seq, d)
```

```python
import functools
import math

import jax
import jax.numpy as jnp
from jax import lax
from jax.experimental import pallas as pl
from jax.experimental.pallas import tpu as pltpu

F32 = jnp.float32
BF16 = jnp.bfloat16
I32 = jnp.int32

EPS = 1e-6
LOG2E = 1.4426950216293335
MASK_VALUE = -1e30
LANES = 128
BF16_SUBLANES = 16
VMEM_LIMIT_BYTES = 48 * 1024 * 1024

RET_HEADS = 4
DIFF_HEADS = 8
TOP_K = 4
SWIGLU_LIMIT = 7.0
SWIGLU_ALPHA = 1.702
LAMBDA_INIT = 0.8 - 0.6 * math.exp(-0.3 * 0)

IN_TM, IN_TN = 1024, 2048
RET_TC = 256
ATT_T = 512
TOK_TM = 512
EXP_BLK = 256
RUN_ALIGN = BF16_SUBLANES
GATHER_RC = 256
SCATTER_RC = 128


def _round_up(n, m):
    return (n + m - 1) // m * m


def _stage_rows(n_exp):
    return _round_up(TOK_TM * TOP_K + n_exp * (RUN_ALIGN - 1), max(GATHER_RC, SCATTER_RC))


SLAB_SIZES = tuple(TOK_TM >> s for s in range(TOK_TM.bit_length()) if TOK_TM >> s >= RUN_ALIGN)
TAIL_SIZES = tuple(s for s in SLAB_SIZES if s < EXP_BLK)


def _params(*sem):
    return pltpu.CompilerParams(dimension_semantics=sem, vmem_limit_bytes=VMEM_LIMIT_BYTES)


def _sigmoid(x):
    return 1.0 / (1.0 + jnp.exp(-x))


def _rms(x, axis=-1):
    return x * lax.rsqrt(jnp.mean(x * x, axis=axis, keepdims=True) + EPS)


def _mod_kernel(c_ref, w_ref, b_ref, o_ref):
    c = c_ref[...]
    s = c * _sigmoid(c)
    o_ref[...] = jnp.dot(s, w_ref[...], preferred_element_type=F32,
                         precision=lax.Precision.HIGHEST) + b_ref[...]


def _adaln_mod(c, w, b):
    bn, d = c.shape
    n = w.shape[1]
    tn = d
    return pl.pallas_call(
        _mod_kernel,
        out_shape=jax.ShapeDtypeStruct((bn, n), F32),
        grid=(n // tn,),
        in_specs=[pl.BlockSpec((bn, d), lambda j: (0, 0)),
                  pl.BlockSpec((d, tn), lambda j: (0, j)),
                  pl.BlockSpec((1, tn), lambda j: (0, j))],
        out_specs=pl.BlockSpec((bn, tn), lambda j: (0, j)),
        compiler_params=_params("arbitrary"),
        name="adaln_mod",
    )(c, w, b)


def _inproj_kernel(x_ref, mod_ref, g_ref, w_ref, o_ref, h_ref):
    @pl.when(pl.program_id(1) == 0)
    def _():
        m = mod_ref[0]
        h = _rms(x_ref[...]) * g_ref[...] * (1.0 + m[1:2]) + m[0:1]
        h_ref[...] = h.astype(BF16)

    o_ref[...] = jnp.dot(h_ref[...], w_ref[...], preferred_element_type=F32).astype(o_ref.dtype)


def _in_proj(x2, mod3, g1, w_in_bf16, seq):
    t, d = x2.shape
    n = w_in_bf16.shape[1]
    tm, tn = IN_TM, IN_TN
    return pl.pallas_call(
        _inproj_kernel,
        out_shape=jax.ShapeDtypeStruct((t, n), BF16),
        grid=(t // tm, n // tn),
        in_specs=[pl.BlockSpec((tm, d), lambda i, j: (i, 0)),
                  pl.BlockSpec((1, 6, d), lambda i, j: (i * tm // seq, 0, 0)),
                  pl.BlockSpec((1, d), lambda i, j: (0, 0)),
                  pl.BlockSpec((d, tn), lambda i, j: (0, j))],
        out_specs=pl.BlockSpec((tm, tn), lambda i, j: (i, j)),
        scratch_shapes=[pltpu.VMEM((tm, d), BF16)],
        compiler_params=_params("arbitrary", "arbitrary"),
        name="in_proj",
    )(x2, mod3, g1, w_in_bf16)


def _ret_kernel(q_ref, k_ref, v_ref, rg_ref, mg_ref, gng_ref, gnb_ref, o_ref, r_ref, *, dk, dv, tc):
    @pl.when(pl.program_id(1) == 0)
    def _():
        r_ref[...] = jnp.zeros_like(r_ref)

    row = lax.broadcasted_iota(I32, (tc, tc), 0)
    col = lax.broadcasted_iota(I32, (tc, tc), 1)
    rel = (row - col).astype(F32)
    pos = lax.broadcasted_iota(I32, (tc, 1), 0).astype(F32)
    scale = dk ** -0.5
    for h in range(RET_HEADS):
        log_g = math.log(1.0 - 2.0 ** (-5.0 - h))
        decay = jnp.where(rel >= 0, jnp.exp(log_g * jnp.maximum(rel, 0.0)), 0.0) * scale
        xi = jnp.exp(log_g * (pos + 1.0))
        zeta = jnp.exp(log_g * (tc - 1.0 - pos))
        g_chunk = math.exp(log_g * tc)
        q = q_ref[:, h * dk:(h + 1) * dk]
        k = k_ref[:, h * dk:(h + 1) * dk]
        v = v_ref[:, h * dv:(h + 1) * dv]
        s = lax.dot_general(q, k, (((1,), (1,)), ((), ())), preferred_element_type=F32)
        y = jnp.dot((s * decay).astype(BF16), v, preferred_element_type=F32)
        state = r_ref[h]
        y = y + jnp.dot(q, state.astype(BF16), preferred_element_type=F32) * xi
        vz = (v.astype(F32) * zeta).astype(BF16)
        kv = lax.dot_general(k, vz, (((0,), (0,)), ((), ())), preferred_element_type=F32)
        r_ref[h] = state * g_chunk + kv * scale

        mu = jnp.mean(y, axis=-1, keepdims=True)
        yc = y - mu
        var = jnp.mean(yc * yc, axis=-1, keepdims=True)
        sl = slice(h * dv, (h + 1) * dv)
        yn = yc * lax.rsqrt(var + EPS) * gng_ref[:, sl] + gnb_ref[:, sl]
        rg = rg_ref[:, sl].astype(F32)
        gate = _sigmoid(mg_ref[:, sl].astype(F32))
        o_ref[:, sl] = (rg * _sigmoid(rg) * yn * gate).astype(o_ref.dtype)


def _retention(proj, gn_g, gn_b, batch, seq, d):
    t = proj.shape[0]
    tc = RET_TC
    nc = seq // tc
    dk = d // (2 * RET_HEADS)
    dv = d // RET_HEADS
    qk_w = RET_HEADS * dk
    row = lambda b, c: b * nc + c
    kernel = functools.partial(_ret_kernel, dk=dk, dv=dv, tc=tc)
    return pl.pallas_call(
        kernel,
        out_shape=jax.ShapeDtypeStruct((t, d), BF16),
        grid=(batch, nc),
        in_specs=[pl.BlockSpec((tc, qk_w), lambda b, c: (row(b, c), 0)),
                  pl.BlockSpec((tc, qk_w), lambda b, c: (row(b, c), 1)),
                  pl.BlockSpec((tc, d), lambda b, c: (row(b, c), 1)),
                  pl.BlockSpec((tc, d), lambda b, c: (row(b, c), 2)),
                  pl.BlockSpec((tc, d), lambda b, c: (row(b, c), 6)),
                  pl.BlockSpec((1, d), lambda b, c: (0, 0)),
                  pl.BlockSpec((1, d), lambda b, c: (0, 0))],
        out_specs=pl.BlockSpec((tc, d), lambda b, c: (row(b, c), 0)),
        scratch_shapes=[pltpu.VMEM((RET_HEADS, dk, dv), F32)],
        compiler_params=_params("arbitrary", "arbitrary"),
        name="retention",
    )(proj, proj, proj, proj, proj, gn_g, gn_b)


def _halves_rms(x, lo, dh):
    sq = x * x
    s_lo = jnp.sum(jnp.where(lo, sq, 0.0), axis=-1, keepdims=True)
    s_hi = jnp.sum(jnp.where(lo, 0.0, sq), axis=-1, keepdims=True)
    inv = jnp.where(lo, lax.rsqrt(s_lo / dh + EPS), lax.rsqrt(s_hi / dh + EPS))
    return x * inv


def _attn_kernel(q_ref, k_ref, v_ref, gate_ref, slope_ref, qca_ref, qcb_ref, gq_ref, gk_ref, gsub_ref,
                 lq1_ref, lk1_ref, lq2_ref, lk2_ref,
                 o_ref, ka_ref, kb_ref, vt_ref, qa_ref, qb_ref, s0_ref, sa_ref, sb_ref, acc_ref, m_ref,
                 *, tile, dh, dv, nk):
    lane = lax.broadcasted_iota(I32, (tile, 2 * dh), 1)
    lo = lane < dh

    slope = slope_ref[0]
    sub = lax.broadcasted_iota(I32, (tile, 2 * dh), 0)
    off = lane & (dh - 1)
    hi_lane = (off < 6) & ((off & 1) == 0)
    lo_lane = (off < 6) & ((off & 1) == 1)
    bias0 = jnp.where(hi_lane, ((sub >> 8) << 8).astype(F32),
                      jnp.where(lo_lane, (sub & 255).astype(F32), 0.0)) * slope
    bias_step = jnp.where(hi_lane, float(tile), 0.0) * slope
    ones_rows = jnp.where(lax.broadcasted_iota(I32, (BF16_SUBLANES, tile), 0) == 0, 1.0, 0.0)
    lo_t = lax.broadcasted_iota(I32, (2 * dh, tile), 0) < dh

    def prepare(c, carry):
        r0 = c * tile if isinstance(c, int) else pl.multiple_of(c * tile, tile)
        kn = _halves_rms(k_ref[pl.ds(r0, tile), :].astype(F32), lo, dh) * gk_ref[...]
        bias = bias0 + lax.convert_element_type(c, F32) * bias_step
        ka_ref[c] = jnp.where(lo, kn, bias).astype(BF16)
        kb_ref[c] = jnp.where(lo, bias, kn).astype(BF16)
        vt_ref[c, :dv, :] = v_ref[pl.ds(r0, tile), :].astype(F32).T.astype(BF16)
        vt_ref[c, dv:, :] = ones_rows.astype(BF16)
        qn = (_halves_rms(q_ref[pl.ds(r0, tile), :].astype(F32), lo, dh) * gq_ref[...]
              * (dh ** -0.5 * LOG2E))
        qn_t = qn.T
        qa_ref[c] = jnp.where(lo_t, qn_t, qca_ref[...]).astype(BF16)
        qb_ref[c] = jnp.where(lo_t, qcb_ref[...], qn_t).astype(BF16)
        return carry

    lax.fori_loop(0, min(2, nk), prepare, 0)
    lam = (jnp.exp(jnp.sum(lq1_ref[...] * lk1_ref[...], axis=-1, keepdims=True))
           - jnp.exp(jnp.sum(lq2_ref[...] * lk2_ref[...], axis=-1, keepdims=True)) + LAMBDA_INIT)

    refs = (gate_ref, gsub_ref, o_ref, ka_ref, kb_ref, vt_ref, qa_ref, qb_ref,
            s0_ref, sa_ref, sb_ref, acc_ref, m_ref)
    s0_ref[0] = jnp.dot(ka_ref[0], qa_ref[0], preferred_element_type=F32)
    s0_ref[1] = jnp.dot(kb_ref[0], qb_ref[0], preferred_element_type=F32)
    _attn_q_tile(0, lam, prepare, *refs, tile=tile, dv=dv, nk=nk)

    def q_tile(qi, carry):
        _attn_q_tile(qi, lam, prepare, *refs, tile=tile, dv=dv, nk=nk)
        return carry

    lax.fori_loop(1, nk, q_tile, 0)


def _attn_q_tile(qi, lam, prepare, gate_ref, gsub_ref, o_ref, ka_ref, kb_ref, vt_ref, qa_ref, qb_ref,
                 s0_ref, sa_ref, sb_ref, acc_ref, m_ref, *, tile, dv, nk):
    def scores(jb, s_ref, q=qi):
        s_ref[0] = jnp.dot(ka_ref[jb], qa_ref[q], preferred_element_type=F32)
        s_ref[1] = jnp.dot(kb_ref[jb], qb_ref[q], preferred_element_type=F32)

    def absorb(jb, s_ref, masked):
        vt = vt_ref[jb]
        for idx in range(2):
            s = s_ref[idx]
            if masked:
                krow = lax.broadcasted_iota(I32, (tile, tile), 0)
                qcol = lax.broadcasted_iota(I32, (tile, tile), 1)
                s = jnp.where(krow > qcol, MASK_VALUE, s)
            m_old = m_ref[idx]
            m_new = jnp.maximum(m_old, jnp.max(s, axis=0, keepdims=True))
            m_ref[idx] = m_new
            p = jnp.exp2(s - m_new).astype(BF16)
            acc_ref[idx] = (acc_ref[idx] * jnp.exp2(m_old - m_new)
                            + jnp.dot(vt, p, preferred_element_type=F32))

    acc_ref[...] = jnp.zeros_like(acc_ref)
    m_ref[...] = jnp.full_like(m_ref, MASK_VALUE)
    if isinstance(qi, int):
        absorb(0, s0_ref, True)
    else:
        scores(1, sa_ref)
        absorb(0, s0_ref, False)

        def pair(j2, carry):
            jb = 2 * j2 + 1
            scores(jb + 1, sb_ref)
            absorb(jb, sa_ref, False)
            scores(jb + 2, sa_ref)
            absorb(jb + 1, sb_ref, False)
            return carry

        lax.fori_loop(0, (qi - 1) // 2, pair, 0)

        @pl.when(qi % 2 == 0)
        def _():
            scores(qi, sb_ref)
            absorb(qi - 1, sa_ref, False)
            absorb(qi, sb_ref, True)

        @pl.when(qi % 2 == 1)
        def _():
            absorb(qi, sa_ref, True)

    ahead = (lambda n: min(qi + n, nk - 1)) if isinstance(qi, int) else (lambda n: jnp.minimum(qi + n, nk - 1))
    scores(0, s0_ref, q=ahead(1))
    prepare(ahead(2), 0)
    a1, a2 = acc_ref[0], acc_ref[1]
    o_t = a1[:dv] * (1.0 / a1[dv:dv + 1]) - lam * (a2[:dv] * (1.0 / a2[dv:dv + 1]))
    o = _rms(o_t, axis=0).T * gsub_ref[...] * (1.0 - LAMBDA_INIT)
    rows = pl.ds(pl.multiple_of(qi * tile, tile), tile)
    o_ref[rows, :] = (o * _sigmoid(gate_ref[rows, :].astype(F32))).astype(o_ref.dtype)


def _bf16_terms(x, n):
    terms, rest = [], jnp.asarray(x, F32)
    for _ in range(n):
        term = rest.astype(BF16).astype(F32)
        terms.append(term)
        rest = rest - term
    return terms


def _diff_attention(proj, g_q, g_k, g_sub, lq1, lk1, lq2, lk2, batch, seq, d):
    t = proj.shape[0]
    tile = ATT_T
    nq = seq // tile
    dh = d // (2 * DIFF_HEADS)
    w = 2 * dh
    assert w == LANES
    q_col, k_col, v_col, gate_col = 3 * d // w, 4 * d // w, 5 * d // w, 7 * d // w
    slopes = 2.0 ** -(jnp.arange(DIFF_HEADS, dtype=F32) + 1.0)
    slopes = jnp.broadcast_to(slopes[:, None, None], (DIFF_HEADS, 1, LANES))
    c1, c2, c3 = _bf16_terms(LOG2E, 3)
    qconst = jnp.zeros((dh,), F32).at[:6].set(jnp.stack([c1, c1, c2, c2, c3, c3]))
    zeros = jnp.zeros((dh,), F32)
    qconst_a = jnp.concatenate([zeros, qconst])[:, None]
    qconst_b = jnp.concatenate([qconst, zeros])[:, None]
    tile2 = lambda a: jnp.concatenate([a, a], axis=-1)
    assert tile % 256 == 0
    small = lambda n: pl.BlockSpec((1, n), lambda b, h: (0, 0))
    kernel = functools.partial(_attn_kernel, tile=tile, dh=dh, dv=w, nk=nq)
    return pl.pallas_call(
        kernel,
        out_shape=jax.ShapeDtypeStruct((t, d), BF16),
        grid=(batch, DIFF_HEADS),
        in_specs=[pl.BlockSpec((seq, w), lambda b, h: (b, q_col + h)),
                  pl.BlockSpec((seq, w), lambda b, h: (b, k_col + h)),
                  pl.BlockSpec((seq, w), lambda b, h: (b, v_col + h)),
                  pl.BlockSpec((seq, w), lambda b, h: (b, gate_col + h)),
                  pl.BlockSpec((1, 1, LANES), lambda b, h: (h, 0, 0)),
                  pl.BlockSpec((w, 1), lambda b, h: (0, 0)), pl.BlockSpec((w, 1), lambda b, h: (0, 0)),
                  small(w), small(w), small(w),
                  small(dh), small(dh), small(dh), small(dh)],
        out_specs=pl.BlockSpec((seq, w), lambda b, h: (b, h)),
        scratch_shapes=[pltpu.VMEM((nq, tile, w), BF16),
                        pltpu.VMEM((nq, tile, w), BF16),
                        pltpu.VMEM((nq, w + BF16_SUBLANES, tile), BF16),
                        pltpu.VMEM((nq, w, tile), BF16),
                        pltpu.VMEM((nq, w, tile), BF16),
                        pltpu.VMEM((2, tile, tile), F32),
                        pltpu.VMEM((2, tile, tile), F32),
                        pltpu.VMEM((2, tile, tile), F32),
                        pltpu.VMEM((2, w + BF16_SUBLANES, tile), F32),
                        pltpu.VMEM((2, 1, tile), F32)],
        compiler_params=_params("arbitrary", "arbitrary"),
        name="diff_attn",
    )(proj, proj, proj, proj, slopes, qconst_a, qconst_b, tile2(g_q), tile2(g_k), g_sub,
      lq1, lk1, lq2, lk2)


def _out_kernel(x_ref, ret_ref, att_ref, mod_ref, wo_ref, g2_ref, wr_ref, br_ref,
                x1_ref, h2_ref, tw_ref, ld_ref, cnt_ref, *, tm, n_exp):
    m = mod_ref[0]
    merged = (ret_ref[...].astype(F32) + att_ref[...].astype(F32)).astype(BF16)
    x1 = x_ref[...] + m[2:3] * jnp.dot(merged, wo_ref[...], preferred_element_type=F32)
    x1_ref[...] = x1
    h2 = _rms(x1) * g2_ref[...] * (1.0 + m[4:5]) + m[3:4]
    hi = h2.astype(BF16)
    h2_ref[...] = hi

    lo = (h2 - hi.astype(F32)).astype(BF16)
    both = jnp.dot(hi, wr_ref[...], preferred_element_type=F32)
    logits = (both[:, :LANES] + both[:, LANES:]
              + jnp.dot(lo, wr_ref[:, :LANES], preferred_element_type=F32) + br_ref[...])
    lt = logits.T[:n_exp]

    erow = lax.broadcasted_iota(I32, (n_exp, tm), 0)
    vals, sels = [], []
    for k in range(TOP_K):
        mx = jnp.max(lt, axis=0, keepdims=True)
        idx = jnp.min(jnp.where(lt == mx, erow, n_exp), axis=0, keepdims=True)
        sel = erow == idx
        vals.append(mx)
        sels.append(sel)
        lt = jnp.where(sel, MASK_VALUE, lt)
    exps = [jnp.exp(v - vals[0]) for v in vals]
    inv = 1.0 / (exps[0] + exps[1] + exps[2] + exps[3])
    for k in range(TOP_K):
        tw_ref[k:k + 1, :] = exps[k] * inv

    chosen = jnp.where(sels[0] | sels[1] | sels[2] | sels[3], 1.0, 0.0)
    before = (lax.broadcasted_iota(I32, (tm, tm), 0) < lax.broadcasted_iota(I32, (tm, tm), 1))
    prefix = jnp.dot(chosen.astype(BF16), jnp.where(before, 1.0, 0.0).astype(BF16),
                     preferred_element_type=F32)
    count = jnp.sum(chosen, axis=1, keepdims=True)
    cnt_ref[...] = count
    padded = jnp.ceil(count * (1.0 / RUN_ALIGN)) * RUN_ALIGN
    below = (lax.broadcasted_iota(I32, (n_exp, n_exp), 1) < lax.broadcasted_iota(I32, (n_exp, n_exp), 0))
    run_start = jnp.dot(jnp.where(below, 1.0, 0.0).astype(BF16),
                        jnp.broadcast_to(padded, (n_exp, LANES)).astype(BF16),
                        preferred_element_type=F32)[:, :1]
    pos = prefix + run_start
    for k in range(TOP_K):
        ld_ref[k:k + 1, :] = jnp.sum(jnp.where(sels[k], pos, 0.0), axis=0, keepdims=True).astype(I32)


def _out_router(x2, ret, att, mod3, w_out_bf16, g2, wr_split, br_pad, seq, n_exp):
    t, d = x2.shape
    tm = TOK_TM
    row = lambda i: (i, 0)
    const = lambda i: (0, 0)
    kernel = functools.partial(_out_kernel, tm=tm, n_exp=n_exp)
    return pl.pallas_call(
        kernel,
        out_shape=(jax.ShapeDtypeStruct((t, d), F32),
                   jax.ShapeDtypeStruct((t, d), BF16),
                   jax.ShapeDtypeStruct((TOP_K, t), F32),
                   jax.ShapeDtypeStruct((TOP_K, t), I32),
                   jax.ShapeDtypeStruct((t // tm, n_exp, 1), F32)),
        grid=(t // tm,),
        in_specs=[pl.BlockSpec((tm, d), row), pl.BlockSpec((tm, d), row), pl.BlockSpec((tm, d), row),
                  pl.BlockSpec((1, 6, d), lambda i: (i * tm // seq, 0, 0)),
                  pl.BlockSpec((d, d), const), pl.BlockSpec((1, d), const),
                  pl.BlockSpec((d, 2 * LANES), const), pl.BlockSpec((1, LANES), const)],
        out_specs=(pl.BlockSpec((tm, d), row), pl.BlockSpec((tm, d), row),
                   pl.BlockSpec((TOP_K, tm), lambda i: (0, i)),
                   pl.BlockSpec((TOP_K, tm), lambda i: (0, i)),
                   pl.BlockSpec((None, n_exp, 1), lambda i: (i, 0, 0))),
        compiler_params=_params("arbitrary"),
        name="out_router",
    )(x2, ret, att, mod3, w_out_bf16, g2, wr_split, br_pad)


def _meta_stride(n_exp):
    return 2 * len(SLAB_SIZES) * n_exp + len(SLAB_SIZES) + 1


def _rows_used(meta_ref, tile, n_exp):
    return meta_ref[tile * _meta_stride(n_exp) + 2 * len(SLAB_SIZES) * n_exp + len(SLAB_SIZES)]


def _run_slabs(meta_ref, tile, n_exp, local_ref, global_ref, sem, *, to_global, wait):
    n_sizes = len(SLAB_SIZES)
    base = tile * _meta_stride(n_exp)
    for c, size in enumerate(SLAB_SIZES):
        def one(p, carry, c=c, size=size):
            if wait:
                local = glob = 0
            else:
                local = pl.multiple_of(meta_ref[base + c * n_exp + p], RUN_ALIGN)
                glob = pl.multiple_of(meta_ref[base + (n_sizes + c) * n_exp + p], RUN_ALIGN)
            loc = local_ref.at[pl.ds(local, size)]
            glo = global_ref.at[pl.ds(glob, size)]
            cp = pltpu.make_async_copy(loc, glo, sem) if to_global else pltpu.make_async_copy(glo, loc, sem)
            if wait:
                cp.wait()
            else:
                cp.start()
            return carry

        lax.fori_loop(0, meta_ref[base + 2 * n_sizes * n_exp + c], one, 0)


def _dispatch_kernel(meta_ref, tail_ref, ld_ref, h_ref, xs_ref, stage_ref, zero_ref, sems, tail_sem,
                     *, tm, n_exp, n_tiles, n_blocks, stage_rows):
    i = pl.program_id(0)
    slot = i & 1
    stage = stage_ref.at[slot]

    def tails(wait):
        def unused_block(b, carry):
            cp = pltpu.make_async_copy(
                zero_ref, xs_ref.at[pl.ds(pl.multiple_of(b * EXP_BLK, EXP_BLK), EXP_BLK)], tail_sem)
            if wait:
                cp.wait()
            else:
                cp.start()
            return carry

        lax.fori_loop(tail_ref[2 * n_exp] // EXP_BLK, n_blocks, unused_block, 0)

        def per_expert(e, carry):
            start, n = tail_ref[e], tail_ref[n_exp + e]
            off = 0
            for size in TAIL_SIZES:
                take = n & size

                @pl.when(take != 0)
                def _(off=off, size=size):
                    cp = pltpu.make_async_copy(
                        zero_ref.at[pl.ds(0, size)],
                        xs_ref.at[pl.ds(pl.multiple_of(start + off, RUN_ALIGN), size)], tail_sem)
                    if wait:
                        cp.wait()
                    else:
                        cp.start()

                off = off + take
            return carry

        lax.fori_loop(0, n_exp, per_expert, 0)

    @pl.when(i == 0)
    def _():
        zero_ref[...] = jnp.zeros_like(zero_ref)
        tails(False)
        tails(True)

    ld = ld_ref[...]
    h = h_ref[...]
    rows_used = _rows_used(meta_ref, i, n_exp)
    for c in range(stage_rows // GATHER_RC):
        @pl.when(c * GATHER_RC < rows_used)
        def _(c=c):
            r = lax.broadcasted_iota(I32, (GATHER_RC, tm), 0) + c * GATHER_RC
            hit = jnp.where(r == ld[0:1], 1.0, jnp.where(r == ld[1:2], 1.0,
                            jnp.where(r == ld[2:3], 1.0, jnp.where(r == ld[3:4], 1.0, 0.0))))
            stage[c * GATHER_RC:(c + 1) * GATHER_RC, :] = jnp.dot(
                hit.astype(BF16), h, preferred_element_type=F32).astype(BF16)

    _run_slabs(meta_ref, i, n_exp, stage, xs_ref, sems.at[slot], to_global=True, wait=False)

    @pl.when(i > 0)
    def _():
        _run_slabs(meta_ref, i - 1, n_exp, stage_ref.at[1 - slot], xs_ref, sems.at[1 - slot],
                   to_global=True, wait=True)

    @pl.when(i == n_tiles - 1)
    def _():
        _run_slabs(meta_ref, i, n_exp, stage, xs_ref, sems.at[slot], to_global=True, wait=True)


def _dispatch(meta, tail, ldest, h2, n_rows, n_exp):
    t, d = h2.shape
    tm = TOK_TM
    n_tiles = t // tm
    stage_rows = _stage_rows(n_exp)
    kernel = functools.partial(_dispatch_kernel, tm=tm, n_exp=n_exp, n_tiles=n_tiles,
                               n_blocks=n_rows // EXP_BLK, stage_rows=stage_rows)
    grid_spec = pltpu.PrefetchScalarGridSpec(
        num_scalar_prefetch=2,
        grid=(n_tiles,),
        in_specs=[pl.BlockSpec((TOP_K, tm), lambda i, m, tl: (0, i)),
                  pl.BlockSpec((tm, d), lambda i, m, tl: (i, 0))],
        out_specs=pl.BlockSpec(memory_space=pl.ANY),
        scratch_shapes=[pltpu.VMEM((2, stage_rows, d), BF16),
                        pltpu.VMEM((EXP_BLK, d), BF16),
                        pltpu.SemaphoreType.DMA((2,)),
                        pltpu.SemaphoreType.DMA],
    )
    return pl.pallas_call(
        kernel,
        out_shape=jax.ShapeDtypeStruct((n_rows, d), BF16),
        grid_spec=grid_spec,
        compiler_params=_params("arbitrary"),
        name="dispatch",
    )(meta, tail, ldest, h2)


def _expert_kernel(blk_e_ref, n_used_ref, x_ref, w1_ref, b1_ref, w2_ref, b2_ref, o_ref, w1b_ref, w2b_ref,
                   *, f):
    i = pl.program_id(0)
    new_expert = jnp.logical_or(i == 0, blk_e_ref[i] != blk_e_ref[jnp.maximum(i - 1, 0)])

    @pl.when(jnp.logical_and(i < n_used_ref[0], new_expert))
    def _():
        w1b_ref[...] = w1_ref[0].astype(BF16)
        w2b_ref[...] = w2_ref[0].astype(BF16)

    @pl.when(i < n_used_ref[0])
    def _():
        gu = jnp.dot(x_ref[...], w1b_ref[...], preferred_element_type=F32) + b1_ref[0]
        g = jnp.minimum(gu[:, :f], SWIGLU_LIMIT)
        u = jnp.clip(gu[:, f:], -SWIGLU_LIMIT, SWIGLU_LIMIT)
        a = g * _sigmoid(SWIGLU_ALPHA * g) * (u + 1.0)
        o_ref[...] = (jnp.dot(a.astype(BF16), w2b_ref[...], preferred_element_type=F32)
                      + b2_ref[0]).astype(o_ref.dtype)

    @pl.when(i >= n_used_ref[0])
    def _():
        o_ref[...] = jnp.zeros_like(o_ref)


def _experts(blk_e, n_used, xs, w1, b1, w2, b2):
    n_rows, d = xs.shape
    n_exp, _, f2 = w1.shape
    f = f2 // 2
    blk = EXP_BLK
    kernel = functools.partial(_expert_kernel, f=f)
    grid_spec = pltpu.PrefetchScalarGridSpec(
        num_scalar_prefetch=2,
        grid=(n_rows // blk,),
        in_specs=[pl.BlockSpec((blk, d), lambda i, e, n: (jnp.minimum(i, n[0] - 1), 0)),
                  pl.BlockSpec((1, d, f2), lambda i, e, n: (e[i], 0, 0)),
                  pl.BlockSpec((1, 1, f2), lambda i, e, n: (e[i], 0, 0)),
                  pl.BlockSpec((1, f, d), lambda i, e, n: (e[i], 0, 0)),
                  pl.BlockSpec((1, 1, d), lambda i, e, n: (e[i], 0, 0))],
        out_specs=pl.BlockSpec((blk, d), lambda i, e, n: (i, 0)),
        scratch_shapes=[pltpu.VMEM((d, f2), BF16), pltpu.VMEM((f, d), BF16)],
    )
    return pl.pallas_call(
        kernel,
        out_shape=jax.ShapeDtypeStruct((n_rows, d), BF16),
        grid_spec=grid_spec,
        compiler_params=_params("arbitrary"),
        name="experts",
    )(blk_e, n_used, xs, w1, b1, w2, b2)


def _combine_kernel(meta_ref, eo_ref, ld_ref, w_ref, x1_ref, mod_ref, o_ref, stage_ref, sel_ref, sems,
                    *, tm, n_exp, n_tiles, stage_rows):
    i = pl.program_id(0)
    slot = i & 1

    @pl.when(i == 0)
    def _():
        stage_ref[...] = jnp.zeros_like(stage_ref)
        _run_slabs(meta_ref, 0, n_exp, stage_ref.at[0], eo_ref, sems.at[0], to_global=False, wait=False)

    @pl.when(i + 1 < n_tiles)
    def _():
        _run_slabs(meta_ref, i + 1, n_exp, stage_ref.at[1 - slot], eo_ref, sems.at[1 - slot],
                   to_global=False, wait=False)

    ld = ld_ref[...]
    w = w_ref[...]
    ldb = [jnp.broadcast_to(ld[:, k:k + 1], (tm, SCATTER_RC)) for k in range(TOP_K)]
    wb = [jnp.broadcast_to(w[:, k:k + 1], (tm, SCATTER_RC)) for k in range(TOP_K)]
    col = lax.broadcasted_iota(I32, (tm, SCATTER_RC), 1)
    for c in range(stage_rows // SCATTER_RC):
        r = col + c * SCATTER_RC
        sel = jnp.where(ldb[0] == r, wb[0], jnp.where(ldb[1] == r, wb[1],
                        jnp.where(ldb[2] == r, wb[2], jnp.where(ldb[3] == r, wb[3], 0.0))))
        sel_ref[:, c * SCATTER_RC:(c + 1) * SCATTER_RC] = sel.astype(BF16)

    _run_slabs(meta_ref, i, n_exp, stage_ref.at[slot], eo_ref, sems.at[slot], to_global=False, wait=True)
    y = jnp.dot(sel_ref[...], stage_ref[slot], preferred_element_type=F32)
    o_ref[...] = x1_ref[...] + mod_ref[0][5:6] * y


def _combine(meta, eo, ld_tok, w_tok, x1, mod3, seq, n_exp):
    t, d = x1.shape
    tm = TOK_TM
    n_tiles = t // tm
    stage_rows = _stage_rows(n_exp)
    kernel = functools.partial(_combine_kernel, tm=tm, n_exp=n_exp, n_tiles=n_tiles, stage_rows=stage_rows)
    grid_spec = pltpu.PrefetchScalarGridSpec(
        num_scalar_prefetch=1,
        grid=(n_tiles,),
        in_specs=[pl.BlockSpec(memory_space=pl.ANY),
                  pl.BlockSpec((tm, TOP_K), lambda i, m: (i, 0)),
                  pl.BlockSpec((tm, TOP_K), lambda i, m: (i, 0)),
                  pl.BlockSpec((tm, d), lambda i, m: (i, 0)),
                  pl.BlockSpec((1, 6, d), lambda i, m: (i * tm // seq, 0, 0))],
        out_specs=pl.BlockSpec((tm, d), lambda i, m: (i, 0)),
        scratch_shapes=[pltpu.VMEM((2, stage_rows, d), BF16),
                        pltpu.VMEM((tm, stage_rows), BF16),
                        pltpu.SemaphoreType.DMA((2,))],
    )
    return pl.pallas_call(
        kernel,
        out_shape=jax.ShapeDtypeStruct((t, d), F32),
        grid_spec=grid_spec,
        compiler_params=_params("arbitrary"),
        name="combine",
    )(meta, eo, ld_tok, w_tok, x1, mod3)


def _routing_tables(tile_cnt, n_tokens):
    n_tiles, n_exp = tile_cnt.shape
    blk = EXP_BLK
    run = _round_up(tile_cnt, RUN_ALIGN)
    local = jnp.cumsum(run, axis=1) - run
    size = jnp.sum(run, axis=0)
    region = _round_up(size, blk)
    region_end = jnp.cumsum(region)
    region_start = region_end - region
    glob = region_start[None, :] + jnp.cumsum(run, axis=0) - run
    sizes = jnp.asarray(SLAB_SIZES, I32)
    has = (run[:, :, None] & sizes) != 0
    before = run[:, :, None] & ~(2 * sizes - 1)
    place = jnp.cumsum(has, axis=1) - has
    hit = has[..., None] & (place[..., None] == jnp.arange(n_exp, dtype=I32))
    listed = lambda start: jnp.sum(
        jnp.where(hit, (start[:, :, None] + before)[..., None], 0), axis=1).reshape(n_tiles, -1)
    meta = jnp.concatenate([listed(local), listed(glob), jnp.sum(has, axis=1),
                            jnp.sum(run, axis=1, keepdims=True)], axis=1).reshape(-1).astype(I32)
    tail =jnp.concatenate([region_start + size, region - size, region_end[-1:]]).astype(I32)
    n_rows = _round_up(n_tokens * TOP_K + n_tiles * n_exp * RUN_ALIGN + n_exp * blk, blk)
    block_row = jnp.arange(n_rows // blk, dtype=I32) * blk
    n_used = (region_end[-1:] // blk).astype(I32)
    blk_e = jnp.sum((region_end[None, :] <= block_row[:, None]).astype(I32), axis=1)
    last_e = jnp.sum((region_end[:-1] < region_end[-1]).astype(I32))
    blk_e = jnp.minimum(blk_e, last_e).astype(I32)
    return meta, tail, blk_e, n_used, n_rows


def kernel(x, c, w_ada, b_ada, g_norm1, w_in, g_ret_gn, b_ret_gn, g_qnorm, g_knorm,
           lambda_q1, lambda_k1, lambda_q2, lambda_k2, g_diff_subln, w_out, g_norm2,
           w_router, b_router, w_expert_in, b_expert_in, w_expert_out, b_expert_out):
    batch, seq, d = x.shape
    depth = w_ada.shape[0]
    assert depth == 1
    t = batch * seq
    n_exp = w_router.shape[-1]
    l = 0

    mod = _adaln_mod(c, w_ada[l], b_ada[l][None, :])
    mod3 = mod.reshape(batch, 6, d)
    x2 = x.reshape(t, d)

    proj = _in_proj(x2, mod3, g_norm1[l][None, :], w_in[l].astype(BF16), seq)
    ret = _retention(proj, g_ret_gn[l][None, :], b_ret_gn[l][None, :], batch, seq, d)
    att = _diff_attention(proj, g_qnorm[l][None, :], g_knorm[l][None, :], g_diff_subln[l][None, :],
                          lambda_q1[l][None, :], lambda_k1[l][None, :],
                          lambda_q2[l][None, :], lambda_k2[l][None, :], batch, seq, d)

    wr = w_router[l]
    wr_hi = wr.astype(BF16)
    wr_lo = (wr - wr_hi.astype(F32)).astype(BF16)
    pad = lambda a: jnp.pad(a, ((0, 0), (0, LANES - n_exp)))
    wr_split = jnp.concatenate([pad(wr_hi), pad(wr_lo)], axis=1)
    br_pad = jnp.pad(b_router[l][None, :], ((0, 0), (0, LANES - n_exp)), constant_values=MASK_VALUE)
    x1, h2, top_w, ldest, tile_cnt = _out_router(
        x2, ret, att, mod3, w_out[l].astype(BF16), g_norm2[l][None, :], wr_split, br_pad, seq, n_exp)

    meta, tail, blk_e, n_used, n_rows = _routing_tables(tile_cnt[:, :, 0].astype(I32), t)
    xs = _dispatch(meta, tail, ldest, h2, n_rows, n_exp)
    eo = _experts(blk_e, n_used, xs, w_expert_in[l], b_expert_in[l][:, None, :],
                  w_expert_out[l], b_expert_out[l][:, None, :])
    out = _combine(meta, eo, ldest.T, top_w.T, x1, mod3, seq, n_exp)
    return out.reshape(batch, seq, d)
```

```python
import functools
import math

import jax
import jax.numpy as jnp
from jax import lax
from jax.experimental import pallas as pl
from jax.experimental.pallas import tpu as pltpu

F32 = jnp.float32
BF16 = jnp.bfloat16
I32 = jnp.int32

EPS = 1e-6
LOG2E = 1.4426950216293335
MASK_VALUE = -1e30
LANES = 128
BF16_SUBLANES = 16
VMEM_LIMIT_BYTES = 48 * 1024 * 1024

RET_HEADS = 4
DIFF_HEADS = 8
TOP_K = 4
SWIGLU_LIMIT = 7.0
SWIGLU_ALPHA = 1.702
LAMBDA_INIT = 0.8 - 0.6 * math.exp(-0.3 * 0)

IN_TM, IN_TN = 1024, 2048
RET_TC = 256
ATT_T = 512
TOK_TM = 512
EXP_BLK = 256
RUN_ALIGN = BF16_SUBLANES
GATHER_RC = 256
SCATTER_RC = 128
COMBINE_KC = 512


def _round_up(n, m):
    return (n + m - 1) // m * m


def _stage_rows(n_exp):
    return _round_up(TOK_TM * TOP_K + n_exp * (RUN_ALIGN - 1), max(GATHER_RC, COMBINE_KC))


SLAB_SIZES = tuple(TOK_TM >> s for s in range(TOK_TM.bit_length()) if TOK_TM >> s >= RUN_ALIGN)
TAIL_SIZES = tuple(s for s in SLAB_SIZES if s < EXP_BLK)


def _params(*sem):
    return pltpu.CompilerParams(dimension_semantics=sem, vmem_limit_bytes=VMEM_LIMIT_BYTES)


def _sigmoid(x):
    return 0.5 * jnp.tanh(0.5 * x) + 0.5


def _rms(x, axis=-1):
    return x * lax.rsqrt(jnp.mean(x * x, axis=axis, keepdims=True) + EPS)


def _mod_kernel(c_ref, w_ref, b_ref, o_ref):
    c = c_ref[...]
    s = c * _sigmoid(c)
    o_ref[...] = jnp.dot(s, w_ref[...], preferred_element_type=F32,
                         precision=lax.Precision.HIGHEST) + b_ref[...]


def _adaln_mod(c, w, b):
    bn, d = c.shape
    n = w.shape[1]
    tn = d
    return pl.pallas_call(
        _mod_kernel,
        out_shape=jax.ShapeDtypeStruct((bn, n), F32),
        grid=(n // tn,),
        in_specs=[pl.BlockSpec((bn, d), lambda j: (0, 0)),
                  pl.BlockSpec((d, tn), lambda j: (0, j)),
                  pl.BlockSpec((1, tn), lambda j: (0, j))],
        out_specs=pl.BlockSpec((bn, tn), lambda j: (0, j)),
        compiler_params=_params("arbitrary"),
        name="adaln_mod",
    )(c, w, b)


def _inproj_kernel(x_ref, mod_ref, g_ref, w_ref, o_ref, h_ref):
    @pl.when(pl.program_id(1) == 0)
    def _():
        m = mod_ref[0]
        h = _rms(x_ref[...]) * g_ref[...] * (1.0 + m[1:2]) + m[0:1]
        h_ref[...] = h.astype(BF16)

    o_ref[...] = jnp.dot(h_ref[...], w_ref[...], preferred_element_type=F32).astype(o_ref.dtype)


def _in_proj(x2, mod3, g1, w_in_bf16, seq):
    t, d = x2.shape
    n = w_in_bf16.shape[1]
    tm, tn = IN_TM, IN_TN
    return pl.pallas_call(
        _inproj_kernel,
        out_shape=jax.ShapeDtypeStruct((t, n), BF16),
        grid=(t // tm, n // tn),
        in_specs=[pl.BlockSpec((tm, d), lambda i, j: (i, 0)),
                  pl.BlockSpec((1, 6, d), lambda i, j: (i * tm // seq, 0, 0)),
                  pl.BlockSpec((1, d), lambda i, j: (0, 0)),
                  pl.BlockSpec((d, tn), lambda i, j: (0, j))],
        out_specs=pl.BlockSpec((tm, tn), lambda i, j: (i, j)),
        scratch_shapes=[pltpu.VMEM((tm, d), BF16)],
        compiler_params=_params("arbitrary", "arbitrary"),
        name="in_proj",
    )(x2, mod3, g1, w_in_bf16)


def _ret_kernel(q_ref, k_ref, v_ref, rg_ref, mg_ref, gng_ref, gnb_ref, o_ref, r_ref, decay_ref,
                *, dk, dv, tc):
    scale = dk ** -0.5
    log_gs = [math.log(1.0 - 2.0 ** (-5.0 - h)) for h in range(RET_HEADS)]

    @pl.when(pl.program_id(1) == 0)
    def _():
        r_ref[...] = jnp.zeros_like(r_ref)
        row = lax.broadcasted_iota(I32, (tc, tc), 0)
        col = lax.broadcasted_iota(I32, (tc, tc), 1)
        rel = (row - col).astype(F32)
        for h, log_g in enumerate(log_gs):
            decay_ref[h] = jnp.where(rel >= 0, jnp.exp(log_g * jnp.maximum(rel, 0.0)), 0.0) * scale

    pos = lax.broadcasted_iota(I32, (tc, 1), 0).astype(F32)
    for h, log_g in enumerate(log_gs):
        decay = decay_ref[h]
        xi = jnp.exp(log_g * (pos + 1.0))
        zeta = jnp.exp(log_g * (tc - 1.0 - pos))
        g_chunk = math.exp(log_g * tc)
        q = q_ref[:, h * dk:(h + 1) * dk]
        k = k_ref[:, h * dk:(h + 1) * dk]
        v = v_ref[:, h * dv:(h + 1) * dv]
        s = lax.dot_general(q, k, (((1,), (1,)), ((), ())), preferred_element_type=F32)
        y = jnp.dot((s * decay).astype(BF16), v, preferred_element_type=F32)
        state = r_ref[h]
        y = y + jnp.dot(q, state.astype(BF16), preferred_element_type=F32) * xi
        vz = (v.astype(F32) * zeta).astype(BF16)
        kv = lax.dot_general(k, vz, (((0,), (0,)), ((), ())), preferred_element_type=F32)
        r_ref[h] = state * g_chunk + kv * scale

        mu = jnp.mean(y, axis=-1, keepdims=True)
        yc = y - mu
        var = jnp.mean(yc * yc, axis=-1, keepdims=True)
        sl = slice(h * dv, (h + 1) * dv)
        yn = yc * lax.rsqrt(var + EPS) * gng_ref[:, sl] + gnb_ref[:, sl]
        rg = rg_ref[:, sl].astype(F32)
        gate = _sigmoid(mg_ref[:, sl].astype(F32))
        o_ref[:, sl] = (rg * _sigmoid(rg) * yn * gate).astype(o_ref.dtype)


def _retention(proj, gn_g, gn_b, batch, seq, d):
    t = proj.shape[0]
    tc = RET_TC
    nc = seq // tc
    dk = d // (2 * RET_HEADS)
    dv = d // RET_HEADS
    qk_w = RET_HEADS * dk
    row = lambda b, c: b * nc + c
    kernel = functools.partial(_ret_kernel, dk=dk, dv=dv, tc=tc)
    return pl.pallas_call(
        kernel,
        out_shape=jax.ShapeDtypeStruct((t, d), BF16),
        grid=(batch, nc),
        in_specs=[pl.BlockSpec((tc, qk_w), lambda b, c: (row(b, c), 0)),
                  pl.BlockSpec((tc, qk_w), lambda b, c: (row(b, c), 1)),
                  pl.BlockSpec((tc, d), lambda b, c: (row(b, c), 1)),
                  pl.BlockSpec((tc, d), lambda b, c: (row(b, c), 2)),
                  pl.BlockSpec((tc, d), lambda b, c: (row(b, c), 6)),
                  pl.BlockSpec((1, d), lambda b, c: (0, 0)),
                  pl.BlockSpec((1, d), lambda b, c: (0, 0))],
        out_specs=pl.BlockSpec((tc, d), lambda b, c: (row(b, c), 0)),
        scratch_shapes=[pltpu.VMEM((RET_HEADS, dk, dv), F32), pltpu.VMEM((RET_HEADS, tc, tc), F32)],
        compiler_params=_params("arbitrary", "arbitrary"),
        name="retention",
    )(proj, proj, proj, proj, proj, gn_g, gn_b)


def _halves_rms(x, lo, dh):
    sq = x * x
    s_lo = jnp.sum(jnp.where(lo, sq, 0.0), axis=-1, keepdims=True)
    s_hi = jnp.sum(jnp.where(lo, 0.0, sq), axis=-1, keepdims=True)
    inv = jnp.where(lo, lax.rsqrt(s_lo / dh + EPS), lax.rsqrt(s_hi / dh + EPS))
    return x * inv


def _attn_kernel(q_ref, k_ref, v_ref, gate_ref, slope_ref, qca_ref, qcb_ref, gq_ref, gk_ref, gsub_ref,
                 lq1_ref, lk1_ref, lq2_ref, lk2_ref,
                 o_ref, ka_ref, kb_ref, vt_ref, qa_ref, qb_ref, s0_ref, sa_ref, sb_ref, acc_ref, m_ref,
                 *, tile, dh, dv, nk):
    lane = lax.broadcasted_iota(I32, (tile, 2 * dh), 1)
    lo = lane < dh

    slope = slope_ref[0]
    sub = lax.broadcasted_iota(I32, (tile, 2 * dh), 0)
    off = lane & (dh - 1)
    hi_lane = (off < 6) & ((off & 1) == 0)
    lo_lane = (off < 6) & ((off & 1) == 1)
    bias0 = jnp.where(hi_lane, ((sub >> 8) << 8).astype(F32),
                      jnp.where(lo_lane, (sub & 255).astype(F32), 0.0)) * slope
    bias_step = jnp.where(hi_lane, float(tile), 0.0) * slope
    ones_rows = jnp.where(lax.broadcasted_iota(I32, (BF16_SUBLANES, tile), 0) == 0, 1.0, 0.0)
    lo_t = lax.broadcasted_iota(I32, (2 * dh, tile), 0) < dh

    def prepare(c, carry):
        r0 = c * tile if isinstance(c, int) else pl.multiple_of(c * tile, tile)
        kn = _halves_rms(k_ref[pl.ds(r0, tile), :].astype(F32), lo, dh) * gk_ref[...]
        bias = bias0 + lax.convert_element_type(c, F32) * bias_step
        ka_ref[c] = jnp.where(lo, kn, bias).astype(BF16)
        kb_ref[c] = jnp.where(lo, bias, kn).astype(BF16)
        vt_ref[c, :dv, :] = v_ref[pl.ds(r0, tile), :].astype(F32).T.astype(BF16)
        vt_ref[c, dv:, :] = ones_rows.astype(BF16)
        qn = (_halves_rms(q_ref[pl.ds(r0, tile), :].astype(F32), lo, dh) * gq_ref[...]
              * (dh ** -0.5 * LOG2E))
        qn_t = qn.T
        qa_ref[c] = jnp.where(lo_t, qn_t, qca_ref[...]).astype(BF16)
        qb_ref[c] = jnp.where(lo_t, qcb_ref[...], qn_t).astype(BF16)
        return carry

    lax.fori_loop(0, min(2, nk), prepare, 0)
    lam = (jnp.exp(jnp.sum(lq1_ref[...] * lk1_ref[...], axis=-1, keepdims=True))
           - jnp.exp(jnp.sum(lq2_ref[...] * lk2_ref[...], axis=-1, keepdims=True)) + LAMBDA_INIT)

    refs = (gate_ref, gsub_ref, o_ref, ka_ref, kb_ref, vt_ref, qa_ref, qb_ref,
            s0_ref, sa_ref, sb_ref, acc_ref, m_ref)
    s0_ref[0] = jnp.dot(ka_ref[0], qa_ref[0], preferred_element_type=F32)
    s0_ref[1] = jnp.dot(kb_ref[0], qb_ref[0], preferred_element_type=F32)
    _attn_q_tile(0, lam, prepare, *refs, tile=tile, dv=dv, nk=nk)

    def q_tile(qi, carry):
        _attn_q_tile(qi, lam, prepare, *refs, tile=tile, dv=dv, nk=nk)
        return carry

    lax.fori_loop(1, nk, q_tile, 0)


def _attn_q_tile(qi, lam, prepare, gate_ref, gsub_ref, o_ref, ka_ref, kb_ref, vt_ref, qa_ref, qb_ref,
                 s0_ref, sa_ref, sb_ref, acc_ref, m_ref, *, tile, dv, nk):
    def scores(jb, s_ref, q=qi):
        s_ref[0] = jnp.dot(ka_ref[jb], qa_ref[q], preferred_element_type=F32)
        s_ref[1] = jnp.dot(kb_ref[jb], qb_ref[q], preferred_element_type=F32)

    def absorb(jb, s_ref, masked):
        vt = vt_ref[jb]
        for idx in range(2):
            s = s_ref[idx]
            if masked:
                krow = lax.broadcasted_iota(I32, (tile, tile), 0)
                qcol = lax.broadcasted_iota(I32, (tile, tile), 1)
                s = jnp.where(krow > qcol, MASK_VALUE, s)
            m_old = m_ref[idx]
            m_new = jnp.maximum(m_old, jnp.max(s, axis=0, keepdims=True))
            m_ref[idx] = m_new
            p = jnp.exp2(s - m_new).astype(BF16)
            acc_ref[idx] = (acc_ref[idx] * jnp.exp2(m_old - m_new)
                            + jnp.dot(vt, p, preferred_element_type=F32))

    acc_ref[...] = jnp.zeros_like(acc_ref)
    m_ref[...] = jnp.full_like(m_ref, MASK_VALUE)
    if isinstance(qi, int):
        absorb(0, s0_ref, True)
    else:
        scores(1, sa_ref)
        absorb(0, s0_ref, False)

        def pair(j2, carry):
            jb = 2 * j2 + 1
            scores(jb + 1, sb_ref)
            absorb(jb, sa_ref, False)
            scores(jb + 2, sa_ref)
            absorb(jb + 1, sb_ref, False)
            return carry

        lax.fori_loop(0, (qi - 1) // 2, pair, 0)

        @pl.when(qi % 2 == 0)
        def _():
            scores(qi, sb_ref)
            absorb(qi - 1, sa_ref, False)
            absorb(qi, sb_ref, True)

        @pl.when(qi % 2 == 1)
        def _():
            absorb(qi, sa_ref, True)

    ahead = (lambda n: min(qi + n, nk - 1)) if isinstance(qi, int) else (lambda n: jnp.minimum(qi + n, nk - 1))
    scores(0, s0_ref, q=ahead(1))
    prepare(ahead(2), 0)
    a1, a2 = acc_ref[0], acc_ref[1]
    o_t = a1[:dv] * (1.0 / a1[dv:dv + 1]) - lam * (a2[:dv] * (1.0 / a2[dv:dv + 1]))
    o = _rms(o_t, axis=0).T * gsub_ref[...] * (1.0 - LAMBDA_INIT)
    rows = pl.ds(pl.multiple_of(qi * tile, tile), tile)
    o_ref[rows, :] = (o * _sigmoid(gate_ref[rows, :].astype(F32))).astype(o_ref.dtype)


def _bf16_terms(x, n):
    terms, rest = [], jnp.asarray(x, F32)
    for _ in range(n):
        term = rest.astype(BF16).astype(F32)
        terms.append(term)
        rest = rest - term
    return terms


def _diff_attention(proj, g_q, g_k, g_sub, lq1, lk1, lq2, lk2, batch, seq, d):
    t = proj.shape[0]
    tile = ATT_T
    nq = seq // tile
    dh = d // (2 * DIFF_HEADS)
    w = 2 * dh
    assert w == LANES
    q_col, k_col, v_col, gate_col = 3 * d // w, 4 * d // w, 5 * d // w, 7 * d // w
    slopes = 2.0 ** -(jnp.arange(DIFF_HEADS, dtype=F32) + 1.0)
    slopes = jnp.broadcast_to(slopes[:, None, None], (DIFF_HEADS, 1, LANES))
    c1, c2, c3 = _bf16_terms(LOG2E, 3)
    qconst = jnp.zeros((dh,), F32).at[:6].set(jnp.stack([c1, c1, c2, c2, c3, c3]))
    zeros = jnp.zeros((dh,), F32)
    qconst_a = jnp.concatenate([zeros, qconst])[:, None]
    qconst_b = jnp.concatenate([qconst, zeros])[:, None]
    tile2 = lambda a: jnp.concatenate([a, a], axis=-1)
    assert tile % 256 == 0
    small = lambda n: pl.BlockSpec((1, n), lambda b, h: (0, 0))
    kernel = functools.partial(_attn_kernel, tile=tile, dh=dh, dv=w, nk=nq)
    return pl.pallas_call(
        kernel,
        out_shape=jax.ShapeDtypeStruct((t, d), BF16),
        grid=(batch, DIFF_HEADS),
        in_specs=[pl.BlockSpec((seq, w), lambda b, h: (b, q_col + h)),
                  pl.BlockSpec((seq, w), lambda b, h: (b, k_col + h)),
                  pl.BlockSpec((seq, w), lambda b, h: (b, v_col + h)),
                  pl.BlockSpec((seq, w), lambda b, h: (b, gate_col + h)),
                  pl.BlockSpec((1, 1, LANES), lambda b, h: (h, 0, 0)),
                  pl.BlockSpec((w, 1), lambda b, h: (0, 0)), pl.BlockSpec((w, 1), lambda b, h: (0, 0)),
                  small(w), small(w), small(w),
                  small(dh), small(dh), small(dh), small(dh)],
        out_specs=pl.BlockSpec((seq, w), lambda b, h: (b, h)),
        scratch_shapes=[pltpu.VMEM((nq, tile, w), BF16),
                        pltpu.VMEM((nq, tile, w), BF16),
                        pltpu.VMEM((nq, w + BF16_SUBLANES, tile), BF16),
                        pltpu.VMEM((nq, w, tile), BF16),
                        pltpu.VMEM((nq, w, tile), BF16),
                        pltpu.VMEM((2, tile, tile), F32),
                        pltpu.VMEM((2, tile, tile), F32),
                        pltpu.VMEM((2, tile, tile), F32),
                        pltpu.VMEM((2, w + BF16_SUBLANES, tile), F32),
                        pltpu.VMEM((2, 1, tile), F32)],
        compiler_params=_params("arbitrary", "arbitrary"),
        name="diff_attn",
    )(proj, proj, proj, proj, slopes, qconst_a, qconst_b, tile2(g_q), tile2(g_k), g_sub,
      lq1, lk1, lq2, lk2)


def _out_kernel(x_ref, ret_ref, att_ref, mod_ref, wo_ref, g2_ref, wr_ref, br_ref,
                x1_ref, h2_ref, tw_ref, ld_ref, cnt_ref, *, tm, n_exp):
    m = mod_ref[0]
    merged = (ret_ref[...].astype(F32) + att_ref[...].astype(F32)).astype(BF16)
    x1 = x_ref[...] + m[2:3] * jnp.dot(merged, wo_ref[...], preferred_element_type=F32)
    x1_ref[...] = x1
    h2 = _rms(x1) * g2_ref[...] * (1.0 + m[4:5]) + m[3:4]
    hi = h2.astype(BF16)
    h2_ref[...] = hi

    lo = (h2 - hi.astype(F32)).astype(BF16)
    both = jnp.dot(hi, wr_ref[...], preferred_element_type=F32)
    logits = (both[:, :LANES] + both[:, LANES:]
              + jnp.dot(lo, wr_ref[:, :LANES], preferred_element_type=F32) + br_ref[...])
    lt = logits.T[:n_exp]

    erow = lax.broadcasted_iota(I32, (n_exp, tm), 0)
    vals, sels = [], []
    for k in range(TOP_K):
        mx = jnp.max(lt, axis=0, keepdims=True)
        idx = jnp.min(jnp.where(lt == mx, erow, n_exp), axis=0, keepdims=True)
        sel = erow == idx
        vals.append(mx)
        sels.append(sel)
        lt = jnp.where(sel, MASK_VALUE, lt)
    exps = [jnp.exp(v - vals[0]) for v in vals]
    inv = 1.0 / (exps[0] + exps[1] + exps[2] + exps[3])
    for k in range(TOP_K):
        tw_ref[k:k + 1, :] = exps[k] * inv

    chosen = jnp.where(sels[0] | sels[1] | sels[2] | sels[3], 1.0, 0.0)
    before = (lax.broadcasted_iota(I32, (tm, tm), 0) < lax.broadcasted_iota(I32, (tm, tm), 1))
    prefix = jnp.dot(chosen.astype(BF16), jnp.where(before, 1.0, 0.0).astype(BF16),
                     preferred_element_type=F32)
    count = jnp.sum(chosen, axis=1, keepdims=True)
    cnt_ref[...] = count
    padded = jnp.ceil(count * (1.0 / RUN_ALIGN)) * RUN_ALIGN
    below = (lax.broadcasted_iota(I32, (n_exp, n_exp), 1) < lax.broadcasted_iota(I32, (n_exp, n_exp), 0))
    run_start = jnp.dot(jnp.where(below, 1.0, 0.0).astype(BF16),
                        jnp.broadcast_to(padded, (n_exp, LANES)).astype(BF16),
                        preferred_element_type=F32)[:, :1]
    pos = prefix + run_start
    for k in range(TOP_K):
        ld_ref[k:k + 1, :] = jnp.sum(jnp.where(sels[k], pos, 0.0), axis=0, keepdims=True).astype(I32)


def _out_router(x2, ret, att, mod3, w_out_bf16, g2, wr_split, br_pad, seq, n_exp):
    t, d = x2.shape
    tm = TOK_TM
    row = lambda i: (i, 0)
    const = lambda i: (0, 0)
    kernel = functools.partial(_out_kernel, tm=tm, n_exp=n_exp)
    return pl.pallas_call(
        kernel,
        out_shape=(jax.ShapeDtypeStruct((t, d), F32),
                   jax.ShapeDtypeStruct((t, d), BF16),
                   jax.ShapeDtypeStruct((TOP_K, t), F32),
                   jax.ShapeDtypeStruct((TOP_K, t), I32),
                   jax.ShapeDtypeStruct((t // tm, n_exp, 1), F32)),
        grid=(t // tm,),
        in_specs=[pl.BlockSpec((tm, d), row), pl.BlockSpec((tm, d), row), pl.BlockSpec((tm, d), row),
                  pl.BlockSpec((1, 6, d), lambda i: (i * tm // seq, 0, 0)),
                  pl.BlockSpec((d, d), const), pl.BlockSpec((1, d), const),
                  pl.BlockSpec((d, 2 * LANES), const), pl.BlockSpec((1, LANES), const)],
        out_specs=(pl.BlockSpec((tm, d), row), pl.BlockSpec((tm, d), row),
                   pl.BlockSpec((TOP_K, tm), lambda i: (0, i)),
                   pl.BlockSpec((TOP_K, tm), lambda i: (0, i)),
                   pl.BlockSpec((None, n_exp, 1), lambda i: (i, 0, 0))),
        compiler_params=_params("arbitrary"),
        name="out_router",
    )(x2, ret, att, mod3, w_out_bf16, g2, wr_split, br_pad)


def _meta_stride(n_exp):
    return 2 * len(SLAB_SIZES) * n_exp + len(SLAB_SIZES)


def _run_slabs(meta_ref, tile, n_exp, local_ref, global_ref, sem, *, to_global, wait):
    n_sizes = len(SLAB_SIZES)
    base = tile * _meta_stride(n_exp)
    for c, size in enumerate(SLAB_SIZES):
        def one(p, carry, c=c, size=size):
            if wait:
                local = glob = 0
            else:
                local = pl.multiple_of(meta_ref[base + c * n_exp + p], RUN_ALIGN)
                glob = pl.multiple_of(meta_ref[base + (n_sizes + c) * n_exp + p], RUN_ALIGN)
            loc = local_ref.at[pl.ds(local, size)]
            glo = global_ref.at[pl.ds(glob, size)]
            cp = pltpu.make_async_copy(loc, glo, sem) if to_global else pltpu.make_async_copy(glo, loc, sem)
            if wait:
                cp.wait()
            else:
                cp.start()
            return carry

        lax.fori_loop(0, meta_ref[base + 2 * n_sizes * n_exp + c], one, 0)


def _dispatch_kernel(meta_ref, tail_ref, ld_ref, h_ref, xs_ref, stage_ref, zero_ref, sems, tail_sem,
                     *, tm, n_exp, n_tiles, n_blocks, stage_rows):
    i = pl.program_id(0)
    slot = i & 1
    stage = stage_ref.at[slot]

    def tails(wait):
        def unused_block(b, carry):
            cp = pltpu.make_async_copy(
                zero_ref, xs_ref.at[pl.ds(pl.multiple_of(b * EXP_BLK, EXP_BLK), EXP_BLK)], tail_sem)
            if wait:
                cp.wait()
            else:
                cp.start()
            return carry

        lax.fori_loop(tail_ref[2 * n_exp] // EXP_BLK, n_blocks, unused_block, 0)

        def per_expert(e, carry):
            start, n = tail_ref[e], tail_ref[n_exp + e]
            off = 0
            for size in TAIL_SIZES:
                take = n & size

                @pl.when(take != 0)
                def _(off=off, size=size):
                    cp = pltpu.make_async_copy(
                        zero_ref.at[pl.ds(0, size)],
                        xs_ref.at[pl.ds(pl.multiple_of(start + off, RUN_ALIGN), size)], tail_sem)
                    if wait:
                        cp.wait()
                    else:
                        cp.start()

                off = off + take
            return carry

        lax.fori_loop(0, n_exp, per_expert, 0)

    @pl.when(i == 0)
    def _():
        zero_ref[...] = jnp.zeros_like(zero_ref)
        tails(False)
        tails(True)

    ld = ld_ref[...]
    h = h_ref[...]
    for c in range(stage_rows // GATHER_RC):
        r = lax.broadcasted_iota(I32, (GATHER_RC, tm), 0) + c * GATHER_RC
        hit = jnp.where(r == ld[0:1], 1.0, jnp.where(r == ld[1:2], 1.0,
                        jnp.where(r == ld[2:3], 1.0, jnp.where(r == ld[3:4], 1.0, 0.0))))
        stage[c * GATHER_RC:(c + 1) * GATHER_RC, :] = jnp.dot(
            hit.astype(BF16), h, preferred_element_type=F32).astype(BF16)

    _run_slabs(meta_ref, i, n_exp, stage, xs_ref, sems.at[slot], to_global=True, wait=False)

    @pl.when(i > 0)
    def _():
        _run_slabs(meta_ref, i - 1, n_exp, stage_ref.at[1 - slot], xs_ref, sems.at[1 - slot],
                   to_global=True, wait=True)

    @pl.when(i == n_tiles - 1)
    def _():
        _run_slabs(meta_ref, i, n_exp, stage, xs_ref, sems.at[slot], to_global=True, wait=True)


def _dispatch(meta, tail, ldest, h2, n_rows, n_exp):
    t, d = h2.shape
    tm = TOK_TM
    n_tiles = t // tm
    stage_rows = _stage_rows(n_exp)
    kernel = functools.partial(_dispatch_kernel, tm=tm, n_exp=n_exp, n_tiles=n_tiles,
                               n_blocks=n_rows // EXP_BLK, stage_rows=stage_rows)
    grid_spec = pltpu.PrefetchScalarGridSpec(
        num_scalar_prefetch=2,
        grid=(n_tiles,),
        in_specs=[pl.BlockSpec((TOP_K, tm), lambda i, m, tl: (0, i)),
                  pl.BlockSpec((tm, d), lambda i, m, tl: (i, 0))],
        out_specs=pl.BlockSpec(memory_space=pl.ANY),
        scratch_shapes=[pltpu.VMEM((2, stage_rows, d), BF16),
                        pltpu.VMEM((EXP_BLK, d), BF16),
                        pltpu.SemaphoreType.DMA((2,)),
                        pltpu.SemaphoreType.DMA],
    )
    return pl.pallas_call(
        kernel,
        out_shape=jax.ShapeDtypeStruct((n_rows, d), BF16),
        grid_spec=grid_spec,
        compiler_params=_params("arbitrary"),
        name="dispatch",
    )(meta, tail, ldest, h2)


def _expert_kernel(blk_e_ref, n_used_ref, x_ref, w1_ref, b1_ref, w2_ref, b2_ref, o_ref, w1b_ref, w2b_ref,
                   *, f):
    i = pl.program_id(0)
    new_expert = jnp.logical_or(i == 0, blk_e_ref[i] != blk_e_ref[jnp.maximum(i - 1, 0)])

    @pl.when(jnp.logical_and(i < n_used_ref[0], new_expert))
    def _():
        w1b_ref[...] = w1_ref[0].astype(BF16)
        w2b_ref[...] = w2_ref[0].astype(BF16)

    @pl.when(i < n_used_ref[0])
    def _():
        gu = jnp.dot(x_ref[...], w1b_ref[...], preferred_element_type=F32) + b1_ref[0]
        g = jnp.minimum(gu[:, :f], SWIGLU_LIMIT)
        u = jnp.clip(gu[:, f:], -SWIGLU_LIMIT, SWIGLU_LIMIT)
        a = g * _sigmoid(SWIGLU_ALPHA * g) * (u + 1.0)
        o_ref[...] = (jnp.dot(a.astype(BF16), w2b_ref[...], preferred_element_type=F32)
                      + b2_ref[0]).astype(o_ref.dtype)

    @pl.when(i >= n_used_ref[0])
    def _():
        o_ref[...] = jnp.zeros_like(o_ref)


def _experts(blk_e, n_used, xs, w1, b1, w2, b2):
    n_rows, d = xs.shape
    n_exp, _, f2 = w1.shape
    f = f2 // 2
    blk = EXP_BLK
    kernel = functools.partial(_expert_kernel, f=f)
    grid_spec = pltpu.PrefetchScalarGridSpec(
        num_scalar_prefetch=2,
        grid=(n_rows // blk,),
        in_specs=[pl.BlockSpec((blk, d), lambda i, e, n: (jnp.minimum(i, n[0] - 1), 0)),
                  pl.BlockSpec((1, d, f2), lambda i, e, n: (e[i], 0, 0)),
                  pl.BlockSpec((1, 1, f2), lambda i, e, n: (e[i], 0, 0)),
                  pl.BlockSpec((1, f, d), lambda i, e, n: (e[i], 0, 0)),
                  pl.BlockSpec((1, 1, d), lambda i, e, n: (e[i], 0, 0))],
        out_specs=pl.BlockSpec((blk, d), lambda i, e, n: (i, 0)),
        scratch_shapes=[pltpu.VMEM((d, f2), BF16), pltpu.VMEM((f, d), BF16)],
    )
    return pl.pallas_call(
        kernel,
        out_shape=jax.ShapeDtypeStruct((n_rows, d), BF16),
        grid_spec=grid_spec,
        compiler_params=_params("arbitrary"),
        name="experts",
    )(blk_e, n_used, xs, w1, b1, w2, b2)


def _combine_kernel(meta_ref, eo_ref, ld_ref, w_ref, x1_ref, mod_ref, o_ref, stage_ref, sems,
                    *, tm, n_exp, n_tiles, stage_rows):
    i = pl.program_id(0)
    slot = i & 1

    @pl.when(i == 0)
    def _():
        stage_ref[...] = jnp.zeros_like(stage_ref)
        _run_slabs(meta_ref, 0, n_exp, stage_ref.at[0], eo_ref, sems.at[0], to_global=False, wait=False)

    @pl.when(i + 1 < n_tiles)
    def _():
        _run_slabs(meta_ref, i + 1, n_exp, stage_ref.at[1 - slot], eo_ref, sems.at[1 - slot],
                   to_global=False, wait=False)

    ld = ld_ref[...]
    w = w_ref[...]
    ldb = [jnp.broadcast_to(ld[:, k:k + 1], (tm, SCATTER_RC)) for k in range(TOP_K)]
    wb = [jnp.broadcast_to(w[:, k:k + 1], (tm, SCATTER_RC)) for k in range(TOP_K)]
    col = lax.broadcasted_iota(I32, (tm, SCATTER_RC), 1)

    def sel_piece(r0):
        r = col + r0
        sel = jnp.where(ldb[0] == r, wb[0], jnp.where(ldb[1] == r, wb[1],
                        jnp.where(ldb[2] == r, wb[2], jnp.where(ldb[3] == r, wb[3], 0.0))))
        return sel.astype(BF16)

    _run_slabs(meta_ref, i, n_exp, stage_ref.at[slot], eo_ref, sems.at[slot], to_global=False, wait=True)
    stage = stage_ref.at[slot]
    y = None
    for c in range(stage_rows // COMBINE_KC):
        r0 = c * COMBINE_KC
        sel = jnp.concatenate([sel_piece(r0 + s) for s in range(0, COMBINE_KC, SCATTER_RC)], axis=1)
        part = jnp.dot(sel, stage[r0:r0 + COMBINE_KC, :], preferred_element_type=F32)
        y = part if y is None else y + part
    o_ref[...] = x1_ref[...] + mod_ref[0][5:6] * y


def _combine(meta, eo, ld_tok, w_tok, x1, mod3, seq, n_exp):
    t, d = x1.shape
    tm = TOK_TM
    n_tiles = t // tm
    stage_rows = _stage_rows(n_exp)
    kernel = functools.partial(_combine_kernel, tm=tm, n_exp=n_exp, n_tiles=n_tiles, stage_rows=stage_rows)
    grid_spec = pltpu.PrefetchScalarGridSpec(
        num_scalar_prefetch=1,
        grid=(n_tiles,),
        in_specs=[pl.BlockSpec(memory_space=pl.ANY),
                  pl.BlockSpec((tm, TOP_K), lambda i, m: (i, 0)),
                  pl.BlockSpec((tm, TOP_K), lambda i, m: (i, 0)),
                  pl.BlockSpec((tm, d), lambda i, m: (i, 0)),
                  pl.BlockSpec((1, 6, d), lambda i, m: (i * tm // seq, 0, 0))],
        out_specs=pl.BlockSpec((tm, d), lambda i, m: (i, 0)),
        scratch_shapes=[pltpu.VMEM((2, stage_rows, d), BF16),
                        pltpu.SemaphoreType.DMA((2,))],
    )
    return pl.pallas_call(
        kernel,
        out_shape=jax.ShapeDtypeStruct((t, d), F32),
        grid_spec=grid_spec,
        compiler_params=_params("arbitrary"),
        name="combine",
    )(meta, eo, ld_tok, w_tok, x1, mod3)


def _routing_tables(tile_cnt, n_tokens):
    n_tiles, n_exp = tile_cnt.shape
    blk = EXP_BLK
    run = _round_up(tile_cnt, RUN_ALIGN)
    local = jnp.cumsum(run, axis=1) - run
    size = jnp.sum(run, axis=0)
    region = _round_up(size, blk)
    region_end = jnp.cumsum(region)
    region_start = region_end - region
    glob = region_start[None, :] + jnp.cumsum(run, axis=0) - run
    sizes = jnp.asarray(SLAB_SIZES, I32)
    has = (run[:, :, None] & sizes) != 0
    before = run[:, :, None] & ~(2 * sizes - 1)
    place = jnp.cumsum(has, axis=1) - has
    hit = has[..., None] & (place[..., None] == jnp.arange(n_exp, dtype=I32))
    listed = lambda start: jnp.sum(
        jnp.where(hit, (start[:, :, None] + before)[..., None], 0), axis=1).reshape(n_tiles, -1)
    meta = jnp.concatenate([listed(local), listed(glob), jnp.sum(has, axis=1)],
                           axis=1).reshape(-1).astype(I32)
    tail = jnp.concatenate([region_start + size, region - size, region_end[-1:]]).astype(I32)
    n_rows = _round_up(n_tokens * TOP_K + n_tiles * n_exp * RUN_ALIGN + n_exp * blk, blk)
    block_row = jnp.arange(n_rows // blk, dtype=I32) * blk
    n_used = (region_end[-1:] // blk).astype(I32)
    blk_e = jnp.sum((region_end[None, :] <= block_row[:, None]).astype(I32), axis=1)
    last_e = jnp.sum((region_end[:-1] < region_end[-1]).astype(I32))
    blk_e = jnp.minimum(blk_e, last_e).astype(I32)
    return meta, tail, blk_e, n_used, n_rows


def kernel(x, c, w_ada, b_ada, g_norm1, w_in, g_ret_gn, b_ret_gn, g_qnorm, g_knorm,
           lambda_q1, lambda_k1, lambda_q2, lambda_k2, g_diff_subln, w_out, g_norm2,
           w_router, b_router, w_expert_in, b_expert_in, w_expert_out, b_expert_out):
    batch, seq, d = x.shape
    depth = w_ada.shape[0]
    assert depth == 1
    t = batch * seq
    n_exp = w_router.shape[-1]
    l = 0

    mod = _adaln_mod(c, w_ada[l], b_ada[l][None, :])
    mod3 = mod.reshape(batch, 6, d)
    x2 = x.reshape(t, d)

    proj = _in_proj(x2, mod3, g_norm1[l][None, :], w_in[l].astype(BF16), seq)
    ret = _retention(proj, g_ret_gn[l][None, :], b_ret_gn[l][None, :], batch, seq, d)
    att = _diff_attention(proj, g_qnorm[l][None, :], g_knorm[l][None, :], g_diff_subln[l][None, :],
                          lambda_q1[l][None, :], lambda_k1[l][None, :],
                          lambda_q2[l][None, :], lambda_k2[l][None, :], batch, seq, d)

    wr = w_router[l]
    wr_hi = wr.astype(BF16)
    wr_lo = (wr - wr_hi.astype(F32)).astype(BF16)
    pad = lambda a: jnp.pad(a, ((0, 0), (0, LANES - n_exp)))
    wr_split = jnp.concatenate([pad(wr_hi), pad(wr_lo)], axis=1)
    br_pad = jnp.pad(b_router[l][None, :], ((0, 0), (0, LANES - n_exp)), constant_values=MASK_VALUE)
    x1, h2, top_w, ldest, tile_cnt = _out_router(
        x2, ret, att, mod3, w_out[l].astype(BF16), g_norm2[l][None, :], wr_split, br_pad, seq, n_exp)

    meta, tail, blk_e, n_used, n_rows = _routing_tables(tile_cnt[:, :, 0].astype(I32), t)
    xs = _dispatch(meta, tail, ldest, h2, n_rows, n_exp)
    eo = _experts(blk_e, n_used, xs, w_expert_in[l], b_expert_in[l][:, None, :],
                  w_expert_out[l], b_expert_out[l][:, None, :])
    out = _combine(meta, eo, ldest.T, top_w.T, x1, mod3, seq, n_exp)
    return out.reshape(batch, seq, d)
```

```python
import functools
import math

import jax
import jax.numpy as jnp
from jax import lax
from jax.experimental import pallas as pl
from jax.experimental.pallas import tpu as pltpu

F32 = jnp.float32
BF16 = jnp.bfloat16
I32 = jnp.int32

EPS = 1e-6
LOG2E = 1.4426950216293335
MASK_VALUE = -1e30
LANES = 128
BF16_SUBLANES = 16
VMEM_LIMIT_BYTES = 48 * 1024 * 1024

RET_HEADS = 4
DIFF_HEADS = 8
TOP_K = 4
SWIGLU_LIMIT = 7.0
SWIGLU_ALPHA = 1.702
LAMBDA_INIT = 0.8 - 0.6 * math.exp(-0.3 * 0)

IN_TM, IN_TN = 1024, 2048
RET_TC = 256
ATT_T = 512
TOK_TM = 512
EXP_BLK = 256
RUN_ALIGN = BF16_SUBLANES
GATHER_RC = 256
SCATTER_RC = 128
COMBINE_KC = 512


def _round_up(n, m):
    return (n + m - 1) // m * m


def _stage_rows(n_exp):
    return _round_up(TOK_TM * TOP_K + n_exp * (RUN_ALIGN - 1), max(GATHER_RC, COMBINE_KC))


SLAB_SIZES = tuple(TOK_TM >> s for s in range(TOK_TM.bit_length()) if TOK_TM >> s >= RUN_ALIGN)
TAIL_SIZES = tuple(s for s in SLAB_SIZES if s < EXP_BLK)


def _params(*sem):
    return pltpu.CompilerParams(dimension_semantics=sem, vmem_limit_bytes=VMEM_LIMIT_BYTES)


def _sigmoid(x):
    return 0.5 * jnp.tanh(0.5 * x) + 0.5


def _rms(x, axis=-1):
    return x * lax.rsqrt(jnp.mean(x * x, axis=axis, keepdims=True) + EPS)


def _mod_kernel(c_ref, w_ref, b_ref, o_ref):
    c = c_ref[...]
    s = c * _sigmoid(c)
    o_ref[...] = jnp.dot(s, w_ref[...], preferred_element_type=F32,
                         precision=lax.Precision.HIGHEST) + b_ref[...]


def _adaln_mod(c, w, b):
    bn, d = c.shape
    n = w.shape[1]
    tn = d
    return pl.pallas_call(
        _mod_kernel,
        out_shape=jax.ShapeDtypeStruct((bn, n), F32),
        grid=(n // tn,),
        in_specs=[pl.BlockSpec((bn, d), lambda j: (0, 0)),
                  pl.BlockSpec((d, tn), lambda j: (0, j)),
                  pl.BlockSpec((1, tn), lambda j: (0, j))],
        out_specs=pl.BlockSpec((bn, tn), lambda j: (0, j)),
        compiler_params=_params("arbitrary"),
        name="adaln_mod",
    )(c, w, b)


def _inproj_kernel(x_ref, mod_ref, g_ref, w_ref, o_ref, h_ref):
    @pl.when(pl.program_id(1) == 0)
    def _():
        m = mod_ref[0]
        h = _rms(x_ref[...]) * g_ref[...] * (1.0 + m[1:2]) + m[0:1]
        h_ref[...] = h.astype(BF16)

    o_ref[...] = jnp.dot(h_ref[...], w_ref[...], preferred_element_type=F32).astype(o_ref.dtype)


def _in_proj(x2, mod3, g1, w_in_bf16, seq):
    t, d = x2.shape
    n = w_in_bf16.shape[1]
    tm, tn = IN_TM, IN_TN
    return pl.pallas_call(
        _inproj_kernel,
        out_shape=jax.ShapeDtypeStruct((t, n), BF16),
        grid=(t // tm, n // tn),
        in_specs=[pl.BlockSpec((tm, d), lambda i, j: (i, 0)),
                  pl.BlockSpec((1, 6, d), lambda i, j: (i * tm // seq, 0, 0)),
                  pl.BlockSpec((1, d), lambda i, j: (0, 0)),
                  pl.BlockSpec((d, tn), lambda i, j: (0, j))],
        out_specs=pl.BlockSpec((tm, tn), lambda i, j: (i, j)),
        scratch_shapes=[pltpu.VMEM((tm, d), BF16)],
        compiler_params=_params("arbitrary", "arbitrary"),
        name="in_proj",
    )(x2, mod3, g1, w_in_bf16)


def _ret_kernel(q_ref, k_ref, v_ref, rg_ref, mg_ref, gng_ref, gnb_ref, o_ref, r_ref, decay_ref,
                *, dk, dv, tc):
    scale = dk ** -0.5
    log_gs = [math.log(1.0 - 2.0 ** (-5.0 - h)) for h in range(RET_HEADS)]

    @pl.when(pl.program_id(1) == 0)
    def _():
        r_ref[...] = jnp.zeros_like(r_ref)
        row = lax.broadcasted_iota(I32, (tc, tc), 0)
        col = lax.broadcasted_iota(I32, (tc, tc), 1)
        rel = (row - col).astype(F32)
        for h, log_g in enumerate(log_gs):
            decay_ref[h] = jnp.where(rel >= 0, jnp.exp(log_g * jnp.maximum(rel, 0.0)), 0.0) * scale

    pos = lax.broadcasted_iota(I32, (tc, 1), 0).astype(F32)
    for h, log_g in enumerate(log_gs):
        decay = decay_ref[h]
        xi = jnp.exp(log_g * (pos + 1.0))
        zeta = jnp.exp(log_g * (tc - 1.0 - pos))
        g_chunk = math.exp(log_g * tc)
        q = q_ref[:, h * dk:(h + 1) * dk]
        k = k_ref[:, h * dk:(h + 1) * dk]
        v = v_ref[:, h * dv:(h + 1) * dv]
        s = lax.dot_general(q, k, (((1,), (1,)), ((), ())), preferred_element_type=F32)
        y = jnp.dot((s * decay).astype(BF16), v, preferred_element_type=F32)
        state = r_ref[h]
        y = y + jnp.dot(q, state.astype(BF16), preferred_element_type=F32) * xi
        vz = (v.astype(F32) * zeta).astype(BF16)
        kv = lax.dot_general(k, vz, (((0,), (0,)), ((), ())), preferred_element_type=F32)
        r_ref[h] = state * g_chunk + kv * scale

        mu = jnp.mean(y, axis=-1, keepdims=True)
        yc = y - mu
        var = jnp.mean(yc * yc, axis=-1, keepdims=True)
        sl = slice(h * dv, (h + 1) * dv)
        yn = yc * lax.rsqrt(var + EPS) * gng_ref[:, sl] + gnb_ref[:, sl]
        rg = rg_ref[:, sl].astype(F32)
        gate = _sigmoid(mg_ref[:, sl].astype(F32))
        o_ref[:, sl] = (rg * _sigmoid(rg) * yn * gate).astype(o_ref.dtype)


def _retention(proj, gn_g, gn_b, batch, seq, d):
    t = proj.shape[0]
    tc = RET_TC
    nc = seq // tc
    dk = d // (2 * RET_HEADS)
    dv = d // RET_HEADS
    qk_w = RET_HEADS * dk
    row = lambda b, c: b * nc + c
    kernel = functools.partial(_ret_kernel, dk=dk, dv=dv, tc=tc)
    return pl.pallas_call(
        kernel,
        out_shape=jax.ShapeDtypeStruct((t, d), BF16),
        grid=(batch, nc),
        in_specs=[pl.BlockSpec((tc, qk_w), lambda b, c: (row(b, c), 0)),
                  pl.BlockSpec((tc, qk_w), lambda b, c: (row(b, c), 1)),
                  pl.BlockSpec((tc, d), lambda b, c: (row(b, c), 1)),
                  pl.BlockSpec((tc, d), lambda b, c: (row(b, c), 2)),
                  pl.BlockSpec((tc, d), lambda b, c: (row(b, c), 6)),
                  pl.BlockSpec((1, d), lambda b, c: (0, 0)),
                  pl.BlockSpec((1, d), lambda b, c: (0, 0))],
        out_specs=pl.BlockSpec((tc, d), lambda b, c: (row(b, c), 0)),
        scratch_shapes=[pltpu.VMEM((RET_HEADS, dk, dv), F32), pltpu.VMEM((RET_HEADS, tc, tc), F32)],
        compiler_params=_params("arbitrary", "arbitrary"),
        name="retention",
    )(proj, proj, proj, proj, proj, gn_g, gn_b)


def _halves_rms(x, lo, dh):
    sq = x * x
    s_lo = jnp.sum(jnp.where(lo, sq, 0.0), axis=-1, keepdims=True)
    s_hi = jnp.sum(jnp.where(lo, 0.0, sq), axis=-1, keepdims=True)
    inv = jnp.where(lo, lax.rsqrt(s_lo / dh + EPS), lax.rsqrt(s_hi / dh + EPS))
    return x * inv


def _attn_kernel(q_ref, k_ref, v_ref, gate_ref, slope_ref, qca_ref, qcb_ref, gq_ref, gk_ref, gsub_ref,
                 lq1_ref, lk1_ref, lq2_ref, lk2_ref,
                 o_ref, ka_ref, kb_ref, vt_ref, qa_ref, qb_ref, s0_ref, sa_ref, sb_ref, acc_ref, m_ref,
                 *, tile, dh, dv, nk):
    lane = lax.broadcasted_iota(I32, (tile, 2 * dh), 1)
    lo = lane < dh

    slope = slope_ref[0]
    sub = lax.broadcasted_iota(I32, (tile, 2 * dh), 0)
    off = lane & (dh - 1)
    hi_lane = (off < 6) & ((off & 1) == 0)
    lo_lane = (off < 6) & ((off & 1) == 1)
    bias0 = jnp.where(hi_lane, ((sub >> 8) << 8).astype(F32),
                      jnp.where(lo_lane, (sub & 255).astype(F32), 0.0)) * slope
    bias_step = jnp.where(hi_lane, float(tile), 0.0) * slope
    ones_rows = jnp.where(lax.broadcasted_iota(I32, (BF16_SUBLANES, tile), 0) == 0, 1.0, 0.0)
    lo_t = lax.broadcasted_iota(I32, (2 * dh, tile), 0) < dh

    def prepare(c, carry):
        r0 = c * tile if isinstance(c, int) else pl.multiple_of(c * tile, tile)
        kn = _halves_rms(k_ref[pl.ds(r0, tile), :].astype(F32), lo, dh) * gk_ref[...]
        bias = bias0 + lax.convert_element_type(c, F32) * bias_step
        ka_ref[c] = jnp.where(lo, kn, bias).astype(BF16)
        kb_ref[c] = jnp.where(lo, bias, kn).astype(BF16)
        vt_ref[c, :dv, :] = v_ref[pl.ds(r0, tile), :].astype(F32).T.astype(BF16)
        vt_ref[c, dv:, :] = ones_rows.astype(BF16)
        qn = (_halves_rms(q_ref[pl.ds(r0, tile), :].astype(F32), lo, dh) * gq_ref[...]
              * (dh ** -0.5 * LOG2E))
        qn_t = qn.T
        qa_ref[c] = jnp.where(lo_t, qn_t, qca_ref[...]).astype(BF16)
        qb_ref[c] = jnp.where(lo_t, qcb_ref[...], qn_t).astype(BF16)
        return carry

    lax.fori_loop(0, min(2, nk), prepare, 0)
    lam = (jnp.exp(jnp.sum(lq1_ref[...] * lk1_ref[...], axis=-1, keepdims=True))
           - jnp.exp(jnp.sum(lq2_ref[...] * lk2_ref[...], axis=-1, keepdims=True)) + LAMBDA_INIT)

    refs = (gate_ref, gsub_ref, o_ref, ka_ref, kb_ref, vt_ref, qa_ref, qb_ref,
            s0_ref, sa_ref, sb_ref, acc_ref, m_ref)
    s0_ref[0] = jnp.dot(ka_ref[0], qa_ref[0], preferred_element_type=F32)
    s0_ref[1] = jnp.dot(kb_ref[0], qb_ref[0], preferred_element_type=F32)
    _attn_q_tile(0, lam, prepare, *refs, tile=tile, dv=dv, nk=nk)

    def q_tile(qi, carry):
        _attn_q_tile(qi, lam, prepare, *refs, tile=tile, dv=dv, nk=nk)
        return carry

    lax.fori_loop(1, nk, q_tile, 0)


def _attn_q_tile(qi, lam, prepare, gate_ref, gsub_ref, o_ref, ka_ref, kb_ref, vt_ref, qa_ref, qb_ref,
                 s0_ref, sa_ref, sb_ref, acc_ref, m_ref, *, tile, dv, nk):
    def scores(jb, s_ref, q=qi):
        s_ref[0] = jnp.dot(ka_ref[jb], qa_ref[q], preferred_element_type=F32)
        s_ref[1] = jnp.dot(kb_ref[jb], qb_ref[q], preferred_element_type=F32)

    def absorb(jb, s_ref, masked):
        vt = vt_ref[jb]
        for idx in range(2):
            s = s_ref[idx]
            if masked:
                krow = lax.broadcasted_iota(I32, (tile, tile), 0)
                qcol = lax.broadcasted_iota(I32, (tile, tile), 1)
                s = jnp.where(krow > qcol, MASK_VALUE, s)
            m_old = m_ref[idx]
            m_new = jnp.maximum(m_old, jnp.max(s, axis=0, keepdims=True))
            m_ref[idx] = m_new
            p = jnp.exp2(s - m_new).astype(BF16)
            acc_ref[idx] = (acc_ref[idx] * jnp.exp2(m_old - m_new)
                            + jnp.dot(vt, p, preferred_element_type=F32))

    acc_ref[...] = jnp.zeros_like(acc_ref)
    m_ref[...] = jnp.full_like(m_ref, MASK_VALUE)
    if isinstance(qi, int):
        absorb(0, s0_ref, True)
    else:
        scores(1, sa_ref)
        absorb(0, s0_ref, False)

        def pair(j2, carry):
            jb = 2 * j2 + 1
            scores(jb + 1, sb_ref)
            absorb(jb, sa_ref, False)
            scores(jb + 2, sa_ref)
            absorb(jb + 1, sb_ref, False)
            return carry

        lax.fori_loop(0, (qi - 1) // 2, pair, 0)

        @pl.when(qi % 2 == 0)
        def _():
            scores(qi, sb_ref)
            absorb(qi - 1, sa_ref, False)
            absorb(qi, sb_ref, True)

        @pl.when(qi % 2 == 1)
        def _():
            absorb(qi, sa_ref, True)

    ahead = (lambda n: min(qi + n, nk - 1)) if isinstance(qi, int) else (lambda n: jnp.minimum(qi + n, nk - 1))
    scores(0, s0_ref, q=ahead(1))
    prepare(ahead(2), 0)
    a1, a2 = acc_ref[0], acc_ref[1]
    o_t = a1[:dv] * (1.0 / a1[dv:dv + 1]) - lam * (a2[:dv] * (1.0 / a2[dv:dv + 1]))
    o = _rms(o_t, axis=0).T * gsub_ref[...] * (1.0 - LAMBDA_INIT)
    rows = pl.ds(pl.multiple_of(qi * tile, tile), tile)
    o_ref[rows, :] = (o * _sigmoid(gate_ref[rows, :].astype(F32))).astype(o_ref.dtype)


def _bf16_terms(x, n):
    terms, rest = [], jnp.asarray(x, F32)
    for _ in range(n):
        term = rest.astype(BF16).astype(F32)
        terms.append(term)
        rest = rest - term
    return terms


def _diff_attention(proj, g_q, g_k, g_sub, lq1, lk1, lq2, lk2, batch, seq, d):
    t = proj.shape[0]
    tile = ATT_T
    nq = seq // tile
    dh = d // (2 * DIFF_HEADS)
    w = 2 * dh
    assert w == LANES
    q_col, k_col, v_col, gate_col = 3 * d // w, 4 * d // w, 5 * d // w, 7 * d // w
    slopes = 2.0 ** -(jnp.arange(DIFF_HEADS, dtype=F32) + 1.0)
    slopes = jnp.broadcast_to(slopes[:, None, None], (DIFF_HEADS, 1, LANES))
    c1, c2, c3 = _bf16_terms(LOG2E, 3)
    qconst = jnp.zeros((dh,), F32).at[:6].set(jnp.stack([c1, c1, c2, c2, c3, c3]))
    zeros = jnp.zeros((dh,), F32)
    qconst_a = jnp.concatenate([zeros, qconst])[:, None]
    qconst_b = jnp.concatenate([qconst, zeros])[:, None]
    tile2 = lambda a: jnp.concatenate([a, a], axis=-1)
    assert tile % 256 == 0
    small = lambda n: pl.BlockSpec((1, n), lambda b, h: (0, 0))
    kernel = functools.partial(_attn_kernel, tile=tile, dh=dh, dv=w, nk=nq)
    return pl.pallas_call(
        kernel,
        out_shape=jax.ShapeDtypeStruct((t, d), BF16),
        grid=(batch, DIFF_HEADS),
        in_specs=[pl.BlockSpec((seq, w), lambda b, h: (b, q_col + h)),
                  pl.BlockSpec((seq, w), lambda b, h: (b, k_col + h)),
                  pl.BlockSpec((seq, w), lambda b, h: (b, v_col + h)),
                  pl.BlockSpec((seq, w), lambda b, h: (b, gate_col + h)),
                  pl.BlockSpec((1, 1, LANES), lambda b, h: (h, 0, 0)),
                  pl.BlockSpec((w, 1), lambda b, h: (0, 0)), pl.BlockSpec((w, 1), lambda b, h: (0, 0)),
                  small(w), small(w), small(w),
                  small(dh), small(dh), small(dh), small(dh)],
        out_specs=pl.BlockSpec((seq, w), lambda b, h: (b, h)),
        scratch_shapes=[pltpu.VMEM((nq, tile, w), BF16),
                        pltpu.VMEM((nq, tile, w), BF16),
                        pltpu.VMEM((nq, w + BF16_SUBLANES, tile), BF16),
                        pltpu.VMEM((nq, w, tile), BF16),
                        pltpu.VMEM((nq, w, tile), BF16),
                        pltpu.VMEM((2, tile, tile), F32),
                        pltpu.VMEM((2, tile, tile), F32),
                        pltpu.VMEM((2, tile, tile), F32),
                        pltpu.VMEM((2, w + BF16_SUBLANES, tile), F32),
                        pltpu.VMEM((2, 1, tile), F32)],
        compiler_params=_params("arbitrary", "arbitrary"),
        name="diff_attn",
    )(proj, proj, proj, proj, slopes, qconst_a, qconst_b, tile2(g_q), tile2(g_k), g_sub,
      lq1, lk1, lq2, lk2)


def _out_kernel(x_ref, ret_ref, att_ref, mod_ref, wo_ref, g2_ref, wr_ref, br_ref,
                x1_ref, h2_ref, tw_ref, ld_ref, cnt_ref, *, tm, n_exp):
    m = mod_ref[0]
    merged = (ret_ref[...].astype(F32) + att_ref[...].astype(F32)).astype(BF16)
    x1 = x_ref[...] + m[2:3] * jnp.dot(merged, wo_ref[...], preferred_element_type=F32)
    x1_ref[...] = x1
    h2 = _rms(x1) * g2_ref[...] * (1.0 + m[4:5]) + m[3:4]
    hi = h2.astype(BF16)
    h2_ref[...] = hi

    lo = (h2 - hi.astype(F32)).astype(BF16)
    both = jnp.dot(hi, wr_ref[...], preferred_element_type=F32)
    logits = (both[:, :LANES] + both[:, LANES:]
              + jnp.dot(lo, wr_ref[:, :LANES], preferred_element_type=F32) + br_ref[...])
    lt = logits.T[:n_exp]

    erow = lax.broadcasted_iota(I32, (n_exp, tm), 0)
    vals, sels = [], []
    for k in range(TOP_K):
        mx = jnp.max(lt, axis=0, keepdims=True)
        idx = jnp.min(jnp.where(lt == mx, erow, n_exp), axis=0, keepdims=True)
        sel = erow == idx
        vals.append(mx)
        sels.append(sel)
        lt = jnp.where(sel, MASK_VALUE, lt)
    exps = [jnp.exp(v - vals[0]) for v in vals]
    inv = 1.0 / (exps[0] + exps[1] + exps[2] + exps[3])
    for k in range(TOP_K):
        tw_ref[k:k + 1, :] = exps[k] * inv

    chosen = jnp.where(sels[0] | sels[1] | sels[2] | sels[3], 1.0, 0.0)
    before = (lax.broadcasted_iota(I32, (tm, tm), 0) < lax.broadcasted_iota(I32, (tm, tm), 1))
    prefix = jnp.dot(chosen.astype(BF16), jnp.where(before, 1.0, 0.0).astype(BF16),
                     preferred_element_type=F32)
    count = jnp.sum(chosen, axis=1, keepdims=True)
    cnt_ref[...] = count
    padded = jnp.ceil(count * (1.0 / RUN_ALIGN)) * RUN_ALIGN
    below = (lax.broadcasted_iota(I32, (n_exp, n_exp), 1) < lax.broadcasted_iota(I32, (n_exp, n_exp), 0))
    run_start = jnp.dot(jnp.where(below, 1.0, 0.0).astype(BF16),
                        jnp.broadcast_to(padded, (n_exp, LANES)).astype(BF16),
                        preferred_element_type=F32)[:, :1]
    pos = prefix + run_start
    for k in range(TOP_K):
        ld_ref[k:k + 1, :] = jnp.sum(jnp.where(sels[k], pos, 0.0), axis=0, keepdims=True).astype(I32)


def _out_router(x2, ret, att, mod3, w_out_bf16, g2, wr_split, br_pad, seq, n_exp):
    t, d = x2.shape
    tm = TOK_TM
    row = lambda i: (i, 0)
    const = lambda i: (0, 0)
    kernel = functools.partial(_out_kernel, tm=tm, n_exp=n_exp)
    return pl.pallas_call(
        kernel,
        out_shape=(jax.ShapeDtypeStruct((t, d), F32),
                   jax.ShapeDtypeStruct((t, d), BF16),
                   jax.ShapeDtypeStruct((TOP_K, t), F32),
                   jax.ShapeDtypeStruct((TOP_K, t), I32),
                   jax.ShapeDtypeStruct((t // tm, n_exp, 1), F32)),
        grid=(t // tm,),
        in_specs=[pl.BlockSpec((tm, d), row), pl.BlockSpec((tm, d), row), pl.BlockSpec((tm, d), row),
                  pl.BlockSpec((1, 6, d), lambda i: (i * tm // seq, 0, 0)),
                  pl.BlockSpec((d, d), const), pl.BlockSpec((1, d), const),
                  pl.BlockSpec((d, 2 * LANES), const), pl.BlockSpec((1, LANES), const)],
        out_specs=(pl.BlockSpec((tm, d), row), pl.BlockSpec((tm, d), row),
                   pl.BlockSpec((TOP_K, tm), lambda i: (0, i)),
                   pl.BlockSpec((TOP_K, tm), lambda i: (0, i)),
                   pl.BlockSpec((None, n_exp, 1), lambda i: (i, 0, 0))),
        compiler_params=_params("arbitrary"),
        name="out_router",
    )(x2, ret, att, mod3, w_out_bf16, g2, wr_split, br_pad)


def _meta_stride(n_exp):
    return 2 * len(SLAB_SIZES) * n_exp + len(SLAB_SIZES)


def _run_slabs(meta_ref, tile, n_exp, local_ref, global_ref, sem, *, to_global, wait):
    n_sizes = len(SLAB_SIZES)
    base = tile * _meta_stride(n_exp)
    for c, size in enumerate(SLAB_SIZES):
        def one(p, carry, c=c, size=size):
            if wait:
                local = glob = 0
            else:
                local = pl.multiple_of(meta_ref[base + c * n_exp + p], RUN_ALIGN)
                glob = pl.multiple_of(meta_ref[base + (n_sizes + c) * n_exp + p], RUN_ALIGN)
            loc = local_ref.at[pl.ds(local, size)]
            glo = global_ref.at[pl.ds(glob, size)]
            cp = pltpu.make_async_copy(loc, glo, sem) if to_global else pltpu.make_async_copy(glo, loc, sem)
            if wait:
                cp.wait()
            else:
                cp.start()
            return carry

        lax.fori_loop(0, meta_ref[base + 2 * n_sizes * n_exp + c], one, 0)


def _dispatch_kernel(meta_ref, tail_ref, ld_ref, h_ref, xs_ref, stage_ref, zero_ref, sems, tail_sem,
                     *, tm, n_exp, n_tiles, n_blocks, stage_rows):
    i = pl.program_id(0)
    slot = i & 1
    stage = stage_ref.at[slot]

    def tails(wait):
        def unused_block(b, carry):
            cp = pltpu.make_async_copy(
                zero_ref, xs_ref.at[pl.ds(pl.multiple_of(b * EXP_BLK, EXP_BLK), EXP_BLK)], tail_sem)
            if wait:
                cp.wait()
            else:
                cp.start()
            return carry

        lax.fori_loop(tail_ref[2 * n_exp] // EXP_BLK, n_blocks, unused_block, 0)

        def per_expert(e, carry):
            start, n = tail_ref[e], tail_ref[n_exp + e]
            off = 0
            for size in TAIL_SIZES:
                take = n & size

                @pl.when(take != 0)
                def _(off=off, size=size):
                    cp = pltpu.make_async_copy(
                        zero_ref.at[pl.ds(0, size)],
                        xs_ref.at[pl.ds(pl.multiple_of(start + off, RUN_ALIGN), size)], tail_sem)
                    if wait:
                        cp.wait()
                    else:
                        cp.start()

                off = off + take
            return carry

        lax.fori_loop(0, n_exp, per_expert, 0)

    @pl.when(i == 0)
    def _():
        zero_ref[...] = jnp.zeros_like(zero_ref)
        tails(False)
        tails(True)

    ld = ld_ref[...]
    h = h_ref[...]
    for c in range(stage_rows // GATHER_RC):
        r = lax.broadcasted_iota(I32, (GATHER_RC, tm), 0) + c * GATHER_RC
        hit = jnp.where(r == ld[0:1], 1.0, jnp.where(r == ld[1:2], 1.0,
                        jnp.where(r == ld[2:3], 1.0, jnp.where(r == ld[3:4], 1.0, 0.0))))
        stage[c * GATHER_RC:(c + 1) * GATHER_RC, :] = jnp.dot(
            hit.astype(BF16), h, preferred_element_type=F32).astype(BF16)

    _run_slabs(meta_ref, i, n_exp, stage, xs_ref, sems.at[slot], to_global=True, wait=False)

    @pl.when(i > 0)
    def _():
        _run_slabs(meta_ref, i - 1, n_exp, stage_ref.at[1 - slot], xs_ref, sems.at[1 - slot],
                   to_global=True, wait=True)

    @pl.when(i == n_tiles - 1)
    def _():
        _run_slabs(meta_ref, i, n_exp, stage, xs_ref, sems.at[slot], to_global=True, wait=True)


def _dispatch(meta, tail, ldest, h2, n_rows, n_exp):
    t, d = h2.shape
    tm = TOK_TM
    n_tiles = t // tm
    stage_rows = _stage_rows(n_exp)
    kernel = functools.partial(_dispatch_kernel, tm=tm, n_exp=n_exp, n_tiles=n_tiles,
                               n_blocks=n_rows // EXP_BLK, stage_rows=stage_rows)
    grid_spec = pltpu.PrefetchScalarGridSpec(
        num_scalar_prefetch=2,
        grid=(n_tiles,),
        in_specs=[pl.BlockSpec((TOP_K, tm), lambda i, m, tl: (0, i)),
                  pl.BlockSpec((tm, d), lambda i, m, tl: (i, 0))],
        out_specs=pl.BlockSpec(memory_space=pl.ANY),
        scratch_shapes=[pltpu.VMEM((2, stage_rows, d), BF16),
                        pltpu.VMEM((EXP_BLK, d), BF16),
                        pltpu.SemaphoreType.DMA((2,)),
                        pltpu.SemaphoreType.DMA],
    )
    return pl.pallas_call(
        kernel,
        out_shape=jax.ShapeDtypeStruct((n_rows, d), BF16),
        grid_spec=grid_spec,
        compiler_params=_params("arbitrary"),
        name="dispatch",
    )(meta, tail, ldest, h2)


def _expert_kernel(region_ref, xs_ref, w1_ref, b1_ref, w2_ref, b2_ref, eo_ref,
                   w1b_ref, w2b_ref, x_buf, o_buf, x_sems, o_sems, *, f, blk, n_exp, n_blocks):
    e = pl.program_id(0)
    first, count = region_ref[e], region_ref[n_exp + e]

    def rows(j):
        return pl.ds(pl.multiple_of((first + j) * blk, blk), blk)

    def x_copy(j, slot):
        return pltpu.make_async_copy(xs_ref.at[rows(j)], x_buf.at[slot], x_sems.at[slot])

    def o_copy(j, slot):
        return pltpu.make_async_copy(o_buf.at[slot], eo_ref.at[rows(j)], o_sems.at[slot])

    @pl.when(count > 0)
    def _():
        x_copy(0, 0).start()
        w1b_ref[...] = w1_ref[0].astype(BF16)
        w2b_ref[...] = w2_ref[0].astype(BF16)

    def block(j, carry):
        slot = j & 1
        x_copy(j, slot).wait()

        @pl.when(j + 1 < count)
        def _():
            x_copy(j + 1, 1 - slot).start()

        @pl.when(j >= 2)
        def _():
            o_copy(j - 2, slot).wait()

        gu = jnp.dot(x_buf[slot], w1b_ref[...], preferred_element_type=F32) + b1_ref[0]
        g = jnp.minimum(gu[:, :f], SWIGLU_LIMIT)
        u = jnp.clip(gu[:, f:], -SWIGLU_LIMIT, SWIGLU_LIMIT)
        a = g * _sigmoid(SWIGLU_ALPHA * g) * (u + 1.0)
        o_buf[slot] = (jnp.dot(a.astype(BF16), w2b_ref[...], preferred_element_type=F32)
                       + b2_ref[0]).astype(o_buf.dtype)
        o_copy(j, slot).start()
        return carry

    lax.fori_loop(0, count, block, 0)

    @pl.when(count >= 2)
    def _():
        o_copy(count - 2, count & 1).wait()

    @pl.when(count >= 1)
    def _():
        o_copy(count - 1, (count - 1) & 1).wait()

    @pl.when(e == n_exp - 1)
    def _():
        o_buf[0] = jnp.zeros(o_buf.shape[1:], o_buf.dtype)
        used = first + count

        def fill(j, wait):
            cp = pltpu.make_async_copy(
                o_buf.at[0], eo_ref.at[pl.ds(pl.multiple_of(j * blk, blk), blk)], o_sems.at[0])
            if wait:
                cp.wait()
            else:
                cp.start()

        lax.fori_loop(used, n_blocks, lambda j, c: (fill(j, False), c)[1], 0)
        lax.fori_loop(used, n_blocks, lambda j, c: (fill(j, True), c)[1], 0)


def _experts(region, xs, w1, b1, w2, b2):
    n_rows, d = xs.shape
    n_exp, _, f2 = w1.shape
    f = f2 // 2
    blk = EXP_BLK
    kernel = functools.partial(_expert_kernel, f=f, blk=blk, n_exp=n_exp, n_blocks=n_rows // blk)
    grid_spec = pltpu.PrefetchScalarGridSpec(
        num_scalar_prefetch=1,
        grid=(n_exp,),
        in_specs=[pl.BlockSpec(memory_space=pl.ANY),
                  pl.BlockSpec((1, d, f2), lambda e, r: (e, 0, 0)),
                  pl.BlockSpec((1, 1, f2), lambda e, r: (e, 0, 0)),
                  pl.BlockSpec((1, f, d), lambda e, r: (e, 0, 0)),
                  pl.BlockSpec((1, 1, d), lambda e, r: (e, 0, 0))],
        out_specs=pl.BlockSpec(memory_space=pl.ANY),
        scratch_shapes=[pltpu.VMEM((d, f2), BF16), pltpu.VMEM((f, d), BF16),
                        pltpu.VMEM((2, blk, d), BF16), pltpu.VMEM((2, blk, d), BF16),
                        pltpu.SemaphoreType.DMA((2,)), pltpu.SemaphoreType.DMA((2,))],
    )
    return pl.pallas_call(
        kernel,
        out_shape=jax.ShapeDtypeStruct((n_rows, d), BF16),
        grid_spec=grid_spec,
        compiler_params=_params("arbitrary"),
        name="experts",
    )(region, xs, w1, b1, w2, b2)


def _combine_kernel(meta_ref, eo_ref, ld_ref, w_ref, x1_ref, mod_ref, o_ref, stage_ref, sems,
                    *, tm, n_exp, n_tiles, stage_rows):
    i = pl.program_id(0)
    slot = i & 1

    @pl.when(i == 0)
    def _():
        stage_ref[...] = jnp.zeros_like(stage_ref)
        _run_slabs(meta_ref, 0, n_exp, stage_ref.at[0], eo_ref, sems.at[0], to_global=False, wait=False)

    @pl.when(i + 1 < n_tiles)
    def _():
        _run_slabs(meta_ref, i + 1, n_exp, stage_ref.at[1 - slot], eo_ref, sems.at[1 - slot],
                   to_global=False, wait=False)

    ld = ld_ref[...]
    w = w_ref[...]
    ldb = [jnp.broadcast_to(ld[:, k:k + 1], (tm, SCATTER_RC)) for k in range(TOP_K)]
    wb = [jnp.broadcast_to(w[:, k:k + 1], (tm, SCATTER_RC)) for k in range(TOP_K)]
    col = lax.broadcasted_iota(I32, (tm, SCATTER_RC), 1)

    def sel_piece(r0):
        r = col + r0
        sel = jnp.where(ldb[0] == r, wb[0], jnp.where(ldb[1] == r, wb[1],
                        jnp.where(ldb[2] == r, wb[2], jnp.where(ldb[3] == r, wb[3], 0.0))))
        return sel.astype(BF16)

    _run_slabs(meta_ref, i, n_exp, stage_ref.at[slot], eo_ref, sems.at[slot], to_global=False, wait=True)
    stage = stage_ref.at[slot]
    y = None
    for c in range(stage_rows // COMBINE_KC):
        r0 = c * COMBINE_KC
        sel = jnp.concatenate([sel_piece(r0 + s) for s in range(0, COMBINE_KC, SCATTER_RC)], axis=1)
        part = jnp.dot(sel, stage[r0:r0 + COMBINE_KC, :], preferred_element_type=F32)
        y = part if y is None else y + part
    o_ref[...] = x1_ref[...] + mod_ref[0][5:6] * y


def _combine(meta, eo, ld_tok, w_tok, x1, mod3, seq, n_exp):
    t, d = x1.shape
    tm = TOK_TM
    n_tiles = t // tm
    stage_rows = _stage_rows(n_exp)
    kernel = functools.partial(_combine_kernel, tm=tm, n_exp=n_exp, n_tiles=n_tiles, stage_rows=stage_rows)
    grid_spec = pltpu.PrefetchScalarGridSpec(
        num_scalar_prefetch=1,
        grid=(n_tiles,),
        in_specs=[pl.BlockSpec(memory_space=pl.ANY),
                  pl.BlockSpec((tm, TOP_K), lambda i, m: (i, 0)),
                  pl.BlockSpec((tm, TOP_K), lambda i, m: (i, 0)),
                  pl.BlockSpec((tm, d), lambda i, m: (i, 0)),
                  pl.BlockSpec((1, 6, d), lambda i, m: (i * tm // seq, 0, 0))],
        out_specs=pl.BlockSpec((tm, d), lambda i, m: (i, 0)),
        scratch_shapes=[pltpu.VMEM((2, stage_rows, d), BF16),
                        pltpu.SemaphoreType.DMA((2,))],
    )
    return pl.pallas_call(
        kernel,
        out_shape=jax.ShapeDtypeStruct((t, d), F32),
        grid_spec=grid_spec,
        compiler_params=_params("arbitrary"),
        name="combine",
    )(meta, eo, ld_tok, w_tok, x1, mod3)


def _routing_tables(tile_cnt, n_tokens):
    n_tiles, n_exp = tile_cnt.shape
    blk = EXP_BLK
    run = _round_up(tile_cnt, RUN_ALIGN)
    local = jnp.cumsum(run, axis=1) - run
    size = jnp.sum(run, axis=0)
    region = _round_up(size, blk)
    region_end = jnp.cumsum(region)
    region_start = region_end - region
    glob = region_start[None, :] + jnp.cumsum(run, axis=0) - run
    sizes = jnp.asarray(SLAB_SIZES, I32)
    has = (run[:, :, None] & sizes) != 0
    before = run[:, :, None] & ~(2 * sizes - 1)
    place = jnp.cumsum(has, axis=1) - has
    hit = has[..., None] & (place[..., None] == jnp.arange(n_exp, dtype=I32))
    listed = lambda start: jnp.sum(
        jnp.where(hit, (start[:, :, None] + before)[..., None], 0), axis=1).reshape(n_tiles, -1)
    meta = jnp.concatenate([listed(local), listed(glob), jnp.sum(has, axis=1)],
                           axis=1).reshape(-1).astype(I32)
    tail = jnp.concatenate([region_start + size, region - size, region_end[-1:]]).astype(I32)
    n_rows = _round_up(n_tokens * TOP_K + n_tiles * n_exp * RUN_ALIGN + n_exp * blk, blk)
    region_blocks = jnp.concatenate([region_start // blk, region // blk]).astype(I32)
    return meta, tail, region_blocks, n_rows


def kernel(x, c, w_ada, b_ada, g_norm1, w_in, g_ret_gn, b_ret_gn, g_qnorm, g_knorm,
           lambda_q1, lambda_k1, lambda_q2, lambda_k2, g_diff_subln, w_out, g_norm2,
           w_router, b_router, w_expert_in, b_expert_in, w_expert_out, b_expert_out):
    batch, seq, d = x.shape
    depth = w_ada.shape[0]
    assert depth == 1
    t = batch * seq
    n_exp = w_router.shape[-1]
    l = 0

    mod = _adaln_mod(c, w_ada[l], b_ada[l][None, :])
    mod3 = mod.reshape(batch, 6, d)
    x2 = x.reshape(t, d)

    proj = _in_proj(x2, mod3, g_norm1[l][None, :], w_in[l].astype(BF16), seq)
    ret = _retention(proj, g_ret_gn[l][None, :], b_ret_gn[l][None, :], batch, seq, d)
    att = _diff_attention(proj, g_qnorm[l][None, :], g_knorm[l][None, :], g_diff_subln[l][None, :],
                          lambda_q1[l][None, :], lambda_k1[l][None, :],
                          lambda_q2[l][None, :], lambda_k2[l][None, :], batch, seq, d)

    wr = w_router[l]
    wr_hi = wr.astype(BF16)
    wr_lo = (wr - wr_hi.astype(F32)).astype(BF16)
    pad = lambda a: jnp.pad(a, ((0, 0), (0, LANES - n_exp)))
    wr_split = jnp.concatenate([pad(wr_hi), pad(wr_lo)], axis=1)
    br_pad = jnp.pad(b_router[l][None, :], ((0, 0), (0, LANES - n_exp)), constant_values=MASK_VALUE)
    x1, h2, top_w, ldest, tile_cnt = _out_router(
        x2, ret, att, mod3, w_out[l].astype(BF16), g_norm2[l][None, :], wr_split, br_pad, seq, n_exp)

    meta, tail, region_blocks, n_rows = _routing_tables(tile_cnt[:, :, 0].astype(I32), t)
    xs = _dispatch(meta, tail, ldest, h2, n_rows, n_exp)
    eo = _experts(region_blocks, xs, w_expert_in[l], b_expert_in[l][:, None, :],
                  w_expert_out[l], b_expert_out[l][:, None, :])
    out = _combine(meta, eo, ldest.T, top_w.T, x1, mod3, seq, n_exp)
    return out.reshape(batch, seq, d)
```

```python
import functools
import math

import jax
import jax.numpy as jnp
from jax import lax
from jax.experimental import pallas as pl
from jax.experimental.pallas import tpu as pltpu

F32 = jnp.float32
BF16 = jnp.bfloat16
I32 = jnp.int32

EPS = 1e-6
LOG2E = 1.4426950216293335
MASK_VALUE = -1e30
LANES = 128
BF16_SUBLANES = 16
VMEM_LIMIT_BYTES = 48 * 1024 * 1024

RET_HEADS = 4
DIFF_HEADS = 8
TOP_K = 4
SWIGLU_LIMIT = 7.0
SWIGLU_ALPHA = 1.702
LAMBDA_INIT = 0.8 - 0.6 * math.exp(-0.3 * 0)

IN_TM, IN_TN = 1024, 2048
RET_TC = 256
ATT_T = 512
TOK_TM = 512
EXP_BLK = 256
RUN_ALIGN = BF16_SUBLANES
GATHER_RC = 256
SCATTER_RC = 128
COMBINE_KC = 512
W_CHUNKS = 16


def _round_up(n, m):
    return (n + m - 1) // m * m


def _stage_rows(n_exp):
    return _round_up(TOK_TM * TOP_K + n_exp * (RUN_ALIGN - 1), max(GATHER_RC, COMBINE_KC))


SLAB_SIZES = tuple(TOK_TM >> s for s in range(TOK_TM.bit_length()) if TOK_TM >> s >= RUN_ALIGN)
TAIL_SIZES = tuple(s for s in SLAB_SIZES if s < EXP_BLK)


def _params(*sem):
    return pltpu.CompilerParams(dimension_semantics=sem, vmem_limit_bytes=VMEM_LIMIT_BYTES)


def _sigmoid(x):
    return 0.5 * jnp.tanh(0.5 * x) + 0.5


def _rms(x, axis=-1):
    return x * lax.rsqrt(jnp.mean(x * x, axis=axis, keepdims=True) + EPS)


def _mod_kernel(c_ref, w_ref, b_ref, o_ref):
    c = c_ref[...]
    s = c * _sigmoid(c)
    o_ref[...] = jnp.dot(s, w_ref[...], preferred_element_type=F32,
                         precision=lax.Precision.HIGHEST) + b_ref[...]


def _adaln_mod(c, w, b):
    bn, d = c.shape
    n = w.shape[1]
    tn = d
    return pl.pallas_call(
        _mod_kernel,
        out_shape=jax.ShapeDtypeStruct((bn, n), F32),
        grid=(n // tn,),
        in_specs=[pl.BlockSpec((bn, d), lambda j: (0, 0)),
                  pl.BlockSpec((d, tn), lambda j: (0, j)),
                  pl.BlockSpec((1, tn), lambda j: (0, j))],
        out_specs=pl.BlockSpec((bn, tn), lambda j: (0, j)),
        compiler_params=_params("arbitrary"),
        name="adaln_mod",
    )(c, w, b)


def _inproj_kernel(x_ref, mod_ref, g_ref, w_ref, o_ref, h_ref):
    @pl.when(pl.program_id(1) == 0)
    def _():
        m = mod_ref[0]
        h = _rms(x_ref[...]) * g_ref[...] * (1.0 + m[1:2]) + m[0:1]
        h_ref[...] = h.astype(BF16)

    o_ref[...] = jnp.dot(h_ref[...], w_ref[...], preferred_element_type=F32).astype(o_ref.dtype)


def _in_proj(x2, mod3, g1, w_in_bf16, seq):
    t, d = x2.shape
    n = w_in_bf16.shape[1]
    tm, tn = IN_TM, IN_TN
    return pl.pallas_call(
        _inproj_kernel,
        out_shape=jax.ShapeDtypeStruct((t, n), BF16),
        grid=(t // tm, n // tn),
        in_specs=[pl.BlockSpec((tm, d), lambda i, j: (i, 0)),
                  pl.BlockSpec((1, 6, d), lambda i, j: (i * tm // seq, 0, 0)),
                  pl.BlockSpec((1, d), lambda i, j: (0, 0)),
                  pl.BlockSpec((d, tn), lambda i, j: (0, j))],
        out_specs=pl.BlockSpec((tm, tn), lambda i, j: (i, j)),
        scratch_shapes=[pltpu.VMEM((tm, d), BF16)],
        compiler_params=_params("arbitrary", "arbitrary"),
        name="in_proj",
    )(x2, mod3, g1, w_in_bf16)


def _ret_kernel(q_ref, k_ref, v_ref, rg_ref, mg_ref, gng_ref, gnb_ref, o_ref, r_ref, decay_ref,
                *, dk, dv, tc):
    scale = dk ** -0.5
    log_gs = [math.log(1.0 - 2.0 ** (-5.0 - h)) for h in range(RET_HEADS)]

    @pl.when(pl.program_id(1) == 0)
    def _():
        r_ref[...] = jnp.zeros_like(r_ref)
        row = lax.broadcasted_iota(I32, (tc, tc), 0)
        col = lax.broadcasted_iota(I32, (tc, tc), 1)
        rel = (row - col).astype(F32)
        for h, log_g in enumerate(log_gs):
            decay_ref[h] = jnp.where(rel >= 0, jnp.exp(log_g * jnp.maximum(rel, 0.0)), 0.0) * scale

    pos = lax.broadcasted_iota(I32, (tc, 1), 0).astype(F32)
    for h, log_g in enumerate(log_gs):
        decay = decay_ref[h]
        xi = jnp.exp(log_g * (pos + 1.0))
        zeta = jnp.exp(log_g * (tc - 1.0 - pos))
        g_chunk = math.exp(log_g * tc)
        q = q_ref[:, h * dk:(h + 1) * dk]
        k = k_ref[:, h * dk:(h + 1) * dk]
        v = v_ref[:, h * dv:(h + 1) * dv]
        s = lax.dot_general(q, k, (((1,), (1,)), ((), ())), preferred_element_type=F32)
        y = jnp.dot((s * decay).astype(BF16), v, preferred_element_type=F32)
        state = r_ref[h]
        y = y + jnp.dot(q, state.astype(BF16), preferred_element_type=F32) * xi
        vz = (v.astype(F32) * zeta).astype(BF16)
        kv = lax.dot_general(k, vz, (((0,), (0,)), ((), ())), preferred_element_type=F32)
        r_ref[h] = state * g_chunk + kv * scale

        mu = jnp.mean(y, axis=-1, keepdims=True)
        yc = y - mu
        var = jnp.mean(yc * yc, axis=-1, keepdims=True)
        sl = slice(h * dv, (h + 1) * dv)
        yn = yc * lax.rsqrt(var + EPS) * gng_ref[:, sl] + gnb_ref[:, sl]
        rg = rg_ref[:, sl].astype(F32)
        gate = _sigmoid(mg_ref[:, sl].astype(F32))
        o_ref[:, sl] = (rg * _sigmoid(rg) * yn * gate).astype(o_ref.dtype)


def _retention(proj, gn_g, gn_b, batch, seq, d):
    t = proj.shape[0]
    tc = RET_TC
    nc = seq // tc
    dk = d // (2 * RET_HEADS)
    dv = d // RET_HEADS
    qk_w = RET_HEADS * dk
    row = lambda b, c: b * nc + c
    kernel = functools.partial(_ret_kernel, dk=dk, dv=dv, tc=tc)
    return pl.pallas_call(
        kernel,
        out_shape=jax.ShapeDtypeStruct((t, d), BF16),
        grid=(batch, nc),
        in_specs=[pl.BlockSpec((tc, qk_w), lambda b, c: (row(b, c), 0)),
                  pl.BlockSpec((tc, qk_w), lambda b, c: (row(b, c), 1)),
                  pl.BlockSpec((tc, d), lambda b, c: (row(b, c), 1)),
                  pl.BlockSpec((tc, d), lambda b, c: (row(b, c), 2)),
                  pl.BlockSpec((tc, d), lambda b, c: (row(b, c), 6)),
                  pl.BlockSpec((1, d), lambda b, c: (0, 0)),
                  pl.BlockSpec((1, d), lambda b, c: (0, 0))],
        out_specs=pl.BlockSpec((tc, d), lambda b, c: (row(b, c), 0)),
        scratch_shapes=[pltpu.VMEM((RET_HEADS, dk, dv), F32), pltpu.VMEM((RET_HEADS, tc, tc), F32)],
        compiler_params=_params("arbitrary", "arbitrary"),
        name="retention",
    )(proj, proj, proj, proj, proj, gn_g, gn_b)


def _halves_rms(x, lo, dh):
    sq = x * x
    s_lo = jnp.sum(jnp.where(lo, sq, 0.0), axis=-1, keepdims=True)
    s_hi = jnp.sum(jnp.where(lo, 0.0, sq), axis=-1, keepdims=True)
    inv = jnp.where(lo, lax.rsqrt(s_lo / dh + EPS), lax.rsqrt(s_hi / dh + EPS))
    return x * inv


def _attn_kernel(q_ref, k_ref, v_ref, gate_ref, slope_ref, qca_ref, qcb_ref, gq_ref, gk_ref, gsub_ref,
                 lq1_ref, lk1_ref, lq2_ref, lk2_ref,
                 o_ref, ka_ref, kb_ref, vt_ref, qa_ref, qb_ref, s0_ref, sa_ref, sb_ref, acc_ref, m_ref,
                 *, tile, dh, dv, nk):
    lane = lax.broadcasted_iota(I32, (tile, 2 * dh), 1)
    lo = lane < dh

    slope = slope_ref[0]
    sub = lax.broadcasted_iota(I32, (tile, 2 * dh), 0)
    off = lane & (dh - 1)
    hi_lane = (off < 6) & ((off & 1) == 0)
    lo_lane = (off < 6) & ((off & 1) == 1)
    bias0 = jnp.where(hi_lane, ((sub >> 8) << 8).astype(F32),
                      jnp.where(lo_lane, (sub & 255).astype(F32), 0.0)) * slope
    bias_step = jnp.where(hi_lane, float(tile), 0.0) * slope
    ones_rows = jnp.where(lax.broadcasted_iota(I32, (BF16_SUBLANES, tile), 0) == 0, 1.0, 0.0)
    lo_t = lax.broadcasted_iota(I32, (2 * dh, tile), 0) < dh

    def prepare(c, carry):
        r0 = c * tile if isinstance(c, int) else pl.multiple_of(c * tile, tile)
        kn = _halves_rms(k_ref[pl.ds(r0, tile), :].astype(F32), lo, dh) * gk_ref[...]
        bias = bias0 + lax.convert_element_type(c, F32) * bias_step
        ka_ref[c] = jnp.where(lo, kn, bias).astype(BF16)
        kb_ref[c] = jnp.where(lo, bias, kn).astype(BF16)
        vt_ref[c, :dv, :] = v_ref[pl.ds(r0, tile), :].astype(F32).T.astype(BF16)
        vt_ref[c, dv:, :] = ones_rows.astype(BF16)
        qn = (_halves_rms(q_ref[pl.ds(r0, tile), :].astype(F32), lo, dh) * gq_ref[...]
              * (dh ** -0.5 * LOG2E))
        qn_t = qn.T
        qa_ref[c] = jnp.where(lo_t, qn_t, qca_ref[...]).astype(BF16)
        qb_ref[c] = jnp.where(lo_t, qcb_ref[...], qn_t).astype(BF16)
        return carry

    lax.fori_loop(0, min(2, nk), prepare, 0)
    lam = (jnp.exp(jnp.sum(lq1_ref[...] * lk1_ref[...], axis=-1, keepdims=True))
           - jnp.exp(jnp.sum(lq2_ref[...] * lk2_ref[...], axis=-1, keepdims=True)) + LAMBDA_INIT)

    refs = (gate_ref, gsub_ref, o_ref, ka_ref, kb_ref, vt_ref, qa_ref, qb_ref,
            s0_ref, sa_ref, sb_ref, acc_ref, m_ref)
    s0_ref[0] = jnp.dot(ka_ref[0], qa_ref[0], preferred_element_type=F32)
    s0_ref[1] = jnp.dot(kb_ref[0], qb_ref[0], preferred_element_type=F32)
    _attn_q_tile(0, lam, prepare, *refs, tile=tile, dv=dv, nk=nk)

    def q_tile(qi, carry):
        _attn_q_tile(qi, lam, prepare, *refs, tile=tile, dv=dv, nk=nk)
        return carry

    lax.fori_loop(1, nk, q_tile, 0)


def _attn_q_tile(qi, lam, prepare, gate_ref, gsub_ref, o_ref, ka_ref, kb_ref, vt_ref, qa_ref, qb_ref,
                 s0_ref, sa_ref, sb_ref, acc_ref, m_ref, *, tile, dv, nk):
    def scores(jb, s_ref, q=qi):
        s_ref[0] = jnp.dot(ka_ref[jb], qa_ref[q], preferred_element_type=F32)
        s_ref[1] = jnp.dot(kb_ref[jb], qb_ref[q], preferred_element_type=F32)

    def absorb(jb, s_ref, masked):
        vt = vt_ref[jb]
        for idx in range(2):
            s = s_ref[idx]
            if masked:
                krow = lax.broadcasted_iota(I32, (tile, tile), 0)
                qcol = lax.broadcasted_iota(I32, (tile, tile), 1)
                s = jnp.where(krow > qcol, MASK_VALUE, s)
            m_old = m_ref[idx]
            m_new = jnp.maximum(m_old, jnp.max(s, axis=0, keepdims=True))
            m_ref[idx] = m_new
            p = jnp.exp2(s - m_new).astype(BF16)
            acc_ref[idx] = (acc_ref[idx] * jnp.exp2(m_old - m_new)
                            + jnp.dot(vt, p, preferred_element_type=F32))

    acc_ref[...] = jnp.zeros_like(acc_ref)
    m_ref[...] = jnp.full_like(m_ref, MASK_VALUE)
    if isinstance(qi, int):
        absorb(0, s0_ref, True)
    else:
        scores(1, sa_ref)
        absorb(0, s0_ref, False)

        def pair(j2, carry):
            jb = 2 * j2 + 1
            scores(jb + 1, sb_ref)
            absorb(jb, sa_ref, False)
            scores(jb + 2, sa_ref)
            absorb(jb + 1, sb_ref, False)
            return carry

        lax.fori_loop(0, (qi - 1) // 2, pair, 0)

        @pl.when(qi % 2 == 0)
        def _():
            scores(qi, sb_ref)
            absorb(qi - 1, sa_ref, False)
            absorb(qi, sb_ref, True)

        @pl.when(qi % 2 == 1)
        def _():
            absorb(qi, sa_ref, True)

    ahead = (lambda n: min(qi + n, nk - 1)) if isinstance(qi, int) else (lambda n: jnp.minimum(qi + n, nk - 1))
    scores(0, s0_ref, q=ahead(1))
    prepare(ahead(2), 0)
    a1, a2 = acc_ref[0], acc_ref[1]
    o_t = a1[:dv] * (1.0 / a1[dv:dv + 1]) - lam * (a2[:dv] * (1.0 / a2[dv:dv + 1]))
    o = _rms(o_t, axis=0).T * gsub_ref[...] * (1.0 - LAMBDA_INIT)
    rows = pl.ds(pl.multiple_of(qi * tile, tile), tile)
    o_ref[rows, :] = (o * _sigmoid(gate_ref[rows, :].astype(F32))).astype(o_ref.dtype)


def _bf16_terms(x, n):
    terms, rest = [], jnp.asarray(x, F32)
    for _ in range(n):
        term = rest.astype(BF16).astype(F32)
        terms.append(term)
        rest = rest - term
    return terms


def _diff_attention(proj, g_q, g_k, g_sub, lq1, lk1, lq2, lk2, batch, seq, d):
    t = proj.shape[0]
    tile = ATT_T
    nq = seq // tile
    dh = d // (2 * DIFF_HEADS)
    w = 2 * dh
    assert w == LANES
    q_col, k_col, v_col, gate_col = 3 * d // w, 4 * d // w, 5 * d // w, 7 * d // w
    slopes = 2.0 ** -(jnp.arange(DIFF_HEADS, dtype=F32) + 1.0)
    slopes = jnp.broadcast_to(slopes[:, None, None], (DIFF_HEADS, 1, LANES))
    c1, c2, c3 = _bf16_terms(LOG2E, 3)
    qconst = jnp.zeros((dh,), F32).at[:6].set(jnp.stack([c1, c1, c2, c2, c3, c3]))
    zeros = jnp.zeros((dh,), F32)
    qconst_a = jnp.concatenate([zeros, qconst])[:, None]
    qconst_b = jnp.concatenate([qconst, zeros])[:, None]
    tile2 = lambda a: jnp.concatenate([a, a], axis=-1)
    assert tile % 256 == 0
    small = lambda n: pl.BlockSpec((1, n), lambda b, h: (0, 0))
    kernel = functools.partial(_attn_kernel, tile=tile, dh=dh, dv=w, nk=nq)
    return pl.pallas_call(
        kernel,
        out_shape=jax.ShapeDtypeStruct((t, d), BF16),
        grid=(batch, DIFF_HEADS),
        in_specs=[pl.BlockSpec((seq, w), lambda b, h: (b, q_col + h)),
                  pl.BlockSpec((seq, w), lambda b, h: (b, k_col + h)),
                  pl.BlockSpec((seq, w), lambda b, h: (b, v_col + h)),
                  pl.BlockSpec((seq, w), lambda b, h: (b, gate_col + h)),
                  pl.BlockSpec((1, 1, LANES), lambda b, h: (h, 0, 0)),
                  pl.BlockSpec((w, 1), lambda b, h: (0, 0)), pl.BlockSpec((w, 1), lambda b, h: (0, 0)),
                  small(w), small(w), small(w),
                  small(dh), small(dh), small(dh), small(dh)],
        out_specs=pl.BlockSpec((seq, w), lambda b, h: (b, h)),
        scratch_shapes=[pltpu.VMEM((nq, tile, w), BF16),
                        pltpu.VMEM((nq, tile, w), BF16),
                        pltpu.VMEM((nq, w + BF16_SUBLANES, tile), BF16),
                        pltpu.VMEM((nq, w, tile), BF16),
                        pltpu.VMEM((nq, w, tile), BF16),
                        pltpu.VMEM((2, tile, tile), F32),
                        pltpu.VMEM((2, tile, tile), F32),
                        pltpu.VMEM((2, tile, tile), F32),
                        pltpu.VMEM((2, w + BF16_SUBLANES, tile), F32),
                        pltpu.VMEM((2, 1, tile), F32)],
        compiler_params=_params("arbitrary", "arbitrary"),
        name="diff_attn",
    )(proj, proj, proj, proj, slopes, qconst_a, qconst_b, tile2(g_q), tile2(g_k), g_sub,
      lq1, lk1, lq2, lk2)


def _out_kernel(x_ref, ret_ref, att_ref, mod_ref, wo_ref, g2_ref, wr_ref, br_ref,
                x1_ref, h2_ref, tw_ref, ld_ref, cnt_ref, *, tm, n_exp):
    m = mod_ref[0]
    merged = (ret_ref[...].astype(F32) + att_ref[...].astype(F32)).astype(BF16)
    x1 = x_ref[...] + m[2:3] * jnp.dot(merged, wo_ref[...], preferred_element_type=F32)
    x1_ref[...] = x1
    h2 = _rms(x1) * g2_ref[...] * (1.0 + m[4:5]) + m[3:4]
    hi = h2.astype(BF16)
    h2_ref[...] = hi

    lo = (h2 - hi.astype(F32)).astype(BF16)
    both = jnp.dot(hi, wr_ref[...], preferred_element_type=F32)
    logits = (both[:, :LANES] + both[:, LANES:]
              + jnp.dot(lo, wr_ref[:, :LANES], preferred_element_type=F32) + br_ref[...])
    lt = logits.T[:n_exp]

    erow = lax.broadcasted_iota(I32, (n_exp, tm), 0)
    vals, sels = [], []
    for k in range(TOP_K):
        mx = jnp.max(lt, axis=0, keepdims=True)
        idx = jnp.min(jnp.where(lt == mx, erow, n_exp), axis=0, keepdims=True)
        sel = erow == idx
        vals.append(mx)
        sels.append(sel)
        lt = jnp.where(sel, MASK_VALUE, lt)
    exps = [jnp.exp(v - vals[0]) for v in vals]
    inv = 1.0 / (exps[0] + exps[1] + exps[2] + exps[3])
    for k in range(TOP_K):
        tw_ref[k:k + 1, :] = exps[k] * inv

    chosen = jnp.where(sels[0] | sels[1] | sels[2] | sels[3], 1.0, 0.0)
    before = (lax.broadcasted_iota(I32, (tm, tm), 0) < lax.broadcasted_iota(I32, (tm, tm), 1))
    prefix = jnp.dot(chosen.astype(BF16), jnp.where(before, 1.0, 0.0).astype(BF16),
                     preferred_element_type=F32)
    count = jnp.sum(chosen, axis=1, keepdims=True)
    cnt_ref[...] = count
    padded = jnp.ceil(count * (1.0 / RUN_ALIGN)) * RUN_ALIGN
    below = (lax.broadcasted_iota(I32, (n_exp, n_exp), 1) < lax.broadcasted_iota(I32, (n_exp, n_exp), 0))
    run_start = jnp.dot(jnp.where(below, 1.0, 0.0).astype(BF16),
                        jnp.broadcast_to(padded, (n_exp, LANES)).astype(BF16),
                        preferred_element_type=F32)[:, :1]
    pos = prefix + run_start
    for k in range(TOP_K):
        ld_ref[k:k + 1, :] = jnp.sum(jnp.where(sels[k], pos, 0.0), axis=0, keepdims=True).astype(I32)


def _out_router(x2, ret, att, mod3, w_out_bf16, g2, wr_split, br_pad, seq, n_exp):
    t, d = x2.shape
    tm = TOK_TM
    row = lambda i: (i, 0)
    const = lambda i: (0, 0)
    kernel = functools.partial(_out_kernel, tm=tm, n_exp=n_exp)
    return pl.pallas_call(
        kernel,
        out_shape=(jax.ShapeDtypeStruct((t, d), F32),
                   jax.ShapeDtypeStruct((t, d), BF16),
                   jax.ShapeDtypeStruct((TOP_K, t), F32),
                   jax.ShapeDtypeStruct((TOP_K, t), I32),
                   jax.ShapeDtypeStruct((t // tm, n_exp, 1), F32)),
        grid=(t // tm,),
        in_specs=[pl.BlockSpec((tm, d), row), pl.BlockSpec((tm, d), row), pl.BlockSpec((tm, d), row),
                  pl.BlockSpec((1, 6, d), lambda i: (i * tm // seq, 0, 0)),
                  pl.BlockSpec((d, d), const), pl.BlockSpec((1, d), const),
                  pl.BlockSpec((d, 2 * LANES), const), pl.BlockSpec((1, LANES), const)],
        out_specs=(pl.BlockSpec((tm, d), row), pl.BlockSpec((tm, d), row),
                   pl.BlockSpec((TOP_K, tm), lambda i: (0, i)),
                   pl.BlockSpec((TOP_K, tm), lambda i: (0, i)),
                   pl.BlockSpec((None, n_exp, 1), lambda i: (i, 0, 0))),
        compiler_params=_params("arbitrary"),
        name="out_router",
    )(x2, ret, att, mod3, w_out_bf16, g2, wr_split, br_pad)


def _meta_stride(n_exp):
    return 2 * len(SLAB_SIZES) * n_exp + len(SLAB_SIZES)


def _run_slabs(meta_ref, tile, n_exp, local_ref, global_ref, sem, *, to_global, wait):
    n_sizes = len(SLAB_SIZES)
    base = tile * _meta_stride(n_exp)
    for c, size in enumerate(SLAB_SIZES):
        def one(p, carry, c=c, size=size):
            if wait:
                local = glob = 0
            else:
                local = pl.multiple_of(meta_ref[base + c * n_exp + p], RUN_ALIGN)
                glob = pl.multiple_of(meta_ref[base + (n_sizes + c) * n_exp + p], RUN_ALIGN)
            loc = local_ref.at[pl.ds(local, size)]
            glo = global_ref.at[pl.ds(glob, size)]
            cp = pltpu.make_async_copy(loc, glo, sem) if to_global else pltpu.make_async_copy(glo, loc, sem)
            if wait:
                cp.wait()
            else:
                cp.start()
            return carry

        lax.fori_loop(0, meta_ref[base + 2 * n_sizes * n_exp + c], one, 0)


def _dispatch_kernel(meta_ref, tail_ref, ld_ref, h_ref, xs_ref, stage_ref, zero_ref, sems, tail_sem,
                     *, tm, n_exp, n_tiles, n_blocks, stage_rows):
    i = pl.program_id(0)
    slot = i & 1
    stage = stage_ref.at[slot]

    def tails(wait):
        def unused_block(b, carry):
            cp = pltpu.make_async_copy(
                zero_ref, xs_ref.at[pl.ds(pl.multiple_of(b * EXP_BLK, EXP_BLK), EXP_BLK)], tail_sem)
            if wait:
                cp.wait()
            else:
                cp.start()
            return carry

        lax.fori_loop(tail_ref[2 * n_exp] // EXP_BLK, n_blocks, unused_block, 0)

        def per_expert(e, carry):
            start, n = tail_ref[e], tail_ref[n_exp + e]
            off = 0
            for size in TAIL_SIZES:
                take = n & size

                @pl.when(take != 0)
                def _(off=off, size=size):
                    cp = pltpu.make_async_copy(
                        zero_ref.at[pl.ds(0, size)],
                        xs_ref.at[pl.ds(pl.multiple_of(start + off, RUN_ALIGN), size)], tail_sem)
                    if wait:
                        cp.wait()
                    else:
                        cp.start()

                off = off + take
            return carry

        lax.fori_loop(0, n_exp, per_expert, 0)

    @pl.when(i == 0)
    def _():
        zero_ref[...] = jnp.zeros_like(zero_ref)
        tails(False)
        tails(True)

    ld = ld_ref[...]
    h = h_ref[...]
    for c in range(stage_rows // GATHER_RC):
        r = lax.broadcasted_iota(I32, (GATHER_RC, tm), 0) + c * GATHER_RC
        hit = jnp.where(r == ld[0:1], 1.0, jnp.where(r == ld[1:2], 1.0,
                        jnp.where(r == ld[2:3], 1.0, jnp.where(r == ld[3:4], 1.0, 0.0))))
        stage[c * GATHER_RC:(c + 1) * GATHER_RC, :] = jnp.dot(
            hit.astype(BF16), h, preferred_element_type=F32).astype(BF16)

    _run_slabs(meta_ref, i, n_exp, stage, xs_ref, sems.at[slot], to_global=True, wait=False)

    @pl.when(i > 0)
    def _():
        _run_slabs(meta_ref, i - 1, n_exp, stage_ref.at[1 - slot], xs_ref, sems.at[1 - slot],
                   to_global=True, wait=True)

    @pl.when(i == n_tiles - 1)
    def _():
        _run_slabs(meta_ref, i, n_exp, stage, xs_ref, sems.at[slot], to_global=True, wait=True)


def _dispatch(meta, tail, ldest, h2, n_rows, n_exp):
    t, d = h2.shape
    tm = TOK_TM
    n_tiles = t // tm
    stage_rows = _stage_rows(n_exp)
    kernel = functools.partial(_dispatch_kernel, tm=tm, n_exp=n_exp, n_tiles=n_tiles,
                               n_blocks=n_rows // EXP_BLK, stage_rows=stage_rows)
    grid_spec = pltpu.PrefetchScalarGridSpec(
        num_scalar_prefetch=2,
        grid=(n_tiles,),
        in_specs=[pl.BlockSpec((TOP_K, tm), lambda i, m, tl: (0, i)),
                  pl.BlockSpec((tm, d), lambda i, m, tl: (i, 0))],
        out_specs=pl.BlockSpec(memory_space=pl.ANY),
        scratch_shapes=[pltpu.VMEM((2, stage_rows, d), BF16),
                        pltpu.VMEM((EXP_BLK, d), BF16),
                        pltpu.SemaphoreType.DMA((2,)),
                        pltpu.SemaphoreType.DMA],
    )
    return pl.pallas_call(
        kernel,
        out_shape=jax.ShapeDtypeStruct((n_rows, d), BF16),
        grid_spec=grid_spec,
        compiler_params=_params("arbitrary"),
        name="dispatch",
    )(meta, tail, ldest, h2)


def _expert_kernel(region_ref, xs_ref, w1_ref, b1_ref, w2_ref, b2_ref, eo_ref,
                   w1f_ref, w2f_ref, w1b_ref, w2b_ref, x_buf, o_buf, w_sems, x_sems, o_sems,
                   *, f, blk, n_exp, n_blocks):
    e = pl.program_id(0)
    first, count = region_ref[e], region_ref[n_exp + e]
    w_slot = e & 1
    d = w1f_ref.shape[1]
    r1, r2 = d // W_CHUNKS, f // W_CHUNKS

    def rows(j):
        return pl.ds(pl.multiple_of((first + j) * blk, blk), blk)

    def x_copy(j, slot):
        return pltpu.make_async_copy(xs_ref.at[rows(j)], x_buf.at[slot], x_sems.at[slot])

    def o_copy(j, slot):
        return pltpu.make_async_copy(o_buf.at[slot], eo_ref.at[rows(j)], o_sems.at[slot])

    def w_chunk(expert, c, slot, wait):
        for src, dst, r in ((w1_ref, w1f_ref, r1), (w2_ref, w2f_ref, r2)):
            piece = pl.ds(c * r if isinstance(c, int) else pl.multiple_of(c * r, r), r)
            cp = pltpu.make_async_copy(src.at[expert, piece], dst.at[slot, piece], w_sems.at[slot])
            if wait:
                cp.wait()
            else:
                cp.start()

    @pl.when(e == 0)
    def _():
        for c in range(W_CHUNKS):
            w_chunk(0, c, 0, False)

    @pl.when(count > 0)
    def _():
        x_copy(0, 0).start()

    for c in range(W_CHUNKS):
        w_chunk(e, c, w_slot, True)

    @pl.when(count > 0)
    def _():
        w1b_ref[...] = w1f_ref[w_slot].astype(BF16)
        w2b_ref[...] = w2f_ref[w_slot].astype(BF16)

    def next_chunk(c):
        @pl.when(e + 1 < n_exp)
        def _():
            w_chunk(e + 1, c, 1 - w_slot, False)

    def block(j, carry):
        slot = j & 1
        x_copy(j, slot).wait()

        @pl.when(j + 1 < count)
        def _():
            x_copy(j + 1, 1 - slot).start()

        @pl.when(j < W_CHUNKS)
        def _():
            next_chunk(j)

        @pl.when(j >= 2)
        def _():
            o_copy(j - 2, slot).wait()

        gu = jnp.dot(x_buf[slot], w1b_ref[...], preferred_element_type=F32) + b1_ref[0]
        g = jnp.minimum(gu[:, :f], SWIGLU_LIMIT)
        u = jnp.clip(gu[:, f:], -SWIGLU_LIMIT, SWIGLU_LIMIT)
        a = g * _sigmoid(SWIGLU_ALPHA * g) * (u + 1.0)
        o_buf[slot] = (jnp.dot(a.astype(BF16), w2b_ref[...], preferred_element_type=F32)
                       + b2_ref[0]).astype(o_buf.dtype)
        o_copy(j, slot).start()
        return carry

    lax.fori_loop(0, count, block, 0)
    lax.fori_loop(jnp.minimum(count, W_CHUNKS), W_CHUNKS, lambda c, carry: (next_chunk(c), carry)[1], 0)

    @pl.when(count >= 2)
    def _():
        o_copy(count - 2, count & 1).wait()

    @pl.when(count >= 1)
    def _():
        o_copy(count - 1, (count - 1) & 1).wait()

    @pl.when(e == n_exp - 1)
    def _():
        o_buf[0] = jnp.zeros(o_buf.shape[1:], o_buf.dtype)
        used = first + count

        def fill(j, wait):
            cp = pltpu.make_async_copy(
                o_buf.at[0], eo_ref.at[pl.ds(pl.multiple_of(j * blk, blk), blk)], o_sems.at[0])
            if wait:
                cp.wait()
            else:
                cp.start()

        lax.fori_loop(used, n_blocks, lambda j, c: (fill(j, False), c)[1], 0)
        lax.fori_loop(used, n_blocks, lambda j, c: (fill(j, True), c)[1], 0)


def _experts(region, xs, w1, b1, w2, b2):
    n_rows, d = xs.shape
    n_exp, _, f2 = w1.shape
    f = f2 // 2
    blk = EXP_BLK
    kernel = functools.partial(_expert_kernel, f=f, blk=blk, n_exp=n_exp, n_blocks=n_rows // blk)
    grid_spec = pltpu.PrefetchScalarGridSpec(
        num_scalar_prefetch=1,
        grid=(n_exp,),
        in_specs=[pl.BlockSpec(memory_space=pl.ANY),
                  pl.BlockSpec(memory_space=pl.ANY),
                  pl.BlockSpec((1, 1, f2), lambda e, r: (e, 0, 0)),
                  pl.BlockSpec(memory_space=pl.ANY),
                  pl.BlockSpec((1, 1, d), lambda e, r: (e, 0, 0))],
        out_specs=pl.BlockSpec(memory_space=pl.ANY),
        scratch_shapes=[pltpu.VMEM((2, d, f2), F32), pltpu.VMEM((2, f, d), F32),
                        pltpu.VMEM((d, f2), BF16), pltpu.VMEM((f, d), BF16),
                        pltpu.VMEM((2, blk, d), BF16), pltpu.VMEM((2, blk, d), BF16),
                        pltpu.SemaphoreType.DMA((2,)), pltpu.SemaphoreType.DMA((2,)),
                        pltpu.SemaphoreType.DMA((2,))],
    )
    return pl.pallas_call(
        kernel,
        out_shape=jax.ShapeDtypeStruct((n_rows, d), BF16),
        grid_spec=grid_spec,
        compiler_params=_params("arbitrary"),
        name="experts",
    )(region, xs, w1, b1, w2, b2)


def _combine_kernel(meta_ref, eo_ref, ld_ref, w_ref, x1_ref, mod_ref, o_ref, stage_ref, sems,
                    *, tm, n_exp, n_tiles, stage_rows):
    i = pl.program_id(0)
    slot = i & 1

    @pl.when(i == 0)
    def _():
        stage_ref[...] = jnp.zeros_like(stage_ref)
        _run_slabs(meta_ref, 0, n_exp, stage_ref.at[0], eo_ref, sems.at[0], to_global=False, wait=False)

    @pl.when(i + 1 < n_tiles)
    def _():
        _run_slabs(meta_ref, i + 1, n_exp, stage_ref.at[1 - slot], eo_ref, sems.at[1 - slot],
                   to_global=False, wait=False)

    ld = ld_ref[...]
    w = w_ref[...]
    ldb = [jnp.broadcast_to(ld[:, k:k + 1], (tm, SCATTER_RC)) for k in range(TOP_K)]
    wb = [jnp.broadcast_to(w[:, k:k + 1], (tm, SCATTER_RC)) for k in range(TOP_K)]
    col = lax.broadcasted_iota(I32, (tm, SCATTER_RC), 1)

    def sel_piece(r0):
        r = col + r0
        sel = jnp.where(ldb[0] == r, wb[0], jnp.where(ldb[1] == r, wb[1],
                        jnp.where(ldb[2] == r, wb[2], jnp.where(ldb[3] == r, wb[3], 0.0))))
        return sel.astype(BF16)

    _run_slabs(meta_ref, i, n_exp, stage_ref.at[slot], eo_ref, sems.at[slot], to_global=False, wait=True)
    stage = stage_ref.at[slot]
    y = None
    for c in range(stage_rows // COMBINE_KC):
        r0 = c * COMBINE_KC
        sel = jnp.concatenate([sel_piece(r0 + s) for s in range(0, COMBINE_KC, SCATTER_RC)], axis=1)
        part = jnp.dot(sel, stage[r0:r0 + COMBINE_KC, :], preferred_element_type=F32)
        y = part if y is None else y + part
    o_ref[...] = x1_ref[...] + mod_ref[0][5:6] * y


def _combine(meta, eo, ld_tok, w_tok, x1, mod3, seq, n_exp):
    t, d = x1.shape
    tm = TOK_TM
    n_tiles = t // tm
    stage_rows = _stage_rows(n_exp)
    kernel = functools.partial(_combine_kernel, tm=tm, n_exp=n_exp, n_tiles=n_tiles, stage_rows=stage_rows)
    grid_spec = pltpu.PrefetchScalarGridSpec(
        num_scalar_prefetch=1,
        grid=(n_tiles,),
        in_specs=[pl.BlockSpec(memory_space=pl.ANY),
                  pl.BlockSpec((tm, TOP_K), lambda i, m: (i, 0)),
                  pl.BlockSpec((tm, TOP_K), lambda i, m: (i, 0)),
                  pl.BlockSpec((tm, d), lambda i, m: (i, 0)),
                  pl.BlockSpec((1, 6, d), lambda i, m: (i * tm // seq, 0, 0))],
        out_specs=pl.BlockSpec((tm, d), lambda i, m: (i, 0)),
        scratch_shapes=[pltpu.VMEM((2, stage_rows, d), BF16),
                        pltpu.SemaphoreType.DMA((2,))],
    )
    return pl.pallas_call(
        kernel,
        out_shape=jax.ShapeDtypeStruct((t, d), F32),
        grid_spec=grid_spec,
        compiler_params=_params("arbitrary"),
        name="combine",
    )(meta, eo, ld_tok, w_tok, x1, mod3)


def _routing_tables(tile_cnt, n_tokens):
    n_tiles, n_exp = tile_cnt.shape
    blk = EXP_BLK
    run = _round_up(tile_cnt, RUN_ALIGN)
    local = jnp.cumsum(run, axis=1) - run
    size = jnp.sum(run, axis=0)
    region = _round_up(size, blk)
    region_end = jnp.cumsum(region)
    region_start = region_end - region
    glob = region_start[None, :] + jnp.cumsum(run, axis=0) - run
    sizes = jnp.asarray(SLAB_SIZES, I32)
    has = (run[:, :, None] & sizes) != 0
    before = run[:, :, None] & ~(2 * sizes - 1)
    place = jnp.cumsum(has, axis=1) - has
    hit = has[..., None] & (place[..., None] == jnp.arange(n_exp, dtype=I32))
    listed = lambda start: jnp.sum(
        jnp.where(hit, (start[:, :, None] + before)[..., None], 0), axis=1).reshape(n_tiles, -1)
    meta = jnp.concatenate([listed(local), listed(glob), jnp.sum(has, axis=1)],
                           axis=1).reshape(-1).astype(I32)
    tail = jnp.concatenate([region_start + size, region - size, region_end[-1:]]).astype(I32)
    n_rows = _round_up(n_tokens * TOP_K + n_tiles * n_exp * RUN_ALIGN + n_exp * blk, blk)
    region_blocks = jnp.concatenate([region_start // blk, region // blk]).astype(I32)
    return meta, tail, region_blocks, n_rows


def kernel(x, c, w_ada, b_ada, g_norm1, w_in, g_ret_gn, b_ret_gn, g_qnorm, g_knorm,
           lambda_q1, lambda_k1, lambda_q2, lambda_k2, g_diff_subln, w_out, g_norm2,
           w_router, b_router, w_expert_in, b_expert_in, w_expert_out, b_expert_out):
    batch, seq, d = x.shape
    depth = w_ada.shape[0]
    assert depth == 1
    t = batch * seq
    n_exp = w_router.shape[-1]
    l = 0

    mod = _adaln_mod(c, w_ada[l], b_ada[l][None, :])
    mod3 = mod.reshape(batch, 6, d)
    x2 = x.reshape(t, d)

    proj = _in_proj(x2, mod3, g_norm1[l][None, :], w_in[l].astype(BF16), seq)
    ret = _retention(proj, g_ret_gn[l][None, :], b_ret_gn[l][None, :], batch, seq, d)
    att = _diff_attention(proj, g_qnorm[l][None, :], g_knorm[l][None, :], g_diff_subln[l][None, :],
                          lambda_q1[l][None, :], lambda_k1[l][None, :],
                          lambda_q2[l][None, :], lambda_k2[l][None, :], batch, seq, d)

    wr = w_router[l]
    wr_hi = wr.astype(BF16)
    wr_lo = (wr - wr_hi.astype(F32)).astype(BF16)
    pad = lambda a: jnp.pad(a, ((0, 0), (0, LANES - n_exp)))
    wr_split = jnp.concatenate([pad(wr_hi), pad(wr_lo)], axis=1)
    br_pad = jnp.pad(b_router[l][None, :], ((0, 0), (0, LANES - n_exp)), constant_values=MASK_VALUE)
    x1, h2, top_w, ldest, tile_cnt = _out_router(
        x2, ret, att, mod3, w_out[l].astype(BF16), g_norm2[l][None, :], wr_split, br_pad, seq, n_exp)

    meta, tail, region_blocks, n_rows = _routing_tables(tile_cnt[:, :, 0].astype(I32), t)
    xs = _dispatch(meta, tail, ldest, h2, n_rows, n_exp)
    eo = _experts(region_blocks, xs, w_expert_in[l], b_expert_in[l][:, None, :],
                  w_expert_out[l], b_expert_out[l][:, None, :])
    out = _combine(meta, eo, ldest.T, top_w.T, x1, mod3, seq, n_exp)
    return out.reshape(batch, seq, d)
```

```python
import functools
import itertools
import math

import jax
import jax.numpy as jnp
from jax import lax
from jax.experimental import pallas as pl
from jax.experimental.pallas import tpu as pltpu

F32 = jnp.float32
BF16 = jnp.bfloat16
I32 = jnp.int32

EPS = 1e-6
LOG2E = 1.4426950216293335
MASK_VALUE = -1e30
LANES = 128
BF16_SUBLANES = 16
VMEM_LIMIT_BYTES = 48 * 1024 * 1024

RET_HEADS = 4
DIFF_HEADS = 8
TOP_K = 4
SWIGLU_LIMIT = 7.0
SWIGLU_ALPHA = 1.702
LAMBDA_INIT = 0.8 - 0.6 * math.exp(-0.3 * 0)

IN_TM, IN_TN = 1024, 2048
RET_TC = 256
RET_CHUNKS_PER_STEP = 2
ATT_T = 512
TOK_TM = 512
EXP_BLK = 256
RUN_ALIGN = BF16_SUBLANES
GATHER_RC = 256
SCATTER_RC = 128
COMBINE_KC = 512
W_CHUNKS = 16


def _round_up(n, m):
    return (n + m - 1) // m * m


def _stage_rows(n_exp):
    return _round_up(TOK_TM * TOP_K + n_exp * (RUN_ALIGN - 1), max(GATHER_RC, COMBINE_KC))


SLAB_SIZES = tuple(TOK_TM >> s for s in range(TOK_TM.bit_length()) if TOK_TM >> s >= RUN_ALIGN)
TAIL_SIZES = tuple(s for s in SLAB_SIZES if s < EXP_BLK)


def _params(*sem):
    return pltpu.CompilerParams(dimension_semantics=sem, vmem_limit_bytes=VMEM_LIMIT_BYTES)


def _sigmoid(x):
    return 0.5 * jnp.tanh(0.5 * x) + 0.5


def _rms(x, axis=-1):
    return x * lax.rsqrt(jnp.mean(x * x, axis=axis, keepdims=True) + EPS)


def _mod_kernel(c_ref, w_ref, b_ref, o_ref):
    c = c_ref[...]
    s = c * _sigmoid(c)
    o_ref[...] = jnp.dot(s, w_ref[...], preferred_element_type=F32,
                         precision=lax.Precision.HIGHEST) + b_ref[...]


def _adaln_mod(c, w, b):
    bn, d = c.shape
    n = w.shape[1]
    tn = d
    return pl.pallas_call(
        _mod_kernel,
        out_shape=jax.ShapeDtypeStruct((bn, n), F32),
        grid=(n // tn,),
        in_specs=[pl.BlockSpec((bn, d), lambda j: (0, 0)),
                  pl.BlockSpec((d, tn), lambda j: (0, j)),
                  pl.BlockSpec((1, tn), lambda j: (0, j))],
        out_specs=pl.BlockSpec((bn, tn), lambda j: (0, j)),
        compiler_params=_params("arbitrary"),
        name="adaln_mod",
    )(c, w, b)


def _inproj_kernel(x_ref, mod_ref, g_ref, w_ref, o_ref, h_ref):
    @pl.when(pl.program_id(1) == 0)
    def _():
        m = mod_ref[0]
        h = _rms(x_ref[...]) * g_ref[...] * (1.0 + m[1:2]) + m[0:1]
        h_ref[...] = h.astype(BF16)

    o_ref[...] = jnp.dot(h_ref[...], w_ref[...], preferred_element_type=F32).astype(o_ref.dtype)


def _in_proj(x2, mod3, g1, w_in_bf16, seq):
    t, d = x2.shape
    n = w_in_bf16.shape[1]
    tm, tn = IN_TM, IN_TN
    return pl.pallas_call(
        _inproj_kernel,
        out_shape=jax.ShapeDtypeStruct((t, n), BF16),
        grid=(t // tm, n // tn),
        in_specs=[pl.BlockSpec((tm, d), lambda i, j: (i, 0)),
                  pl.BlockSpec((1, 6, d), lambda i, j: (i * tm // seq, 0, 0)),
                  pl.BlockSpec((1, d), lambda i, j: (0, 0)),
                  pl.BlockSpec((d, tn), lambda i, j: (0, j))],
        out_specs=pl.BlockSpec((tm, tn), lambda i, j: (i, j)),
        scratch_shapes=[pltpu.VMEM((tm, d), BF16)],
        compiler_params=_params("arbitrary", "arbitrary"),
        name="in_proj",
    )(x2, mod3, g1, w_in_bf16)


def _ret_kernel(q_ref, k_ref, v_ref, rg_ref, mg_ref, gng_ref, gnb_ref, o_ref, r_ref, decay_ref,
                *, dk, dv, tc):
    scale = dk ** -0.5
    log_gs = [math.log(1.0 - 2.0 ** (-5.0 - h)) for h in range(RET_HEADS)]

    @pl.when(pl.program_id(1) == 0)
    def _():
        r_ref[...] = jnp.zeros_like(r_ref)
        row = lax.broadcasted_iota(I32, (tc, tc), 0)
        col = lax.broadcasted_iota(I32, (tc, tc), 1)
        rel = (row - col).astype(F32)
        for h, log_g in enumerate(log_gs):
            decay_ref[h] = jnp.where(rel >= 0, jnp.exp(log_g * jnp.maximum(rel, 0.0)), 0.0) * scale

    pos = lax.broadcasted_iota(I32, (tc, 1), 0).astype(F32)
    for sub, (h, log_g) in itertools.product(range(RET_CHUNKS_PER_STEP), enumerate(log_gs)):
        rows = slice(sub * tc, (sub + 1) * tc)
        decay = decay_ref[h]
        xi = jnp.exp(log_g * (pos + 1.0))
        zeta = jnp.exp(log_g * (tc - 1.0 - pos))
        g_chunk = math.exp(log_g * tc)
        q = q_ref[rows, h * dk:(h + 1) * dk]
        k = k_ref[rows, h * dk:(h + 1) * dk]
        v = v_ref[rows, h * dv:(h + 1) * dv]
        s = lax.dot_general(q, k, (((1,), (1,)), ((), ())), preferred_element_type=F32)
        y = jnp.dot((s * decay).astype(BF16), v, preferred_element_type=F32)
        state = r_ref[h]
        y = y + jnp.dot(q, state.astype(BF16), preferred_element_type=F32) * xi
        vz = (v.astype(F32) * zeta).astype(BF16)
        kv = lax.dot_general(k, vz, (((0,), (0,)), ((), ())), preferred_element_type=F32)
        r_ref[h] = state * g_chunk + kv * scale

        mu = jnp.mean(y, axis=-1, keepdims=True)
        yc = y - mu
        var = jnp.mean(yc * yc, axis=-1, keepdims=True)
        sl = slice(h * dv, (h + 1) * dv)
        yn = yc * lax.rsqrt(var + EPS) * gng_ref[:, sl] + gnb_ref[:, sl]
        rg = rg_ref[rows, sl].astype(F32)
        gate = _sigmoid(mg_ref[rows, sl].astype(F32))
        o_ref[rows, sl] = (rg * _sigmoid(rg) * yn * gate).astype(o_ref.dtype)


def _retention(proj, gn_g, gn_b, batch, seq, d):
    t = proj.shape[0]
    tc = RET_TC
    tr = tc * RET_CHUNKS_PER_STEP
    nc = seq // tr
    dk = d // (2 * RET_HEADS)
    dv = d // RET_HEADS
    qk_w = RET_HEADS * dk
    row = lambda b, c: b * nc + c
    kernel = functools.partial(_ret_kernel, dk=dk, dv=dv, tc=tc)
    return pl.pallas_call(
        kernel,
        out_shape=jax.ShapeDtypeStruct((t, d), BF16),
        grid=(batch, nc),
        in_specs=[pl.BlockSpec((tr, qk_w), lambda b, c: (row(b, c), 0)),
                  pl.BlockSpec((tr, qk_w), lambda b, c: (row(b, c), 1)),
                  pl.BlockSpec((tr, d), lambda b, c: (row(b, c), 1)),
                  pl.BlockSpec((tr, d), lambda b, c: (row(b, c), 2)),
                  pl.BlockSpec((tr, d), lambda b, c: (row(b, c), 6)),
                  pl.BlockSpec((1, d), lambda b, c: (0, 0)),
                  pl.BlockSpec((1, d), lambda b, c: (0, 0))],
        out_specs=pl.BlockSpec((tr, d), lambda b, c: (row(b, c), 0)),
        scratch_shapes=[pltpu.VMEM((RET_HEADS, dk, dv), F32), pltpu.VMEM((RET_HEADS, tc, tc), F32)],
        compiler_params=_params("arbitrary", "arbitrary"),
        name="retention",
    )(proj, proj, proj, proj, proj, gn_g, gn_b)


def _halves_rms(x, lo, dh):
    sq = x * x
    s_lo = jnp.sum(jnp.where(lo, sq, 0.0), axis=-1, keepdims=True)
    s_hi = jnp.sum(jnp.where(lo, 0.0, sq), axis=-1, keepdims=True)
    inv = jnp.where(lo, lax.rsqrt(s_lo / dh + EPS), lax.rsqrt(s_hi / dh + EPS))
    return x * inv


def _attn_kernel(q_ref, k_ref, v_ref, gate_ref, slope_ref, qca_ref, qcb_ref, gq_ref, gk_ref, gsub_ref,
                 lq1_ref, lk1_ref, lq2_ref, lk2_ref,
                 o_ref, ka_ref, kb_ref, vt_ref, qa_ref, qb_ref, s0_ref, sa_ref, sb_ref, acc_ref, m_ref,
                 *, tile, dh, dv, nk):
    lane = lax.broadcasted_iota(I32, (tile, 2 * dh), 1)
    lo = lane < dh

    slope = slope_ref[0]
    sub = lax.broadcasted_iota(I32, (tile, 2 * dh), 0)
    off = lane & (dh - 1)
    hi_lane = (off < 6) & ((off & 1) == 0)
    lo_lane = (off < 6) & ((off & 1) == 1)
    bias0 = jnp.where(hi_lane, ((sub >> 8) << 8).astype(F32),
                      jnp.where(lo_lane, (sub & 255).astype(F32), 0.0)) * slope
    bias_step = jnp.where(hi_lane, float(tile), 0.0) * slope
    ones_rows = jnp.where(lax.broadcasted_iota(I32, (BF16_SUBLANES, tile), 0) == 0, 1.0, 0.0)
    lo_t = lax.broadcasted_iota(I32, (2 * dh, tile), 0) < dh

    def prepare(c, carry):
        r0 = c * tile if isinstance(c, int) else pl.multiple_of(c * tile, tile)
        kn = _halves_rms(k_ref[pl.ds(r0, tile), :].astype(F32), lo, dh) * gk_ref[...]
        bias = bias0 + lax.convert_element_type(c, F32) * bias_step
        ka_ref[c] = jnp.where(lo, kn, bias).astype(BF16)
        kb_ref[c] = jnp.where(lo, bias, kn).astype(BF16)
        vt_ref[c, :dv, :] = v_ref[pl.ds(r0, tile), :].astype(F32).T.astype(BF16)
        vt_ref[c, dv:, :] = ones_rows.astype(BF16)
        qn = (_halves_rms(q_ref[pl.ds(r0, tile), :].astype(F32), lo, dh) * gq_ref[...]
              * (dh ** -0.5 * LOG2E))
        qn_t = qn.T
        qa_ref[c] = jnp.where(lo_t, qn_t, qca_ref[...]).astype(BF16)
        qb_ref[c] = jnp.where(lo_t, qcb_ref[...], qn_t).astype(BF16)
        return carry

    lax.fori_loop(0, min(2, nk), prepare, 0)
    lam = (jnp.exp(jnp.sum(lq1_ref[...] * lk1_ref[...], axis=-1, keepdims=True))
           - jnp.exp(jnp.sum(lq2_ref[...] * lk2_ref[...], axis=-1, keepdims=True)) + LAMBDA_INIT)

    refs = (gate_ref, gsub_ref, o_ref, ka_ref, kb_ref, vt_ref, qa_ref, qb_ref,
            s0_ref, sa_ref, sb_ref, acc_ref, m_ref)
    _attn_scores(ka_ref, kb_ref, qa_ref, qb_ref, 0, 0, s0_ref, tile)
    _attn_q_tile(0, lam, prepare, *refs, tile=tile, dv=dv, nk=nk)

    def q_tile(qi, carry):
        _attn_q_tile(qi, lam, prepare, *refs, tile=tile, dv=dv, nk=nk)
        return carry

    lax.fori_loop(1, nk, q_tile, 0)


def _attn_scores(ka_ref, kb_ref, qa_ref, qb_ref, jb, q, s_ref, tile):
    for idx, (k_ref, q_ref) in enumerate(((ka_ref, qa_ref), (kb_ref, qb_ref))):
        s = jnp.dot(k_ref[jb], q_ref[q], preferred_element_type=F32)
        s_ref[idx, :tile, :] = s
        s_ref[idx, tile:tile + 1, :] = jnp.max(s, axis=0, keepdims=True)


def _attn_q_tile(qi, lam, prepare, gate_ref, gsub_ref, o_ref, ka_ref, kb_ref, vt_ref, qa_ref, qb_ref,
                 s0_ref, sa_ref, sb_ref, acc_ref, m_ref, *, tile, dv, nk):
    def scores(jb, s_ref, q=qi):
        _attn_scores(ka_ref, kb_ref, qa_ref, qb_ref, jb, q, s_ref, tile)

    def absorb(jb, s_ref, masked):
        vt = vt_ref[jb]
        for idx in range(2):
            s = s_ref[idx, :tile, :]
            if masked:
                krow = lax.broadcasted_iota(I32, (tile, tile), 0)
                qcol = lax.broadcasted_iota(I32, (tile, tile), 1)
                s = jnp.where(krow > qcol, MASK_VALUE, s)
                block_max = jnp.max(s, axis=0, keepdims=True)
            else:
                block_max = s_ref[idx, tile:tile + 1, :]
            m_old = m_ref[idx]
            m_new = jnp.maximum(m_old, block_max)
            m_ref[idx] = m_new
            p = jnp.exp2(s - m_new).astype(BF16)
            acc_ref[idx] = (acc_ref[idx] * jnp.exp2(m_old - m_new)
                            + jnp.dot(vt, p, preferred_element_type=F32))

    acc_ref[...] = jnp.zeros_like(acc_ref)
    m_ref[...] = jnp.full_like(m_ref, MASK_VALUE)
    if isinstance(qi, int):
        absorb(0, s0_ref, True)
    else:
        scores(1, sa_ref)
        absorb(0, s0_ref, False)

        def pair(j2, carry):
            jb = 2 * j2 + 1
            scores(jb + 1, sb_ref)
            absorb(jb, sa_ref, False)
            scores(jb + 2, sa_ref)
            absorb(jb + 1, sb_ref, False)
            return carry

        lax.fori_loop(0, (qi - 1) // 2, pair, 0)

        @pl.when(qi % 2 == 0)
        def _():
            scores(qi, sb_ref)
            absorb(qi - 1, sa_ref, False)
            absorb(qi, sb_ref, True)

        @pl.when(qi % 2 == 1)
        def _():
            absorb(qi, sa_ref, True)

    ahead = (lambda n: min(qi + n, nk - 1)) if isinstance(qi, int) else (lambda n: jnp.minimum(qi + n, nk - 1))
    scores(0, s0_ref, q=ahead(1))
    prepare(ahead(2), 0)
    a1, a2 = acc_ref[0], acc_ref[1]
    o_t = a1[:dv] * (1.0 / a1[dv:dv + 1]) - lam * (a2[:dv] * (1.0 / a2[dv:dv + 1]))
    o = _rms(o_t, axis=0).T * gsub_ref[...] * (1.0 - LAMBDA_INIT)
    rows = pl.ds(pl.multiple_of(qi * tile, tile), tile)
    o_ref[rows, :] = (o * _sigmoid(gate_ref[rows, :].astype(F32))).astype(o_ref.dtype)


def _bf16_terms(x, n):
    terms, rest = [], jnp.asarray(x, F32)
    for _ in range(n):
        term = rest.astype(BF16).astype(F32)
        terms.append(term)
        rest = rest - term
    return terms


def _diff_attention(proj, g_q, g_k, g_sub, lq1, lk1, lq2, lk2, batch, seq, d):
    t = proj.shape[0]
    tile = ATT_T
    nq = seq // tile
    dh = d // (2 * DIFF_HEADS)
    w = 2 * dh
    assert w == LANES
    q_col, k_col, v_col, gate_col = 3 * d // w, 4 * d // w, 5 * d // w, 7 * d // w
    slopes = 2.0 ** -(jnp.arange(DIFF_HEADS, dtype=F32) + 1.0)
    slopes = jnp.broadcast_to(slopes[:, None, None], (DIFF_HEADS, 1, LANES))
    c1, c2, c3 = _bf16_terms(LOG2E, 3)
    qconst = jnp.zeros((dh,), F32).at[:6].set(jnp.stack([c1, c1, c2, c2, c3, c3]))
    zeros = jnp.zeros((dh,), F32)
    qconst_a = jnp.concatenate([zeros, qconst])[:, None]
    qconst_b = jnp.concatenate([qconst, zeros])[:, None]
    tile2 = lambda a: jnp.concatenate([a, a], axis=-1)
    assert tile % 256 == 0
    small = lambda n: pl.BlockSpec((1, n), lambda b, h: (0, 0))
    kernel = functools.partial(_attn_kernel, tile=tile, dh=dh, dv=w, nk=nq)
    return pl.pallas_call(
        kernel,
        out_shape=jax.ShapeDtypeStruct((t, d), BF16),
        grid=(batch, DIFF_HEADS),
        in_specs=[pl.BlockSpec((seq, w), lambda b, h: (b, q_col + h)),
                  pl.BlockSpec((seq, w), lambda b, h: (b, k_col + h)),
                  pl.BlockSpec((seq, w), lambda b, h: (b, v_col + h)),
                  pl.BlockSpec((seq, w), lambda b, h: (b, gate_col + h)),
                  pl.BlockSpec((1, 1, LANES), lambda b, h: (h, 0, 0)),
                  pl.BlockSpec((w, 1), lambda b, h: (0, 0)), pl.BlockSpec((w, 1), lambda b, h: (0, 0)),
                  small(w), small(w), small(w),
                  small(dh), small(dh), small(dh), small(dh)],
        out_specs=pl.BlockSpec((seq, w), lambda b, h: (b, h)),
        scratch_shapes=[pltpu.VMEM((nq, tile, w), BF16),
                        pltpu.VMEM((nq, tile, w), BF16),
                        pltpu.VMEM((nq, w + BF16_SUBLANES, tile), BF16),
                        pltpu.VMEM((nq, w, tile), BF16),
                        pltpu.VMEM((nq, w, tile), BF16),
                        pltpu.VMEM((2, tile + 8, tile), F32),
                        pltpu.VMEM((2, tile + 8, tile), F32),
                        pltpu.VMEM((2, tile + 8, tile), F32),
                        pltpu.VMEM((2, w + BF16_SUBLANES, tile), F32),
                        pltpu.VMEM((2, 1, tile), F32)],
        compiler_params=_params("arbitrary", "arbitrary"),
        name="diff_attn",
    )(proj, proj, proj, proj, slopes, qconst_a, qconst_b, tile2(g_q), tile2(g_k), g_sub,
      lq1, lk1, lq2, lk2)


def _out_kernel(x_ref, ret_ref, att_ref, mod_ref, wo_ref, g2_ref, wr_ref, br_ref,
                x1_ref, h2_ref, tw_ref, ld_ref, cnt_ref, *, tm, n_exp):
    m = mod_ref[0]
    merged = (ret_ref[...].astype(F32) + att_ref[...].astype(F32)).astype(BF16)
    x1 = x_ref[...] + m[2:3] * jnp.dot(merged, wo_ref[...], preferred_element_type=F32)
    x1_ref[...] = x1
    h2 = _rms(x1) * g2_ref[...] * (1.0 + m[4:5]) + m[3:4]
    hi = h2.astype(BF16)
    h2_ref[...] = hi

    lo = (h2 - hi.astype(F32)).astype(BF16)
    both = jnp.dot(hi, wr_ref[...], preferred_element_type=F32)
    logits = (both[:, :LANES] + both[:, LANES:]
              + jnp.dot(lo, wr_ref[:, :LANES], preferred_element_type=F32) + br_ref[...])
    lt = logits.T[:n_exp]

    erow = lax.broadcasted_iota(I32, (n_exp, tm), 0)
    vals, sels = [], []
    for k in range(TOP_K):
        mx = jnp.max(lt, axis=0, keepdims=True)
        idx = jnp.min(jnp.where(lt == mx, erow, n_exp), axis=0, keepdims=True)
        sel = erow == idx
        vals.append(mx)
        sels.append(sel)
        lt = jnp.where(sel, MASK_VALUE, lt)
    exps = [jnp.exp(v - vals[0]) for v in vals]
    inv = 1.0 / (exps[0] + exps[1] + exps[2] + exps[3])
    for k in range(TOP_K):
        tw_ref[k:k + 1, :] = exps[k] * inv

    chosen = jnp.where(sels[0] | sels[1] | sels[2] | sels[3], 1.0, 0.0)
    before = (lax.broadcasted_iota(I32, (tm, tm), 0) < lax.broadcasted_iota(I32, (tm, tm), 1))
    prefix = jnp.dot(chosen.astype(BF16), jnp.where(before, 1.0, 0.0).astype(BF16),
                     preferred_element_type=F32)
    count = jnp.sum(chosen, axis=1, keepdims=True)
    cnt_ref[...] = count
    padded = jnp.ceil(count * (1.0 / RUN_ALIGN)) * RUN_ALIGN
    below = (lax.broadcasted_iota(I32, (n_exp, n_exp), 1) < lax.broadcasted_iota(I32, (n_exp, n_exp), 0))
    run_start = jnp.dot(jnp.where(below, 1.0, 0.0).astype(BF16),
                        jnp.broadcast_to(padded, (n_exp, LANES)).astype(BF16),
                        preferred_element_type=F32)[:, :1]
    pos = prefix + run_start
    for k in range(TOP_K):
        ld_ref[k:k + 1, :] = jnp.sum(jnp.where(sels[k], pos, 0.0), axis=0, keepdims=True).astype(I32)


def _out_router(x2, ret, att, mod3, w_out_bf16, g2, wr_split, br_pad, seq, n_exp):
    t, d = x2.shape
    tm = TOK_TM
    row = lambda i: (i, 0)
    const = lambda i: (0, 0)
    kernel = functools.partial(_out_kernel, tm=tm, n_exp=n_exp)
    return pl.pallas_call(
        kernel,
        out_shape=(jax.ShapeDtypeStruct((t, d), F32),
                   jax.ShapeDtypeStruct((t, d), BF16),
                   jax.ShapeDtypeStruct((TOP_K, t), F32),
                   jax.ShapeDtypeStruct((TOP_K, t), I32),
                   jax.ShapeDtypeStruct((t // tm, n_exp, 1), F32)),
        grid=(t // tm,),
        in_specs=[pl.BlockSpec((tm, d), row), pl.BlockSpec((tm, d), row), pl.BlockSpec((tm, d), row),
                  pl.BlockSpec((1, 6, d), lambda i: (i * tm // seq, 0, 0)),
                  pl.BlockSpec((d, d), const), pl.BlockSpec((1, d), const),
                  pl.BlockSpec((d, 2 * LANES), const), pl.BlockSpec((1, LANES), const)],
        out_specs=(pl.BlockSpec((tm, d), row), pl.BlockSpec((tm, d), row),
                   pl.BlockSpec((TOP_K, tm), lambda i: (0, i)),
                   pl.BlockSpec((TOP_K, tm), lambda i: (0, i)),
                   pl.BlockSpec((None, n_exp, 1), lambda i: (i, 0, 0))),
        compiler_params=_params("arbitrary"),
        name="out_router",
    )(x2, ret, att, mod3, w_out_bf16, g2, wr_split, br_pad)


def _meta_stride(n_exp):
    return 2 * len(SLAB_SIZES) * n_exp + len(SLAB_SIZES)


def _run_slabs(meta_ref, tile, n_exp, local_ref, global_ref, sem, *, to_global, wait):
    n_sizes = len(SLAB_SIZES)
    base = tile * _meta_stride(n_exp)
    for c, size in enumerate(SLAB_SIZES):
        def one(p, carry, c=c, size=size):
            if wait:
                local = glob = 0
            else:
                local = pl.multiple_of(meta_ref[base + c * n_exp + p], RUN_ALIGN)
                glob = pl.multiple_of(meta_ref[base + (n_sizes + c) * n_exp + p], RUN_ALIGN)
            loc = local_ref.at[pl.ds(local, size)]
            glo = global_ref.at[pl.ds(glob, size)]
            cp = pltpu.make_async_copy(loc, glo, sem) if to_global else pltpu.make_async_copy(glo, loc, sem)
            if wait:
                cp.wait()
            else:
                cp.start()
            return carry

        lax.fori_loop(0, meta_ref[base + 2 * n_sizes * n_exp + c], one, 0)


def _dispatch_kernel(meta_ref, tail_ref, ld_ref, h_ref, xs_ref, stage_ref, zero_ref, sems, tail_sem,
                     *, tm, n_exp, n_tiles, n_blocks, stage_rows):
    i = pl.program_id(0)
    slot = i & 1
    stage = stage_ref.at[slot]

    def tails(wait):
        def unused_block(b, carry):
            cp = pltpu.make_async_copy(
                zero_ref, xs_ref.at[pl.ds(pl.multiple_of(b * EXP_BLK, EXP_BLK), EXP_BLK)], tail_sem)
            if wait:
                cp.wait()
            else:
                cp.start()
            return carry

        lax.fori_loop(tail_ref[2 * n_exp] // EXP_BLK, n_blocks, unused_block, 0)

        def per_expert(e, carry):
            start, n = tail_ref[e], tail_ref[n_exp + e]
            off = 0
            for size in TAIL_SIZES:
                take = n & size

                @pl.when(take != 0)
                def _(off=off, size=size):
                    cp = pltpu.make_async_copy(
                        zero_ref.at[pl.ds(0, size)],
                        xs_ref.at[pl.ds(pl.multiple_of(start + off, RUN_ALIGN), size)], tail_sem)
                    if wait:
                        cp.wait()
                    else:
                        cp.start()

                off = off + take
            return carry

        lax.fori_loop(0, n_exp, per_expert, 0)

    @pl.when(i == 0)
    def _():
        zero_ref[...] = jnp.zeros_like(zero_ref)
        tails(False)
        tails(True)

    ld = ld_ref[...]
    h = h_ref[...]
    for c in range(stage_rows // GATHER_RC):
        r = lax.broadcasted_iota(I32, (GATHER_RC, tm), 0) + c * GATHER_RC
        hit = jnp.where(r == ld[0:1], 1.0, jnp.where(r == ld[1:2], 1.0,
                        jnp.where(r == ld[2:3], 1.0, jnp.where(r == ld[3:4], 1.0, 0.0))))
        stage[c * GATHER_RC:(c + 1) * GATHER_RC, :] = jnp.dot(
            hit.astype(BF16), h, preferred_element_type=F32).astype(BF16)

    _run_slabs(meta_ref, i, n_exp, stage, xs_ref, sems.at[slot], to_global=True, wait=False)

    @pl.when(i > 0)
    def _():
        _run_slabs(meta_ref, i - 1, n_exp, stage_ref.at[1 - slot], xs_ref, sems.at[1 - slot],
                   to_global=True, wait=True)

    @pl.when(i == n_tiles - 1)
    def _():
        _run_slabs(meta_ref, i, n_exp, stage, xs_ref, sems.at[slot], to_global=True, wait=True)


def _dispatch(meta, tail, ldest, h2, n_rows, n_exp):
    t, d = h2.shape
    tm = TOK_TM
    n_tiles = t // tm
    stage_rows = _stage_rows(n_exp)
    kernel = functools.partial(_dispatch_kernel, tm=tm, n_exp=n_exp, n_tiles=n_tiles,
                               n_blocks=n_rows // EXP_BLK, stage_rows=stage_rows)
    grid_spec = pltpu.PrefetchScalarGridSpec(
        num_scalar_prefetch=2,
        grid=(n_tiles,),
        in_specs=[pl.BlockSpec((TOP_K, tm), lambda i, m, tl: (0, i)),
                  pl.BlockSpec((tm, d), lambda i, m, tl: (i, 0))],
        out_specs=pl.BlockSpec(memory_space=pl.ANY),
        scratch_shapes=[pltpu.VMEM((2, stage_rows, d), BF16),
                        pltpu.VMEM((EXP_BLK, d), BF16),
                        pltpu.SemaphoreType.DMA((2,)),
                        pltpu.SemaphoreType.DMA],
    )
    return pl.pallas_call(
        kernel,
        out_shape=jax.ShapeDtypeStruct((n_rows, d), BF16),
        grid_spec=grid_spec,
        compiler_params=_params("arbitrary"),
        name="dispatch",
    )(meta, tail, ldest, h2)


def _expert_kernel(region_ref, xs_ref, w1_ref, b1_ref, w2_ref, b2_ref, eo_ref,
                   w1f_ref, w2f_ref, w1b_ref, w2b_ref, x_buf, o_buf, w_sems, x_sems, o_sems,
                   *, f, blk, n_exp, n_blocks):
    e = pl.program_id(0)
    first, count = region_ref[e], region_ref[n_exp + e]
    w_slot = e & 1
    d = w1f_ref.shape[1]
    r1, r2 = d // W_CHUNKS, f // W_CHUNKS

    def rows(j):
        return pl.ds(pl.multiple_of((first + j) * blk, blk), blk)

    def x_copy(j, slot):
        return pltpu.make_async_copy(xs_ref.at[rows(j)], x_buf.at[slot], x_sems.at[slot])

    def o_copy(j, slot):
        return pltpu.make_async_copy(o_buf.at[slot], eo_ref.at[rows(j)], o_sems.at[slot])

    def w_chunk(expert, c, slot, wait):
        for src, dst, r in ((w1_ref, w1f_ref, r1), (w2_ref, w2f_ref, r2)):
            piece = pl.ds(c * r if isinstance(c, int) else pl.multiple_of(c * r, r), r)
            cp = pltpu.make_async_copy(src.at[expert, piece], dst.at[slot, piece], w_sems.at[slot])
            if wait:
                cp.wait()
            else:
                cp.start()

    @pl.when(e == 0)
    def _():
        for c in range(W_CHUNKS):
            w_chunk(0, c, 0, False)

    @pl.when(count > 0)
    def _():
        x_copy(0, 0).start()

    for c in range(W_CHUNKS):
        w_chunk(e, c, w_slot, True)

    @pl.when(count > 0)
    def _():
        w1b_ref[...] = w1f_ref[w_slot].astype(BF16)
        w2b_ref[...] = w2f_ref[w_slot].astype(BF16)

    def next_chunk(c):
        @pl.when(e + 1 < n_exp)
        def _():
            w_chunk(e + 1, c, 1 - w_slot, False)

    def block(j, carry):
        slot = j & 1
        x_copy(j, slot).wait()

        @pl.when(j + 1 < count)
        def _():
            x_copy(j + 1, 1 - slot).start()

        @pl.when(j < W_CHUNKS)
        def _():
            next_chunk(j)

        @pl.when(j >= 2)
        def _():
            o_copy(j - 2, slot).wait()

        gu = jnp.dot(x_buf[slot], w1b_ref[...], preferred_element_type=F32) + b1_ref[0]
        g = jnp.minimum(gu[:, :f], SWIGLU_LIMIT)
        u = jnp.clip(gu[:, f:], -SWIGLU_LIMIT, SWIGLU_LIMIT)
        a = g * _sigmoid(SWIGLU_ALPHA * g) * (u + 1.0)
        o_buf[slot] = (jnp.dot(a.astype(BF16), w2b_ref[...], preferred_element_type=F32)
                       + b2_ref[0]).astype(o_buf.dtype)
        o_copy(j, slot).start()
        return carry

    lax.fori_loop(0, count, block, 0)
    lax.fori_loop(jnp.minimum(count, W_CHUNKS), W_CHUNKS, lambda c, carry: (next_chunk(c), carry)[1], 0)

    @pl.when(count >= 2)
    def _():
        o_copy(count - 2, count & 1).wait()

    @pl.when(count >= 1)
    def _():
        o_copy(count - 1, (count - 1) & 1).wait()

    @pl.when(e == n_exp - 1)
    def _():
        o_buf[0] = jnp.zeros(o_buf.shape[1:], o_buf.dtype)
        used = first + count

        def fill(j, wait):
            cp = pltpu.make_async_copy(
                o_buf.at[0], eo_ref.at[pl.ds(pl.multiple_of(j * blk, blk), blk)], o_sems.at[0])
            if wait:
                cp.wait()
            else:
                cp.start()

        lax.fori_loop(used, n_blocks, lambda j, c: (fill(j, False), c)[1], 0)
        lax.fori_loop(used, n_blocks, lambda j, c: (fill(j, True), c)[1], 0)


def _experts(region, xs, w1, b1, w2, b2):
    n_rows, d = xs.shape
    n_exp, _, f2 = w1.shape
    f = f2 // 2
    blk = EXP_BLK
    kernel = functools.partial(_expert_kernel, f=f, blk=blk, n_exp=n_exp, n_blocks=n_rows // blk)
    grid_spec = pltpu.PrefetchScalarGridSpec(
        num_scalar_prefetch=1,
        grid=(n_exp,),
        in_specs=[pl.BlockSpec(memory_space=pl.ANY),
                  pl.BlockSpec(memory_space=pl.ANY),
                  pl.BlockSpec((1, 1, f2), lambda e, r: (e, 0, 0)),
                  pl.BlockSpec(memory_space=pl.ANY),
                  pl.BlockSpec((1, 1, d), lambda e, r: (e, 0, 0))],
        out_specs=pl.BlockSpec(memory_space=pl.ANY),
        scratch_shapes=[pltpu.VMEM((2, d, f2), F32), pltpu.VMEM((2, f, d), F32),
                        pltpu.VMEM((d, f2), BF16), pltpu.VMEM((f, d), BF16),
                        pltpu.VMEM((2, blk, d), BF16), pltpu.VMEM((2, blk, d), BF16),
                        pltpu.SemaphoreType.DMA((2,)), pltpu.SemaphoreType.DMA((2,)),
                        pltpu.SemaphoreType.DMA((2,))],
    )
    return pl.pallas_call(
        kernel,
        out_shape=jax.ShapeDtypeStruct((n_rows, d), BF16),
        grid_spec=grid_spec,
        compiler_params=_params("arbitrary"),
        name="experts",
    )(region, xs, w1, b1, w2, b2)


def _combine_kernel(meta_ref, eo_ref, ld_ref, w_ref, x1_ref, mod_ref, o_ref, stage_ref, sems,
                    *, tm, n_exp, n_tiles, stage_rows):
    i = pl.program_id(0)
    slot = i & 1

    @pl.when(i == 0)
    def _():
        stage_ref[...] = jnp.zeros_like(stage_ref)
        _run_slabs(meta_ref, 0, n_exp, stage_ref.at[0], eo_ref, sems.at[0], to_global=False, wait=False)

    @pl.when(i + 1 < n_tiles)
    def _():
        _run_slabs(meta_ref, i + 1, n_exp, stage_ref.at[1 - slot], eo_ref, sems.at[1 - slot],
                   to_global=False, wait=False)

    ld = ld_ref[...]
    w = w_ref[...]
    ldb = [jnp.broadcast_to(ld[:, k:k + 1], (tm, SCATTER_RC)) for k in range(TOP_K)]
    wb = [jnp.broadcast_to(w[:, k:k + 1], (tm, SCATTER_RC)) for k in range(TOP_K)]
    col = lax.broadcasted_iota(I32, (tm, SCATTER_RC), 1)

    def sel_piece(r0):
        r = col + r0
        sel = jnp.where(ldb[0] == r, wb[0], jnp.where(ldb[1] == r, wb[1],
                        jnp.where(ldb[2] == r, wb[2], jnp.where(ldb[3] == r, wb[3], 0.0))))
        return sel.astype(BF16)

    _run_slabs(meta_ref, i, n_exp, stage_ref.at[slot], eo_ref, sems.at[slot], to_global=False, wait=True)
    stage = stage_ref.at[slot]
    y = None
    for c in range(stage_rows // COMBINE_KC):
        r0 = c * COMBINE_KC
        sel = jnp.concatenate([sel_piece(r0 + s) for s in range(0, COMBINE_KC, SCATTER_RC)], axis=1)
        part = jnp.dot(sel, stage[r0:r0 + COMBINE_KC, :], preferred_element_type=F32)
        y = part if y is None else y + part
    o_ref[...] = x1_ref[...] + mod_ref[0][5:6] * y


def _combine(meta, eo, ld_tok, w_tok, x1, mod3, seq, n_exp):
    t, d = x1.shape
    tm = TOK_TM
    n_tiles = t // tm
    stage_rows = _stage_rows(n_exp)
    kernel = functools.partial(_combine_kernel, tm=tm, n_exp=n_exp, n_tiles=n_tiles, stage_rows=stage_rows)
    grid_spec = pltpu.PrefetchScalarGridSpec(
        num_scalar_prefetch=1,
        grid=(n_tiles,),
        in_specs=[pl.BlockSpec(memory_space=pl.ANY),
                  pl.BlockSpec((tm, TOP_K), lambda i, m: (i, 0)),
                  pl.BlockSpec((tm, TOP_K), lambda i, m: (i, 0)),
                  pl.BlockSpec((tm, d), lambda i, m: (i, 0)),
                  pl.BlockSpec((1, 6, d), lambda i, m: (i * tm // seq, 0, 0))],
        out_specs=pl.BlockSpec((tm, d), lambda i, m: (i, 0)),
        scratch_shapes=[pltpu.VMEM((2, stage_rows, d), BF16),
                        pltpu.SemaphoreType.DMA((2,))],
    )
    return pl.pallas_call(
        kernel,
        out_shape=jax.ShapeDtypeStruct((t, d), F32),
        grid_spec=grid_spec,
        compiler_params=_params("arbitrary"),
        name="combine",
    )(meta, eo, ld_tok, w_tok, x1, mod3)


def _routing_tables(tile_cnt, n_tokens):
    n_tiles, n_exp = tile_cnt.shape
    blk = EXP_BLK
    run = _round_up(tile_cnt, RUN_ALIGN)
    local = jnp.cumsum(run, axis=1) - run
    size = jnp.sum(run, axis=0)
    region = _round_up(size, blk)
    region_end = jnp.cumsum(region)
    region_start = region_end - region
    glob = region_start[None, :] + jnp.cumsum(run, axis=0) - run
    sizes = jnp.asarray(SLAB_SIZES, I32)
    has = (run[:, :, None] & sizes) != 0
    before = run[:, :, None] & ~(2 * sizes - 1)
    place = jnp.cumsum(has, axis=1) - has
    hit = has[..., None] & (place[..., None] == jnp.arange(n_exp, dtype=I32))
    listed = lambda start: jnp.sum(
        jnp.where(hit, (start[:, :, None] + before)[..., None], 0), axis=1).reshape(n_tiles, -1)
    meta = jnp.concatenate([listed(local), listed(glob), jnp.sum(has, axis=1)],
                           axis=1).reshape(-1).astype(I32)
    tail = jnp.concatenate([region_start + size, region - size, region_end[-1:]]).astype(I32)
    n_rows = _round_up(n_tokens * TOP_K + n_tiles * n_exp * RUN_ALIGN + n_exp * blk, blk)
    region_blocks = jnp.concatenate([region_start // blk, region // blk]).astype(I32)
    return meta, tail, region_blocks, n_rows


def kernel(x, c, w_ada, b_ada, g_norm1, w_in, g_ret_gn, b_ret_gn, g_qnorm, g_knorm,
           lambda_q1, lambda_k1, lambda_q2, lambda_k2, g_diff_subln, w_out, g_norm2,
           w_router, b_router, w_expert_in, b_expert_in, w_expert_out, b_expert_out):
    batch, seq, d = x.shape
    depth = w_ada.shape[0]
    assert depth == 1
    t = batch * seq
    n_exp = w_router.shape[-1]
    l = 0

    mod = _adaln_mod(c, w_ada[l], b_ada[l][None, :])
    mod3 = mod.reshape(batch, 6, d)
    x2 = x.reshape(t, d)

    proj = _in_proj(x2, mod3, g_norm1[l][None, :], w_in[l].astype(BF16), seq)
    ret = _retention(proj, g_ret_gn[l][None, :], b_ret_gn[l][None, :], batch, seq, d)
    att = _diff_attention(proj, g_qnorm[l][None, :], g_knorm[l][None, :], g_diff_subln[l][None, :],
                          lambda_q1[l][None, :], lambda_k1[l][None, :],
                          lambda_q2[l][None, :], lambda_k2[l][None, :], batch, seq, d)

    wr = w_router[l]
    wr_hi = wr.astype(BF16)
    wr_lo = (wr - wr_hi.astype(F32)).astype(BF16)
    pad = lambda a: jnp.pad(a, ((0, 0), (0, LANES - n_exp)))
    wr_split = jnp.concatenate([pad(wr_hi), pad(wr_lo)], axis=1)
    br_pad = jnp.pad(b_router[l][None, :], ((0, 0), (0, LANES - n_exp)), constant_values=MASK_VALUE)
    x1, h2, top_w, ldest, tile_cnt = _out_router(
        x2, ret, att, mod3, w_out[l].astype(BF16), g_norm2[l][None, :], wr_split, br_pad, seq, n_exp)

    meta, tail, region_blocks, n_rows = _routing_tables(tile_cnt[:, :, 0].astype(I32), t)
    xs = _dispatch(meta, tail, ldest, h2, n_rows, n_exp)
    eo = _experts(region_blocks, xs, w_expert_in[l], b_expert_in[l][:, None, :],
                  w_expert_out[l], b_expert_out[l][:, None, :])
    out = _combine(meta, eo, ldest.T, top_w.T, x1, mod3, seq, n_exp)
    return out.reshape(batch, seq, d)
```

```python
import functools
import itertools
import math

import jax
import jax.numpy as jnp
from jax import lax
from jax.experimental import pallas as pl
from jax.experimental.pallas import tpu as pltpu

F32 = jnp.float32
BF16 = jnp.bfloat16
I32 = jnp.int32

EPS = 1e-6
LOG2E = 1.4426950216293335
MASK_VALUE = -1e30
LANES = 128
BF16_SUBLANES = 16
VMEM_LIMIT_BYTES = 48 * 1024 * 1024

RET_HEADS = 4
DIFF_HEADS = 8
TOP_K = 4
SWIGLU_LIMIT = 7.0
SWIGLU_ALPHA = 1.702
LAMBDA_INIT = 0.8 - 0.6 * math.exp(-0.3 * 0)

IN_TM, IN_TN = 1024, 2048
RET_TC = 256
RET_CHUNKS_PER_STEP = 2
ATT_T = 512
TOK_TM = 512
EXP_BLK = 512
RUN_ALIGN = BF16_SUBLANES
GATHER_RC = 256
SCATTER_RC = 128
COMBINE_KC = 512
W_CHUNKS = 8


def _round_up(n, m):
    return (n + m - 1) // m * m


def _stage_rows(n_exp):
    return _round_up(TOK_TM * TOP_K + n_exp * (RUN_ALIGN - 1), max(GATHER_RC, COMBINE_KC))


SLAB_SIZES = tuple(TOK_TM >> s for s in range(TOK_TM.bit_length()) if TOK_TM >> s >= RUN_ALIGN)
TAIL_SIZES = tuple(s for s in SLAB_SIZES if s < EXP_BLK)


def _params(*sem):
    return pltpu.CompilerParams(dimension_semantics=sem, vmem_limit_bytes=VMEM_LIMIT_BYTES)


def _sigmoid(x):
    return 0.5 * jnp.tanh(0.5 * x) + 0.5


def _rms(x, axis=-1):
    return x * lax.rsqrt(jnp.mean(x * x, axis=axis, keepdims=True) + EPS)


def _mod_kernel(c_ref, w_ref, b_ref, o_ref):
    c = c_ref[...]
    s = c * _sigmoid(c)
    o_ref[...] = jnp.dot(s, w_ref[...], preferred_element_type=F32,
                         precision=lax.Precision.HIGHEST) + b_ref[...]


def _adaln_mod(c, w, b):
    bn, d = c.shape
    n = w.shape[1]
    tn = d
    return pl.pallas_call(
        _mod_kernel,
        out_shape=jax.ShapeDtypeStruct((bn, n), F32),
        grid=(n // tn,),
        in_specs=[pl.BlockSpec((bn, d), lambda j: (0, 0)),
                  pl.BlockSpec((d, tn), lambda j: (0, j)),
                  pl.BlockSpec((1, tn), lambda j: (0, j))],
        out_specs=pl.BlockSpec((bn, tn), lambda j: (0, j)),
        compiler_params=_params("arbitrary"),
        name="adaln_mod",
    )(c, w, b)


def _inproj_kernel(x_ref, mod_ref, g_ref, w_ref, o_ref, h_ref):
    @pl.when(pl.program_id(1) == 0)
    def _():
        m = mod_ref[0]
        h = _rms(x_ref[...]) * g_ref[...] * (1.0 + m[1:2]) + m[0:1]
        h_ref[...] = h.astype(BF16)

    o_ref[...] = jnp.dot(h_ref[...], w_ref[...], preferred_element_type=F32).astype(o_ref.dtype)


def _in_proj(x2, mod3, g1, w_in_bf16, seq):
    t, d = x2.shape
    n = w_in_bf16.shape[1]
    tm, tn = IN_TM, IN_TN
    return pl.pallas_call(
        _inproj_kernel,
        out_shape=jax.ShapeDtypeStruct((t, n), BF16),
        grid=(t // tm, n // tn),
        in_specs=[pl.BlockSpec((tm, d), lambda i, j: (i, 0)),
                  pl.BlockSpec((1, 6, d), lambda i, j: (i * tm // seq, 0, 0)),
                  pl.BlockSpec((1, d), lambda i, j: (0, 0)),
                  pl.BlockSpec((d, tn), lambda i, j: (0, j))],
        out_specs=pl.BlockSpec((tm, tn), lambda i, j: (i, j)),
        scratch_shapes=[pltpu.VMEM((tm, d), BF16)],
        compiler_params=_params("arbitrary", "arbitrary"),
        name="in_proj",
    )(x2, mod3, g1, w_in_bf16)


def _ret_kernel(q_ref, k_ref, v_ref, rg_ref, mg_ref, gng_ref, gnb_ref, o_ref, r_ref, decay_ref,
                *, dk, dv, tc):
    scale = dk ** -0.5
    log_gs = [math.log(1.0 - 2.0 ** (-5.0 - h)) for h in range(RET_HEADS)]

    @pl.when(pl.program_id(1) == 0)
    def _():
        r_ref[...] = jnp.zeros_like(r_ref)
        row = lax.broadcasted_iota(I32, (tc, tc), 0)
        col = lax.broadcasted_iota(I32, (tc, tc), 1)
        rel = (row - col).astype(F32)
        for h, log_g in enumerate(log_gs):
            decay_ref[h] = jnp.where(rel >= 0, jnp.exp(log_g * jnp.maximum(rel, 0.0)), 0.0) * scale

    pos = lax.broadcasted_iota(I32, (tc, 1), 0).astype(F32)
    for sub, (h, log_g) in itertools.product(range(RET_CHUNKS_PER_STEP), enumerate(log_gs)):
        rows = slice(sub * tc, (sub + 1) * tc)
        decay = decay_ref[h]
        xi = jnp.exp(log_g * (pos + 1.0))
        zeta = jnp.exp(log_g * (tc - 1.0 - pos))
        g_chunk = math.exp(log_g * tc)
        q = q_ref[rows, h * dk:(h + 1) * dk]
        k = k_ref[rows, h * dk:(h + 1) * dk]
        v = v_ref[rows, h * dv:(h + 1) * dv]
        s = lax.dot_general(q, k, (((1,), (1,)), ((), ())), preferred_element_type=F32)
        y = jnp.dot((s * decay).astype(BF16), v, preferred_element_type=F32)
        state = r_ref[h]
        y = y + jnp.dot(q, state.astype(BF16), preferred_element_type=F32) * xi
        vz = (v.astype(F32) * zeta).astype(BF16)
        kv = lax.dot_general(k, vz, (((0,), (0,)), ((), ())), preferred_element_type=F32)
        r_ref[h] = state * g_chunk + kv * scale

        mu = jnp.mean(y, axis=-1, keepdims=True)
        yc = y - mu
        var = jnp.mean(yc * yc, axis=-1, keepdims=True)
        sl = slice(h * dv, (h + 1) * dv)
        yn = yc * lax.rsqrt(var + EPS) * gng_ref[:, sl] + gnb_ref[:, sl]
        rg = rg_ref[rows, sl].astype(F32)
        gate = _sigmoid(mg_ref[rows, sl].astype(F32))
        o_ref[rows, sl] = (rg * _sigmoid(rg) * yn * gate).astype(o_ref.dtype)


def _retention(proj, gn_g, gn_b, batch, seq, d):
    t = proj.shape[0]
    tc = RET_TC
    tr = tc * RET_CHUNKS_PER_STEP
    nc = seq // tr
    dk = d // (2 * RET_HEADS)
    dv = d // RET_HEADS
    qk_w = RET_HEADS * dk
    row = lambda b, c: b * nc + c
    kernel = functools.partial(_ret_kernel, dk=dk, dv=dv, tc=tc)
    return pl.pallas_call(
        kernel,
        out_shape=jax.ShapeDtypeStruct((t, d), BF16),
        grid=(batch, nc),
        in_specs=[pl.BlockSpec((tr, qk_w), lambda b, c: (row(b, c), 0)),
                  pl.BlockSpec((tr, qk_w), lambda b, c: (row(b, c), 1)),
                  pl.BlockSpec((tr, d), lambda b, c: (row(b, c), 1)),
                  pl.BlockSpec((tr, d), lambda b, c: (row(b, c), 2)),
                  pl.BlockSpec((tr, d), lambda b, c: (row(b, c), 6)),
                  pl.BlockSpec((1, d), lambda b, c: (0, 0)),
                  pl.BlockSpec((1, d), lambda b, c: (0, 0))],
        out_specs=pl.BlockSpec((tr, d), lambda b, c: (row(b, c), 0)),
        scratch_shapes=[pltpu.VMEM((RET_HEADS, dk, dv), F32), pltpu.VMEM((RET_HEADS, tc, tc), F32)],
        compiler_params=_params("arbitrary", "arbitrary"),
        name="retention",
    )(proj, proj, proj, proj, proj, gn_g, gn_b)


def _halves_rms(x, lo, dh):
    sq = x * x
    s_lo = jnp.sum(jnp.where(lo, sq, 0.0), axis=-1, keepdims=True)
    s_hi = jnp.sum(jnp.where(lo, 0.0, sq), axis=-1, keepdims=True)
    inv = jnp.where(lo, lax.rsqrt(s_lo / dh + EPS), lax.rsqrt(s_hi / dh + EPS))
    return x * inv


def _attn_kernel(q_ref, k_ref, v_ref, gate_ref, slope_ref, qca_ref, qcb_ref, gq_ref, gk_ref, gsub_ref,
                 lq1_ref, lk1_ref, lq2_ref, lk2_ref,
                 o_ref, ka_ref, kb_ref, vt_ref, qa_ref, qb_ref, s0_ref, sa_ref, sb_ref, acc_ref, m_ref,
                 *, tile, dh, dv, nk):
    lane = lax.broadcasted_iota(I32, (tile, 2 * dh), 1)
    lo = lane < dh

    slope = slope_ref[0]
    sub = lax.broadcasted_iota(I32, (tile, 2 * dh), 0)
    off = lane & (dh - 1)
    hi_lane = (off < 6) & ((off & 1) == 0)
    lo_lane = (off < 6) & ((off & 1) == 1)
    bias0 = jnp.where(hi_lane, ((sub >> 8) << 8).astype(F32),
                      jnp.where(lo_lane, (sub & 255).astype(F32), 0.0)) * slope
    bias_step = jnp.where(hi_lane, float(tile), 0.0) * slope
    ones_rows = jnp.where(lax.broadcasted_iota(I32, (BF16_SUBLANES, tile), 0) == 0, 1.0, 0.0)
    lo_t = lax.broadcasted_iota(I32, (2 * dh, tile), 0) < dh

    def prepare(c, carry):
        r0 = c * tile if isinstance(c, int) else pl.multiple_of(c * tile, tile)
        kn = _halves_rms(k_ref[pl.ds(r0, tile), :].astype(F32), lo, dh) * gk_ref[...]
        bias = bias0 + lax.convert_element_type(c, F32) * bias_step
        ka_ref[c] = jnp.where(lo, kn, bias).astype(BF16)
        kb_ref[c] = jnp.where(lo, bias, kn).astype(BF16)
        vt_ref[c, :dv, :] = v_ref[pl.ds(r0, tile), :].astype(F32).T.astype(BF16)
        vt_ref[c, dv:, :] = ones_rows.astype(BF16)
        qn = (_halves_rms(q_ref[pl.ds(r0, tile), :].astype(F32), lo, dh) * gq_ref[...]
              * (dh ** -0.5 * LOG2E))
        qn_t = qn.T
        qa_ref[c] = jnp.where(lo_t, qn_t, qca_ref[...]).astype(BF16)
        qb_ref[c] = jnp.where(lo_t, qcb_ref[...], qn_t).astype(BF16)
        return carry

    lax.fori_loop(0, min(2, nk), prepare, 0)
    lam = (jnp.exp(jnp.sum(lq1_ref[...] * lk1_ref[...], axis=-1, keepdims=True))
           - jnp.exp(jnp.sum(lq2_ref[...] * lk2_ref[...], axis=-1, keepdims=True)) + LAMBDA_INIT)

    refs = (gate_ref, gsub_ref, o_ref, ka_ref, kb_ref, vt_ref, qa_ref, qb_ref,
            s0_ref, sa_ref, sb_ref, acc_ref, m_ref)
    _attn_scores(ka_ref, kb_ref, qa_ref, qb_ref, 0, 0, s0_ref, tile)
    _attn_q_tile(0, lam, prepare, *refs, tile=tile, dv=dv, nk=nk)

    def q_tile(qi, carry):
        _attn_q_tile(qi, lam, prepare, *refs, tile=tile, dv=dv, nk=nk)
        return carry

    lax.fori_loop(1, nk, q_tile, 0)


def _attn_scores(ka_ref, kb_ref, qa_ref, qb_ref, jb, q, s_ref, tile):
    for idx, (k_ref, q_ref) in enumerate(((ka_ref, qa_ref), (kb_ref, qb_ref))):
        s = jnp.dot(k_ref[jb], q_ref[q], preferred_element_type=F32)
        s_ref[idx, :tile, :] = s
        s_ref[idx, tile:tile + 1, :] = jnp.max(s, axis=0, keepdims=True)


def _attn_q_tile(qi, lam, prepare, gate_ref, gsub_ref, o_ref, ka_ref, kb_ref, vt_ref, qa_ref, qb_ref,
                 s0_ref, sa_ref, sb_ref, acc_ref, m_ref, *, tile, dv, nk):
    def scores(jb, s_ref, q=qi):
        _attn_scores(ka_ref, kb_ref, qa_ref, qb_ref, jb, q, s_ref, tile)

    def absorb(jb, s_ref, masked):
        vt = vt_ref[jb]
        for idx in range(2):
            s = s_ref[idx, :tile, :]
            if masked:
                krow = lax.broadcasted_iota(I32, (tile, tile), 0)
                qcol = lax.broadcasted_iota(I32, (tile, tile), 1)
                s = jnp.where(krow > qcol, MASK_VALUE, s)
                block_max = jnp.max(s, axis=0, keepdims=True)
            else:
                block_max = s_ref[idx, tile:tile + 1, :]
            m_old = m_ref[idx]
            m_new = jnp.maximum(m_old, block_max)
            m_ref[idx] = m_new
            p = jnp.exp2(s - m_new).astype(BF16)
            acc_ref[idx] = (acc_ref[idx] * jnp.exp2(m_old - m_new)
                            + jnp.dot(vt, p, preferred_element_type=F32))

    acc_ref[...] = jnp.zeros_like(acc_ref)
    m_ref[...] = jnp.full_like(m_ref, MASK_VALUE)
    if isinstance(qi, int):
        absorb(0, s0_ref, True)
    else:
        scores(1, sa_ref)
        absorb(0, s0_ref, False)

        def pair(j2, carry):
            jb = 2 * j2 + 1
            scores(jb + 1, sb_ref)
            absorb(jb, sa_ref, False)
            scores(jb + 2, sa_ref)
            absorb(jb + 1, sb_ref, False)
            return carry

        lax.fori_loop(0, (qi - 1) // 2, pair, 0)

        @pl.when(qi % 2 == 0)
        def _():
            scores(qi, sb_ref)
            absorb(qi - 1, sa_ref, False)
            absorb(qi, sb_ref, True)

        @pl.when(qi % 2 == 1)
        def _():
            absorb(qi, sa_ref, True)

    ahead = (lambda n: min(qi + n, nk - 1)) if isinstance(qi, int) else (lambda n: jnp.minimum(qi + n, nk - 1))
    scores(0, s0_ref, q=ahead(1))
    prepare(ahead(2), 0)
    a1, a2 = acc_ref[0], acc_ref[1]
    o_t = a1[:dv] * (1.0 / a1[dv:dv + 1]) - lam * (a2[:dv] * (1.0 / a2[dv:dv + 1]))
    o = _rms(o_t, axis=0).T * gsub_ref[...] * (1.0 - LAMBDA_INIT)
    rows = pl.ds(pl.multiple_of(qi * tile, tile), tile)
    o_ref[rows, :] = (o * _sigmoid(gate_ref[rows, :].astype(F32))).astype(o_ref.dtype)


def _bf16_terms(x, n):
    terms, rest = [], jnp.asarray(x, F32)
    for _ in range(n):
        term = rest.astype(BF16).astype(F32)
        terms.append(term)
        rest = rest - term
    return terms


def _diff_attention(proj, g_q, g_k, g_sub, lq1, lk1, lq2, lk2, batch, seq, d):
    t = proj.shape[0]
    tile = ATT_T
    nq = seq // tile
    dh = d // (2 * DIFF_HEADS)
    w = 2 * dh
    assert w == LANES
    q_col, k_col, v_col, gate_col = 3 * d // w, 4 * d // w, 5 * d // w, 7 * d // w
    slopes = 2.0 ** -(jnp.arange(DIFF_HEADS, dtype=F32) + 1.0)
    slopes = jnp.broadcast_to(slopes[:, None, None], (DIFF_HEADS, 1, LANES))
    c1, c2, c3 = _bf16_terms(LOG2E, 3)
    qconst = jnp.zeros((dh,), F32).at[:6].set(jnp.stack([c1, c1, c2, c2, c3, c3]))
    zeros = jnp.zeros((dh,), F32)
    qconst_a = jnp.concatenate([zeros, qconst])[:, None]
    qconst_b = jnp.concatenate([qconst, zeros])[:, None]
    tile2 = lambda a: jnp.concatenate([a, a], axis=-1)
    assert tile % 256 == 0
    small = lambda n: pl.BlockSpec((1, n), lambda b, h: (0, 0))
    kernel = functools.partial(_attn_kernel, tile=tile, dh=dh, dv=w, nk=nq)
    return pl.pallas_call(
        kernel,
        out_shape=jax.ShapeDtypeStruct((t, d), BF16),
        grid=(batch, DIFF_HEADS),
        in_specs=[pl.BlockSpec((seq, w), lambda b, h: (b, q_col + h)),
                  pl.BlockSpec((seq, w), lambda b, h: (b, k_col + h)),
                  pl.BlockSpec((seq, w), lambda b, h: (b, v_col + h)),
                  pl.BlockSpec((seq, w), lambda b, h: (b, gate_col + h)),
                  pl.BlockSpec((1, 1, LANES), lambda b, h: (h, 0, 0)),
                  pl.BlockSpec((w, 1), lambda b, h: (0, 0)), pl.BlockSpec((w, 1), lambda b, h: (0, 0)),
                  small(w), small(w), small(w),
                  small(dh), small(dh), small(dh), small(dh)],
        out_specs=pl.BlockSpec((seq, w), lambda b, h: (b, h)),
        scratch_shapes=[pltpu.VMEM((nq, tile, w), BF16),
                        pltpu.VMEM((nq, tile, w), BF16),
                        pltpu.VMEM((nq, w + BF16_SUBLANES, tile), BF16),
                        pltpu.VMEM((nq, w, tile), BF16),
                        pltpu.VMEM((nq, w, tile), BF16),
                        pltpu.VMEM((2, tile + 8, tile), F32),
                        pltpu.VMEM((2, tile + 8, tile), F32),
                        pltpu.VMEM((2, tile + 8, tile), F32),
                        pltpu.VMEM((2, w + BF16_SUBLANES, tile), F32),
                        pltpu.VMEM((2, 1, tile), F32)],
        compiler_params=_params("arbitrary", "arbitrary"),
        name="diff_attn",
    )(proj, proj, proj, proj, slopes, qconst_a, qconst_b, tile2(g_q), tile2(g_k), g_sub,
      lq1, lk1, lq2, lk2)


def _out_kernel(x_ref, ret_ref, att_ref, mod_ref, wo_ref, g2_ref, wr_ref, br_ref,
                x1_ref, h2_ref, tw_ref, ld_ref, cnt_ref, *, tm, n_exp):
    m = mod_ref[0]
    merged = (ret_ref[...].astype(F32) + att_ref[...].astype(F32)).astype(BF16)
    x1 = x_ref[...] + m[2:3] * jnp.dot(merged, wo_ref[...], preferred_element_type=F32)
    x1_ref[...] = x1
    h2 = _rms(x1) * g2_ref[...] * (1.0 + m[4:5]) + m[3:4]
    hi = h2.astype(BF16)
    h2_ref[...] = hi

    lo = (h2 - hi.astype(F32)).astype(BF16)
    both = jnp.dot(hi, wr_ref[...], preferred_element_type=F32)
    logits = (both[:, :LANES] + both[:, LANES:]
              + jnp.dot(lo, wr_ref[:, :LANES], preferred_element_type=F32) + br_ref[...])
    lt = logits.T[:n_exp]

    erow = lax.broadcasted_iota(I32, (n_exp, tm), 0)
    vals, sels = [], []
    for k in range(TOP_K):
        mx = jnp.max(lt, axis=0, keepdims=True)
        idx = jnp.min(jnp.where(lt == mx, erow, n_exp), axis=0, keepdims=True)
        sel = erow == idx
        vals.append(mx)
        sels.append(sel)
        lt = jnp.where(sel, MASK_VALUE, lt)
    exps = [jnp.exp(v - vals[0]) for v in vals]
    inv = 1.0 / (exps[0] + exps[1] + exps[2] + exps[3])
    for k in range(TOP_K):
        tw_ref[k:k + 1, :] = exps[k] * inv

    chosen = jnp.where(sels[0] | sels[1] | sels[2] | sels[3], 1.0, 0.0)
    before = (lax.broadcasted_iota(I32, (tm, tm), 0) < lax.broadcasted_iota(I32, (tm, tm), 1))
    prefix = jnp.dot(chosen.astype(BF16), jnp.where(before, 1.0, 0.0).astype(BF16),
                     preferred_element_type=F32)
    count = jnp.sum(chosen, axis=1, keepdims=True)
    cnt_ref[...] = count
    padded = jnp.ceil(count * (1.0 / RUN_ALIGN)) * RUN_ALIGN
    below = (lax.broadcasted_iota(I32, (n_exp, n_exp), 1) < lax.broadcasted_iota(I32, (n_exp, n_exp), 0))
    run_start = jnp.dot(jnp.where(below, 1.0, 0.0).astype(BF16),
                        jnp.broadcast_to(padded, (n_exp, LANES)).astype(BF16),
                        preferred_element_type=F32)[:, :1]
    pos = prefix + run_start
    for k in range(TOP_K):
        ld_ref[k:k + 1, :] = jnp.sum(jnp.where(sels[k], pos, 0.0), axis=0, keepdims=True).astype(I32)


def _out_router(x2, ret, att, mod3, w_out_bf16, g2, wr_split, br_pad, seq, n_exp):
    t, d = x2.shape
    tm = TOK_TM
    row = lambda i: (i, 0)
    const = lambda i: (0, 0)
    kernel = functools.partial(_out_kernel, tm=tm, n_exp=n_exp)
    return pl.pallas_call(
        kernel,
        out_shape=(jax.ShapeDtypeStruct((t, d), F32),
                   jax.ShapeDtypeStruct((t, d), BF16),
                   jax.ShapeDtypeStruct((TOP_K, t), F32),
                   jax.ShapeDtypeStruct((TOP_K, t), I32),
                   jax.ShapeDtypeStruct((t // tm, n_exp, 1), F32)),
        grid=(t // tm,),
        in_specs=[pl.BlockSpec((tm, d), row), pl.BlockSpec((tm, d), row), pl.BlockSpec((tm, d), row),
                  pl.BlockSpec((1, 6, d), lambda i: (i * tm // seq, 0, 0)),
                  pl.BlockSpec((d, d), const), pl.BlockSpec((1, d), const),
                  pl.BlockSpec((d, 2 * LANES), const), pl.BlockSpec((1, LANES), const)],
        out_specs=(pl.BlockSpec((tm, d), row), pl.BlockSpec((tm, d), row),
                   pl.BlockSpec((TOP_K, tm), lambda i: (0, i)),
                   pl.BlockSpec((TOP_K, tm), lambda i: (0, i)),
                   pl.BlockSpec((None, n_exp, 1), lambda i: (i, 0, 0))),
        compiler_params=_params("arbitrary"),
        name="out_router",
    )(x2, ret, att, mod3, w_out_bf16, g2, wr_split, br_pad)


def _meta_stride(n_exp):
    return 2 * len(SLAB_SIZES) * n_exp + len(SLAB_SIZES)


def _run_slabs(meta_ref, tile, n_exp, local_ref, global_ref, sem, *, to_global, wait):
    n_sizes = len(SLAB_SIZES)
    base = tile * _meta_stride(n_exp)
    for c, size in enumerate(SLAB_SIZES):
        def one(p, carry, c=c, size=size):
            if wait:
                local = glob = 0
            else:
                local = pl.multiple_of(meta_ref[base + c * n_exp + p], RUN_ALIGN)
                glob = pl.multiple_of(meta_ref[base + (n_sizes + c) * n_exp + p], RUN_ALIGN)
            loc = local_ref.at[pl.ds(local, size)]
            glo = global_ref.at[pl.ds(glob, size)]
            cp = pltpu.make_async_copy(loc, glo, sem) if to_global else pltpu.make_async_copy(glo, loc, sem)
            if wait:
                cp.wait()
            else:
                cp.start()
            return carry

        lax.fori_loop(0, meta_ref[base + 2 * n_sizes * n_exp + c], one, 0)


def _dispatch_kernel(meta_ref, tail_ref, ld_ref, h_ref, xs_ref, stage_ref, zero_ref, sems, tail_sem,
                     *, tm, n_exp, n_tiles, n_blocks, stage_rows):
    i = pl.program_id(0)
    slot = i & 1
    stage = stage_ref.at[slot]

    def tails(wait):
        def unused_block(b, carry):
            cp = pltpu.make_async_copy(
                zero_ref, xs_ref.at[pl.ds(pl.multiple_of(b * EXP_BLK, EXP_BLK), EXP_BLK)], tail_sem)
            if wait:
                cp.wait()
            else:
                cp.start()
            return carry

        lax.fori_loop(tail_ref[2 * n_exp] // EXP_BLK, n_blocks, unused_block, 0)

        def per_expert(e, carry):
            start, n = tail_ref[e], tail_ref[n_exp + e]
            off = 0
            for size in TAIL_SIZES:
                take = n & size

                @pl.when(take != 0)
                def _(off=off, size=size):
                    cp = pltpu.make_async_copy(
                        zero_ref.at[pl.ds(0, size)],
                        xs_ref.at[pl.ds(pl.multiple_of(start + off, RUN_ALIGN), size)], tail_sem)
                    if wait:
                        cp.wait()
                    else:
                        cp.start()

                off = off + take
            return carry

        lax.fori_loop(0, n_exp, per_expert, 0)

    @pl.when(i == 0)
    def _():
        zero_ref[...] = jnp.zeros_like(zero_ref)
        tails(False)
        tails(True)

    ld = ld_ref[...]
    h = h_ref[...]
    for c in range(stage_rows // GATHER_RC):
        r = lax.broadcasted_iota(I32, (GATHER_RC, tm), 0) + c * GATHER_RC
        hit = jnp.where(r == ld[0:1], 1.0, jnp.where(r == ld[1:2], 1.0,
                        jnp.where(r == ld[2:3], 1.0, jnp.where(r == ld[3:4], 1.0, 0.0))))
        stage[c * GATHER_RC:(c + 1) * GATHER_RC, :] = jnp.dot(
            hit.astype(BF16), h, preferred_element_type=F32).astype(BF16)

    _run_slabs(meta_ref, i, n_exp, stage, xs_ref, sems.at[slot], to_global=True, wait=False)

    @pl.when(i > 0)
    def _():
        _run_slabs(meta_ref, i - 1, n_exp, stage_ref.at[1 - slot], xs_ref, sems.at[1 - slot],
                   to_global=True, wait=True)

    @pl.when(i == n_tiles - 1)
    def _():
        _run_slabs(meta_ref, i, n_exp, stage, xs_ref, sems.at[slot], to_global=True, wait=True)


def _dispatch(meta, tail, ldest, h2, n_rows, n_exp):
    t, d = h2.shape
    tm = TOK_TM
    n_tiles = t // tm
    stage_rows = _stage_rows(n_exp)
    kernel = functools.partial(_dispatch_kernel, tm=tm, n_exp=n_exp, n_tiles=n_tiles,
                               n_blocks=n_rows // EXP_BLK, stage_rows=stage_rows)
    grid_spec = pltpu.PrefetchScalarGridSpec(
        num_scalar_prefetch=2,
        grid=(n_tiles,),
        in_specs=[pl.BlockSpec((TOP_K, tm), lambda i, m, tl: (0, i)),
                  pl.BlockSpec((tm, d), lambda i, m, tl: (i, 0))],
        out_specs=pl.BlockSpec(memory_space=pl.ANY),
        scratch_shapes=[pltpu.VMEM((2, stage_rows, d), BF16),
                        pltpu.VMEM((EXP_BLK, d), BF16),
                        pltpu.SemaphoreType.DMA((2,)),
                        pltpu.SemaphoreType.DMA],
    )
    return pl.pallas_call(
        kernel,
        out_shape=jax.ShapeDtypeStruct((n_rows, d), BF16),
        grid_spec=grid_spec,
        compiler_params=_params("arbitrary"),
        name="dispatch",
    )(meta, tail, ldest, h2)


def _expert_kernel(region_ref, xs_ref, w1_ref, b1_ref, w2_ref, b2_ref, eo_ref,
                   w1f_ref, w2f_ref, w1b_ref, w2b_ref, x_buf, o_buf, w_sems, x_sems, o_sems,
                   *, f, blk, n_exp, n_blocks):
    e = pl.program_id(0)
    first, count = region_ref[e], region_ref[n_exp + e]
    w_slot = e & 1
    d = w1f_ref.shape[1]
    r1, r2 = d // W_CHUNKS, f // W_CHUNKS

    def rows(j):
        return pl.ds(pl.multiple_of((first + j) * blk, blk), blk)

    def x_copy(j, slot):
        return pltpu.make_async_copy(xs_ref.at[rows(j)], x_buf.at[slot], x_sems.at[slot])

    def o_copy(j, slot):
        return pltpu.make_async_copy(o_buf.at[slot], eo_ref.at[rows(j)], o_sems.at[slot])

    def w_chunk(expert, c, slot, wait):
        for src, dst, r in ((w1_ref, w1f_ref, r1), (w2_ref, w2f_ref, r2)):
            piece = pl.ds(c * r if isinstance(c, int) else pl.multiple_of(c * r, r), r)
            cp = pltpu.make_async_copy(src.at[expert, piece], dst.at[slot, piece], w_sems.at[slot])
            if wait:
                cp.wait()
            else:
                cp.start()

    @pl.when(e == 0)
    def _():
        for c in range(W_CHUNKS):
            w_chunk(0, c, 0, False)

    @pl.when(count > 0)
    def _():
        x_copy(0, 0).start()

    for c in range(W_CHUNKS):
        w_chunk(e, c, w_slot, True)

    @pl.when(count > 0)
    def _():
        w1b_ref[...] = w1f_ref[w_slot].astype(BF16)
        w2b_ref[...] = w2f_ref[w_slot].astype(BF16)

    def next_chunk(c):
        @pl.when(e + 1 < n_exp)
        def _():
            w_chunk(e + 1, c, 1 - w_slot, False)

    def block(j, carry):
        slot = j & 1
        x_copy(j, slot).wait()

        @pl.when(j + 1 < count)
        def _():
            x_copy(j + 1, 1 - slot).start()

        @pl.when(j < W_CHUNKS)
        def _():
            next_chunk(j)

        @pl.when(j >= 2)
        def _():
            o_copy(j - 2, slot).wait()

        gu = jnp.dot(x_buf[slot], w1b_ref[...], preferred_element_type=F32) + b1_ref[0]
        g = jnp.minimum(gu[:, :f], SWIGLU_LIMIT)
        u = jnp.clip(gu[:, f:], -SWIGLU_LIMIT, SWIGLU_LIMIT)
        a = g * _sigmoid(SWIGLU_ALPHA * g) * (u + 1.0)
        o_buf[slot] = (jnp.dot(a.astype(BF16), w2b_ref[...], preferred_element_type=F32)
                       + b2_ref[0]).astype(o_buf.dtype)
        o_copy(j, slot).start()
        return carry

    lax.fori_loop(0, count, block, 0)
    lax.fori_loop(jnp.minimum(count, W_CHUNKS), W_CHUNKS, lambda c, carry: (next_chunk(c), carry)[1], 0)

    @pl.when(count >= 2)
    def _():
        o_copy(count - 2, count & 1).wait()

    @pl.when(count >= 1)
    def _():
        o_copy(count - 1, (count - 1) & 1).wait()

    @pl.when(e == n_exp - 1)
    def _():
        o_buf[0] = jnp.zeros(o_buf.shape[1:], o_buf.dtype)
        used = first + count

        def fill(j, wait):
            cp = pltpu.make_async_copy(
                o_buf.at[0], eo_ref.at[pl.ds(pl.multiple_of(j * blk, blk), blk)], o_sems.at[0])
            if wait:
                cp.wait()
            else:
                cp.start()

        lax.fori_loop(used, n_blocks, lambda j, c: (fill(j, False), c)[1], 0)
        lax.fori_loop(used, n_blocks, lambda j, c: (fill(j, True), c)[1], 0)


def _experts(region, xs, w1, b1, w2, b2):
    n_rows, d = xs.shape
    n_exp, _, f2 = w1.shape
    f = f2 // 2
    blk = EXP_BLK
    kernel = functools.partial(_expert_kernel, f=f, blk=blk, n_exp=n_exp, n_blocks=n_rows // blk)
    grid_spec = pltpu.PrefetchScalarGridSpec(
        num_scalar_prefetch=1,
        grid=(n_exp,),
        in_specs=[pl.BlockSpec(memory_space=pl.ANY),
                  pl.BlockSpec(memory_space=pl.ANY),
                  pl.BlockSpec((1, 1, f2), lambda e, r: (e, 0, 0)),
                  pl.BlockSpec(memory_space=pl.ANY),
                  pl.BlockSpec((1, 1, d), lambda e, r: (e, 0, 0))],
        out_specs=pl.BlockSpec(memory_space=pl.ANY),
        scratch_shapes=[pltpu.VMEM((2, d, f2), F32), pltpu.VMEM((2, f, d), F32),
                        pltpu.VMEM((d, f2), BF16), pltpu.VMEM((f, d), BF16),
                        pltpu.VMEM((2, blk, d), BF16), pltpu.VMEM((2, blk, d), BF16),
                        pltpu.SemaphoreType.DMA((2,)), pltpu.SemaphoreType.DMA((2,)),
                        pltpu.SemaphoreType.DMA((2,))],
    )
    return pl.pallas_call(
        kernel,
        out_shape=jax.ShapeDtypeStruct((n_rows, d), BF16),
        grid_spec=grid_spec,
        compiler_params=_params("arbitrary"),
        name="experts",
    )(region, xs, w1, b1, w2, b2)


def _combine_kernel(meta_ref, eo_ref, ld_ref, w_ref, x1_ref, mod_ref, o_ref, stage_ref, sems,
                    *, tm, n_exp, n_tiles, stage_rows):
    i = pl.program_id(0)
    slot = i & 1

    @pl.when(i == 0)
    def _():
        stage_ref[...] = jnp.zeros_like(stage_ref)
        _run_slabs(meta_ref, 0, n_exp, stage_ref.at[0], eo_ref, sems.at[0], to_global=False, wait=False)

    @pl.when(i + 1 < n_tiles)
    def _():
        _run_slabs(meta_ref, i + 1, n_exp, stage_ref.at[1 - slot], eo_ref, sems.at[1 - slot],
                   to_global=False, wait=False)

    ld = ld_ref[...]
    w = w_ref[...]
    ldb = [jnp.broadcast_to(ld[:, k:k + 1], (tm, SCATTER_RC)) for k in range(TOP_K)]
    wb = [jnp.broadcast_to(w[:, k:k + 1], (tm, SCATTER_RC)) for k in range(TOP_K)]
    col = lax.broadcasted_iota(I32, (tm, SCATTER_RC), 1)

    def sel_piece(r0):
        r = col + r0
        sel = jnp.where(ldb[0] == r, wb[0], jnp.where(ldb[1] == r, wb[1],
                        jnp.where(ldb[2] == r, wb[2], jnp.where(ldb[3] == r, wb[3], 0.0))))
        return sel.astype(BF16)

    _run_slabs(meta_ref, i, n_exp, stage_ref.at[slot], eo_ref, sems.at[slot], to_global=False, wait=True)
    stage = stage_ref.at[slot]
    y = None
    for c in range(stage_rows // COMBINE_KC):
        r0 = c * COMBINE_KC
        sel = jnp.concatenate([sel_piece(r0 + s) for s in range(0, COMBINE_KC, SCATTER_RC)], axis=1)
        part = jnp.dot(sel, stage[r0:r0 + COMBINE_KC, :], preferred_element_type=F32)
        y = part if y is None else y + part
    o_ref[...] = x1_ref[...] + mod_ref[0][5:6] * y


def _combine(meta, eo, ld_tok, w_tok, x1, mod3, seq, n_exp):
    t, d = x1.shape
    tm = TOK_TM
    n_tiles = t // tm
    stage_rows = _stage_rows(n_exp)
    kernel = functools.partial(_combine_kernel, tm=tm, n_exp=n_exp, n_tiles=n_tiles, stage_rows=stage_rows)
    grid_spec = pltpu.PrefetchScalarGridSpec(
        num_scalar_prefetch=1,
        grid=(n_tiles,),
        in_specs=[pl.BlockSpec(memory_space=pl.ANY),
                  pl.BlockSpec((tm, TOP_K), lambda i, m: (i, 0)),
                  pl.BlockSpec((tm, TOP_K), lambda i, m: (i, 0)),
                  pl.BlockSpec((tm, d), lambda i, m: (i, 0)),
                  pl.BlockSpec((1, 6, d), lambda i, m: (i * tm // seq, 0, 0))],
        out_specs=pl.BlockSpec((tm, d), lambda i, m: (i, 0)),
        scratch_shapes=[pltpu.VMEM((2, stage_rows, d), BF16),
                        pltpu.SemaphoreType.DMA((2,))],
    )
    return pl.pallas_call(
        kernel,
        out_shape=jax.ShapeDtypeStruct((t, d), F32),
        grid_spec=grid_spec,
        compiler_params=_params("arbitrary"),
        name="combine",
    )(meta, eo, ld_tok, w_tok, x1, mod3)


def _routing_tables(tile_cnt, n_tokens):
    n_tiles, n_exp = tile_cnt.shape
    blk = EXP_BLK
    run = _round_up(tile_cnt, RUN_ALIGN)
    local = jnp.cumsum(run, axis=1) - run
    size = jnp.sum(run, axis=0)
    region = _round_up(size, blk)
    region_end = jnp.cumsum(region)
    region_start = region_end - region
    glob = region_start[None, :] + jnp.cumsum(run, axis=0) - run
    sizes = jnp.asarray(SLAB_SIZES, I32)
    has = (run[:, :, None] & sizes) != 0
    before = run[:, :, None] & ~(2 * sizes - 1)
    place = jnp.cumsum(has, axis=1) - has
    hit = has[..., None] & (place[..., None] == jnp.arange(n_exp, dtype=I32))
    listed = lambda start: jnp.sum(
        jnp.where(hit, (start[:, :, None] + before)[..., None], 0), axis=1).reshape(n_tiles, -1)
    meta = jnp.concatenate([listed(local), listed(glob), jnp.sum(has, axis=1)],
                           axis=1).reshape(-1).astype(I32)
    tail = jnp.concatenate([region_start + size, region - size, region_end[-1:]]).astype(I32)
    n_rows = _round_up(n_tokens * TOP_K + n_tiles * n_exp * RUN_ALIGN + n_exp * blk, blk)
    region_blocks = jnp.concatenate([region_start // blk, region // blk]).astype(I32)
    return meta, tail, region_blocks, n_rows


def kernel(x, c, w_ada, b_ada, g_norm1, w_in, g_ret_gn, b_ret_gn, g_qnorm, g_knorm,
           lambda_q1, lambda_k1, lambda_q2, lambda_k2, g_diff_subln, w_out, g_norm2,
           w_router, b_router, w_expert_in, b_expert_in, w_expert_out, b_expert_out):
    batch, seq, d = x.shape
    depth = w_ada.shape[0]
    assert depth == 1
    t = batch * seq
    n_exp = w_router.shape[-1]
    l = 0

    mod = _adaln_mod(c, w_ada[l], b_ada[l][None, :])
    mod3 = mod.reshape(batch, 6, d)
    x2 = x.reshape(t, d)

    proj = _in_proj(x2, mod3, g_norm1[l][None, :], w_in[l].astype(BF16), seq)
    ret = _retention(proj, g_ret_gn[l][None, :], b_ret_gn[l][None, :], batch, seq, d)
    att = _diff_attention(proj, g_qnorm[l][None, :], g_knorm[l][None, :], g_diff_subln[l][None, :],
                          lambda_q1[l][None, :], lambda_k1[l][None, :],
                          lambda_q2[l][None, :], lambda_k2[l][None, :], batch, seq, d)

    wr = w_router[l]
    wr_hi = wr.astype(BF16)
    wr_lo = (wr - wr_hi.astype(F32)).astype(BF16)
    pad = lambda a: jnp.pad(a, ((0, 0), (0, LANES - n_exp)))
    wr_split = jnp.concatenate([pad(wr_hi), pad(wr_lo)], axis=1)
    br_pad = jnp.pad(b_router[l][None, :], ((0, 0), (0, LANES - n_exp)), constant_values=MASK_VALUE)
    x1, h2, top_w, ldest, tile_cnt = _out_router(
        x2, ret, att, mod3, w_out[l].astype(BF16), g_norm2[l][None, :], wr_split, br_pad, seq, n_exp)

    meta, tail, region_blocks, n_rows = _routing_tables(tile_cnt[:, :, 0].astype(I32), t)
    xs = _dispatch(meta, tail, ldest, h2, n_rows, n_exp)
    eo = _experts(region_blocks, xs, w_expert_in[l], b_expert_in[l][:, None, :],
                  w_expert_out[l], b_expert_out[l][:, None, :])
    out = _combine(meta, eo, ldest.T, top_w.T, x1, mod3, seq, n_exp)
    return out.reshape(batch, seq, d)
```

```python
import functools
import itertools
import math

import jax
import jax.numpy as jnp
from jax import lax
from jax.experimental import pallas as pl
from jax.experimental.pallas import tpu as pltpu

F32 = jnp.float32
BF16 = jnp.bfloat16
I32 = jnp.int32

EPS = 1e-6
LOG2E = 1.4426950216293335
MASK_VALUE = -1e30
LANES = 128
BF16_SUBLANES = 16
VMEM_LIMIT_BYTES = 48 * 1024 * 1024

RET_HEADS = 4
DIFF_HEADS = 8
TOP_K = 4
SWIGLU_LIMIT = 7.0
SWIGLU_ALPHA = 1.702
LAMBDA_INIT = 0.8 - 0.6 * math.exp(-0.3 * 0)

IN_TM, IN_TN = 1024, 2048
RET_TC = 256
RET_CHUNKS_PER_STEP = 2
ATT_T = 512
TOK_TM = 512
EXP_BLK = 512
RUN_ALIGN = BF16_SUBLANES
GATHER_RC = 256
SCATTER_RC = 128
COMBINE_KC = 512
W_CHUNKS = 8


def _round_up(n, m):
    return (n + m - 1) // m * m


def _stage_rows(n_exp):
    return _round_up(TOK_TM * TOP_K + n_exp * (RUN_ALIGN - 1), max(GATHER_RC, COMBINE_KC))


SLAB_SIZES = tuple(TOK_TM >> s for s in range(TOK_TM.bit_length()) if TOK_TM >> s >= RUN_ALIGN)
TAIL_SIZES = tuple(s for s in SLAB_SIZES if s < EXP_BLK)


def _params(*sem):
    return pltpu.CompilerParams(dimension_semantics=sem, vmem_limit_bytes=VMEM_LIMIT_BYTES)


def _sigmoid(x):
    return 0.5 * jnp.tanh(0.5 * x) + 0.5


def _rms(x, axis=-1):
    return x * lax.rsqrt(jnp.mean(x * x, axis=axis, keepdims=True) + EPS)


def _mod_kernel(c_ref, w_ref, b_ref, o_ref):
    c = c_ref[...]
    s = c * _sigmoid(c)
    o_ref[...] = jnp.dot(s, w_ref[...], preferred_element_type=F32,
                         precision=lax.Precision.HIGHEST) + b_ref[...]


def _adaln_mod(c, w, b):
    bn, d = c.shape
    n = w.shape[1]
    tn = d
    return pl.pallas_call(
        _mod_kernel,
        out_shape=jax.ShapeDtypeStruct((bn, n), F32),
        grid=(n // tn,),
        in_specs=[pl.BlockSpec((bn, d), lambda j: (0, 0)),
                  pl.BlockSpec((d, tn), lambda j: (0, j)),
                  pl.BlockSpec((1, tn), lambda j: (0, j))],
        out_specs=pl.BlockSpec((bn, tn), lambda j: (0, j)),
        compiler_params=_params("arbitrary"),
        name="adaln_mod",
    )(c, w, b)


def _inproj_kernel(x_ref, mod_ref, g_ref, w_ref, o_ref, h_ref):
    @pl.when(pl.program_id(1) == 0)
    def _():
        m = mod_ref[0]
        h = _rms(x_ref[...]) * g_ref[...] * (1.0 + m[1:2]) + m[0:1]
        h_ref[...] = h.astype(BF16)

    o_ref[...] = jnp.dot(h_ref[...], w_ref[...], preferred_element_type=F32).astype(o_ref.dtype)


def _in_proj(x2, mod3, g1, w_in_bf16, seq):
    t, d = x2.shape
    n = w_in_bf16.shape[1]
    tm, tn = IN_TM, IN_TN
    return pl.pallas_call(
        _inproj_kernel,
        out_shape=jax.ShapeDtypeStruct((t, n), BF16),
        grid=(t // tm, n // tn),
        in_specs=[pl.BlockSpec((tm, d), lambda i, j: (i, 0)),
                  pl.BlockSpec((1, 6, d), lambda i, j: (i * tm // seq, 0, 0)),
                  pl.BlockSpec((1, d), lambda i, j: (0, 0)),
                  pl.BlockSpec((d, tn), lambda i, j: (0, j))],
        out_specs=pl.BlockSpec((tm, tn), lambda i, j: (i, j)),
        scratch_shapes=[pltpu.VMEM((tm, d), BF16)],
        compiler_params=_params("arbitrary", "arbitrary"),
        name="in_proj",
    )(x2, mod3, g1, w_in_bf16)


def _ret_kernel(q_ref, k_ref, v_ref, rg_ref, mg_ref, gng_ref, gnb_ref, o_ref, r_ref, decay_ref,
                *, dk, dv, tc):
    scale = dk ** -0.5
    log_gs = [math.log(1.0 - 2.0 ** (-5.0 - h)) for h in range(RET_HEADS)]

    @pl.when(pl.program_id(1) == 0)
    def _():
        r_ref[...] = jnp.zeros_like(r_ref)
        row = lax.broadcasted_iota(I32, (tc, tc), 0)
        col = lax.broadcasted_iota(I32, (tc, tc), 1)
        rel = (row - col).astype(F32)
        for h, log_g in enumerate(log_gs):
            decay_ref[h] = jnp.where(rel >= 0, jnp.exp(log_g * jnp.maximum(rel, 0.0)), 0.0) * scale

    pos = lax.broadcasted_iota(I32, (tc, 1), 0).astype(F32)
    for sub, (h, log_g) in itertools.product(range(RET_CHUNKS_PER_STEP), enumerate(log_gs)):
        rows = slice(sub * tc, (sub + 1) * tc)
        decay = decay_ref[h]
        xi = jnp.exp(log_g * (pos + 1.0))
        zeta = jnp.exp(log_g * (tc - 1.0 - pos))
        g_chunk = math.exp(log_g * tc)
        q = q_ref[rows, h * dk:(h + 1) * dk]
        k = k_ref[rows, h * dk:(h + 1) * dk]
        v = v_ref[rows, h * dv:(h + 1) * dv]
        s = lax.dot_general(q, k, (((1,), (1,)), ((), ())), preferred_element_type=F32)
        y = jnp.dot((s * decay).astype(BF16), v, preferred_element_type=F32)
        state = r_ref[h]
        y = y + jnp.dot(q, state.astype(BF16), preferred_element_type=F32) * xi
        vz = (v.astype(F32) * zeta).astype(BF16)
        kv = lax.dot_general(k, vz, (((0,), (0,)), ((), ())), preferred_element_type=F32)
        r_ref[h] = state * g_chunk + kv * scale

        mu = jnp.mean(y, axis=-1, keepdims=True)
        yc = y - mu
        var = jnp.mean(yc * yc, axis=-1, keepdims=True)
        sl = slice(h * dv, (h + 1) * dv)
        yn = yc * lax.rsqrt(var + EPS) * gng_ref[:, sl] + gnb_ref[:, sl]
        rg = rg_ref[rows, sl].astype(F32)
        gate = _sigmoid(mg_ref[rows, sl].astype(F32))
        o_ref[rows, sl] = (rg * _sigmoid(rg) * yn * gate).astype(o_ref.dtype)


def _retention(proj, gn_g, gn_b, batch, seq, d):
    t = proj.shape[0]
    tc = RET_TC
    tr = tc * RET_CHUNKS_PER_STEP
    nc = seq // tr
    dk = d // (2 * RET_HEADS)
    dv = d // RET_HEADS
    qk_w = RET_HEADS * dk
    row = lambda b, c: b * nc + c
    kernel = functools.partial(_ret_kernel, dk=dk, dv=dv, tc=tc)
    return pl.pallas_call(
        kernel,
        out_shape=jax.ShapeDtypeStruct((t, d), BF16),
        grid=(batch, nc),
        in_specs=[pl.BlockSpec((tr, qk_w), lambda b, c: (row(b, c), 0)),
                  pl.BlockSpec((tr, qk_w), lambda b, c: (row(b, c), 1)),
                  pl.BlockSpec((tr, d), lambda b, c: (row(b, c), 1)),
                  pl.BlockSpec((tr, d), lambda b, c: (row(b, c), 2)),
                  pl.BlockSpec((tr, d), lambda b, c: (row(b, c), 6)),
                  pl.BlockSpec((1, d), lambda b, c: (0, 0)),
                  pl.BlockSpec((1, d), lambda b, c: (0, 0))],
        out_specs=pl.BlockSpec((tr, d), lambda b, c: (row(b, c), 0)),
        scratch_shapes=[pltpu.VMEM((RET_HEADS, dk, dv), F32), pltpu.VMEM((RET_HEADS, tc, tc), F32)],
        compiler_params=_params("arbitrary", "arbitrary"),
        name="retention",
    )(proj, proj, proj, proj, proj, gn_g, gn_b)


def _halves_rms(x, lo, dh):
    sq = x * x
    s_lo = jnp.sum(jnp.where(lo, sq, 0.0), axis=-1, keepdims=True)
    s_hi = jnp.sum(jnp.where(lo, 0.0, sq), axis=-1, keepdims=True)
    inv = jnp.where(lo, lax.rsqrt(s_lo / dh + EPS), lax.rsqrt(s_hi / dh + EPS))
    return x * inv


def _attn_kernel(q_ref, k_ref, v_ref, gate_ref, slope_ref, qca_ref, qcb_ref, gq_ref, gk_ref, gsub_ref,
                 lq1_ref, lk1_ref, lq2_ref, lk2_ref,
                 o_ref, ka_ref, kb_ref, vt_ref, qa_ref, qb_ref, s0_ref, sa_ref, sb_ref, acc_ref, m_ref,
                 *, tile, dh, dv, nk):
    lane = lax.broadcasted_iota(I32, (tile, 2 * dh), 1)
    lo = lane < dh

    slope = slope_ref[0]
    sub = lax.broadcasted_iota(I32, (tile, 2 * dh), 0)
    off = lane & (dh - 1)
    hi_lane = (off < 6) & ((off & 1) == 0)
    lo_lane = (off < 6) & ((off & 1) == 1)
    bias0 = jnp.where(hi_lane, ((sub >> 8) << 8).astype(F32),
                      jnp.where(lo_lane, (sub & 255).astype(F32), 0.0)) * slope
    bias_step = jnp.where(hi_lane, float(tile), 0.0) * slope
    ones_rows = jnp.where(lax.broadcasted_iota(I32, (BF16_SUBLANES, tile), 0) == 0, 1.0, 0.0)
    lo_t = lax.broadcasted_iota(I32, (2 * dh, tile), 0) < dh

    def prepare(c, carry):
        r0 = c * tile if isinstance(c, int) else pl.multiple_of(c * tile, tile)
        kn = _halves_rms(k_ref[pl.ds(r0, tile), :].astype(F32), lo, dh) * gk_ref[...]
        bias = bias0 + lax.convert_element_type(c, F32) * bias_step
        ka_ref[c] = jnp.where(lo, kn, bias).astype(BF16)
        kb_ref[c] = jnp.where(lo, bias, kn).astype(BF16)
        vt_ref[c, :dv, :] = v_ref[pl.ds(r0, tile), :].astype(F32).T.astype(BF16)
        vt_ref[c, dv:, :] = ones_rows.astype(BF16)
        qn = (_halves_rms(q_ref[pl.ds(r0, tile), :].astype(F32), lo, dh) * gq_ref[...]
              * (dh ** -0.5 * LOG2E))
        qn_t = qn.T
        qa_ref[c] = jnp.where(lo_t, qn_t, qca_ref[...]).astype(BF16)
        qb_ref[c] = jnp.where(lo_t, qcb_ref[...], qn_t).astype(BF16)
        return carry

    lax.fori_loop(0, min(2, nk), prepare, 0)
    lam = (jnp.exp(jnp.sum(lq1_ref[...] * lk1_ref[...], axis=-1, keepdims=True))
           - jnp.exp(jnp.sum(lq2_ref[...] * lk2_ref[...], axis=-1, keepdims=True)) + LAMBDA_INIT)

    refs = (gate_ref, gsub_ref, o_ref, ka_ref, kb_ref, vt_ref, qa_ref, qb_ref,
            s0_ref, sa_ref, sb_ref, acc_ref, m_ref)
    _attn_scores(ka_ref, kb_ref, qa_ref, qb_ref, 0, 0, s0_ref, tile)
    _attn_q_tile(0, lam, prepare, *refs, tile=tile, dv=dv, nk=nk)

    def q_tile(qi, carry):
        _attn_q_tile(qi, lam, prepare, *refs, tile=tile, dv=dv, nk=nk)
        return carry

    lax.fori_loop(1, nk, q_tile, 0)


def _attn_scores(ka_ref, kb_ref, qa_ref, qb_ref, jb, q, s_ref, tile):
    for idx, (k_ref, q_ref) in enumerate(((ka_ref, qa_ref), (kb_ref, qb_ref))):
        s = jnp.dot(k_ref[jb], q_ref[q], preferred_element_type=F32)
        s_ref[idx, :tile, :] = s
        s_ref[idx, tile:tile + 1, :] = jnp.max(s, axis=0, keepdims=True)


def _attn_q_tile(qi, lam, prepare, gate_ref, gsub_ref, o_ref, ka_ref, kb_ref, vt_ref, qa_ref, qb_ref,
                 s0_ref, sa_ref, sb_ref, acc_ref, m_ref, *, tile, dv, nk):
    def scores(jb, s_ref, q=qi):
        _attn_scores(ka_ref, kb_ref, qa_ref, qb_ref, jb, q, s_ref, tile)

    def absorb(jb, s_ref, masked):
        vt = vt_ref[jb]
        for idx in range(2):
            s = s_ref[idx, :tile, :]
            if masked:
                krow = lax.broadcasted_iota(I32, (tile, tile), 0)
                qcol = lax.broadcasted_iota(I32, (tile, tile), 1)
                s = jnp.where(krow > qcol, MASK_VALUE, s)
                block_max = jnp.max(s, axis=0, keepdims=True)
            else:
                block_max = s_ref[idx, tile:tile + 1, :]
            m_old = m_ref[idx]
            m_new = jnp.maximum(m_old, block_max)
            m_ref[idx] = m_new
            p = jnp.exp2(s - m_new).astype(BF16)
            acc_ref[idx] = (acc_ref[idx] * jnp.exp2(m_old - m_new)
                            + jnp.dot(vt, p, preferred_element_type=F32))

    acc_ref[...] = jnp.zeros_like(acc_ref)
    m_ref[...] = jnp.full_like(m_ref, MASK_VALUE)
    if isinstance(qi, int):
        absorb(0, s0_ref, True)
    else:
        scores(1, sa_ref)
        absorb(0, s0_ref, False)

        def pair(j2, carry):
            jb = 2 * j2 + 1
            scores(jb + 1, sb_ref)
            absorb(jb, sa_ref, False)
            scores(jb + 2, sa_ref)
            absorb(jb + 1, sb_ref, False)
            return carry

        lax.fori_loop(0, (qi - 1) // 2, pair, 0)

        @pl.when(qi % 2 == 0)
        def _():
            scores(qi, sb_ref)
            absorb(qi - 1, sa_ref, False)
            absorb(qi, sb_ref, True)

        @pl.when(qi % 2 == 1)
        def _():
            absorb(qi, sa_ref, True)

    ahead = (lambda n: min(qi + n, nk - 1)) if isinstance(qi, int) else (lambda n: jnp.minimum(qi + n, nk - 1))
    scores(0, s0_ref, q=ahead(1))
    prepare(ahead(2), 0)
    a1, a2 = acc_ref[0], acc_ref[1]
    o_t = a1[:dv] * (1.0 / a1[dv:dv + 1]) - lam * (a2[:dv] * (1.0 / a2[dv:dv + 1]))
    o = _rms(o_t, axis=0).T * gsub_ref[...] * (1.0 - LAMBDA_INIT)
    rows = pl.ds(pl.multiple_of(qi * tile, tile), tile)
    o_ref[rows, :] = (o * _sigmoid(gate_ref[rows, :].astype(F32))).astype(o_ref.dtype)


def _bf16_terms(x, n):
    terms, rest = [], jnp.asarray(x, F32)
    for _ in range(n):
        term = rest.astype(BF16).astype(F32)
        terms.append(term)
        rest = rest - term
    return terms


def _diff_attention(proj, g_q, g_k, g_sub, lq1, lk1, lq2, lk2, batch, seq, d):
    t = proj.shape[0]
    tile = ATT_T
    nq = seq // tile
    dh = d // (2 * DIFF_HEADS)
    w = 2 * dh
    assert w == LANES
    q_col, k_col, v_col, gate_col = 3 * d // w, 4 * d // w, 5 * d // w, 7 * d // w
    slopes = 2.0 ** -(jnp.arange(DIFF_HEADS, dtype=F32) + 1.0)
    slopes = jnp.broadcast_to(slopes[:, None, None], (DIFF_HEADS, 1, LANES))
    c1, c2, c3 = _bf16_terms(LOG2E, 3)
    qconst = jnp.zeros((dh,), F32).at[:6].set(jnp.stack([c1, c1, c2, c2, c3, c3]))
    zeros = jnp.zeros((dh,), F32)
    qconst_a = jnp.concatenate([zeros, qconst])[:, None]
    qconst_b = jnp.concatenate([qconst, zeros])[:, None]
    tile2 = lambda a: jnp.concatenate([a, a], axis=-1)
    assert tile % 256 == 0
    small = lambda n: pl.BlockSpec((1, n), lambda b, h: (0, 0))
    kernel = functools.partial(_attn_kernel, tile=tile, dh=dh, dv=w, nk=nq)
    return pl.pallas_call(
        kernel,
        out_shape=jax.ShapeDtypeStruct((t, d), BF16),
        grid=(batch, DIFF_HEADS),
        in_specs=[pl.BlockSpec((seq, w), lambda b, h: (b, q_col + h)),
                  pl.BlockSpec((seq, w), lambda b, h: (b, k_col + h)),
                  pl.BlockSpec((seq, w), lambda b, h: (b, v_col + h)),
                  pl.BlockSpec((seq, w), lambda b, h: (b, gate_col + h)),
                  pl.BlockSpec((1, 1, LANES), lambda b, h: (h, 0, 0)),
                  pl.BlockSpec((w, 1), lambda b, h: (0, 0)), pl.BlockSpec((w, 1), lambda b, h: (0, 0)),
                  small(w), small(w), small(w),
                  small(dh), small(dh), small(dh), small(dh)],
        out_specs=pl.BlockSpec((seq, w), lambda b, h: (b, h)),
        scratch_shapes=[pltpu.VMEM((nq, tile, w), BF16),
                        pltpu.VMEM((nq, tile, w), BF16),
                        pltpu.VMEM((nq, w + BF16_SUBLANES, tile), BF16),
                        pltpu.VMEM((nq, w, tile), BF16),
                        pltpu.VMEM((nq, w, tile), BF16),
                        pltpu.VMEM((2, tile + 8, tile), F32),
                        pltpu.VMEM((2, tile + 8, tile), F32),
                        pltpu.VMEM((2, tile + 8, tile), F32),
                        pltpu.VMEM((2, w + BF16_SUBLANES, tile), F32),
                        pltpu.VMEM((2, 1, tile), F32)],
        compiler_params=_params("arbitrary", "arbitrary"),
        name="diff_attn",
    )(proj, proj, proj, proj, slopes, qconst_a, qconst_b, tile2(g_q), tile2(g_k), g_sub,
      lq1, lk1, lq2, lk2)


def _out_kernel(x_ref, ret_ref, att_ref, mod_ref, wo_ref, g2_ref, wr_ref, br_ref,
                x1_ref, h2_ref, tw_ref, ld_ref, cnt_ref, *, tm, n_exp):
    m = mod_ref[0]
    merged = (ret_ref[...].astype(F32) + att_ref[...].astype(F32)).astype(BF16)
    x1 = x_ref[...] + m[2:3] * jnp.dot(merged, wo_ref[...], preferred_element_type=F32)
    x1_ref[...] = x1
    h2 = _rms(x1) * g2_ref[...] * (1.0 + m[4:5]) + m[3:4]
    hi = h2.astype(BF16)
    h2_ref[...] = hi

    lo = (h2 - hi.astype(F32)).astype(BF16)
    both = jnp.dot(hi, wr_ref[...], preferred_element_type=F32)
    logits = (both[:, :LANES] + both[:, LANES:]
              + jnp.dot(lo, wr_ref[:, :LANES], preferred_element_type=F32) + br_ref[...])
    lt = logits.T[:n_exp]

    erow = lax.broadcasted_iota(I32, (n_exp, tm), 0)
    vals, sels = [], []
    for k in range(TOP_K):
        mx = jnp.max(lt, axis=0, keepdims=True)
        idx = jnp.min(jnp.where(lt == mx, erow, n_exp), axis=0, keepdims=True)
        sel = erow == idx
        vals.append(mx)
        sels.append(sel)
        lt = jnp.where(sel, MASK_VALUE, lt)
    exps = [jnp.exp(v - vals[0]) for v in vals]
    inv = 1.0 / (exps[0] + exps[1] + exps[2] + exps[3])
    for k in range(TOP_K):
        tw_ref[k:k + 1, :] = exps[k] * inv

    chosen = jnp.where(sels[0] | sels[1] | sels[2] | sels[3], 1.0, 0.0)
    before = (lax.broadcasted_iota(I32, (tm, tm), 0) < lax.broadcasted_iota(I32, (tm, tm), 1))
    prefix = jnp.dot(chosen.astype(BF16), jnp.where(before, 1.0, 0.0).astype(BF16),
                     preferred_element_type=F32)
    count = jnp.sum(chosen, axis=1, keepdims=True)
    cnt_ref[...] = count
    padded = jnp.ceil(count * (1.0 / RUN_ALIGN)) * RUN_ALIGN
    below = (lax.broadcasted_iota(I32, (n_exp, n_exp), 1) < lax.broadcasted_iota(I32, (n_exp, n_exp), 0))
    run_start = jnp.dot(jnp.where(below, 1.0, 0.0).astype(BF16),
                        jnp.broadcast_to(padded, (n_exp, LANES)).astype(BF16),
                        preferred_element_type=F32)[:, :1]
    pos = prefix + run_start
    for k in range(TOP_K):
        ld_ref[k:k + 1, :] = jnp.sum(jnp.where(sels[k], pos, 0.0), axis=0, keepdims=True).astype(I32)


def _out_router(x2, ret, att, mod3, w_out_bf16, g2, wr_split, br_pad, seq, n_exp):
    t, d = x2.shape
    tm = TOK_TM
    row = lambda i: (i, 0)
    const = lambda i: (0, 0)
    kernel = functools.partial(_out_kernel, tm=tm, n_exp=n_exp)
    return pl.pallas_call(
        kernel,
        out_shape=(jax.ShapeDtypeStruct((t, d), F32),
                   jax.ShapeDtypeStruct((t, d), BF16),
                   jax.ShapeDtypeStruct((TOP_K, t), F32),
                   jax.ShapeDtypeStruct((TOP_K, t), I32),
                   jax.ShapeDtypeStruct((t // tm, n_exp, 1), F32)),
        grid=(t // tm,),
        in_specs=[pl.BlockSpec((tm, d), row), pl.BlockSpec((tm, d), row), pl.BlockSpec((tm, d), row),
                  pl.BlockSpec((1, 6, d), lambda i: (i * tm // seq, 0, 0)),
                  pl.BlockSpec((d, d), const), pl.BlockSpec((1, d), const),
                  pl.BlockSpec((d, 2 * LANES), const), pl.BlockSpec((1, LANES), const)],
        out_specs=(pl.BlockSpec((tm, d), row), pl.BlockSpec((tm, d), row),
                   pl.BlockSpec((TOP_K, tm), lambda i: (0, i)),
                   pl.BlockSpec((TOP_K, tm), lambda i: (0, i)),
                   pl.BlockSpec((None, n_exp, 1), lambda i: (i, 0, 0))),
        compiler_params=_params("arbitrary"),
        name="out_router",
    )(x2, ret, att, mod3, w_out_bf16, g2, wr_split, br_pad)


def _meta_stride(n_exp):
    return 2 * len(SLAB_SIZES) * n_exp + len(SLAB_SIZES)


def _run_slabs(meta_ref, tile, n_exp, local_ref, global_ref, sem, *, to_global, wait):
    n_sizes = len(SLAB_SIZES)
    base = tile * _meta_stride(n_exp)
    for c, size in enumerate(SLAB_SIZES):
        def one(p, carry, c=c, size=size):
            if wait:
                local = glob = 0
            else:
                local = pl.multiple_of(meta_ref[base + c * n_exp + p], RUN_ALIGN)
                glob = pl.multiple_of(meta_ref[base + (n_sizes + c) * n_exp + p], RUN_ALIGN)
            loc = local_ref.at[pl.ds(local, size)]
            glo = global_ref.at[pl.ds(glob, size)]
            cp = pltpu.make_async_copy(loc, glo, sem) if to_global else pltpu.make_async_copy(glo, loc, sem)
            if wait:
                cp.wait()
            else:
                cp.start(priority=c % 2)
            return carry

        lax.fori_loop(0, meta_ref[base + 2 * n_sizes * n_exp + c], one, 0)


def _dispatch_kernel(meta_ref, tail_ref, ld_ref, h_ref, xs_ref, stage_ref, zero_ref, sems, tail_sem,
                     *, tm, n_exp, n_tiles, n_blocks, stage_rows):
    i = pl.program_id(0)
    slot = i & 1
    stage = stage_ref.at[slot]

    def tails(wait):
        def unused_block(b, carry):
            cp = pltpu.make_async_copy(
                zero_ref, xs_ref.at[pl.ds(pl.multiple_of(b * EXP_BLK, EXP_BLK), EXP_BLK)], tail_sem)
            if wait:
                cp.wait()
            else:
                cp.start()
            return carry

        lax.fori_loop(tail_ref[2 * n_exp] // EXP_BLK, n_blocks, unused_block, 0)

        def per_expert(e, carry):
            start, n = tail_ref[e], tail_ref[n_exp + e]
            off = 0
            for size in TAIL_SIZES:
                take = n & size

                @pl.when(take != 0)
                def _(off=off, size=size):
                    cp = pltpu.make_async_copy(
                        zero_ref.at[pl.ds(0, size)],
                        xs_ref.at[pl.ds(pl.multiple_of(start + off, RUN_ALIGN), size)], tail_sem)
                    if wait:
                        cp.wait()
                    else:
                        cp.start()

                off = off + take
            return carry

        lax.fori_loop(0, n_exp, per_expert, 0)

    @pl.when(i == 0)
    def _():
        zero_ref[...] = jnp.zeros_like(zero_ref)
        tails(False)
        tails(True)

    ld = ld_ref[...]
    h = h_ref[...]
    for c in range(stage_rows // GATHER_RC):
        r = lax.broadcasted_iota(I32, (GATHER_RC, tm), 0) + c * GATHER_RC
        hit = jnp.where(r == ld[0:1], 1.0, jnp.where(r == ld[1:2], 1.0,
                        jnp.where(r == ld[2:3], 1.0, jnp.where(r == ld[3:4], 1.0, 0.0))))
        stage[c * GATHER_RC:(c + 1) * GATHER_RC, :] = jnp.dot(
            hit.astype(BF16), h, preferred_element_type=F32).astype(BF16)

    _run_slabs(meta_ref, i, n_exp, stage, xs_ref, sems.at[slot], to_global=True, wait=False)

    @pl.when(i > 0)
    def _():
        _run_slabs(meta_ref, i - 1, n_exp, stage_ref.at[1 - slot], xs_ref, sems.at[1 - slot],
                   to_global=True, wait=True)

    @pl.when(i == n_tiles - 1)
    def _():
        _run_slabs(meta_ref, i, n_exp, stage, xs_ref, sems.at[slot], to_global=True, wait=True)


def _dispatch(meta, tail, ldest, h2, n_rows, n_exp):
    t, d = h2.shape
    tm = TOK_TM
    n_tiles = t // tm
    stage_rows = _stage_rows(n_exp)
    kernel = functools.partial(_dispatch_kernel, tm=tm, n_exp=n_exp, n_tiles=n_tiles,
                               n_blocks=n_rows // EXP_BLK, stage_rows=stage_rows)
    grid_spec = pltpu.PrefetchScalarGridSpec(
        num_scalar_prefetch=2,
        grid=(n_tiles,),
        in_specs=[pl.BlockSpec((TOP_K, tm), lambda i, m, tl: (0, i)),
                  pl.BlockSpec((tm, d), lambda i, m, tl: (i, 0))],
        out_specs=pl.BlockSpec(memory_space=pl.ANY),
        scratch_shapes=[pltpu.VMEM((2, stage_rows, d), BF16),
                        pltpu.VMEM((EXP_BLK, d), BF16),
                        pltpu.SemaphoreType.DMA((2,)),
                        pltpu.SemaphoreType.DMA],
    )
    return pl.pallas_call(
        kernel,
        out_shape=jax.ShapeDtypeStruct((n_rows, d), BF16),
        grid_spec=grid_spec,
        compiler_params=_params("arbitrary"),
        name="dispatch",
    )(meta, tail, ldest, h2)


def _expert_kernel(region_ref, xs_ref, w1_ref, b1_ref, w2_ref, b2_ref, eo_ref,
                   w1f_ref, w2f_ref, w1b_ref, w2b_ref, x_buf, o_buf, w_sems, x_sems, o_sems,
                   *, f, blk, n_exp, n_blocks):
    e = pl.program_id(0)
    first, count = region_ref[e], region_ref[n_exp + e]
    w_slot = e & 1
    d = w1f_ref.shape[1]
    r1, r2 = d // W_CHUNKS, f // W_CHUNKS

    def rows(j):
        return pl.ds(pl.multiple_of((first + j) * blk, blk), blk)

    def x_copy(j, slot):
        return pltpu.make_async_copy(xs_ref.at[rows(j)], x_buf.at[slot], x_sems.at[slot])

    def o_copy(j, slot):
        return pltpu.make_async_copy(o_buf.at[slot], eo_ref.at[rows(j)], o_sems.at[slot])

    def w_chunk(expert, c, slot, wait):
        for src, dst, r in ((w1_ref, w1f_ref, r1), (w2_ref, w2f_ref, r2)):
            piece = pl.ds(c * r if isinstance(c, int) else pl.multiple_of(c * r, r), r)
            cp = pltpu.make_async_copy(src.at[expert, piece], dst.at[slot, piece], w_sems.at[slot])
            if wait:
                cp.wait()
            else:
                cp.start(priority=1)

    @pl.when(e == 0)
    def _():
        for c in range(W_CHUNKS):
            w_chunk(0, c, 0, False)

    @pl.when(count > 0)
    def _():
        x_copy(0, 0).start()

    for c in range(W_CHUNKS):
        w_chunk(e, c, w_slot, True)

    @pl.when(count > 0)
    def _():
        w1b_ref[...] = w1f_ref[w_slot].astype(BF16)
        w2b_ref[...] = w2f_ref[w_slot].astype(BF16)

    def next_chunk(c):
        @pl.when(e + 1 < n_exp)
        def _():
            w_chunk(e + 1, c, 1 - w_slot, False)

    def block(j, carry):
        slot = j & 1
        x_copy(j, slot).wait()

        @pl.when(j + 1 < count)
        def _():
            x_copy(j + 1, 1 - slot).start()

        @pl.when(j < W_CHUNKS)
        def _():
            next_chunk(j)

        @pl.when(j >= 2)
        def _():
            o_copy(j - 2, slot).wait()

        gu = jnp.dot(x_buf[slot], w1b_ref[...], preferred_element_type=F32) + b1_ref[0]
        g = jnp.minimum(gu[:, :f], SWIGLU_LIMIT)
        u = jnp.clip(gu[:, f:], -SWIGLU_LIMIT, SWIGLU_LIMIT)
        a = g * _sigmoid(SWIGLU_ALPHA * g) * (u + 1.0)
        o_buf[slot] = (jnp.dot(a.astype(BF16), w2b_ref[...], preferred_element_type=F32)
                       + b2_ref[0]).astype(o_buf.dtype)
        o_copy(j, slot).start()
        return carry

    lax.fori_loop(0, count, block, 0)
    lax.fori_loop(jnp.minimum(count, W_CHUNKS), W_CHUNKS, lambda c, carry: (next_chunk(c), carry)[1], 0)

    @pl.when(count >= 2)
    def _():
        o_copy(count - 2, count & 1).wait()

    @pl.when(count >= 1)
    def _():
        o_copy(count - 1, (count - 1) & 1).wait()

    @pl.when(e == n_exp - 1)
    def _():
        o_buf[0] = jnp.zeros(o_buf.shape[1:], o_buf.dtype)
        used = first + count

        def fill(j, wait):
            cp = pltpu.make_async_copy(
                o_buf.at[0], eo_ref.at[pl.ds(pl.multiple_of(j * blk, blk), blk)], o_sems.at[0])
            if wait:
                cp.wait()
            else:
                cp.start()

        lax.fori_loop(used, n_blocks, lambda j, c: (fill(j, False), c)[1], 0)
        lax.fori_loop(used, n_blocks, lambda j, c: (fill(j, True), c)[1], 0)


def _experts(region, xs, w1, b1, w2, b2):
    n_rows, d = xs.shape
    n_exp, _, f2 = w1.shape
    f = f2 // 2
    blk = EXP_BLK
    kernel = functools.partial(_expert_kernel, f=f, blk=blk, n_exp=n_exp, n_blocks=n_rows // blk)
    grid_spec = pltpu.PrefetchScalarGridSpec(
        num_scalar_prefetch=1,
        grid=(n_exp,),
        in_specs=[pl.BlockSpec(memory_space=pl.ANY),
                  pl.BlockSpec(memory_space=pl.ANY),
                  pl.BlockSpec((1, 1, f2), lambda e, r: (e, 0, 0)),
                  pl.BlockSpec(memory_space=pl.ANY),
                  pl.BlockSpec((1, 1, d), lambda e, r: (e, 0, 0))],
        out_specs=pl.BlockSpec(memory_space=pl.ANY),
        scratch_shapes=[pltpu.VMEM((2, d, f2), F32), pltpu.VMEM((2, f, d), F32),
                        pltpu.VMEM((d, f2), BF16), pltpu.VMEM((f, d), BF16),
                        pltpu.VMEM((2, blk, d), BF16), pltpu.VMEM((2, blk, d), BF16),
                        pltpu.SemaphoreType.DMA((2,)), pltpu.SemaphoreType.DMA((2,)),
                        pltpu.SemaphoreType.DMA((2,))],
    )
    return pl.pallas_call(
        kernel,
        out_shape=jax.ShapeDtypeStruct((n_rows, d), BF16),
        grid_spec=grid_spec,
        compiler_params=_params("arbitrary"),
        name="experts",
    )(region, xs, w1, b1, w2, b2)


def _combine_kernel(meta_ref, eo_ref, ld_ref, w_ref, x1_ref, mod_ref, o_ref, stage_ref, sems,
                    *, tm, n_exp, n_tiles, stage_rows):
    i = pl.program_id(0)
    slot = i & 1

    @pl.when(i == 0)
    def _():
        stage_ref[...] = jnp.zeros_like(stage_ref)
        _run_slabs(meta_ref, 0, n_exp, stage_ref.at[0], eo_ref, sems.at[0], to_global=False, wait=False)

    @pl.when(i + 1 < n_tiles)
    def _():
        _run_slabs(meta_ref, i + 1, n_exp, stage_ref.at[1 - slot], eo_ref, sems.at[1 - slot],
                   to_global=False, wait=False)

    ld = ld_ref[...]
    w = w_ref[...]
    ldb = [jnp.broadcast_to(ld[:, k:k + 1], (tm, SCATTER_RC)) for k in range(TOP_K)]
    wb = [jnp.broadcast_to(w[:, k:k + 1], (tm, SCATTER_RC)) for k in range(TOP_K)]
    col = lax.broadcasted_iota(I32, (tm, SCATTER_RC), 1)

    def sel_piece(r0):
        r = col + r0
        sel = jnp.where(ldb[0] == r, wb[0], jnp.where(ldb[1] == r, wb[1],
                        jnp.where(ldb[2] == r, wb[2], jnp.where(ldb[3] == r, wb[3], 0.0))))
        return sel.astype(BF16)

    _run_slabs(meta_ref, i, n_exp, stage_ref.at[slot], eo_ref, sems.at[slot], to_global=False, wait=True)
    stage = stage_ref.at[slot]
    y = None
    for c in range(stage_rows // COMBINE_KC):
        r0 = c * COMBINE_KC
        sel = jnp.concatenate([sel_piece(r0 + s) for s in range(0, COMBINE_KC, SCATTER_RC)], axis=1)
        part = jnp.dot(sel, stage[r0:r0 + COMBINE_KC, :], preferred_element_type=F32)
        y = part if y is None else y + part
    o_ref[...] = x1_ref[...] + mod_ref[0][5:6] * y


def _combine(meta, eo, ld_tok, w_tok, x1, mod3, seq, n_exp):
    t, d = x1.shape
    tm = TOK_TM
    n_tiles = t // tm
    stage_rows = _stage_rows(n_exp)
    kernel = functools.partial(_combine_kernel, tm=tm, n_exp=n_exp, n_tiles=n_tiles, stage_rows=stage_rows)
    grid_spec = pltpu.PrefetchScalarGridSpec(
        num_scalar_prefetch=1,
        grid=(n_tiles,),
        in_specs=[pl.BlockSpec(memory_space=pl.ANY),
                  pl.BlockSpec((tm, TOP_K), lambda i, m: (i, 0)),
                  pl.BlockSpec((tm, TOP_K), lambda i, m: (i, 0)),
                  pl.BlockSpec((tm, d), lambda i, m: (i, 0)),
                  pl.BlockSpec((1, 6, d), lambda i, m: (i * tm // seq, 0, 0))],
        out_specs=pl.BlockSpec((tm, d), lambda i, m: (i, 0)),
        scratch_shapes=[pltpu.VMEM((2, stage_rows, d), BF16),
                        pltpu.SemaphoreType.DMA((2,))],
    )
    return pl.pallas_call(
        kernel,
        out_shape=jax.ShapeDtypeStruct((t, d), F32),
        grid_spec=grid_spec,
        compiler_params=_params("arbitrary"),
        name="combine",
    )(meta, eo, ld_tok, w_tok, x1, mod3)


def _routing_tables(tile_cnt, n_tokens):
    n_tiles, n_exp = tile_cnt.shape
    blk = EXP_BLK
    run = _round_up(tile_cnt, RUN_ALIGN)
    local = jnp.cumsum(run, axis=1) - run
    size = jnp.sum(run, axis=0)
    region = _round_up(size, blk)
    region_end = jnp.cumsum(region)
    region_start = region_end - region
    glob = region_start[None, :] + jnp.cumsum(run, axis=0) - run
    sizes = jnp.asarray(SLAB_SIZES, I32)
    has = (run[:, :, None] & sizes) != 0
    before = run[:, :, None] & ~(2 * sizes - 1)
    place = jnp.cumsum(has, axis=1) - has
    hit = has[..., None] & (place[..., None] == jnp.arange(n_exp, dtype=I32))
    listed = lambda start: jnp.sum(
        jnp.where(hit, (start[:, :, None] + before)[..., None], 0), axis=1).reshape(n_tiles, -1)
    meta = jnp.concatenate([listed(local), listed(glob), jnp.sum(has, axis=1)],
                           axis=1).reshape(-1).astype(I32)
    tail = jnp.concatenate([region_start + size, region - size, region_end[-1:]]).astype(I32)
    n_rows = _round_up(n_tokens * TOP_K + n_tiles * n_exp * RUN_ALIGN + n_exp * blk, blk)
    region_blocks = jnp.concatenate([region_start // blk, region // blk]).astype(I32)
    return meta, tail, region_blocks, n_rows


def kernel(x, c, w_ada, b_ada, g_norm1, w_in, g_ret_gn, b_ret_gn, g_qnorm, g_knorm,
           lambda_q1, lambda_k1, lambda_q2, lambda_k2, g_diff_subln, w_out, g_norm2,
           w_router, b_router, w_expert_in, b_expert_in, w_expert_out, b_expert_out):
    batch, seq, d = x.shape
    depth = w_ada.shape[0]
    assert depth == 1
    t = batch * seq
    n_exp = w_router.shape[-1]
    l = 0

    mod = _adaln_mod(c, w_ada[l], b_ada[l][None, :])
    mod3 = mod.reshape(batch, 6, d)
    x2 = x.reshape(t, d)

    proj = _in_proj(x2, mod3, g_norm1[l][None, :], w_in[l].astype(BF16), seq)
    ret = _retention(proj, g_ret_gn[l][None, :], b_ret_gn[l][None, :], batch, seq, d)
    att = _diff_attention(proj, g_qnorm[l][None, :], g_knorm[l][None, :], g_diff_subln[l][None, :],
                          lambda_q1[l][None, :], lambda_k1[l][None, :],
                          lambda_q2[l][None, :], lambda_k2[l][None, :], batch, seq, d)

    wr = w_router[l]
    wr_hi = wr.astype(BF16)
    wr_lo = (wr - wr_hi.astype(F32)).astype(BF16)
    pad = lambda a: jnp.pad(a, ((0, 0), (0, LANES - n_exp)))
    wr_split = jnp.concatenate([pad(wr_hi), pad(wr_lo)], axis=1)
    br_pad = jnp.pad(b_router[l][None, :], ((0, 0), (0, LANES - n_exp)), constant_values=MASK_VALUE)
    x1, h2, top_w, ldest, tile_cnt = _out_router(
        x2, ret, att, mod3, w_out[l].astype(BF16), g_norm2[l][None, :], wr_split, br_pad, seq, n_exp)

    meta, tail, region_blocks, n_rows = _routing_tables(tile_cnt[:, :, 0].astype(I32), t)
    xs = _dispatch(meta, tail, ldest, h2, n_rows, n_exp)
    eo = _experts(region_blocks, xs, w_expert_in[l], b_expert_in[l][:, None, :],
                  w_expert_out[l], b_expert_out[l][:, None, :])
    out = _combine(meta, eo, ldest.T, top_w.T, x1, mod3, seq, n_exp)
    return out.reshape(batch, seq, d)
```

```python
import functools
import itertools
import math

import jax
import jax.numpy as jnp
from jax import lax
from jax.experimental import pallas as pl
from jax.experimental.pallas import tpu as pltpu

F32 = jnp.float32
BF16 = jnp.bfloat16
I32 = jnp.int32

EPS = 1e-6
LOG2E = 1.4426950216293335
MASK_VALUE = -1e30
LANES = 128
BF16_SUBLANES = 16
VMEM_LIMIT_BYTES = 48 * 1024 * 1024

RET_HEADS = 4
DIFF_HEADS = 8
TOP_K = 4
SWIGLU_LIMIT = 7.0
SWIGLU_ALPHA = 1.702
LAMBDA_INIT = 0.8 - 0.6 * math.exp(-0.3 * 0)

IN_TM, IN_TN = 1024, 2048
RET_TC = 256
RET_CHUNKS_PER_STEP = 2
ATT_T = 512
TOK_TM = 512
EXP_BLK = 512
RUN_ALIGN = BF16_SUBLANES
GATHER_RC = 256
SCATTER_RC = 128
COMBINE_KC = 512
W_CHUNKS = 8


def _round_up(n, m):
    return (n + m - 1) // m * m


def _stage_rows(n_exp):
    return _round_up(TOK_TM * TOP_K + n_exp * (RUN_ALIGN - 1), max(GATHER_RC, COMBINE_KC))


SLAB_SIZES = tuple(TOK_TM >> s for s in range(TOK_TM.bit_length()) if TOK_TM >> s >= RUN_ALIGN)
TAIL_SIZES = tuple(s for s in SLAB_SIZES if s < EXP_BLK)


def _params(*sem):
    return pltpu.CompilerParams(dimension_semantics=sem, vmem_limit_bytes=VMEM_LIMIT_BYTES)


def _sigmoid(x):
    return 0.5 * jnp.tanh(0.5 * x) + 0.5


def _rms(x, axis=-1):
    return x * lax.rsqrt(jnp.mean(x * x, axis=axis, keepdims=True) + EPS)


def _mod_kernel(c_ref, w_ref, b_ref, o_ref):
    c = c_ref[...]
    s = c * _sigmoid(c)
    o_ref[...] = jnp.dot(s, w_ref[...], preferred_element_type=F32,
                         precision=lax.Precision.HIGHEST) + b_ref[...]


def _adaln_mod(c, w, b):
    bn, d = c.shape
    n = w.shape[1]
    tn = d
    return pl.pallas_call(
        _mod_kernel,
        out_shape=jax.ShapeDtypeStruct((bn, n), F32),
        grid=(n // tn,),
        in_specs=[pl.BlockSpec((bn, d), lambda j: (0, 0)),
                  pl.BlockSpec((d, tn), lambda j: (0, j)),
                  pl.BlockSpec((1, tn), lambda j: (0, j))],
        out_specs=pl.BlockSpec((bn, tn), lambda j: (0, j)),
        compiler_params=_params("arbitrary"),
        name="adaln_mod",
    )(c, w, b)


def _inproj_kernel(x_ref, mod_ref, g_ref, w_ref, o_ref, h_ref):
    @pl.when(pl.program_id(1) == 0)
    def _():
        m = mod_ref[0]
        h = _rms(x_ref[...]) * g_ref[...] * (1.0 + m[1:2]) + m[0:1]
        h_ref[...] = h.astype(BF16)

    o_ref[...] = jnp.dot(h_ref[...], w_ref[...], preferred_element_type=F32).astype(o_ref.dtype)


def _in_proj(x2, mod3, g1, w_in_bf16, seq):
    t, d = x2.shape
    n = w_in_bf16.shape[1]
    tm, tn = IN_TM, IN_TN
    return pl.pallas_call(
        _inproj_kernel,
        out_shape=jax.ShapeDtypeStruct((t, n), BF16),
        grid=(t // tm, n // tn),
        in_specs=[pl.BlockSpec((tm, d), lambda i, j: (i, 0)),
                  pl.BlockSpec((1, 6, d), lambda i, j: (i * tm // seq, 0, 0)),
                  pl.BlockSpec((1, d), lambda i, j: (0, 0)),
                  pl.BlockSpec((d, tn), lambda i, j: (0, j))],
        out_specs=pl.BlockSpec((tm, tn), lambda i, j: (i, j)),
        scratch_shapes=[pltpu.VMEM((tm, d), BF16)],
        compiler_params=_params("arbitrary", "arbitrary"),
        name="in_proj",
    )(x2, mod3, g1, w_in_bf16)


def _ret_kernel(q_ref, k_ref, v_ref, rg_ref, mg_ref, gng_ref, gnb_ref, o_ref, r_ref, decay_ref,
                *, dk, dv, tc):
    scale = dk ** -0.5
    log_gs = [math.log(1.0 - 2.0 ** (-5.0 - h)) for h in range(RET_HEADS)]

    @pl.when(pl.program_id(1) == 0)
    def _():
        r_ref[...] = jnp.zeros_like(r_ref)
        row = lax.broadcasted_iota(I32, (tc, tc), 0)
        col = lax.broadcasted_iota(I32, (tc, tc), 1)
        rel = (row - col).astype(F32)
        for h, log_g in enumerate(log_gs):
            decay_ref[h] = jnp.where(rel >= 0, jnp.exp(log_g * jnp.maximum(rel, 0.0)), 0.0) * scale

    pos = lax.broadcasted_iota(I32, (tc, 1), 0).astype(F32)
    for sub, (h, log_g) in itertools.product(range(RET_CHUNKS_PER_STEP), enumerate(log_gs)):
        rows = slice(sub * tc, (sub + 1) * tc)
        decay = decay_ref[h]
        xi = jnp.exp(log_g * (pos + 1.0))
        zeta = jnp.exp(log_g * (tc - 1.0 - pos))
        g_chunk = math.exp(log_g * tc)
        q = q_ref[rows, h * dk:(h + 1) * dk]
        k = k_ref[rows, h * dk:(h + 1) * dk]
        v = v_ref[rows, h * dv:(h + 1) * dv]
        s = lax.dot_general(q, k, (((1,), (1,)), ((), ())), preferred_element_type=F32)
        y = jnp.dot((s * decay).astype(BF16), v, preferred_element_type=F32)
        state = r_ref[h]
        y = y + jnp.dot(q, state.astype(BF16), preferred_element_type=F32) * xi
        vz = (v.astype(F32) * zeta).astype(BF16)
        kv = lax.dot_general(k, vz, (((0,), (0,)), ((), ())), preferred_element_type=F32)
        r_ref[h] = state * g_chunk + kv * scale

        mu = jnp.mean(y, axis=-1, keepdims=True)
        yc = y - mu
        var = jnp.mean(yc * yc, axis=-1, keepdims=True)
        sl = slice(h * dv, (h + 1) * dv)
        yn = yc * lax.rsqrt(var + EPS) * gng_ref[:, sl] + gnb_ref[:, sl]
        rg = rg_ref[rows, sl].astype(F32)
        gate = _sigmoid(mg_ref[rows, sl].astype(F32))
        o_ref[rows, sl] = (rg * _sigmoid(rg) * yn * gate).astype(o_ref.dtype)


def _retention(proj, gn_g, gn_b, batch, seq, d):
    t = proj.shape[0]
    tc = RET_TC
    tr = tc * RET_CHUNKS_PER_STEP
    nc = seq // tr
    dk = d // (2 * RET_HEADS)
    dv = d // RET_HEADS
    qk_w = RET_HEADS * dk
    row = lambda b, c: b * nc + c
    kernel = functools.partial(_ret_kernel, dk=dk, dv=dv, tc=tc)
    return pl.pallas_call(
        kernel,
        out_shape=jax.ShapeDtypeStruct((t, d), BF16),
        grid=(batch, nc),
        in_specs=[pl.BlockSpec((tr, qk_w), lambda b, c: (row(b, c), 0)),
                  pl.BlockSpec((tr, qk_w), lambda b, c: (row(b, c), 1)),
                  pl.BlockSpec((tr, d), lambda b, c: (row(b, c), 1)),
                  pl.BlockSpec((tr, d), lambda b, c: (row(b, c), 2)),
                  pl.BlockSpec((tr, d), lambda b, c: (row(b, c), 6)),
                  pl.BlockSpec((1, d), lambda b, c: (0, 0)),
                  pl.BlockSpec((1, d), lambda b, c: (0, 0))],
        out_specs=pl.BlockSpec((tr, d), lambda b, c: (row(b, c), 0)),
        scratch_shapes=[pltpu.VMEM((RET_HEADS, dk, dv), F32), pltpu.VMEM((RET_HEADS, tc, tc), F32)],
        compiler_params=_params("arbitrary", "arbitrary"),
        name="retention",
    )(proj, proj, proj, proj, proj, gn_g, gn_b)


def _halves_rms(x, lo, dh):
    sq = x * x
    s_lo = jnp.sum(jnp.where(lo, sq, 0.0), axis=-1, keepdims=True)
    s_hi = jnp.sum(jnp.where(lo, 0.0, sq), axis=-1, keepdims=True)
    inv = jnp.where(lo, lax.rsqrt(s_lo / dh + EPS), lax.rsqrt(s_hi / dh + EPS))
    return x * inv


def _attn_kernel(q_ref, k_ref, v_ref, gate_ref, slope_ref, qca_ref, qcb_ref, gq_ref, gk_ref, gsub_ref,
                 lq1_ref, lk1_ref, lq2_ref, lk2_ref,
                 o_ref, ka_ref, kb_ref, vt_ref, qa_ref, qb_ref, s0_ref, sa_ref, sb_ref, acc_ref, m_ref,
                 *, tile, dh, dv, nk):
    lane = lax.broadcasted_iota(I32, (tile, 2 * dh), 1)
    lo = lane < dh

    slope = slope_ref[0]
    sub = lax.broadcasted_iota(I32, (tile, 2 * dh), 0)
    off = lane & (dh - 1)
    hi_lane = (off < 6) & ((off & 1) == 0)
    lo_lane = (off < 6) & ((off & 1) == 1)
    bias0 = jnp.where(hi_lane, ((sub >> 8) << 8).astype(F32),
                      jnp.where(lo_lane, (sub & 255).astype(F32), 0.0)) * slope
    bias_step = jnp.where(hi_lane, float(tile), 0.0) * slope
    ones_rows = jnp.where(lax.broadcasted_iota(I32, (BF16_SUBLANES, tile), 0) == 0, 1.0, 0.0)
    lo_t = lax.broadcasted_iota(I32, (2 * dh, tile), 0) < dh

    def prepare(c, carry):
        r0 = c * tile if isinstance(c, int) else pl.multiple_of(c * tile, tile)
        kn = _halves_rms(k_ref[pl.ds(r0, tile), :].astype(F32), lo, dh) * gk_ref[...]
        bias = bias0 + lax.convert_element_type(c, F32) * bias_step
        ka_ref[c] = jnp.where(lo, kn, bias).astype(BF16)
        kb_ref[c] = jnp.where(lo, bias, kn).astype(BF16)
        vt_ref[c, :dv, :] = v_ref[pl.ds(r0, tile), :].astype(F32).T.astype(BF16)
        vt_ref[c, dv:, :] = ones_rows.astype(BF16)
        qn = (_halves_rms(q_ref[pl.ds(r0, tile), :].astype(F32), lo, dh) * gq_ref[...]
              * (dh ** -0.5 * LOG2E))
        qn_t = qn.T
        qa_ref[c] = jnp.where(lo_t, qn_t, qca_ref[...]).astype(BF16)
        qb_ref[c] = jnp.where(lo_t, qcb_ref[...], qn_t).astype(BF16)
        return carry

    lax.fori_loop(0, min(2, nk), prepare, 0)
    lam = (jnp.exp(jnp.sum(lq1_ref[...] * lk1_ref[...], axis=-1, keepdims=True))
           - jnp.exp(jnp.sum(lq2_ref[...] * lk2_ref[...], axis=-1, keepdims=True)) + LAMBDA_INIT)

    refs = (gate_ref, gsub_ref, o_ref, ka_ref, kb_ref, vt_ref, qa_ref, qb_ref,
            s0_ref, sa_ref, sb_ref, acc_ref, m_ref)
    _attn_scores(ka_ref, kb_ref, qa_ref, qb_ref, 0, 0, s0_ref, tile, diagonal=True)
    _attn_q_tile(0, lam, prepare, *refs, tile=tile, dv=dv, nk=nk)

    def q_tile(qi, carry):
        _attn_q_tile(qi, lam, prepare, *refs, tile=tile, dv=dv, nk=nk)
        return carry

    lax.fori_loop(1, nk, q_tile, 0)


def _attn_scores(ka_ref, kb_ref, qa_ref, qb_ref, jb, q, s_ref, tile, diagonal=False):
    h = tile // 2
    for idx, (k_ref, q_ref) in enumerate(((ka_ref, qa_ref), (kb_ref, qb_ref))):
        if diagonal:
            s_ref[idx, :h, :] = jnp.dot(k_ref[jb, :h, :], q_ref[q], preferred_element_type=F32)
            s_ref[idx, h:tile, h:] = jnp.dot(k_ref[jb, h:, :], q_ref[q, :, h:], preferred_element_type=F32)
        else:
            s = jnp.dot(k_ref[jb], q_ref[q], preferred_element_type=F32)
            s_ref[idx, :tile, :] = s
            s_ref[idx, tile:tile + 1, :] = jnp.max(s, axis=0, keepdims=True)


def _attn_q_tile(qi, lam, prepare, gate_ref, gsub_ref, o_ref, ka_ref, kb_ref, vt_ref, qa_ref, qb_ref,
                 s0_ref, sa_ref, sb_ref, acc_ref, m_ref, *, tile, dv, nk):
    def scores(jb, s_ref, q=qi, diagonal=False):
        _attn_scores(ka_ref, kb_ref, qa_ref, qb_ref, jb, q, s_ref, tile, diagonal)

    def absorb(jb, s_ref, masked):
        vt = vt_ref[jb]
        if masked:
            return absorb_diagonal(s_ref, vt)
        for idx in range(2):
            m_old = m_ref[idx]
            m_new = jnp.maximum(m_old, s_ref[idx, tile:tile + 1, :])
            m_ref[idx] = m_new
            p = jnp.exp2(s_ref[idx, :tile, :] - m_new).astype(BF16)
            acc_ref[idx] = (acc_ref[idx] * jnp.exp2(m_old - m_new)
                            + jnp.dot(vt, p, preferred_element_type=F32))

    def absorb_diagonal(s_ref, vt):
        h = tile // 2
        future = lax.broadcasted_iota(I32, (h, h), 0) > lax.broadcasted_iota(I32, (h, h), 1)
        for idx in range(2):
            early = s_ref[idx, :h, :]
            early = jnp.concatenate([jnp.where(future, MASK_VALUE, early[:, :h]), early[:, h:]], axis=1)
            late = jnp.where(future, MASK_VALUE, s_ref[idx, h:tile, h:])
            m_old = m_ref[idx]
            m_new = jnp.maximum(m_old, jnp.max(early, axis=0, keepdims=True))
            m_new = jnp.concatenate(
                [m_new[:, :h], jnp.maximum(m_new[:, h:], jnp.max(late, axis=0, keepdims=True))], axis=1)
            m_ref[idx] = m_new
            p_early = jnp.exp2(early - m_new).astype(BF16)
            p_late = jnp.exp2(late - m_new[:, h:]).astype(BF16)
            acc_ref[idx] = (acc_ref[idx] * jnp.exp2(m_old - m_new)
                            + jnp.dot(vt[:, :h], p_early, preferred_element_type=F32))
            acc_ref[idx, :, h:] = acc_ref[idx, :, h:] + jnp.dot(vt[:, h:], p_late,
                                                                preferred_element_type=F32)

    ahead = (lambda n: min(qi + n, nk - 1)) if isinstance(qi, int) else (lambda n: jnp.minimum(qi + n, nk - 1))

    def diagonal_block_and_output(s_ref):
        if s_ref is not s0_ref:
            scores(0, s0_ref, q=ahead(1))
        absorb(qi, s_ref, True)
        if s_ref is s0_ref:
            scores(0, s0_ref, q=ahead(1))
        prepare(ahead(2), 0)
        a1, a2 = acc_ref[0], acc_ref[1]
        o_t = a1[:dv] * (1.0 / a1[dv:dv + 1]) - lam * (a2[:dv] * (1.0 / a2[dv:dv + 1]))
        o = _rms(o_t, axis=0).T * gsub_ref[...] * (1.0 - LAMBDA_INIT)
        rows = pl.ds(pl.multiple_of(qi * tile, tile), tile)
        o_ref[rows, :] = (o * _sigmoid(gate_ref[rows, :].astype(F32))).astype(o_ref.dtype)

    acc_ref[...] = jnp.zeros_like(acc_ref)
    m_ref[...] = jnp.full_like(m_ref, MASK_VALUE)
    if isinstance(qi, int):
        diagonal_block_and_output(s0_ref)
    else:
        scores(1, sa_ref)
        absorb(0, s0_ref, False)

        def pair(j2, carry):
            jb = 2 * j2 + 1
            scores(jb + 1, sb_ref)
            absorb(jb, sa_ref, False)
            scores(jb + 2, sa_ref)
            absorb(jb + 1, sb_ref, False)
            return carry

        lax.fori_loop(0, (qi - 1) // 2, pair, 0)

        @pl.when(qi % 2 == 0)
        def _():
            scores(qi, sb_ref, diagonal=True)
            absorb(qi - 1, sa_ref, False)
            diagonal_block_and_output(sb_ref)

        @pl.when(qi % 2 == 1)
        def _():
            diagonal_block_and_output(sa_ref)


def _bf16_terms(x, n):
    terms, rest = [], jnp.asarray(x, F32)
    for _ in range(n):
        term = rest.astype(BF16).astype(F32)
        terms.append(term)
        rest = rest - term
    return terms


def _diff_attention(proj, g_q, g_k, g_sub, lq1, lk1, lq2, lk2, batch, seq, d):
    t = proj.shape[0]
    tile = ATT_T
    nq = seq // tile
    dh = d // (2 * DIFF_HEADS)
    w = 2 * dh
    assert w == LANES
    q_col, k_col, v_col, gate_col = 3 * d // w, 4 * d // w, 5 * d // w, 7 * d // w
    slopes = 2.0 ** -(jnp.arange(DIFF_HEADS, dtype=F32) + 1.0)
    slopes = jnp.broadcast_to(slopes[:, None, None], (DIFF_HEADS, 1, LANES))
    c1, c2, c3 = _bf16_terms(LOG2E, 3)
    qconst = jnp.zeros((dh,), F32).at[:6].set(jnp.stack([c1, c1, c2, c2, c3, c3]))
    zeros = jnp.zeros((dh,), F32)
    qconst_a = jnp.concatenate([zeros, qconst])[:, None]
    qconst_b = jnp.concatenate([qconst, zeros])[:, None]
    tile2 = lambda a: jnp.concatenate([a, a], axis=-1)
    assert tile % 256 == 0
    small = lambda n: pl.BlockSpec((1, n), lambda b, h: (0, 0))
    kernel = functools.partial(_attn_kernel, tile=tile, dh=dh, dv=w, nk=nq)
    return pl.pallas_call(
        kernel,
        out_shape=jax.ShapeDtypeStruct((t, d), BF16),
        grid=(batch, DIFF_HEADS),
        in_specs=[pl.BlockSpec((seq, w), lambda b, h: (b, q_col + h)),
                  pl.BlockSpec((seq, w), lambda b, h: (b, k_col + h)),
                  pl.BlockSpec((seq, w), lambda b, h: (b, v_col + h)),
                  pl.BlockSpec((seq, w), lambda b, h: (b, gate_col + h)),
                  pl.BlockSpec((1, 1, LANES), lambda b, h: (h, 0, 0)),
                  pl.BlockSpec((w, 1), lambda b, h: (0, 0)), pl.BlockSpec((w, 1), lambda b, h: (0, 0)),
                  small(w), small(w), small(w),
                  small(dh), small(dh), small(dh), small(dh)],
        out_specs=pl.BlockSpec((seq, w), lambda b, h: (b, h)),
        scratch_shapes=[pltpu.VMEM((nq, tile, w), BF16),
                        pltpu.VMEM((nq, tile, w), BF16),
                        pltpu.VMEM((nq, w + BF16_SUBLANES, tile), BF16),
                        pltpu.VMEM((nq, w, tile), BF16),
                        pltpu.VMEM((nq, w, tile), BF16),
                        pltpu.VMEM((2, tile + 8, tile), F32),
                        pltpu.VMEM((2, tile + 8, tile), F32),
                        pltpu.VMEM((2, tile + 8, tile), F32),
                        pltpu.VMEM((2, w + BF16_SUBLANES, tile), F32),
                        pltpu.VMEM((2, 1, tile), F32)],
        compiler_params=_params("arbitrary", "arbitrary"),
        name="diff_attn",
    )(proj, proj, proj, proj, slopes, qconst_a, qconst_b, tile2(g_q), tile2(g_k), g_sub,
      lq1, lk1, lq2, lk2)


def _out_kernel(x_ref, ret_ref, att_ref, mod_ref, wo_ref, g2_ref, wr_ref, br_ref,
                x1_ref, h2_ref, tw_ref, ld_ref, cnt_ref, *, tm, n_exp):
    m = mod_ref[0]
    merged = (ret_ref[...].astype(F32) + att_ref[...].astype(F32)).astype(BF16)
    x1 = x_ref[...] + m[2:3] * jnp.dot(merged, wo_ref[...], preferred_element_type=F32)
    x1_ref[...] = x1
    h2 = _rms(x1) * g2_ref[...] * (1.0 + m[4:5]) + m[3:4]
    hi = h2.astype(BF16)
    h2_ref[...] = hi

    lo = (h2 - hi.astype(F32)).astype(BF16)
    both = jnp.dot(hi, wr_ref[...], preferred_element_type=F32)
    logits = (both[:, :LANES] + both[:, LANES:]
              + jnp.dot(lo, wr_ref[:, :LANES], preferred_element_type=F32) + br_ref[...])
    lt = logits.T[:n_exp]

    erow = lax.broadcasted_iota(I32, (n_exp, tm), 0)
    vals, sels = [], []
    for k in range(TOP_K):
        mx = jnp.max(lt, axis=0, keepdims=True)
        idx = jnp.min(jnp.where(lt == mx, erow, n_exp), axis=0, keepdims=True)
        sel = erow == idx
        vals.append(mx)
        sels.append(sel)
        lt = jnp.where(sel, MASK_VALUE, lt)
    exps = [jnp.exp(v - vals[0]) for v in vals]
    inv = 1.0 / (exps[0] + exps[1] + exps[2] + exps[3])
    for k in range(TOP_K):
        tw_ref[k:k + 1, :] = exps[k] * inv

    chosen = jnp.where(sels[0] | sels[1] | sels[2] | sels[3], 1.0, 0.0)
    before = (lax.broadcasted_iota(I32, (tm, tm), 0) < lax.broadcasted_iota(I32, (tm, tm), 1))
    prefix = jnp.dot(chosen.astype(BF16), jnp.where(before, 1.0, 0.0).astype(BF16),
                     preferred_element_type=F32)
    count = jnp.sum(chosen, axis=1, keepdims=True)
    cnt_ref[...] = count
    padded = jnp.ceil(count * (1.0 / RUN_ALIGN)) * RUN_ALIGN
    below = (lax.broadcasted_iota(I32, (n_exp, n_exp), 1) < lax.broadcasted_iota(I32, (n_exp, n_exp), 0))
    run_start = jnp.dot(jnp.where(below, 1.0, 0.0).astype(BF16),
                        jnp.broadcast_to(padded, (n_exp, LANES)).astype(BF16),
                        preferred_element_type=F32)[:, :1]
    pos = prefix + run_start
    for k in range(TOP_K):
        ld_ref[k:k + 1, :] = jnp.sum(jnp.where(sels[k], pos, 0.0), axis=0, keepdims=True).astype(I32)


def _out_router(x2, ret, att, mod3, w_out_bf16, g2, wr_split, br_pad, seq, n_exp):
    t, d = x2.shape
    tm = TOK_TM
    row = lambda i: (i, 0)
    const = lambda i: (0, 0)
    kernel = functools.partial(_out_kernel, tm=tm, n_exp=n_exp)
    return pl.pallas_call(
        kernel,
        out_shape=(jax.ShapeDtypeStruct((t, d), F32),
                   jax.ShapeDtypeStruct((t, d), BF16),
                   jax.ShapeDtypeStruct((TOP_K, t), F32),
                   jax.ShapeDtypeStruct((TOP_K, t), I32),
                   jax.ShapeDtypeStruct((t // tm, n_exp, 1), F32)),
        grid=(t // tm,),
        in_specs=[pl.BlockSpec((tm, d), row), pl.BlockSpec((tm, d), row), pl.BlockSpec((tm, d), row),
                  pl.BlockSpec((1, 6, d), lambda i: (i * tm // seq, 0, 0)),
                  pl.BlockSpec((d, d), const), pl.BlockSpec((1, d), const),
                  pl.BlockSpec((d, 2 * LANES), const), pl.BlockSpec((1, LANES), const)],
        out_specs=(pl.BlockSpec((tm, d), row), pl.BlockSpec((tm, d), row),
                   pl.BlockSpec((TOP_K, tm), lambda i: (0, i)),
                   pl.BlockSpec((TOP_K, tm), lambda i: (0, i)),
                   pl.BlockSpec((None, n_exp, 1), lambda i: (i, 0, 0))),
        compiler_params=_params("arbitrary"),
        name="out_router",
    )(x2, ret, att, mod3, w_out_bf16, g2, wr_split, br_pad)


def _meta_stride(n_exp):
    return 2 * len(SLAB_SIZES) * n_exp + len(SLAB_SIZES)


def _run_slabs(meta_ref, tile, n_exp, local_ref, global_ref, sem, *, to_global, wait):
    n_sizes = len(SLAB_SIZES)
    base = tile * _meta_stride(n_exp)
    for c, size in enumerate(SLAB_SIZES):
        def one(p, carry, c=c, size=size):
            if wait:
                local = glob = 0
            else:
                local = pl.multiple_of(meta_ref[base + c * n_exp + p], RUN_ALIGN)
                glob = pl.multiple_of(meta_ref[base + (n_sizes + c) * n_exp + p], RUN_ALIGN)
            loc = local_ref.at[pl.ds(local, size)]
            glo = global_ref.at[pl.ds(glob, size)]
            cp = pltpu.make_async_copy(loc, glo, sem) if to_global else pltpu.make_async_copy(glo, loc, sem)
            if wait:
                cp.wait()
            else:
                cp.start()
            return carry

        lax.fori_loop(0, meta_ref[base + 2 * n_sizes * n_exp + c], one, 0)


def _dispatch_kernel(meta_ref, tail_ref, ld_ref, h_ref, xs_ref, stage_ref, zero_ref, sems, tail_sem,
                     *, tm, n_exp, n_tiles, n_blocks, stage_rows):
    i = pl.program_id(0)
    slot = i & 1
    stage = stage_ref.at[slot]

    def tails(wait):
        def unused_block(b, carry):
            cp = pltpu.make_async_copy(
                zero_ref, xs_ref.at[pl.ds(pl.multiple_of(b * EXP_BLK, EXP_BLK), EXP_BLK)], tail_sem)
            if wait:
                cp.wait()
            else:
                cp.start()
            return carry

        lax.fori_loop(tail_ref[2 * n_exp] // EXP_BLK, n_blocks, unused_block, 0)

        def per_expert(e, carry):
            start, n = tail_ref[e], tail_ref[n_exp + e]
            off = 0
            for size in TAIL_SIZES:
                take = n & size

                @pl.when(take != 0)
                def _(off=off, size=size):
                    cp = pltpu.make_async_copy(
                        zero_ref.at[pl.ds(0, size)],
                        xs_ref.at[pl.ds(pl.multiple_of(start + off, RUN_ALIGN), size)], tail_sem)
                    if wait:
                        cp.wait()
                    else:
                        cp.start()

                off = off + take
            return carry

        lax.fori_loop(0, n_exp, per_expert, 0)

    @pl.when(i == 0)
    def _():
        zero_ref[...] = jnp.zeros_like(zero_ref)
        tails(False)
        tails(True)

    ld = ld_ref[...]
    h = h_ref[...]
    for c in range(stage_rows // GATHER_RC):
        r = lax.broadcasted_iota(I32, (GATHER_RC, tm), 0) + c * GATHER_RC
        hit = jnp.where(r == ld[0:1], 1.0, jnp.where(r == ld[1:2], 1.0,
                        jnp.where(r == ld[2:3], 1.0, jnp.where(r == ld[3:4], 1.0, 0.0))))
        stage[c * GATHER_RC:(c + 1) * GATHER_RC, :] = jnp.dot(
            hit.astype(BF16), h, preferred_element_type=F32).astype(BF16)

    _run_slabs(meta_ref, i, n_exp, stage, xs_ref, sems.at[slot], to_global=True, wait=False)

    @pl.when(i > 0)
    def _():
        _run_slabs(meta_ref, i - 1, n_exp, stage_ref.at[1 - slot], xs_ref, sems.at[1 - slot],
                   to_global=True, wait=True)

    @pl.when(i == n_tiles - 1)
    def _():
        _run_slabs(meta_ref, i, n_exp, stage, xs_ref, sems.at[slot], to_global=True, wait=True)


def _dispatch(meta, tail, ldest, h2, n_rows, n_exp):
    t, d = h2.shape
    tm = TOK_TM
    n_tiles = t // tm
    stage_rows = _stage_rows(n_exp)
    kernel = functools.partial(_dispatch_kernel, tm=tm, n_exp=n_exp, n_tiles=n_tiles,
                               n_blocks=n_rows // EXP_BLK, stage_rows=stage_rows)
    grid_spec = pltpu.PrefetchScalarGridSpec(
        num_scalar_prefetch=2,
        grid=(n_tiles,),
        in_specs=[pl.BlockSpec((TOP_K, tm), lambda i, m, tl: (0, i)),
                  pl.BlockSpec((tm, d), lambda i, m, tl: (i, 0))],
        out_specs=pl.BlockSpec(memory_space=pl.ANY),
        scratch_shapes=[pltpu.VMEM((2, stage_rows, d), BF16),
                        pltpu.VMEM((EXP_BLK, d), BF16),
                        pltpu.SemaphoreType.DMA((2,)),
                        pltpu.SemaphoreType.DMA],
    )
    return pl.pallas_call(
        kernel,
        out_shape=jax.ShapeDtypeStruct((n_rows, d), BF16),
        grid_spec=grid_spec,
        compiler_params=_params("arbitrary"),
        name="dispatch",
    )(meta, tail, ldest, h2)


def _expert_kernel(region_ref, xs_ref, w1_ref, b1_ref, w2_ref, b2_ref, eo_ref,
                   w1f_ref, w2f_ref, w1b_ref, w2b_ref, x_buf, o_buf, w_sems, x_sems, o_sems,
                   *, f, blk, n_exp, n_blocks):
    e = pl.program_id(0)
    first, count = region_ref[e], region_ref[n_exp + e]
    w_slot = e & 1
    d = w1f_ref.shape[1]
    r1, r2 = d // W_CHUNKS, f // W_CHUNKS

    def rows(j):
        return pl.ds(pl.multiple_of((first + j) * blk, blk), blk)

    def x_copy(j, slot):
        return pltpu.make_async_copy(xs_ref.at[rows(j)], x_buf.at[slot], x_sems.at[slot])

    def o_copy(j, slot):
        return pltpu.make_async_copy(o_buf.at[slot], eo_ref.at[rows(j)], o_sems.at[slot])

    def w_chunk(expert, c, slot, wait):
        for src, dst, r in ((w1_ref, w1f_ref, r1), (w2_ref, w2f_ref, r2)):
            piece = pl.ds(c * r if isinstance(c, int) else pl.multiple_of(c * r, r), r)
            cp = pltpu.make_async_copy(src.at[expert, piece], dst.at[slot, piece], w_sems.at[slot])
            if wait:
                cp.wait()
            else:
                cp.start()

    @pl.when(e == 0)
    def _():
        for c in range(W_CHUNKS):
            w_chunk(0, c, 0, False)

    @pl.when(count > 0)
    def _():
        x_copy(0, 0).start()

    for c in range(W_CHUNKS):
        w_chunk(e, c, w_slot, True)

    @pl.when(count > 0)
    def _():
        w1b_ref[...] = w1f_ref[w_slot].astype(BF16)
        w2b_ref[...] = w2f_ref[w_slot].astype(BF16)

    def next_chunk(c):
        @pl.when(e + 1 < n_exp)
        def _():
            w_chunk(e + 1, c, 1 - w_slot, False)

    def block(j, carry):
        slot = j & 1
        x_copy(j, slot).wait()

        @pl.when(j + 1 < count)
        def _():
            x_copy(j + 1, 1 - slot).start()

        @pl.when(j < W_CHUNKS)
        def _():
            next_chunk(j)

        @pl.when(j >= 2)
        def _():
            o_copy(j - 2, slot).wait()

        gu = jnp.dot(x_buf[slot], w1b_ref[...], preferred_element_type=F32) + b1_ref[0]
        g = jnp.minimum(gu[:, :f], SWIGLU_LIMIT)
        u = jnp.clip(gu[:, f:], -SWIGLU_LIMIT, SWIGLU_LIMIT)
        a = g * _sigmoid(SWIGLU_ALPHA * g) * (u + 1.0)
        o_buf[slot] = (jnp.dot(a.astype(BF16), w2b_ref[...], preferred_element_type=F32)
                       + b2_ref[0]).astype(o_buf.dtype)
        o_copy(j, slot).start()
        return carry

    lax.fori_loop(0, count, block, 0)
    lax.fori_loop(jnp.minimum(count, W_CHUNKS), W_CHUNKS, lambda c, carry: (next_chunk(c), carry)[1], 0)

    @pl.when(count >= 2)
    def _():
        o_copy(count - 2, count & 1).wait()

    @pl.when(count >= 1)
    def _():
        o_copy(count - 1, (count - 1) & 1).wait()

    @pl.when(e == n_exp - 1)
    def _():
        o_buf[0] = jnp.zeros(o_buf.shape[1:], o_buf.dtype)
        used = first + count

        def fill(j, wait):
            cp = pltpu.make_async_copy(
                o_buf.at[0], eo_ref.at[pl.ds(pl.multiple_of(j * blk, blk), blk)], o_sems.at[0])
            if wait:
                cp.wait()
            else:
                cp.start()

        lax.fori_loop(used, n_blocks, lambda j, c: (fill(j, False), c)[1], 0)
        lax.fori_loop(used, n_blocks, lambda j, c: (fill(j, True), c)[1], 0)


def _experts(region, xs, w1, b1, w2, b2):
    n_rows, d = xs.shape
    n_exp, _, f2 = w1.shape
    f = f2 // 2
    blk = EXP_BLK
    kernel = functools.partial(_expert_kernel, f=f, blk=blk, n_exp=n_exp, n_blocks=n_rows // blk)
    grid_spec = pltpu.PrefetchScalarGridSpec(
        num_scalar_prefetch=1,
        grid=(n_exp,),
        in_specs=[pl.BlockSpec(memory_space=pl.ANY),
                  pl.BlockSpec(memory_space=pl.ANY),
                  pl.BlockSpec((1, 1, f2), lambda e, r: (e, 0, 0)),
                  pl.BlockSpec(memory_space=pl.ANY),
                  pl.BlockSpec((1, 1, d), lambda e, r: (e, 0, 0))],
        out_specs=pl.BlockSpec(memory_space=pl.ANY),
        scratch_shapes=[pltpu.VMEM((2, d, f2), F32), pltpu.VMEM((2, f, d), F32),
                        pltpu.VMEM((d, f2), BF16), pltpu.VMEM((f, d), BF16),
                        pltpu.VMEM((2, blk, d), BF16), pltpu.VMEM((2, blk, d), BF16),
                        pltpu.SemaphoreType.DMA((2,)), pltpu.SemaphoreType.DMA((2,)),
                        pltpu.SemaphoreType.DMA((2,))],
    )
    return pl.pallas_call(
        kernel,
        out_shape=jax.ShapeDtypeStruct((n_rows, d), BF16),
        grid_spec=grid_spec,
        compiler_params=_params("arbitrary"),
        name="experts",
    )(region, xs, w1, b1, w2, b2)


def _combine_kernel(meta_ref, eo_ref, ld_ref, w_ref, x1_ref, mod_ref, o_ref, stage_ref, sems,
                    *, tm, n_exp, n_tiles, stage_rows):
    i = pl.program_id(0)
    slot = i & 1

    @pl.when(i == 0)
    def _():
        stage_ref[...] = jnp.zeros_like(stage_ref)
        _run_slabs(meta_ref, 0, n_exp, stage_ref.at[0], eo_ref, sems.at[0], to_global=False, wait=False)

    @pl.when(i + 1 < n_tiles)
    def _():
        _run_slabs(meta_ref, i + 1, n_exp, stage_ref.at[1 - slot], eo_ref, sems.at[1 - slot],
                   to_global=False, wait=False)

    ld = ld_ref[...]
    w = w_ref[...]
    ldb = [jnp.broadcast_to(ld[:, k:k + 1], (tm, SCATTER_RC)) for k in range(TOP_K)]
    wb = [jnp.broadcast_to(w[:, k:k + 1], (tm, SCATTER_RC)) for k in range(TOP_K)]
    col = lax.broadcasted_iota(I32, (tm, SCATTER_RC), 1)

    def sel_piece(r0):
        r = col + r0
        sel = jnp.where(ldb[0] == r, wb[0], jnp.where(ldb[1] == r, wb[1],
                        jnp.where(ldb[2] == r, wb[2], jnp.where(ldb[3] == r, wb[3], 0.0))))
        return sel.astype(BF16)

    _run_slabs(meta_ref, i, n_exp, stage_ref.at[slot], eo_ref, sems.at[slot], to_global=False, wait=True)
    stage = stage_ref.at[slot]
    y = None
    for c in range(stage_rows // COMBINE_KC):
        r0 = c * COMBINE_KC
        sel = jnp.concatenate([sel_piece(r0 + s) for s in range(0, COMBINE_KC, SCATTER_RC)], axis=1)
        part = jnp.dot(sel, stage[r0:r0 + COMBINE_KC, :], preferred_element_type=F32)
        y = part if y is None else y + part
    o_ref[...] = x1_ref[...] + mod_ref[0][5:6] * y


def _combine(meta, eo, ld_tok, w_tok, x1, mod3, seq, n_exp):
    t, d = x1.shape
    tm = TOK_TM
    n_tiles = t // tm
    stage_rows = _stage_rows(n_exp)
    kernel = functools.partial(_combine_kernel, tm=tm, n_exp=n_exp, n_tiles=n_tiles, stage_rows=stage_rows)
    grid_spec = pltpu.PrefetchScalarGridSpec(
        num_scalar_prefetch=1,
        grid=(n_tiles,),
        in_specs=[pl.BlockSpec(memory_space=pl.ANY),
                  pl.BlockSpec((tm, TOP_K), lambda i, m: (i, 0)),
                  pl.BlockSpec((tm, TOP_K), lambda i, m: (i, 0)),
                  pl.BlockSpec((tm, d), lambda i, m: (i, 0)),
                  pl.BlockSpec((1, 6, d), lambda i, m: (i * tm // seq, 0, 0))],
        out_specs=pl.BlockSpec((tm, d), lambda i, m: (i, 0)),
        scratch_shapes=[pltpu.VMEM((2, stage_rows, d), BF16),
                        pltpu.SemaphoreType.DMA((2,))],
    )
    return pl.pallas_call(
        kernel,
        out_shape=jax.ShapeDtypeStruct((t, d), F32),
        grid_spec=grid_spec,
        compiler_params=_params("arbitrary"),
        name="combine",
    )(meta, eo, ld_tok, w_tok, x1, mod3)


def _routing_tables(tile_cnt, n_tokens):
    n_tiles, n_exp = tile_cnt.shape
    blk = EXP_BLK
    run = _round_up(tile_cnt, RUN_ALIGN)
    local = jnp.cumsum(run, axis=1) - run
    size = jnp.sum(run, axis=0)
    region = _round_up(size, blk)
    region_end = jnp.cumsum(region)
    region_start = region_end - region
    glob = region_start[None, :] + jnp.cumsum(run, axis=0) - run
    sizes = jnp.asarray(SLAB_SIZES, I32)
    has = (run[:, :, None] & sizes) != 0
    before = run[:, :, None] & ~(2 * sizes - 1)
    place = jnp.cumsum(has, axis=1) - has
    hit = has[..., None] & (place[..., None] == jnp.arange(n_exp, dtype=I32))
    listed = lambda start: jnp.sum(
        jnp.where(hit, (start[:, :, None] + before)[..., None], 0), axis=1).reshape(n_tiles, -1)
    meta = jnp.concatenate([listed(local), listed(glob), jnp.sum(has, axis=1)],
                           axis=1).reshape(-1).astype(I32)
    tail = jnp.concatenate([region_start + size, region - size, region_end[-1:]]).astype(I32)
    n_rows = _round_up(n_tokens * TOP_K + n_tiles * n_exp * RUN_ALIGN + n_exp * blk, blk)
    region_blocks = jnp.concatenate([region_start // blk, region // blk]).astype(I32)
    return meta, tail, region_blocks, n_rows


def kernel(x, c, w_ada, b_ada, g_norm1, w_in, g_ret_gn, b_ret_gn, g_qnorm, g_knorm,
           lambda_q1, lambda_k1, lambda_q2, lambda_k2, g_diff_subln, w_out, g_norm2,
           w_router, b_router, w_expert_in, b_expert_in, w_expert_out, b_expert_out):
    batch, seq, d = x.shape
    depth = w_ada.shape[0]
    assert depth == 1
    t = batch * seq
    n_exp = w_router.shape[-1]
    l = 0

    mod = _adaln_mod(c, w_ada[l], b_ada[l][None, :])
    mod3 = mod.reshape(batch, 6, d)
    x2 = x.reshape(t, d)

    proj = _in_proj(x2, mod3, g_norm1[l][None, :], w_in[l].astype(BF16), seq)
    ret = _retention(proj, g_ret_gn[l][None, :], b_ret_gn[l][None, :], batch, seq, d)
    att = _diff_attention(proj, g_qnorm[l][None, :], g_knorm[l][None, :], g_diff_subln[l][None, :],
                          lambda_q1[l][None, :], lambda_k1[l][None, :],
                          lambda_q2[l][None, :], lambda_k2[l][None, :], batch, seq, d)

    wr = w_router[l]
    wr_hi = wr.astype(BF16)
    wr_lo = (wr - wr_hi.astype(F32)).astype(BF16)
    pad = lambda a: jnp.pad(a, ((0, 0), (0, LANES - n_exp)))
    wr_split = jnp.concatenate([pad(wr_hi), pad(wr_lo)], axis=1)
    br_pad = jnp.pad(b_router[l][None, :], ((0, 0), (0, LANES - n_exp)), constant_values=MASK_VALUE)
    x1, h2, top_w, ldest, tile_cnt = _out_router(
        x2, ret, att, mod3, w_out[l].astype(BF16), g_norm2[l][None, :], wr_split, br_pad, seq, n_exp)

    meta, tail, region_blocks, n_rows = _routing_tables(tile_cnt[:, :, 0].astype(I32), t)
    xs = _dispatch(meta, tail, ldest, h2, n_rows, n_exp)
    eo = _experts(region_blocks, xs, w_expert_in[l], b_expert_in[l][:, None, :],
                  w_expert_out[l], b_expert_out[l][:, None, :])
    out = _combine(meta, eo, ldest.T, top_w.T, x1, mod3, seq, n_exp)
    return out.reshape(batch, seq, d)
```

```python
import functools
import itertools
import math

import jax
import jax.numpy as jnp
from jax import lax
from jax.experimental import pallas as pl
from jax.experimental.pallas import tpu as pltpu

F32 = jnp.float32
BF16 = jnp.bfloat16
I32 = jnp.int32

EPS = 1e-6
LOG2E = 1.4426950216293335
MASK_VALUE = -1e30
LANES = 128
BF16_SUBLANES = 16
VMEM_LIMIT_BYTES = 48 * 1024 * 1024

RET_HEADS = 4
DIFF_HEADS = 8
TOP_K = 4
SWIGLU_LIMIT = 7.0
SWIGLU_ALPHA = 1.702
LAMBDA_INIT = 0.8 - 0.6 * math.exp(-0.3 * 0)

IN_TM, IN_TN = 1024, 2048
RET_TC = 256
RET_CHUNKS_PER_STEP = 2
ATT_T = 512
TOK_TM = 512
EXP_BLK = 512
RUN_ALIGN = BF16_SUBLANES
GATHER_RC = 256
COMBINE_KC = 512
W_CHUNKS = 8


def _round_up(n, m):
    return (n + m - 1) // m * m


def _stage_rows(n_exp):
    return _round_up(TOK_TM * TOP_K + n_exp * (RUN_ALIGN - 1), max(GATHER_RC, COMBINE_KC))


SLAB_SIZES = tuple(TOK_TM >> s for s in range(TOK_TM.bit_length()) if TOK_TM >> s >= RUN_ALIGN)
TAIL_SIZES = tuple(s for s in SLAB_SIZES if s < EXP_BLK)


def _params(*sem):
    return pltpu.CompilerParams(dimension_semantics=sem, vmem_limit_bytes=VMEM_LIMIT_BYTES)


def _sigmoid(x):
    return 0.5 * jnp.tanh(0.5 * x) + 0.5


def _rms(x, axis=-1):
    return x * lax.rsqrt(jnp.mean(x * x, axis=axis, keepdims=True) + EPS)


def _mod_kernel(c_ref, w_ref, b_ref, o_ref):
    c = c_ref[...]
    s = c * _sigmoid(c)
    o_ref[...] = jnp.dot(s, w_ref[...], preferred_element_type=F32,
                         precision=lax.Precision.HIGHEST) + b_ref[...]


def _adaln_mod(c, w, b):
    bn, d = c.shape
    n = w.shape[1]
    tn = d
    return pl.pallas_call(
        _mod_kernel,
        out_shape=jax.ShapeDtypeStruct((bn, n), F32),
        grid=(n // tn,),
        in_specs=[pl.BlockSpec((bn, d), lambda j: (0, 0)),
                  pl.BlockSpec((d, tn), lambda j: (0, j)),
                  pl.BlockSpec((1, tn), lambda j: (0, j))],
        out_specs=pl.BlockSpec((bn, tn), lambda j: (0, j)),
        compiler_params=_params("arbitrary"),
        name="adaln_mod",
    )(c, w, b)


def _inproj_kernel(x_ref, mod_ref, g_ref, w_ref, o_ref, h_ref):
    @pl.when(pl.program_id(1) == 0)
    def _():
        m = mod_ref[0]
        h = _rms(x_ref[...]) * g_ref[...] * (1.0 + m[1:2]) + m[0:1]
        h_ref[...] = h.astype(BF16)

    o_ref[...] = jnp.dot(h_ref[...], w_ref[...], preferred_element_type=F32).astype(o_ref.dtype)


def _in_proj(x2, mod3, g1, w_in_bf16, seq):
    t, d = x2.shape
    n = w_in_bf16.shape[1]
    tm, tn = IN_TM, IN_TN
    return pl.pallas_call(
        _inproj_kernel,
        out_shape=jax.ShapeDtypeStruct((t, n), BF16),
        grid=(t // tm, n // tn),
        in_specs=[pl.BlockSpec((tm, d), lambda i, j: (i, 0)),
                  pl.BlockSpec((1, 6, d), lambda i, j: (i * tm // seq, 0, 0)),
                  pl.BlockSpec((1, d), lambda i, j: (0, 0)),
                  pl.BlockSpec((d, tn), lambda i, j: (0, j))],
        out_specs=pl.BlockSpec((tm, tn), lambda i, j: (i, j)),
        scratch_shapes=[pltpu.VMEM((tm, d), BF16)],
        compiler_params=_params("arbitrary", "arbitrary"),
        name="in_proj",
    )(x2, mod3, g1, w_in_bf16)


def _ret_kernel(q_ref, k_ref, v_ref, rg_ref, mg_ref, gng_ref, gnb_ref, o_ref, r_ref, decay_ref,
                *, dk, dv, tc):
    scale = dk ** -0.5
    log_gs = [math.log(1.0 - 2.0 ** (-5.0 - h)) for h in range(RET_HEADS)]

    @pl.when(pl.program_id(1) == 0)
    def _():
        r_ref[...] = jnp.zeros_like(r_ref)
        row = lax.broadcasted_iota(I32, (tc, tc), 0)
        col = lax.broadcasted_iota(I32, (tc, tc), 1)
        rel = (row - col).astype(F32)
        for h, log_g in enumerate(log_gs):
            decay_ref[h] = jnp.where(rel >= 0, jnp.exp(log_g * jnp.maximum(rel, 0.0)), 0.0) * scale

    pos = lax.broadcasted_iota(I32, (tc, 1), 0).astype(F32)
    for sub, (h, log_g) in itertools.product(range(RET_CHUNKS_PER_STEP), enumerate(log_gs)):
        rows = slice(sub * tc, (sub + 1) * tc)
        decay = decay_ref[h]
        xi = jnp.exp(log_g * (pos + 1.0))
        zeta = jnp.exp(log_g * (tc - 1.0 - pos))
        g_chunk = math.exp(log_g * tc)
        q = q_ref[rows, h * dk:(h + 1) * dk]
        k = k_ref[rows, h * dk:(h + 1) * dk]
        v = v_ref[rows, h * dv:(h + 1) * dv]
        s = lax.dot_general(q, k, (((1,), (1,)), ((), ())), preferred_element_type=F32)
        y = jnp.dot((s * decay).astype(BF16), v, preferred_element_type=F32)
        state = r_ref[h]
        y = y + jnp.dot(q, state.astype(BF16), preferred_element_type=F32) * xi
        vz = (v.astype(F32) * zeta).astype(BF16)
        kv = lax.dot_general(k, vz, (((0,), (0,)), ((), ())), preferred_element_type=F32)
        r_ref[h] = state * g_chunk + kv * scale

        mu = jnp.mean(y, axis=-1, keepdims=True)
        yc = y - mu
        var = jnp.mean(yc * yc, axis=-1, keepdims=True)
        sl = slice(h * dv, (h + 1) * dv)
        yn = yc * lax.rsqrt(var + EPS) * gng_ref[:, sl] + gnb_ref[:, sl]
        rg = rg_ref[rows, sl].astype(F32)
        gate = _sigmoid(mg_ref[rows, sl].astype(F32))
        o_ref[rows, sl] = (rg * _sigmoid(rg) * yn * gate).astype(o_ref.dtype)


def _retention(proj, gn_g, gn_b, batch, seq, d):
    t = proj.shape[0]
    tc = RET_TC
    tr = tc * RET_CHUNKS_PER_STEP
    nc = seq // tr
    dk = d // (2 * RET_HEADS)
    dv = d // RET_HEADS
    qk_w = RET_HEADS * dk
    row = lambda b, c: b * nc + c
    kernel = functools.partial(_ret_kernel, dk=dk, dv=dv, tc=tc)
    return pl.pallas_call(
        kernel,
        out_shape=jax.ShapeDtypeStruct((t, d), BF16),
        grid=(batch, nc),
        in_specs=[pl.BlockSpec((tr, qk_w), lambda b, c: (row(b, c), 0)),
                  pl.BlockSpec((tr, qk_w), lambda b, c: (row(b, c), 1)),
                  pl.BlockSpec((tr, d), lambda b, c: (row(b, c), 1)),
                  pl.BlockSpec((tr, d), lambda b, c: (row(b, c), 2)),
                  pl.BlockSpec((tr, d), lambda b, c: (row(b, c), 6)),
                  pl.BlockSpec((1, d), lambda b, c: (0, 0)),
                  pl.BlockSpec((1, d), lambda b, c: (0, 0))],
        out_specs=pl.BlockSpec((tr, d), lambda b, c: (row(b, c), 0)),
        scratch_shapes=[pltpu.VMEM((RET_HEADS, dk, dv), F32), pltpu.VMEM((RET_HEADS, tc, tc), F32)],
        compiler_params=_params("arbitrary", "arbitrary"),
        name="retention",
    )(proj, proj, proj, proj, proj, gn_g, gn_b)


def _halves_rms(x, lo, dh):
    sq = x * x
    s_lo = jnp.sum(jnp.where(lo, sq, 0.0), axis=-1, keepdims=True)
    s_hi = jnp.sum(jnp.where(lo, 0.0, sq), axis=-1, keepdims=True)
    inv = jnp.where(lo, lax.rsqrt(s_lo / dh + EPS), lax.rsqrt(s_hi / dh + EPS))
    return x * inv


def _attn_kernel(q_ref, k_ref, v_ref, gate_ref, slope_ref, qca_ref, qcb_ref, gq_ref, gk_ref, gsub_ref,
                 lq1_ref, lk1_ref, lq2_ref, lk2_ref,
                 o_ref, ka_ref, kb_ref, vt_ref, qa_ref, qb_ref, s0_ref, sa_ref, sb_ref, acc_ref, m_ref,
                 *, tile, dh, dv, nk):
    lane = lax.broadcasted_iota(I32, (tile, 2 * dh), 1)
    lo = lane < dh

    slope = slope_ref[0]
    sub = lax.broadcasted_iota(I32, (tile, 2 * dh), 0)
    off = lane & (dh - 1)
    hi_lane = (off < 6) & ((off & 1) == 0)
    lo_lane = (off < 6) & ((off & 1) == 1)
    bias0 = jnp.where(hi_lane, ((sub >> 8) << 8).astype(F32),
                      jnp.where(lo_lane, (sub & 255).astype(F32), 0.0)) * slope
    bias_step = jnp.where(hi_lane, float(tile), 0.0) * slope
    ones_rows = jnp.where(lax.broadcasted_iota(I32, (BF16_SUBLANES, tile), 0) == 0, 1.0, 0.0)
    lo_t = lax.broadcasted_iota(I32, (2 * dh, tile), 0) < dh

    def prepare(c, carry):
        r0 = c * tile if isinstance(c, int) else pl.multiple_of(c * tile, tile)
        kn = _halves_rms(k_ref[pl.ds(r0, tile), :].astype(F32), lo, dh) * gk_ref[...]
        bias = bias0 + lax.convert_element_type(c, F32) * bias_step
        ka_ref[c] = jnp.where(lo, kn, bias).astype(BF16)
        kb_ref[c] = jnp.where(lo, bias, kn).astype(BF16)
        vt_ref[c, :dv, :] = v_ref[pl.ds(r0, tile), :].astype(F32).T.astype(BF16)
        vt_ref[c, dv:, :] = ones_rows.astype(BF16)
        qn = (_halves_rms(q_ref[pl.ds(r0, tile), :].astype(F32), lo, dh) * gq_ref[...]
              * (dh ** -0.5 * LOG2E))
        qn_t = qn.T
        qa_ref[c] = jnp.where(lo_t, qn_t, qca_ref[...]).astype(BF16)
        qb_ref[c] = jnp.where(lo_t, qcb_ref[...], qn_t).astype(BF16)
        return carry

    lax.fori_loop(0, min(2, nk), prepare, 0)
    lam = (jnp.exp(jnp.sum(lq1_ref[...] * lk1_ref[...], axis=-1, keepdims=True))
           - jnp.exp(jnp.sum(lq2_ref[...] * lk2_ref[...], axis=-1, keepdims=True)) + LAMBDA_INIT)

    refs = (gate_ref, gsub_ref, o_ref, ka_ref, kb_ref, vt_ref, qa_ref, qb_ref,
            s0_ref, sa_ref, sb_ref, acc_ref, m_ref)
    _attn_scores(ka_ref, kb_ref, qa_ref, qb_ref, 0, 0, s0_ref, tile, diagonal=True)
    n_peeled = min(3, nk)
    for qi in range(n_peeled):
        _attn_q_tile(qi, lam, prepare, *refs, tile=tile, dv=dv, nk=nk, peeled=True)

    def q_tile(qi, carry):
        _attn_q_tile(jnp.asarray(qi, I32), lam, prepare, *refs, tile=tile, dv=dv, nk=nk, peeled=False)
        return carry

    lax.fori_loop(n_peeled, nk, q_tile, 0)


def _attn_scores(ka_ref, kb_ref, qa_ref, qb_ref, jb, q, s_ref, tile, diagonal=False):
    h = tile // 2
    for idx, (k_ref, q_ref) in enumerate(((ka_ref, qa_ref), (kb_ref, qb_ref))):
        if diagonal:
            s_ref[idx, :h, :] = jnp.dot(k_ref[jb, :h, :], q_ref[q], preferred_element_type=F32)
            s_ref[idx, h:tile, h:] = jnp.dot(k_ref[jb, h:, :], q_ref[q, :, h:], preferred_element_type=F32)
        else:
            s = jnp.dot(k_ref[jb], q_ref[q], preferred_element_type=F32)
            s_ref[idx, :tile, :] = s
            s_ref[idx, tile:tile + 1, :] = jnp.max(s, axis=0, keepdims=True)


def _attn_q_tile(qi, lam, prepare, gate_ref, gsub_ref, o_ref, ka_ref, kb_ref, vt_ref, qa_ref, qb_ref,
                 s0_ref, sa_ref, sb_ref, acc_ref, m_ref, *, tile, dv, nk, peeled):
    def scores(jb, s_ref, q=qi, diagonal=False):
        _attn_scores(ka_ref, kb_ref, qa_ref, qb_ref, jb, q, s_ref, tile, diagonal)

    def absorb(jb, s_ref, masked):
        vt = vt_ref[jb]
        if masked:
            return absorb_diagonal(s_ref, vt)
        for idx in range(2):
            m_old = m_ref[idx]
            m_new = jnp.maximum(m_old, s_ref[idx, tile:tile + 1, :])
            m_ref[idx] = m_new
            p = jnp.exp2(s_ref[idx, :tile, :] - m_new).astype(BF16)
            acc_ref[idx] = (acc_ref[idx] * jnp.exp2(m_old - m_new)
                            + jnp.dot(vt, p, preferred_element_type=F32))

    def absorb_diagonal(s_ref, vt):
        h = tile // 2
        future = lax.broadcasted_iota(I32, (h, h), 0) > lax.broadcasted_iota(I32, (h, h), 1)
        for idx in range(2):
            early = s_ref[idx, :h, :]
            early = jnp.concatenate([jnp.where(future, MASK_VALUE, early[:, :h]), early[:, h:]], axis=1)
            late = jnp.where(future, MASK_VALUE, s_ref[idx, h:tile, h:])
            m_old = m_ref[idx]
            m_new = jnp.maximum(m_old, jnp.max(early, axis=0, keepdims=True))
            m_new = jnp.concatenate(
                [m_new[:, :h], jnp.maximum(m_new[:, h:], jnp.max(late, axis=0, keepdims=True))], axis=1)
            m_ref[idx] = m_new
            p_early = jnp.exp2(early - m_new).astype(BF16)
            p_late = jnp.exp2(late - m_new[:, h:]).astype(BF16)
            acc_ref[idx] = (acc_ref[idx] * jnp.exp2(m_old - m_new)
                            + jnp.dot(vt[:, :h], p_early, preferred_element_type=F32))
            acc_ref[idx, :, h:] = acc_ref[idx, :, h:] + jnp.dot(vt[:, h:], p_late,
                                                                preferred_element_type=F32)

    ahead = (lambda n: min(qi + n, nk - 1)) if peeled else (lambda n: jnp.minimum(qi + n, nk - 1))

    def diagonal_block_and_output(s_ref):
        if s_ref is not s0_ref:
            scores(0, s0_ref, q=ahead(1))
        absorb(qi, s_ref, True)
        if s_ref is s0_ref:
            scores(0, s0_ref, q=ahead(1))
        prepare(ahead(2), 0)
        a1, a2 = acc_ref[0], acc_ref[1]
        o_t = a1[:dv] * (1.0 / a1[dv:dv + 1]) - lam * (a2[:dv] * (1.0 / a2[dv:dv + 1]))
        o = _rms(o_t, axis=0).T * gsub_ref[...] * (1.0 - LAMBDA_INIT)
        rows = pl.ds(pl.multiple_of(qi * tile, tile), tile)
        o_ref[rows, :] = (o * _sigmoid(gate_ref[rows, :].astype(F32))).astype(o_ref.dtype)

    acc_ref[...] = jnp.zeros_like(acc_ref)
    m_ref[...] = jnp.full_like(m_ref, MASK_VALUE)
    if peeled and qi == 0:
        diagonal_block_and_output(s0_ref)
        return
    scores(1, sa_ref, diagonal=peeled and qi == 1)
    absorb(0, s0_ref, False)

    def pair(j2, carry):
        jb = 2 * j2 + 1
        scores(jb + 1, sb_ref)
        absorb(jb, sa_ref, False)
        scores(jb + 2, sa_ref)
        absorb(jb + 1, sb_ref, False)
        return carry

    def even_tail():
        scores(qi, sb_ref, diagonal=True)
        absorb(qi - 1, sa_ref, False)
        diagonal_block_and_output(sb_ref)

    def odd_tail():
        diagonal_block_and_output(sa_ref)

    if peeled:
        assert qi <= 2
        (even_tail if qi % 2 == 0 else odd_tail)()
    else:
        lax.fori_loop(0, (qi - 1) // 2, pair, 0)
        pl.when(qi % 2 == 0)(even_tail)
        pl.when(qi % 2 == 1)(odd_tail)


def _bf16_terms(x, n):
    terms, rest = [], jnp.asarray(x, F32)
    for _ in range(n):
        term = rest.astype(BF16).astype(F32)
        terms.append(term)
        rest = rest - term
    return terms


def _diff_attention(proj, g_q, g_k, g_sub, lq1, lk1, lq2, lk2, batch, seq, d):
    t = proj.shape[0]
    tile = ATT_T
    nq = seq // tile
    dh = d // (2 * DIFF_HEADS)
    w = 2 * dh
    assert w == LANES
    q_col, k_col, v_col, gate_col = 3 * d // w, 4 * d // w, 5 * d // w, 7 * d // w
    slopes = 2.0 ** -(jnp.arange(DIFF_HEADS, dtype=F32) + 1.0)
    slopes = jnp.broadcast_to(slopes[:, None, None], (DIFF_HEADS, 1, LANES))
    c1, c2, c3 = _bf16_terms(LOG2E, 3)
    qconst = jnp.zeros((dh,), F32).at[:6].set(jnp.stack([c1, c1, c2, c2, c3, c3]))
    zeros = jnp.zeros((dh,), F32)
    qconst_a = jnp.concatenate([zeros, qconst])[:, None]
    qconst_b = jnp.concatenate([qconst, zeros])[:, None]
    tile2 = lambda a: jnp.concatenate([a, a], axis=-1)
    assert tile % 256 == 0
    small = lambda n: pl.BlockSpec((1, n), lambda b, h: (0, 0))
    kernel = functools.partial(_attn_kernel, tile=tile, dh=dh, dv=w, nk=nq)
    return pl.pallas_call(
        kernel,
        out_shape=jax.ShapeDtypeStruct((t, d), BF16),
        grid=(batch, DIFF_HEADS),
        in_specs=[pl.BlockSpec((seq, w), lambda b, h: (b, q_col + h)),
                  pl.BlockSpec((seq, w), lambda b, h: (b, k_col + h)),
                  pl.BlockSpec((seq, w), lambda b, h: (b, v_col + h)),
                  pl.BlockSpec((seq, w), lambda b, h: (b, gate_col + h)),
                  pl.BlockSpec((1, 1, LANES), lambda b, h: (h, 0, 0)),
                  pl.BlockSpec((w, 1), lambda b, h: (0, 0)), pl.BlockSpec((w, 1), lambda b, h: (0, 0)),
                  small(w), small(w), small(w),
                  small(dh), small(dh), small(dh), small(dh)],
        out_specs=pl.BlockSpec((seq, w), lambda b, h: (b, h)),
        scratch_shapes=[pltpu.VMEM((nq, tile, w), BF16),
                        pltpu.VMEM((nq, tile, w), BF16),
                        pltpu.VMEM((nq, w + BF16_SUBLANES, tile), BF16),
                        pltpu.VMEM((nq, w, tile), BF16),
                        pltpu.VMEM((nq, w, tile), BF16),
                        pltpu.VMEM((2, tile + 8, tile), F32),
                        pltpu.VMEM((2, tile + 8, tile), F32),
                        pltpu.VMEM((2, tile + 8, tile), F32),
                        pltpu.VMEM((2, w + BF16_SUBLANES, tile), F32),
                        pltpu.VMEM((2, 1, tile), F32)],
        compiler_params=_params("arbitrary", "arbitrary"),
        name="diff_attn",
    )(proj, proj, proj, proj, slopes, qconst_a, qconst_b, tile2(g_q), tile2(g_k), g_sub,
      lq1, lk1, lq2, lk2)


def _out_kernel(x_ref, ret_ref, att_ref, mod_ref, wo_ref, g2_ref, wr_ref, br_ref,
                x1_ref, h2_ref, tw_ref, ld_ref, cnt_ref, *, tm, n_exp):
    m = mod_ref[0]
    merged = (ret_ref[...].astype(F32) + att_ref[...].astype(F32)).astype(BF16)
    x1 = x_ref[...] + m[2:3] * jnp.dot(merged, wo_ref[...], preferred_element_type=F32)
    x1_ref[...] = x1
    h2 = _rms(x1) * g2_ref[...] * (1.0 + m[4:5]) + m[3:4]
    hi = h2.astype(BF16)
    h2_ref[...] = hi

    lo = (h2 - hi.astype(F32)).astype(BF16)
    both = jnp.dot(hi, wr_ref[...], preferred_element_type=F32)
    logits = (both[:, :LANES] + both[:, LANES:]
              + jnp.dot(lo, wr_ref[:, :LANES], preferred_element_type=F32) + br_ref[...])
    lt = logits.T[:n_exp]

    erow = lax.broadcasted_iota(I32, (n_exp, tm), 0)
    vals, sels = [], []
    for k in range(TOP_K):
        mx = jnp.max(lt, axis=0, keepdims=True)
        idx = jnp.min(jnp.where(lt == mx, erow, n_exp), axis=0, keepdims=True)
        sel = erow == idx
        vals.append(mx)
        sels.append(sel)
        lt = jnp.where(sel, MASK_VALUE, lt)
    exps = [jnp.exp(v - vals[0]) for v in vals]
    inv = 1.0 / (exps[0] + exps[1] + exps[2] + exps[3])
    for k in range(TOP_K):
        tw_ref[k:k + 1, :] = exps[k] * inv

    chosen = jnp.where(sels[0] | sels[1] | sels[2] | sels[3], 1.0, 0.0)
    before = (lax.broadcasted_iota(I32, (tm, tm), 0) < lax.broadcasted_iota(I32, (tm, tm), 1))
    prefix = jnp.dot(chosen.astype(BF16), jnp.where(before, 1.0, 0.0).astype(BF16),
                     preferred_element_type=F32)
    count = jnp.sum(chosen, axis=1, keepdims=True)
    cnt_ref[...] = count
    padded = jnp.ceil(count * (1.0 / RUN_ALIGN)) * RUN_ALIGN
    below = (lax.broadcasted_iota(I32, (n_exp, n_exp), 1) < lax.broadcasted_iota(I32, (n_exp, n_exp), 0))
    run_start = jnp.dot(jnp.where(below, 1.0, 0.0).astype(BF16),
                        jnp.broadcast_to(padded, (n_exp, LANES)).astype(BF16),
                        preferred_element_type=F32)[:, :1]
    pos = prefix + run_start
    for k in range(TOP_K):
        ld_ref[k:k + 1, :] = jnp.sum(jnp.where(sels[k], pos, 0.0), axis=0, keepdims=True).astype(I32)


def _out_router(x2, ret, att, mod3, w_out_bf16, g2, wr_split, br_pad, seq, n_exp):
    t, d = x2.shape
    tm = TOK_TM
    row = lambda i: (i, 0)
    const = lambda i: (0, 0)
    kernel = functools.partial(_out_kernel, tm=tm, n_exp=n_exp)
    return pl.pallas_call(
        kernel,
        out_shape=(jax.ShapeDtypeStruct((t, d), F32),
                   jax.ShapeDtypeStruct((t, d), BF16),
                   jax.ShapeDtypeStruct((TOP_K, t), F32),
                   jax.ShapeDtypeStruct((TOP_K, t), I32),
                   jax.ShapeDtypeStruct((t // tm, n_exp, 1), F32)),
        grid=(t // tm,),
        in_specs=[pl.BlockSpec((tm, d), row), pl.BlockSpec((tm, d), row), pl.BlockSpec((tm, d), row),
                  pl.BlockSpec((1, 6, d), lambda i: (i * tm // seq, 0, 0)),
                  pl.BlockSpec((d, d), const), pl.BlockSpec((1, d), const),
                  pl.BlockSpec((d, 2 * LANES), const), pl.BlockSpec((1, LANES), const)],
        out_specs=(pl.BlockSpec((tm, d), row), pl.BlockSpec((tm, d), row),
                   pl.BlockSpec((TOP_K, tm), lambda i: (0, i)),
                   pl.BlockSpec((TOP_K, tm), lambda i: (0, i)),
                   pl.BlockSpec((None, n_exp, 1), lambda i: (i, 0, 0))),
        compiler_params=_params("arbitrary"),
        name="out_router",
    )(x2, ret, att, mod3, w_out_bf16, g2, wr_split, br_pad)


def _meta_stride(n_exp):
    return 2 * len(SLAB_SIZES) * n_exp + len(SLAB_SIZES)


def _run_slabs(meta_ref, tile, n_exp, local_ref, global_ref, sem, *, to_global, wait):
    n_sizes = len(SLAB_SIZES)
    base = tile * _meta_stride(n_exp)
    for c, size in enumerate(SLAB_SIZES):
        def one(p, carry, c=c, size=size):
            if wait:
                local = glob = 0
            else:
                local = pl.multiple_of(meta_ref[base + c * n_exp + p], RUN_ALIGN)
                glob = pl.multiple_of(meta_ref[base + (n_sizes + c) * n_exp + p], RUN_ALIGN)
            loc = local_ref.at[pl.ds(local, size)]
            glo = global_ref.at[pl.ds(glob, size)]
            cp = pltpu.make_async_copy(loc, glo, sem) if to_global else pltpu.make_async_copy(glo, loc, sem)
            if wait:
                cp.wait()
            else:
                cp.start()
            return carry

        lax.fori_loop(0, meta_ref[base + 2 * n_sizes * n_exp + c], one, 0)


def _dispatch_kernel(meta_ref, tail_ref, ld_ref, h_ref, xs_ref, stage_ref, zero_ref, sems, tail_sem,
                     *, tm, n_exp, n_tiles, n_blocks, stage_rows):
    i = pl.program_id(0)
    slot = i & 1
    stage = stage_ref.at[slot]

    def tails(wait):
        def unused_block(b, carry):
            cp = pltpu.make_async_copy(
                zero_ref, xs_ref.at[pl.ds(pl.multiple_of(b * EXP_BLK, EXP_BLK), EXP_BLK)], tail_sem)
            if wait:
                cp.wait()
            else:
                cp.start()
            return carry

        lax.fori_loop(tail_ref[2 * n_exp] // EXP_BLK, n_blocks, unused_block, 0)

        def per_expert(e, carry):
            start, n = tail_ref[e], tail_ref[n_exp + e]
            off = 0
            for size in TAIL_SIZES:
                take = n & size

                @pl.when(take != 0)
                def _(off=off, size=size):
                    cp = pltpu.make_async_copy(
                        zero_ref.at[pl.ds(0, size)],
                        xs_ref.at[pl.ds(pl.multiple_of(start + off, RUN_ALIGN), size)], tail_sem)
                    if wait:
                        cp.wait()
                    else:
                        cp.start()

                off = off + take
            return carry

        lax.fori_loop(0, n_exp, per_expert, 0)

    @pl.when(i == 0)
    def _():
        zero_ref[...] = jnp.zeros_like(zero_ref)
        tails(False)
        tails(True)

    ld = ld_ref[...]
    h = h_ref[...]
    for c in range(stage_rows // GATHER_RC):
        r = lax.broadcasted_iota(I32, (GATHER_RC, tm), 0) + c * GATHER_RC
        hit = jnp.where(r == ld[0:1], 1.0, jnp.where(r == ld[1:2], 1.0,
                        jnp.where(r == ld[2:3], 1.0, jnp.where(r == ld[3:4], 1.0, 0.0))))
        stage[c * GATHER_RC:(c + 1) * GATHER_RC, :] = jnp.dot(
            hit.astype(BF16), h, preferred_element_type=F32).astype(BF16)

    _run_slabs(meta_ref, i, n_exp, stage, xs_ref, sems.at[slot], to_global=True, wait=False)

    @pl.when(i > 0)
    def _():
        _run_slabs(meta_ref, i - 1, n_exp, stage_ref.at[1 - slot], xs_ref, sems.at[1 - slot],
                   to_global=True, wait=True)

    @pl.when(i == n_tiles - 1)
    def _():
        _run_slabs(meta_ref, i, n_exp, stage, xs_ref, sems.at[slot], to_global=True, wait=True)


def _dispatch(meta, tail, ldest, h2, n_rows, n_exp):
    t, d = h2.shape
    tm = TOK_TM
    n_tiles = t // tm
    stage_rows = _stage_rows(n_exp)
    kernel = functools.partial(_dispatch_kernel, tm=tm, n_exp=n_exp, n_tiles=n_tiles,
                               n_blocks=n_rows // EXP_BLK, stage_rows=stage_rows)
    grid_spec = pltpu.PrefetchScalarGridSpec(
        num_scalar_prefetch=2,
        grid=(n_tiles,),
        in_specs=[pl.BlockSpec((TOP_K, tm), lambda i, m, tl: (0, i)),
                  pl.BlockSpec((tm, d), lambda i, m, tl: (i, 0))],
        out_specs=pl.BlockSpec(memory_space=pl.ANY),
        scratch_shapes=[pltpu.VMEM((2, stage_rows, d), BF16),
                        pltpu.VMEM((EXP_BLK, d), BF16),
                        pltpu.SemaphoreType.DMA((2,)),
                        pltpu.SemaphoreType.DMA],
    )
    return pl.pallas_call(
        kernel,
        out_shape=jax.ShapeDtypeStruct((n_rows, d), BF16),
        grid_spec=grid_spec,
        compiler_params=_params("arbitrary"),
        name="dispatch",
    )(meta, tail, ldest, h2)


def _expert_kernel(region_ref, xs_ref, w1_ref, b1_ref, w2_ref, b2_ref, eo_ref,
                   w1f_ref, w2f_ref, w1b_ref, w2b_ref, x_buf, o_buf, w_sems, x_sems, o_sems,
                   *, f, blk, n_exp, n_blocks):
    e = pl.program_id(0)
    first, count = region_ref[e], region_ref[n_exp + e]
    w_slot = e & 1
    d = w1f_ref.shape[1]
    r1, r2 = d // W_CHUNKS, f // W_CHUNKS

    def rows(j):
        return pl.ds(pl.multiple_of((first + j) * blk, blk), blk)

    def x_copy(j, slot):
        return pltpu.make_async_copy(xs_ref.at[rows(j)], x_buf.at[slot], x_sems.at[slot])

    def o_copy(j, slot):
        return pltpu.make_async_copy(o_buf.at[slot], eo_ref.at[rows(j)], o_sems.at[slot])

    def w_chunk(expert, c, slot, wait):
        for src, dst, r in ((w1_ref, w1f_ref, r1), (w2_ref, w2f_ref, r2)):
            piece = pl.ds(c * r if isinstance(c, int) else pl.multiple_of(c * r, r), r)
            cp = pltpu.make_async_copy(src.at[expert, piece], dst.at[slot, piece], w_sems.at[slot])
            if wait:
                cp.wait()
            else:
                cp.start()

    @pl.when(e == 0)
    def _():
        for c in range(W_CHUNKS):
            w_chunk(0, c, 0, False)

    @pl.when(count > 0)
    def _():
        x_copy(0, 0).start()

    for c in range(W_CHUNKS):
        w_chunk(e, c, w_slot, True)

    @pl.when(count > 0)
    def _():
        w1b_ref[...] = w1f_ref[w_slot].astype(BF16)
        w2b_ref[...] = w2f_ref[w_slot].astype(BF16)

    def next_chunk(c):
        @pl.when(e + 1 < n_exp)
        def _():
            w_chunk(e + 1, c, 1 - w_slot, False)

    def block(j, carry):
        slot = j & 1
        x_copy(j, slot).wait()

        @pl.when(j + 1 < count)
        def _():
            x_copy(j + 1, 1 - slot).start()

        @pl.when(j < W_CHUNKS)
        def _():
            next_chunk(j)

        @pl.when(j >= 2)
        def _():
            o_copy(j - 2, slot).wait()

        gu = jnp.dot(x_buf[slot], w1b_ref[...], preferred_element_type=F32) + b1_ref[0]
        g = jnp.minimum(gu[:, :f], SWIGLU_LIMIT)
        u = jnp.clip(gu[:, f:], -SWIGLU_LIMIT, SWIGLU_LIMIT)
        a = g * _sigmoid(SWIGLU_ALPHA * g) * (u + 1.0)
        o_buf[slot] = (jnp.dot(a.astype(BF16), w2b_ref[...], preferred_element_type=F32)
                       + b2_ref[0]).astype(o_buf.dtype)
        o_copy(j, slot).start()
        return carry

    lax.fori_loop(0, count, block, 0)
    lax.fori_loop(jnp.minimum(count, W_CHUNKS), W_CHUNKS, lambda c, carry: (next_chunk(c), carry)[1], 0)

    @pl.when(count >= 2)
    def _():
        o_copy(count - 2, count & 1).wait()

    @pl.when(count >= 1)
    def _():
        o_copy(count - 1, (count - 1) & 1).wait()

    @pl.when(e == n_exp - 1)
    def _():
        o_buf[0] = jnp.zeros(o_buf.shape[1:], o_buf.dtype)
        used = first + count

        def fill(j, wait):
            cp = pltpu.make_async_copy(
                o_buf.at[0], eo_ref.at[pl.ds(pl.multiple_of(j * blk, blk), blk)], o_sems.at[0])
            if wait:
                cp.wait()
            else:
                cp.start()

        lax.fori_loop(used, n_blocks, lambda j, c: (fill(j, False), c)[1], 0)
        lax.fori_loop(used, n_blocks, lambda j, c: (fill(j, True), c)[1], 0)


def _experts(region, xs, w1, b1, w2, b2):
    n_rows, d = xs.shape
    n_exp, _, f2 = w1.shape
    f = f2 // 2
    blk = EXP_BLK
    kernel = functools.partial(_expert_kernel, f=f, blk=blk, n_exp=n_exp, n_blocks=n_rows // blk)
    grid_spec = pltpu.PrefetchScalarGridSpec(
        num_scalar_prefetch=1,
        grid=(n_exp,),
        in_specs=[pl.BlockSpec(memory_space=pl.ANY),
                  pl.BlockSpec(memory_space=pl.ANY),
                  pl.BlockSpec((1, 1, f2), lambda e, r: (e, 0, 0)),
                  pl.BlockSpec(memory_space=pl.ANY),
                  pl.BlockSpec((1, 1, d), lambda e, r: (e, 0, 0))],
        out_specs=pl.BlockSpec(memory_space=pl.ANY),
        scratch_shapes=[pltpu.VMEM((2, d, f2), F32), pltpu.VMEM((2, f, d), F32),
                        pltpu.VMEM((d, f2), BF16), pltpu.VMEM((f, d), BF16),
                        pltpu.VMEM((2, blk, d), BF16), pltpu.VMEM((2, blk, d), BF16),
                        pltpu.SemaphoreType.DMA((2,)), pltpu.SemaphoreType.DMA((2,)),
                        pltpu.SemaphoreType.DMA((2,))],
    )
    return pl.pallas_call(
        kernel,
        out_shape=jax.ShapeDtypeStruct((n_rows, d), BF16),
        grid_spec=grid_spec,
        compiler_params=_params("arbitrary"),
        name="experts",
    )(region, xs, w1, b1, w2, b2)


def _combine_kernel(meta_ref, eo_ref, ld_ref, w_ref, x1_ref, mod_ref, o_ref, stage_ref, sems,
                    *, tm, n_exp, n_tiles, stage_rows):
    i = pl.program_id(0)
    slot = i & 1

    @pl.when(i == 0)
    def _():
        stage_ref[...] = jnp.zeros_like(stage_ref)
        _run_slabs(meta_ref, 0, n_exp, stage_ref.at[0], eo_ref, sems.at[0], to_global=False, wait=False)

    @pl.when(i + 1 < n_tiles)
    def _():
        _run_slabs(meta_ref, i + 1, n_exp, stage_ref.at[1 - slot], eo_ref, sems.at[1 - slot],
                   to_global=False, wait=False)

    ld = ld_ref[...]
    w = w_ref[...]

    _run_slabs(meta_ref, i, n_exp, stage_ref.at[slot], eo_ref, sems.at[slot], to_global=False, wait=True)
    stage = stage_ref.at[slot]
    y = None
    for c in range(stage_rows // COMBINE_KC):
        r0 = c * COMBINE_KC
        r = lax.broadcasted_iota(I32, (COMBINE_KC, tm), 0) + r0
        sel = jnp.where(r == ld[0:1], w[0:1], jnp.where(r == ld[1:2], w[1:2],
                        jnp.where(r == ld[2:3], w[2:3], jnp.where(r == ld[3:4], w[3:4], 0.0))))
        part = lax.dot_general(sel.astype(BF16), stage[r0:r0 + COMBINE_KC, :], (((0,), (0,)), ((), ())),
                               preferred_element_type=F32)
        y = part if y is None else y + part
    o_ref[...] = x1_ref[...] + mod_ref[0][5:6] * y


def _combine(meta, eo, ld_tok, w_tok, x1, mod3, seq, n_exp):
    t, d = x1.shape
    tm = TOK_TM
    n_tiles = t // tm
    stage_rows = _stage_rows(n_exp)
    kernel = functools.partial(_combine_kernel, tm=tm, n_exp=n_exp, n_tiles=n_tiles, stage_rows=stage_rows)
    grid_spec = pltpu.PrefetchScalarGridSpec(
        num_scalar_prefetch=1,
        grid=(n_tiles,),
        in_specs=[pl.BlockSpec(memory_space=pl.ANY),
                  pl.BlockSpec((TOP_K, tm), lambda i, m: (0, i)),
                  pl.BlockSpec((TOP_K, tm), lambda i, m: (0, i)),
                  pl.BlockSpec((tm, d), lambda i, m: (i, 0)),
                  pl.BlockSpec((1, 6, d), lambda i, m: (i * tm // seq, 0, 0))],
        out_specs=pl.BlockSpec((tm, d), lambda i, m: (i, 0)),
        scratch_shapes=[pltpu.VMEM((2, stage_rows, d), BF16),
                        pltpu.SemaphoreType.DMA((2,))],
    )
    return pl.pallas_call(
        kernel,
        out_shape=jax.ShapeDtypeStruct((t, d), F32),
        grid_spec=grid_spec,
        compiler_params=_params("arbitrary"),
        name="combine",
    )(meta, eo, ld_tok, w_tok, x1, mod3)


def _routing_tables(tile_cnt, n_tokens):
    n_tiles, n_exp = tile_cnt.shape
    blk = EXP_BLK
    run = _round_up(tile_cnt, RUN_ALIGN)
    local = jnp.cumsum(run, axis=1) - run
    size = jnp.sum(run, axis=0)
    region = _round_up(size, blk)
    region_end = jnp.cumsum(region)
    region_start = region_end - region
    glob = region_start[None, :] + jnp.cumsum(run, axis=0) - run
    sizes = jnp.asarray(SLAB_SIZES, I32)
    has = (run[:, :, None] & sizes) != 0
    before = run[:, :, None] & ~(2 * sizes - 1)
    place = jnp.cumsum(has, axis=1) - has
    hit = has[..., None] & (place[..., None] == jnp.arange(n_exp, dtype=I32))
    listed = lambda start: jnp.sum(
        jnp.where(hit, (start[:, :, None] + before)[..., None], 0), axis=1).reshape(n_tiles, -1)
    meta = jnp.concatenate([listed(local), listed(glob), jnp.sum(has, axis=1)],
                           axis=1).reshape(-1).astype(I32)
    tail = jnp.concatenate([region_start + size, region - size, region_end[-1:]]).astype(I32)
    n_rows = _round_up(n_tokens * TOP_K + n_tiles * n_exp * RUN_ALIGN + n_exp * blk, blk)
    region_blocks = jnp.concatenate([region_start // blk, region // blk]).astype(I32)
    return meta, tail, region_blocks, n_rows


def kernel(x, c, w_ada, b_ada, g_norm1, w_in, g_ret_gn, b_ret_gn, g_qnorm, g_knorm,
           lambda_q1, lambda_k1, lambda_q2, lambda_k2, g_diff_subln, w_out, g_norm2,
           w_router, b_router, w_expert_in, b_expert_in, w_expert_out, b_expert_out):
    batch, seq, d = x.shape
    depth = w_ada.shape[0]
    assert depth == 1
    t = batch * seq
    n_exp = w_router.shape[-1]
    l = 0

    mod = _adaln_mod(c, w_ada[l], b_ada[l][None, :])
    mod3 = mod.reshape(batch, 6, d)
    x2 = x.reshape(t, d)

    proj = _in_proj(x2, mod3, g_norm1[l][None, :], w_in[l].astype(BF16), seq)
    ret = _retention(proj, g_ret_gn[l][None, :], b_ret_gn[l][None, :], batch, seq, d)
    att = _diff_attention(proj, g_qnorm[l][None, :], g_knorm[l][None, :], g_diff_subln[l][None, :],
                          lambda_q1[l][None, :], lambda_k1[l][None, :],
                          lambda_q2[l][None, :], lambda_k2[l][None, :], batch, seq, d)

    wr = w_router[l]
    wr_hi = wr.astype(BF16)
    wr_lo = (wr - wr_hi.astype(F32)).astype(BF16)
    pad = lambda a: jnp.pad(a, ((0, 0), (0, LANES - n_exp)))
    wr_split = jnp.concatenate([pad(wr_hi), pad(wr_lo)], axis=1)
    br_pad = jnp.pad(b_router[l][None, :], ((0, 0), (0, LANES - n_exp)), constant_values=MASK_VALUE)
    x1, h2, top_w, ldest, tile_cnt = _out_router(
        x2, ret, att, mod3, w_out[l].astype(BF16), g_norm2[l][None, :], wr_split, br_pad, seq, n_exp)

    meta, tail, region_blocks, n_rows = _routing_tables(tile_cnt[:, :, 0].astype(I32), t)
    xs = _dispatch(meta, tail, ldest, h2, n_rows, n_exp)
    eo = _experts(region_blocks, xs, w_expert_in[l], b_expert_in[l][:, None, :],
                  w_expert_out[l], b_expert_out[l][:, None, :])
    out = _combine(meta, eo, ldest, top_w, x1, mod3, seq, n_exp)
    return out.reshape(batch, seq, d)
```

```python
import functools
import itertools
import math

import jax
import jax.numpy as jnp
from jax import lax
from jax.experimental import pallas as pl
from jax.experimental.pallas import tpu as pltpu

F32 = jnp.float32
BF16 = jnp.bfloat16
I32 = jnp.int32

EPS = 1e-6
LOG2E = 1.4426950216293335
MASK_VALUE = -1e30
LANES = 128
BF16_SUBLANES = 16
VMEM_LIMIT_BYTES = 48 * 1024 * 1024

RET_HEADS = 4
DIFF_HEADS = 8
TOP_K = 4
SWIGLU_LIMIT = 7.0
SWIGLU_ALPHA = 1.702
LAMBDA_INIT = 0.8 - 0.6 * math.exp(-0.3 * 0)

IN_TM, IN_TN = 1024, 2048
RET_TC = 256
RET_CHUNKS_PER_STEP = 2
ATT_T = 512
ATT_PEELED_TILES = 8
TOK_TM = 512
EXP_BLK = 512
RUN_ALIGN = BF16_SUBLANES
GATHER_RC = 256
COMBINE_KC = 512
W_CHUNKS = 8


def _round_up(n, m):
    return (n + m - 1) // m * m


def _stage_rows(n_exp):
    return _round_up(TOK_TM * TOP_K + n_exp * (RUN_ALIGN - 1), max(GATHER_RC, COMBINE_KC))


SLAB_SIZES = tuple(TOK_TM >> s for s in range(TOK_TM.bit_length()) if TOK_TM >> s >= RUN_ALIGN)
TAIL_SIZES = tuple(s for s in SLAB_SIZES if s < EXP_BLK)


def _params(*sem):
    return pltpu.CompilerParams(dimension_semantics=sem, vmem_limit_bytes=VMEM_LIMIT_BYTES)


def _sigmoid(x):
    return 0.5 * jnp.tanh(0.5 * x) + 0.5


def _rms(x, axis=-1):
    return x * lax.rsqrt(jnp.mean(x * x, axis=axis, keepdims=True) + EPS)


def _mod_kernel(c_ref, w_ref, b_ref, o_ref):
    c = c_ref[...]
    s = c * _sigmoid(c)
    o_ref[...] = jnp.dot(s, w_ref[...], preferred_element_type=F32,
                         precision=lax.Precision.HIGHEST) + b_ref[...]


def _adaln_mod(c, w, b):
    bn, d = c.shape
    n = w.shape[1]
    tn = d
    return pl.pallas_call(
        _mod_kernel,
        out_shape=jax.ShapeDtypeStruct((bn, n), F32),
        grid=(n // tn,),
        in_specs=[pl.BlockSpec((bn, d), lambda j: (0, 0)),
                  pl.BlockSpec((d, tn), lambda j: (0, j)),
                  pl.BlockSpec((1, tn), lambda j: (0, j))],
        out_specs=pl.BlockSpec((bn, tn), lambda j: (0, j)),
        compiler_params=_params("arbitrary"),
        name="adaln_mod",
    )(c, w, b)


def _inproj_kernel(x_ref, mod_ref, g_ref, w_ref, o_ref, h_ref):
    @pl.when(pl.program_id(1) == 0)
    def _():
        m = mod_ref[0]
        h = _rms(x_ref[...]) * g_ref[...] * (1.0 + m[1:2]) + m[0:1]
        h_ref[...] = h.astype(BF16)

    o_ref[...] = jnp.dot(h_ref[...], w_ref[...], preferred_element_type=F32).astype(o_ref.dtype)


def _in_proj(x2, mod3, g1, w_in_bf16, seq):
    t, d = x2.shape
    n = w_in_bf16.shape[1]
    tm, tn = IN_TM, IN_TN
    return pl.pallas_call(
        _inproj_kernel,
        out_shape=jax.ShapeDtypeStruct((t, n), BF16),
        grid=(t // tm, n // tn),
        in_specs=[pl.BlockSpec((tm, d), lambda i, j: (i, 0)),
                  pl.BlockSpec((1, 6, d), lambda i, j: (i * tm // seq, 0, 0)),
                  pl.BlockSpec((1, d), lambda i, j: (0, 0)),
                  pl.BlockSpec((d, tn), lambda i, j: (0, j))],
        out_specs=pl.BlockSpec((tm, tn), lambda i, j: (i, j)),
        scratch_shapes=[pltpu.VMEM((tm, d), BF16)],
        compiler_params=_params("arbitrary", "arbitrary"),
        name="in_proj",
    )(x2, mod3, g1, w_in_bf16)


def _ret_kernel(q_ref, k_ref, v_ref, rg_ref, mg_ref, gng_ref, gnb_ref, o_ref, r_ref, decay_ref,
                *, dk, dv, tc):
    scale = dk ** -0.5
    log_gs = [math.log(1.0 - 2.0 ** (-5.0 - h)) for h in range(RET_HEADS)]

    @pl.when(pl.program_id(1) == 0)
    def _():
        r_ref[...] = jnp.zeros_like(r_ref)
        row = lax.broadcasted_iota(I32, (tc, tc), 0)
        col = lax.broadcasted_iota(I32, (tc, tc), 1)
        rel = (row - col).astype(F32)
        for h, log_g in enumerate(log_gs):
            decay_ref[h] = jnp.where(rel >= 0, jnp.exp(log_g * jnp.maximum(rel, 0.0)), 0.0) * scale

    pos = lax.broadcasted_iota(I32, (tc, 1), 0).astype(F32)
    for sub, (h, log_g) in itertools.product(range(RET_CHUNKS_PER_STEP), enumerate(log_gs)):
        rows = slice(sub * tc, (sub + 1) * tc)
        decay = decay_ref[h]
        xi = jnp.exp(log_g * (pos + 1.0))
        zeta = jnp.exp(log_g * (tc - 1.0 - pos))
        g_chunk = math.exp(log_g * tc)
        q = q_ref[rows, h * dk:(h + 1) * dk]
        k = k_ref[rows, h * dk:(h + 1) * dk]
        v = v_ref[rows, h * dv:(h + 1) * dv]
        s = lax.dot_general(q, k, (((1,), (1,)), ((), ())), preferred_element_type=F32)
        y = jnp.dot((s * decay).astype(BF16), v, preferred_element_type=F32)
        state = r_ref[h]
        y = y + jnp.dot(q, state.astype(BF16), preferred_element_type=F32) * xi
        vz = (v.astype(F32) * zeta).astype(BF16)
        kv = lax.dot_general(k, vz, (((0,), (0,)), ((), ())), preferred_element_type=F32)
        r_ref[h] = state * g_chunk + kv * scale

        mu = jnp.mean(y, axis=-1, keepdims=True)
        yc = y - mu
        var = jnp.mean(yc * yc, axis=-1, keepdims=True)
        sl = slice(h * dv, (h + 1) * dv)
        yn = yc * lax.rsqrt(var + EPS) * gng_ref[:, sl] + gnb_ref[:, sl]
        rg = rg_ref[rows, sl].astype(F32)
        gate = _sigmoid(mg_ref[rows, sl].astype(F32))
        o_ref[rows, sl] = (rg * _sigmoid(rg) * yn * gate).astype(o_ref.dtype)


def _retention(proj, gn_g, gn_b, batch, seq, d):
    t = proj.shape[0]
    tc = RET_TC
    tr = tc * RET_CHUNKS_PER_STEP
    nc = seq // tr
    dk = d // (2 * RET_HEADS)
    dv = d // RET_HEADS
    qk_w = RET_HEADS * dk
    row = lambda b, c: b * nc + c
    kernel = functools.partial(_ret_kernel, dk=dk, dv=dv, tc=tc)
    return pl.pallas_call(
        kernel,
        out_shape=jax.ShapeDtypeStruct((t, d), BF16),
        grid=(batch, nc),
        in_specs=[pl.BlockSpec((tr, qk_w), lambda b, c: (row(b, c), 0)),
                  pl.BlockSpec((tr, qk_w), lambda b, c: (row(b, c), 1)),
                  pl.BlockSpec((tr, d), lambda b, c: (row(b, c), 1)),
                  pl.BlockSpec((tr, d), lambda b, c: (row(b, c), 2)),
                  pl.BlockSpec((tr, d), lambda b, c: (row(b, c), 6)),
                  pl.BlockSpec((1, d), lambda b, c: (0, 0)),
                  pl.BlockSpec((1, d), lambda b, c: (0, 0))],
        out_specs=pl.BlockSpec((tr, d), lambda b, c: (row(b, c), 0)),
        scratch_shapes=[pltpu.VMEM((RET_HEADS, dk, dv), F32), pltpu.VMEM((RET_HEADS, tc, tc), F32)],
        compiler_params=_params("arbitrary", "arbitrary"),
        name="retention",
    )(proj, proj, proj, proj, proj, gn_g, gn_b)


def _halves_rms(x, lo, dh):
    sq = x * x
    s_lo = jnp.sum(jnp.where(lo, sq, 0.0), axis=-1, keepdims=True)
    s_hi = jnp.sum(jnp.where(lo, 0.0, sq), axis=-1, keepdims=True)
    inv = jnp.where(lo, lax.rsqrt(s_lo / dh + EPS), lax.rsqrt(s_hi / dh + EPS))
    return x * inv


def _attn_kernel(q_ref, k_ref, v_ref, gate_ref, slope_ref, qca_ref, qcb_ref, gq_ref, gk_ref, gsub_ref,
                 lq1_ref, lk1_ref, lq2_ref, lk2_ref,
                 o_ref, ka_ref, kb_ref, vt_ref, qa_ref, qb_ref, s0_ref, sa_ref, sb_ref, acc_ref, m_ref,
                 *, tile, dh, dv, nk):
    lane = lax.broadcasted_iota(I32, (tile, 2 * dh), 1)
    lo = lane < dh

    slope = slope_ref[0]
    sub = lax.broadcasted_iota(I32, (tile, 2 * dh), 0)
    off = lane & (dh - 1)
    hi_lane = (off < 6) & ((off & 1) == 0)
    lo_lane = (off < 6) & ((off & 1) == 1)
    bias0 = jnp.where(hi_lane, ((sub >> 8) << 8).astype(F32),
                      jnp.where(lo_lane, (sub & 255).astype(F32), 0.0)) * slope
    bias_step = jnp.where(hi_lane, float(tile), 0.0) * slope
    ones_rows = jnp.where(lax.broadcasted_iota(I32, (BF16_SUBLANES, tile), 0) == 0, 1.0, 0.0)
    lo_t = lax.broadcasted_iota(I32, (2 * dh, tile), 0) < dh

    def prepare(c, carry):
        r0 = c * tile if isinstance(c, int) else pl.multiple_of(c * tile, tile)
        kn = _halves_rms(k_ref[pl.ds(r0, tile), :].astype(F32), lo, dh) * gk_ref[...]
        bias = bias0 + lax.convert_element_type(c, F32) * bias_step
        ka_ref[c] = jnp.where(lo, kn, bias).astype(BF16)
        kb_ref[c] = jnp.where(lo, bias, kn).astype(BF16)
        vt_ref[c, :dv, :] = v_ref[pl.ds(r0, tile), :].astype(F32).T.astype(BF16)
        vt_ref[c, dv:, :] = ones_rows.astype(BF16)
        qn = (_halves_rms(q_ref[pl.ds(r0, tile), :].astype(F32), lo, dh) * gq_ref[...]
              * (dh ** -0.5 * LOG2E))
        qn_t = qn.T
        qa_ref[c] = jnp.where(lo_t, qn_t, qca_ref[...]).astype(BF16)
        qb_ref[c] = jnp.where(lo_t, qcb_ref[...], qn_t).astype(BF16)
        return carry

    lax.fori_loop(0, min(2, nk), prepare, 0)
    lam = (jnp.exp(jnp.sum(lq1_ref[...] * lk1_ref[...], axis=-1, keepdims=True))
           - jnp.exp(jnp.sum(lq2_ref[...] * lk2_ref[...], axis=-1, keepdims=True)) + LAMBDA_INIT)

    refs = (gate_ref, gsub_ref, o_ref, ka_ref, kb_ref, vt_ref, qa_ref, qb_ref,
            s0_ref, sa_ref, sb_ref, acc_ref, m_ref)
    _attn_scores(ka_ref, kb_ref, qa_ref, qb_ref, 0, 0, s0_ref, tile, diagonal=True)
    n_peeled = min(ATT_PEELED_TILES, nk)
    for qi in range(n_peeled):
        _attn_q_tile(qi, lam, prepare, *refs, tile=tile, dv=dv, nk=nk, peeled=True)

    def q_tile(qi, carry):
        _attn_q_tile(jnp.asarray(qi, I32), lam, prepare, *refs, tile=tile, dv=dv, nk=nk, peeled=False)
        return carry

    lax.fori_loop(n_peeled, nk, q_tile, 0)


def _attn_scores(ka_ref, kb_ref, qa_ref, qb_ref, jb, q, s_ref, tile, diagonal=False):
    h = tile // 2
    for idx, (k_ref, q_ref) in enumerate(((ka_ref, qa_ref), (kb_ref, qb_ref))):
        if diagonal:
            s_ref[idx, :h, :] = jnp.dot(k_ref[jb, :h, :], q_ref[q], preferred_element_type=F32)
            s_ref[idx, h:tile, h:] = jnp.dot(k_ref[jb, h:, :], q_ref[q, :, h:], preferred_element_type=F32)
        else:
            s = jnp.dot(k_ref[jb], q_ref[q], preferred_element_type=F32)
            s_ref[idx, :tile, :] = s
            s_ref[idx, tile:tile + 1, :] = jnp.max(s, axis=0, keepdims=True)


def _attn_q_tile(qi, lam, prepare, gate_ref, gsub_ref, o_ref, ka_ref, kb_ref, vt_ref, qa_ref, qb_ref,
                 s0_ref, sa_ref, sb_ref, acc_ref, m_ref, *, tile, dv, nk, peeled):
    def scores(jb, s_ref, q=qi, diagonal=False):
        _attn_scores(ka_ref, kb_ref, qa_ref, qb_ref, jb, q, s_ref, tile, diagonal)

    def absorb(jb, s_ref, masked):
        vt = vt_ref[jb]
        if masked:
            return absorb_diagonal(s_ref, vt)
        for idx in range(2):
            m_old = m_ref[idx]
            m_new = jnp.maximum(m_old, s_ref[idx, tile:tile + 1, :])
            m_ref[idx] = m_new
            p = jnp.exp2(s_ref[idx, :tile, :] - m_new).astype(BF16)
            acc_ref[idx] = (acc_ref[idx] * jnp.exp2(m_old - m_new)
                            + jnp.dot(vt, p, preferred_element_type=F32))

    def absorb_diagonal(s_ref, vt):
        h = tile // 2
        future = lax.broadcasted_iota(I32, (h, h), 0) > lax.broadcasted_iota(I32, (h, h), 1)
        for idx in range(2):
            early = s_ref[idx, :h, :]
            early = jnp.concatenate([jnp.where(future, MASK_VALUE, early[:, :h]), early[:, h:]], axis=1)
            late = jnp.where(future, MASK_VALUE, s_ref[idx, h:tile, h:])
            m_old = m_ref[idx]
            m_new = jnp.maximum(m_old, jnp.max(early, axis=0, keepdims=True))
            m_new = jnp.concatenate(
                [m_new[:, :h], jnp.maximum(m_new[:, h:], jnp.max(late, axis=0, keepdims=True))], axis=1)
            m_ref[idx] = m_new
            p_early = jnp.exp2(early - m_new).astype(BF16)
            p_late = jnp.exp2(late - m_new[:, h:]).astype(BF16)
            acc_ref[idx] = (acc_ref[idx] * jnp.exp2(m_old - m_new)
                            + jnp.dot(vt[:, :h], p_early, preferred_element_type=F32))
            acc_ref[idx, :, h:] = acc_ref[idx, :, h:] + jnp.dot(vt[:, h:], p_late,
                                                                preferred_element_type=F32)

    ahead = (lambda n: min(qi + n, nk - 1)) if peeled else (lambda n: jnp.minimum(qi + n, nk - 1))

    def diagonal_block_and_output(s_ref):
        if s_ref is not s0_ref:
            scores(0, s0_ref, q=ahead(1))
        absorb(qi, s_ref, True)
        if s_ref is s0_ref:
            scores(0, s0_ref, q=ahead(1))
        prepare(ahead(2), 0)
        a1, a2 = acc_ref[0], acc_ref[1]
        o_t = a1[:dv] * (1.0 / a1[dv:dv + 1]) - lam * (a2[:dv] * (1.0 / a2[dv:dv + 1]))
        o = _rms(o_t, axis=0).T * gsub_ref[...] * (1.0 - LAMBDA_INIT)
        rows = pl.ds(pl.multiple_of(qi * tile, tile), tile)
        o_ref[rows, :] = (o * _sigmoid(gate_ref[rows, :].astype(F32))).astype(o_ref.dtype)

    acc_ref[...] = jnp.zeros_like(acc_ref)
    m_ref[...] = jnp.full_like(m_ref, MASK_VALUE)
    if peeled and qi == 0:
        diagonal_block_and_output(s0_ref)
        return
    scores(1, sa_ref, diagonal=peeled and qi == 1)
    absorb(0, s0_ref, False)

    def pair(j2, carry):
        jb = 2 * j2 + 1
        scores(jb + 1, sb_ref)
        absorb(jb, sa_ref, False)
        scores(jb + 2, sa_ref)
        absorb(jb + 1, sb_ref, False)
        return carry

    def even_tail():
        scores(qi, sb_ref, diagonal=True)
        absorb(qi - 1, sa_ref, False)
        diagonal_block_and_output(sb_ref)

    def odd_tail():
        diagonal_block_and_output(sa_ref)

    if peeled:
        for j2 in range((qi - 1) // 2):
            pair(j2, 0)
        (even_tail if qi % 2 == 0 else odd_tail)()
    else:
        lax.fori_loop(0, (qi - 1) // 2, pair, 0)
        pl.when(qi % 2 == 0)(even_tail)
        pl.when(qi % 2 == 1)(odd_tail)


def _bf16_terms(x, n):
    terms, rest = [], jnp.asarray(x, F32)
    for _ in range(n):
        term = rest.astype(BF16).astype(F32)
        terms.append(term)
        rest = rest - term
    return terms


def _diff_attention(proj, g_q, g_k, g_sub, lq1, lk1, lq2, lk2, batch, seq, d):
    t = proj.shape[0]
    tile = ATT_T
    nq = seq // tile
    dh = d // (2 * DIFF_HEADS)
    w = 2 * dh
    assert w == LANES
    q_col, k_col, v_col, gate_col = 3 * d // w, 4 * d // w, 5 * d // w, 7 * d // w
    slopes = 2.0 ** -(jnp.arange(DIFF_HEADS, dtype=F32) + 1.0)
    slopes = jnp.broadcast_to(slopes[:, None, None], (DIFF_HEADS, 1, LANES))
    c1, c2, c3 = _bf16_terms(LOG2E, 3)
    qconst = jnp.zeros((dh,), F32).at[:6].set(jnp.stack([c1, c1, c2, c2, c3, c3]))
    zeros = jnp.zeros((dh,), F32)
    qconst_a = jnp.concatenate([zeros, qconst])[:, None]
    qconst_b = jnp.concatenate([qconst, zeros])[:, None]
    tile2 = lambda a: jnp.concatenate([a, a], axis=-1)
    assert tile % 256 == 0
    small = lambda n: pl.BlockSpec((1, n), lambda b, h: (0, 0))
    kernel = functools.partial(_attn_kernel, tile=tile, dh=dh, dv=w, nk=nq)
    return pl.pallas_call(
        kernel,
        out_shape=jax.ShapeDtypeStruct((t, d), BF16),
        grid=(batch, DIFF_HEADS),
        in_specs=[pl.BlockSpec((seq, w), lambda b, h: (b, q_col + h)),
                  pl.BlockSpec((seq, w), lambda b, h: (b, k_col + h)),
                  pl.BlockSpec((seq, w), lambda b, h: (b, v_col + h)),
                  pl.BlockSpec((seq, w), lambda b, h: (b, gate_col + h)),
                  pl.BlockSpec((1, 1, LANES), lambda b, h: (h, 0, 0)),
                  pl.BlockSpec((w, 1), lambda b, h: (0, 0)), pl.BlockSpec((w, 1), lambda b, h: (0, 0)),
                  small(w), small(w), small(w),
                  small(dh), small(dh), small(dh), small(dh)],
        out_specs=pl.BlockSpec((seq, w), lambda b, h: (b, h)),
        scratch_shapes=[pltpu.VMEM((nq, tile, w), BF16),
                        pltpu.VMEM((nq, tile, w), BF16),
                        pltpu.VMEM((nq, w + BF16_SUBLANES, tile), BF16),
                        pltpu.VMEM((nq, w, tile), BF16),
                        pltpu.VMEM((nq, w, tile), BF16),
                        pltpu.VMEM((2, tile + 8, tile), F32),
                        pltpu.VMEM((2, tile + 8, tile), F32),
                        pltpu.VMEM((2, tile + 8, tile), F32),
                        pltpu.VMEM((2, w + BF16_SUBLANES, tile), F32),
                        pltpu.VMEM((2, 1, tile), F32)],
        compiler_params=_params("arbitrary", "arbitrary"),
        name="diff_attn",
    )(proj, proj, proj, proj, slopes, qconst_a, qconst_b, tile2(g_q), tile2(g_k), g_sub,
      lq1, lk1, lq2, lk2)


def _out_kernel(x_ref, ret_ref, att_ref, mod_ref, wo_ref, g2_ref, wr_ref, br_ref,
                x1_ref, h2_ref, tw_ref, ld_ref, cnt_ref, *, tm, n_exp):
    m = mod_ref[0]
    merged = (ret_ref[...].astype(F32) + att_ref[...].astype(F32)).astype(BF16)
    x1 = x_ref[...] + m[2:3] * jnp.dot(merged, wo_ref[...], preferred_element_type=F32)
    x1_ref[...] = x1
    h2 = _rms(x1) * g2_ref[...] * (1.0 + m[4:5]) + m[3:4]
    hi = h2.astype(BF16)
    h2_ref[...] = hi

    lo = (h2 - hi.astype(F32)).astype(BF16)
    both = jnp.dot(hi, wr_ref[...], preferred_element_type=F32)
    logits = (both[:, :LANES] + both[:, LANES:]
              + jnp.dot(lo, wr_ref[:, :LANES], preferred_element_type=F32) + br_ref[...])
    lt = logits.T[:n_exp]

    erow = lax.broadcasted_iota(I32, (n_exp, tm), 0)
    vals, sels = [], []
    for k in range(TOP_K):
        mx = jnp.max(lt, axis=0, keepdims=True)
        idx = jnp.min(jnp.where(lt == mx, erow, n_exp), axis=0, keepdims=True)
        sel = erow == idx
        vals.append(mx)
        sels.append(sel)
        lt = jnp.where(sel, MASK_VALUE, lt)
    exps = [jnp.exp(v - vals[0]) for v in vals]
    inv = 1.0 / (exps[0] + exps[1] + exps[2] + exps[3])
    for k in range(TOP_K):
        tw_ref[k:k + 1, :] = exps[k] * inv

    chosen = jnp.where(sels[0] | sels[1] | sels[2] | sels[3], 1.0, 0.0)
    before = (lax.broadcasted_iota(I32, (tm, tm), 0) < lax.broadcasted_iota(I32, (tm, tm), 1))
    prefix = jnp.dot(chosen.astype(BF16), jnp.where(before, 1.0, 0.0).astype(BF16),
                     preferred_element_type=F32)
    count = jnp.sum(chosen, axis=1, keepdims=True)
    cnt_ref[...] = count
    padded = jnp.ceil(count * (1.0 / RUN_ALIGN)) * RUN_ALIGN
    below = (lax.broadcasted_iota(I32, (n_exp, n_exp), 1) < lax.broadcasted_iota(I32, (n_exp, n_exp), 0))
    run_start = jnp.dot(jnp.where(below, 1.0, 0.0).astype(BF16),
                        jnp.broadcast_to(padded, (n_exp, LANES)).astype(BF16),
                        preferred_element_type=F32)[:, :1]
    pos = prefix + run_start
    for k in range(TOP_K):
        ld_ref[k:k + 1, :] = jnp.sum(jnp.where(sels[k], pos, 0.0), axis=0, keepdims=True).astype(I32)


def _out_router(x2, ret, att, mod3, w_out_bf16, g2, wr_split, br_pad, seq, n_exp):
    t, d = x2.shape
    tm = TOK_TM
    row = lambda i: (i, 0)
    const = lambda i: (0, 0)
    kernel = functools.partial(_out_kernel, tm=tm, n_exp=n_exp)
    return pl.pallas_call(
        kernel,
        out_shape=(jax.ShapeDtypeStruct((t, d), F32),
                   jax.ShapeDtypeStruct((t, d), BF16),
                   jax.ShapeDtypeStruct((TOP_K, t), F32),
                   jax.ShapeDtypeStruct((TOP_K, t), I32),
                   jax.ShapeDtypeStruct((t // tm, n_exp, 1), F32)),
        grid=(t // tm,),
        in_specs=[pl.BlockSpec((tm, d), row), pl.BlockSpec((tm, d), row), pl.BlockSpec((tm, d), row),
                  pl.BlockSpec((1, 6, d), lambda i: (i * tm // seq, 0, 0)),
                  pl.BlockSpec((d, d), const), pl.BlockSpec((1, d), const),
                  pl.BlockSpec((d, 2 * LANES), const), pl.BlockSpec((1, LANES), const)],
        out_specs=(pl.BlockSpec((tm, d), row), pl.BlockSpec((tm, d), row),
                   pl.BlockSpec((TOP_K, tm), lambda i: (0, i)),
                   pl.BlockSpec((TOP_K, tm), lambda i: (0, i)),
                   pl.BlockSpec((None, n_exp, 1), lambda i: (i, 0, 0))),
        compiler_params=_params("arbitrary"),
        name="out_router",
    )(x2, ret, att, mod3, w_out_bf16, g2, wr_split, br_pad)


def _meta_stride(n_exp):
    return 2 * len(SLAB_SIZES) * n_exp + len(SLAB_SIZES)


def _run_slabs(meta_ref, tile, n_exp, local_ref, global_ref, sem, *, to_global, wait):
    n_sizes = len(SLAB_SIZES)
    base = tile * _meta_stride(n_exp)
    for c, size in enumerate(SLAB_SIZES):
        def one(p, carry, c=c, size=size):
            if wait:
                local = glob = 0
            else:
                local = pl.multiple_of(meta_ref[base + c * n_exp + p], RUN_ALIGN)
                glob = pl.multiple_of(meta_ref[base + (n_sizes + c) * n_exp + p], RUN_ALIGN)
            loc = local_ref.at[pl.ds(local, size)]
            glo = global_ref.at[pl.ds(glob, size)]
            cp = pltpu.make_async_copy(loc, glo, sem) if to_global else pltpu.make_async_copy(glo, loc, sem)
            if wait:
                cp.wait()
            else:
                cp.start()
            return carry

        lax.fori_loop(0, meta_ref[base + 2 * n_sizes * n_exp + c], one, 0)


def _dispatch_kernel(meta_ref, tail_ref, ld_ref, h_ref, xs_ref, stage_ref, zero_ref, sems, tail_sem,
                     *, tm, n_exp, n_tiles, n_blocks, stage_rows):
    i = pl.program_id(0)
    slot = i & 1
    stage = stage_ref.at[slot]

    def tails(wait):
        def unused_block(b, carry):
            cp = pltpu.make_async_copy(
                zero_ref, xs_ref.at[pl.ds(pl.multiple_of(b * EXP_BLK, EXP_BLK), EXP_BLK)], tail_sem)
            if wait:
                cp.wait()
            else:
                cp.start()
            return carry

        lax.fori_loop(tail_ref[2 * n_exp] // EXP_BLK, n_blocks, unused_block, 0)

        def per_expert(e, carry):
            start, n = tail_ref[e], tail_ref[n_exp + e]
            off = 0
            for size in TAIL_SIZES:
                take = n & size

                @pl.when(take != 0)
                def _(off=off, size=size):
                    cp = pltpu.make_async_copy(
                        zero_ref.at[pl.ds(0, size)],
                        xs_ref.at[pl.ds(pl.multiple_of(start + off, RUN_ALIGN), size)], tail_sem)
                    if wait:
                        cp.wait()
                    else:
                        cp.start()

                off = off + take
            return carry

        lax.fori_loop(0, n_exp, per_expert, 0)

    @pl.when(i == 0)
    def _():
        zero_ref[...] = jnp.zeros_like(zero_ref)
        tails(False)
        tails(True)

    ld = ld_ref[...]
    h = h_ref[...]
    for c in range(stage_rows // GATHER_RC):
        r = lax.broadcasted_iota(I32, (GATHER_RC, tm), 0) + c * GATHER_RC
        hit = jnp.where(r == ld[0:1], 1.0, jnp.where(r == ld[1:2], 1.0,
                        jnp.where(r == ld[2:3], 1.0, jnp.where(r == ld[3:4], 1.0, 0.0))))
        stage[c * GATHER_RC:(c + 1) * GATHER_RC, :] = jnp.dot(
            hit.astype(BF16), h, preferred_element_type=F32).astype(BF16)

    _run_slabs(meta_ref, i, n_exp, stage, xs_ref, sems.at[slot], to_global=True, wait=False)

    @pl.when(i > 0)
    def _():
        _run_slabs(meta_ref, i - 1, n_exp, stage_ref.at[1 - slot], xs_ref, sems.at[1 - slot],
                   to_global=True, wait=True)

    @pl.when(i == n_tiles - 1)
    def _():
        _run_slabs(meta_ref, i, n_exp, stage, xs_ref, sems.at[slot], to_global=True, wait=True)


def _dispatch(meta, tail, ldest, h2, n_rows, n_exp):
    t, d = h2.shape
    tm = TOK_TM
    n_tiles = t // tm
    stage_rows = _stage_rows(n_exp)
    kernel = functools.partial(_dispatch_kernel, tm=tm, n_exp=n_exp, n_tiles=n_tiles,
                               n_blocks=n_rows // EXP_BLK, stage_rows=stage_rows)
    grid_spec = pltpu.PrefetchScalarGridSpec(
        num_scalar_prefetch=2,
        grid=(n_tiles,),
        in_specs=[pl.BlockSpec((TOP_K, tm), lambda i, m, tl: (0, i)),
                  pl.BlockSpec((tm, d), lambda i, m, tl: (i, 0))],
        out_specs=pl.BlockSpec(memory_space=pl.ANY),
        scratch_shapes=[pltpu.VMEM((2, stage_rows, d), BF16),
                        pltpu.VMEM((EXP_BLK, d), BF16),
                        pltpu.SemaphoreType.DMA((2,)),
                        pltpu.SemaphoreType.DMA],
    )
    return pl.pallas_call(
        kernel,
        out_shape=jax.ShapeDtypeStruct((n_rows, d), BF16),
        grid_spec=grid_spec,
        compiler_params=_params("arbitrary"),
        name="dispatch",
    )(meta, tail, ldest, h2)


def _expert_kernel(region_ref, xs_ref, w1_ref, b1_ref, w2_ref, b2_ref, eo_ref,
                   w1f_ref, w2f_ref, w1b_ref, w2b_ref, x_buf, o_buf, w_sems, x_sems, o_sems,
                   *, f, blk, n_exp, n_blocks):
    e = pl.program_id(0)
    first, count = region_ref[e], region_ref[n_exp + e]
    w_slot = e & 1
    d = w1f_ref.shape[1]
    r1, r2 = d // W_CHUNKS, f // W_CHUNKS

    def rows(j):
        return pl.ds(pl.multiple_of((first + j) * blk, blk), blk)

    def x_copy(j, slot):
        return pltpu.make_async_copy(xs_ref.at[rows(j)], x_buf.at[slot], x_sems.at[slot])

    def o_copy(j, slot):
        return pltpu.make_async_copy(o_buf.at[slot], eo_ref.at[rows(j)], o_sems.at[slot])

    def w_chunk(expert, c, slot, wait):
        for src, dst, r in ((w1_ref, w1f_ref, r1), (w2_ref, w2f_ref, r2)):
            piece = pl.ds(c * r if isinstance(c, int) else pl.multiple_of(c * r, r), r)
            cp = pltpu.make_async_copy(src.at[expert, piece], dst.at[slot, piece], w_sems.at[slot])
            if wait:
                cp.wait()
            else:
                cp.start()

    @pl.when(e == 0)
    def _():
        for c in range(W_CHUNKS):
            w_chunk(0, c, 0, False)

    @pl.when(count > 0)
    def _():
        x_copy(0, 0).start()

    for c in range(W_CHUNKS):
        w_chunk(e, c, w_slot, True)

    @pl.when(count > 0)
    def _():
        w1b_ref[...] = w1f_ref[w_slot].astype(BF16)
        w2b_ref[...] = w2f_ref[w_slot].astype(BF16)

    def next_chunk(c):
        @pl.when(e + 1 < n_exp)
        def _():
            w_chunk(e + 1, c, 1 - w_slot, False)

    def block(j, carry):
        slot = j & 1
        x_copy(j, slot).wait()

        @pl.when(j + 1 < count)
        def _():
            x_copy(j + 1, 1 - slot).start()

        @pl.when(j < W_CHUNKS)
        def _():
            next_chunk(j)

        @pl.when(j >= 2)
        def _():
            o_copy(j - 2, slot).wait()

        gu = jnp.dot(x_buf[slot], w1b_ref[...], preferred_element_type=F32) + b1_ref[0]
        g = jnp.minimum(gu[:, :f], SWIGLU_LIMIT)
        u = jnp.clip(gu[:, f:], -SWIGLU_LIMIT, SWIGLU_LIMIT)
        a = g * _sigmoid(SWIGLU_ALPHA * g) * (u + 1.0)
        o_buf[slot] = (jnp.dot(a.astype(BF16), w2b_ref[...], preferred_element_type=F32)
                       + b2_ref[0]).astype(o_buf.dtype)
        o_copy(j, slot).start()
        return carry

    lax.fori_loop(0, count, block, 0)
    lax.fori_loop(jnp.minimum(count, W_CHUNKS), W_CHUNKS, lambda c, carry: (next_chunk(c), carry)[1], 0)

    @pl.when(count >= 2)
    def _():
        o_copy(count - 2, count & 1).wait()

    @pl.when(count >= 1)
    def _():
        o_copy(count - 1, (count - 1) & 1).wait()

    @pl.when(e == n_exp - 1)
    def _():
        o_buf[0] = jnp.zeros(o_buf.shape[1:], o_buf.dtype)
        used = first + count

        def fill(j, wait):
            cp = pltpu.make_async_copy(
                o_buf.at[0], eo_ref.at[pl.ds(pl.multiple_of(j * blk, blk), blk)], o_sems.at[0])
            if wait:
                cp.wait()
            else:
                cp.start()

        lax.fori_loop(used, n_blocks, lambda j, c: (fill(j, False), c)[1], 0)
        lax.fori_loop(used, n_blocks, lambda j, c: (fill(j, True), c)[1], 0)


def _experts(region, xs, w1, b1, w2, b2):
    n_rows, d = xs.shape
    n_exp, _, f2 = w1.shape
    f = f2 // 2
    blk = EXP_BLK
    kernel = functools.partial(_expert_kernel, f=f, blk=blk, n_exp=n_exp, n_blocks=n_rows // blk)
    grid_spec = pltpu.PrefetchScalarGridSpec(
        num_scalar_prefetch=1,
        grid=(n_exp,),
        in_specs=[pl.BlockSpec(memory_space=pl.ANY),
                  pl.BlockSpec(memory_space=pl.ANY),
                  pl.BlockSpec((1, 1, f2), lambda e, r: (e, 0, 0)),
                  pl.BlockSpec(memory_space=pl.ANY),
                  pl.BlockSpec((1, 1, d), lambda e, r: (e, 0, 0))],
        out_specs=pl.BlockSpec(memory_space=pl.ANY),
        scratch_shapes=[pltpu.VMEM((2, d, f2), F32), pltpu.VMEM((2, f, d), F32),
                        pltpu.VMEM((d, f2), BF16), pltpu.VMEM((f, d), BF16),
                        pltpu.VMEM((2, blk, d), BF16), pltpu.VMEM((2, blk, d), BF16),
                        pltpu.SemaphoreType.DMA((2,)), pltpu.SemaphoreType.DMA((2,)),
                        pltpu.SemaphoreType.DMA((2,))],
    )
    return pl.pallas_call(
        kernel,
        out_shape=jax.ShapeDtypeStruct((n_rows, d), BF16),
        grid_spec=grid_spec,
        compiler_params=_params("arbitrary"),
        name="experts",
    )(region, xs, w1, b1, w2, b2)


def _combine_kernel(meta_ref, eo_ref, ld_ref, w_ref, x1_ref, mod_ref, o_ref, stage_ref, sems,
                    *, tm, n_exp, n_tiles, stage_rows):
    i = pl.program_id(0)
    slot = i & 1

    @pl.when(i == 0)
    def _():
        stage_ref[...] = jnp.zeros_like(stage_ref)
        _run_slabs(meta_ref, 0, n_exp, stage_ref.at[0], eo_ref, sems.at[0], to_global=False, wait=False)

    @pl.when(i + 1 < n_tiles)
    def _():
        _run_slabs(meta_ref, i + 1, n_exp, stage_ref.at[1 - slot], eo_ref, sems.at[1 - slot],
                   to_global=False, wait=False)

    ld = ld_ref[...]
    w = w_ref[...]

    _run_slabs(meta_ref, i, n_exp, stage_ref.at[slot], eo_ref, sems.at[slot], to_global=False, wait=True)
    stage = stage_ref.at[slot]
    y = None
    for c in range(stage_rows // COMBINE_KC):
        r0 = c * COMBINE_KC
        r = lax.broadcasted_iota(I32, (COMBINE_KC, tm), 0) + r0
        sel = jnp.where(r == ld[0:1], w[0:1], jnp.where(r == ld[1:2], w[1:2],
                        jnp.where(r == ld[2:3], w[2:3], jnp.where(r == ld[3:4], w[3:4], 0.0))))
        part = lax.dot_general(sel.astype(BF16), stage[r0:r0 + COMBINE_KC, :], (((0,), (0,)), ((), ())),
                               preferred_element_type=F32)
        y = part if y is None else y + part
    o_ref[...] = x1_ref[...] + mod_ref[0][5:6] * y


def _combine(meta, eo, ld_tok, w_tok, x1, mod3, seq, n_exp):
    t, d = x1.shape
    tm = TOK_TM
    n_tiles = t // tm
    stage_rows = _stage_rows(n_exp)
    kernel = functools.partial(_combine_kernel, tm=tm, n_exp=n_exp, n_tiles=n_tiles, stage_rows=stage_rows)
    grid_spec = pltpu.PrefetchScalarGridSpec(
        num_scalar_prefetch=1,
        grid=(n_tiles,),
        in_specs=[pl.BlockSpec(memory_space=pl.ANY),
                  pl.BlockSpec((TOP_K, tm), lambda i, m: (0, i)),
                  pl.BlockSpec((TOP_K, tm), lambda i, m: (0, i)),
                  pl.BlockSpec((tm, d), lambda i, m: (i, 0)),
                  pl.BlockSpec((1, 6, d), lambda i, m: (i * tm // seq, 0, 0))],
        out_specs=pl.BlockSpec((tm, d), lambda i, m: (i, 0)),
        scratch_shapes=[pltpu.VMEM((2, stage_rows, d), BF16),
                        pltpu.SemaphoreType.DMA((2,))],
    )
    return pl.pallas_call(
        kernel,
        out_shape=jax.ShapeDtypeStruct((t, d), F32),
        grid_spec=grid_spec,
        compiler_params=_params("arbitrary"),
        name="combine",
    )(meta, eo, ld_tok, w_tok, x1, mod3)


def _routing_tables(tile_cnt, n_tokens):
    n_tiles, n_exp = tile_cnt.shape
    blk = EXP_BLK
    run = _round_up(tile_cnt, RUN_ALIGN)
    local = jnp.cumsum(run, axis=1) - run
    size = jnp.sum(run, axis=0)
    region = _round_up(size, blk)
    region_end = jnp.cumsum(region)
    region_start = region_end - region
    glob = region_start[None, :] + jnp.cumsum(run, axis=0) - run
    sizes = jnp.asarray(SLAB_SIZES, I32)
    has = (run[:, :, None] & sizes) != 0
    before = run[:, :, None] & ~(2 * sizes - 1)
    place = jnp.cumsum(has, axis=1) - has
    hit = has[..., None] & (place[..., None] == jnp.arange(n_exp, dtype=I32))
    listed = lambda start: jnp.sum(
        jnp.where(hit, (start[:, :, None] + before)[..., None], 0), axis=1).reshape(n_tiles, -1)
    meta = jnp.concatenate([listed(local), listed(glob), jnp.sum(has, axis=1)],
                           axis=1).reshape(-1).astype(I32)
    tail = jnp.concatenate([region_start + size, region - size, region_end[-1:]]).astype(I32)
    n_rows = _round_up(n_tokens * TOP_K + n_tiles * n_exp * RUN_ALIGN + n_exp * blk, blk)
    region_blocks = jnp.concatenate([region_start // blk, region // blk]).astype(I32)
    return meta, tail, region_blocks, n_rows


def kernel(x, c, w_ada, b_ada, g_norm1, w_in, g_ret_gn, b_ret_gn, g_qnorm, g_knorm,
           lambda_q1, lambda_k1, lambda_q2, lambda_k2, g_diff_subln, w_out, g_norm2,
           w_router, b_router, w_expert_in, b_expert_in, w_expert_out, b_expert_out):
    batch, seq, d = x.shape
    depth = w_ada.shape[0]
    assert depth == 1
    t = batch * seq
    n_exp = w_router.shape[-1]
    l = 0

    mod = _adaln_mod(c, w_ada[l], b_ada[l][None, :])
    mod3 = mod.reshape(batch, 6, d)
    x2 = x.reshape(t, d)

    proj = _in_proj(x2, mod3, g_norm1[l][None, :], w_in[l].astype(BF16), seq)
    ret = _retention(proj, g_ret_gn[l][None, :], b_ret_gn[l][None, :], batch, seq, d)
    att = _diff_attention(proj, g_qnorm[l][None, :], g_knorm[l][None, :], g_diff_subln[l][None, :],
                          lambda_q1[l][None, :], lambda_k1[l][None, :],
                          lambda_q2[l][None, :], lambda_k2[l][None, :], batch, seq, d)

    wr = w_router[l]
    wr_hi = wr.astype(BF16)
    wr_lo = (wr - wr_hi.astype(F32)).astype(BF16)
    pad = lambda a: jnp.pad(a, ((0, 0), (0, LANES - n_exp)))
    wr_split = jnp.concatenate([pad(wr_hi), pad(wr_lo)], axis=1)
    br_pad = jnp.pad(b_router[l][None, :], ((0, 0), (0, LANES - n_exp)), constant_values=MASK_VALUE)
    x1, h2, top_w, ldest, tile_cnt = _out_router(
        x2, ret, att, mod3, w_out[l].astype(BF16), g_norm2[l][None, :], wr_split, br_pad, seq, n_exp)

    meta, tail, region_blocks, n_rows = _routing_tables(tile_cnt[:, :, 0].astype(I32), t)
    xs = _dispatch(meta, tail, ldest, h2, n_rows, n_exp)
    eo = _experts(region_blocks, xs, w_expert_in[l], b_expert_in[l][:, None, :],
                  w_expert_out[l], b_expert_out[l][:, None, :])
    out = _combine(meta, eo, ldest, top_w, x1, mod3, seq, n_exp)
    return out.reshape(batch, seq, d)
```

```python
import functools
import itertools
import math

import jax
import jax.numpy as jnp
from jax import lax
from jax.experimental import pallas as pl
from jax.experimental.pallas import tpu as pltpu

F32 = jnp.float32
BF16 = jnp.bfloat16
I32 = jnp.int32

EPS = 1e-6
LOG2E = 1.4426950216293335
MASK_VALUE = -1e30
LANES = 128
BF16_SUBLANES = 16
VMEM_LIMIT_BYTES = 48 * 1024 * 1024

RET_HEADS = 4
DIFF_HEADS = 8
TOP_K = 4
SWIGLU_LIMIT = 7.0
SWIGLU_ALPHA = 1.702
LAMBDA_INIT = 0.8 - 0.6 * math.exp(-0.3 * 0)

IN_TM, IN_TN = 1024, 2048
RET_TC = 256
RET_CHUNKS_PER_STEP = 2
ATT_T = 512
ATT_PEELED_TILES = 8
TOK_TM = 512
EXP_BLK = 512
RUN_ALIGN = BF16_SUBLANES
GATHER_RC = 256
COMBINE_KC = 512
W_CHUNKS = 8


def _round_up(n, m):
    return (n + m - 1) // m * m


def _stage_rows(n_exp):
    return _round_up(TOK_TM * TOP_K + n_exp * (RUN_ALIGN - 1), max(GATHER_RC, COMBINE_KC))


SLAB_SIZES = tuple(TOK_TM >> s for s in range(TOK_TM.bit_length()) if TOK_TM >> s >= RUN_ALIGN)
TAIL_SIZES = tuple(s for s in SLAB_SIZES if s < EXP_BLK)


def _params(*sem):
    return pltpu.CompilerParams(dimension_semantics=sem, vmem_limit_bytes=VMEM_LIMIT_BYTES)


def _sigmoid(x):
    return 0.5 * jnp.tanh(0.5 * x) + 0.5


def _rms(x, axis=-1):
    return x * lax.rsqrt(jnp.mean(x * x, axis=axis, keepdims=True) + EPS)


def _mod_kernel(c_ref, w_ref, b_ref, o_ref):
    c = c_ref[...]
    s = c * _sigmoid(c)
    o_ref[...] = jnp.dot(s, w_ref[...], preferred_element_type=F32,
                         precision=lax.Precision.HIGHEST) + b_ref[...]


def _adaln_mod(c, w, b):
    bn, d = c.shape
    n = w.shape[1]
    tn = d
    return pl.pallas_call(
        _mod_kernel,
        out_shape=jax.ShapeDtypeStruct((bn, n), F32),
        grid=(n // tn,),
        in_specs=[pl.BlockSpec((bn, d), lambda j: (0, 0)),
                  pl.BlockSpec((d, tn), lambda j: (0, j)),
                  pl.BlockSpec((1, tn), lambda j: (0, j))],
        out_specs=pl.BlockSpec((bn, tn), lambda j: (0, j)),
        compiler_params=_params("arbitrary"),
        name="adaln_mod",
    )(c, w, b)


def _inproj_kernel(x_ref, mod_ref, g_ref, w_ref, o_ref, h_ref):
    @pl.when(pl.program_id(1) == 0)
    def _():
        m = mod_ref[0]
        h = _rms(x_ref[...]) * g_ref[...] * (1.0 + m[1:2]) + m[0:1]
        h_ref[...] = h.astype(BF16)

    o_ref[...] = jnp.dot(h_ref[...], w_ref[...], preferred_element_type=F32).astype(o_ref.dtype)


def _in_proj(x2, mod3, g1, w_in_bf16, seq):
    t, d = x2.shape
    n = w_in_bf16.shape[1]
    tm, tn = IN_TM, IN_TN
    return pl.pallas_call(
        _inproj_kernel,
        out_shape=jax.ShapeDtypeStruct((t, n), BF16),
        grid=(t // tm, n // tn),
        in_specs=[pl.BlockSpec((tm, d), lambda i, j: (i, 0)),
                  pl.BlockSpec((1, 6, d), lambda i, j: (i * tm // seq, 0, 0)),
                  pl.BlockSpec((1, d), lambda i, j: (0, 0)),
                  pl.BlockSpec((d, tn), lambda i, j: (0, j))],
        out_specs=pl.BlockSpec((tm, tn), lambda i, j: (i, j)),
        scratch_shapes=[pltpu.VMEM((tm, d), BF16)],
        compiler_params=_params("arbitrary", "arbitrary"),
        name="in_proj",
    )(x2, mod3, g1, w_in_bf16)


def _ret_kernel(q_ref, k_ref, v_ref, rg_ref, mg_ref, gng_ref, gnb_ref, o_ref, r_ref, decay_ref,
                *, dk, dv, tc):
    scale = dk ** -0.5
    log_gs = [math.log(1.0 - 2.0 ** (-5.0 - h)) for h in range(RET_HEADS)]

    @pl.when(pl.program_id(1) == 0)
    def _():
        r_ref[...] = jnp.zeros_like(r_ref)
        row = lax.broadcasted_iota(I32, (tc, tc), 0)
        col = lax.broadcasted_iota(I32, (tc, tc), 1)
        rel = (row - col).astype(F32)
        for h, log_g in enumerate(log_gs):
            decay_ref[h] = jnp.where(rel >= 0, jnp.exp(log_g * jnp.maximum(rel, 0.0)), 0.0) * scale

    pos = lax.broadcasted_iota(I32, (tc, 1), 0).astype(F32)
    for sub, (h, log_g) in itertools.product(range(RET_CHUNKS_PER_STEP), enumerate(log_gs)):
        rows = slice(sub * tc, (sub + 1) * tc)
        decay = decay_ref[h]
        xi = jnp.exp(log_g * (pos + 1.0))
        zeta = jnp.exp(log_g * (tc - 1.0 - pos))
        g_chunk = math.exp(log_g * tc)
        q = q_ref[rows, h * dk:(h + 1) * dk]
        k = k_ref[rows, h * dk:(h + 1) * dk]
        v = v_ref[rows, h * dv:(h + 1) * dv]
        s = lax.dot_general(q, k, (((1,), (1,)), ((), ())), preferred_element_type=F32)
        y = jnp.dot((s * decay).astype(BF16), v, preferred_element_type=F32)
        state = r_ref[h]
        y = y + jnp.dot(q, state.astype(BF16), preferred_element_type=F32) * xi
        vz = (v.astype(F32) * zeta).astype(BF16)
        kv = lax.dot_general(k, vz, (((0,), (0,)), ((), ())), preferred_element_type=F32)
        r_ref[h] = state * g_chunk + kv * scale

        mu = jnp.mean(y, axis=-1, keepdims=True)
        yc = y - mu
        var = jnp.mean(yc * yc, axis=-1, keepdims=True)
        sl = slice(h * dv, (h + 1) * dv)
        yn = yc * lax.rsqrt(var + EPS) * gng_ref[:, sl] + gnb_ref[:, sl]
        rg = rg_ref[rows, sl].astype(F32)
        gate = _sigmoid(mg_ref[rows, sl].astype(F32))
        o_ref[rows, sl] = (rg * _sigmoid(rg) * yn * gate).astype(o_ref.dtype)


def _retention(proj, gn_g, gn_b, batch, seq, d):
    t = proj.shape[0]
    tc = RET_TC
    tr = tc * RET_CHUNKS_PER_STEP
    nc = seq // tr
    dk = d // (2 * RET_HEADS)
    dv = d // RET_HEADS
    qk_w = RET_HEADS * dk
    row = lambda b, c: b * nc + c
    kernel = functools.partial(_ret_kernel, dk=dk, dv=dv, tc=tc)
    return pl.pallas_call(
        kernel,
        out_shape=jax.ShapeDtypeStruct((t, d), BF16),
        grid=(batch, nc),
        in_specs=[pl.BlockSpec((tr, qk_w), lambda b, c: (row(b, c), 0)),
                  pl.BlockSpec((tr, qk_w), lambda b, c: (row(b, c), 1)),
                  pl.BlockSpec((tr, d), lambda b, c: (row(b, c), 1)),
                  pl.BlockSpec((tr, d), lambda b, c: (row(b, c), 2)),
                  pl.BlockSpec((tr, d), lambda b, c: (row(b, c), 6)),
                  pl.BlockSpec((1, d), lambda b, c: (0, 0)),
                  pl.BlockSpec((1, d), lambda b, c: (0, 0))],
        out_specs=pl.BlockSpec((tr, d), lambda b, c: (row(b, c), 0)),
        scratch_shapes=[pltpu.VMEM((RET_HEADS, dk, dv), F32), pltpu.VMEM((RET_HEADS, tc, tc), F32)],
        compiler_params=_params("arbitrary", "arbitrary"),
        name="retention",
    )(proj, proj, proj, proj, proj, gn_g, gn_b)


def _halves_rms(x, lo, dh):
    sq = x * x
    s_lo = jnp.sum(jnp.where(lo, sq, 0.0), axis=-1, keepdims=True)
    s_hi = jnp.sum(jnp.where(lo, 0.0, sq), axis=-1, keepdims=True)
    inv = jnp.where(lo, lax.rsqrt(s_lo / dh + EPS), lax.rsqrt(s_hi / dh + EPS))
    return x * inv


def _attn_kernel(q_ref, k_ref, v_ref, gate_ref, slope_ref, qca_ref, qcb_ref, gq_ref, gk_ref, gsub_ref,
                 lq1_ref, lk1_ref, lq2_ref, lk2_ref,
                 o_ref, ka_ref, kb_ref, vt_ref, qa_ref, qb_ref, s0_ref, sa_ref, sb_ref, acc_ref, m_ref,
                 *, tile, dh, dv, nk):
    lane = lax.broadcasted_iota(I32, (tile, 2 * dh), 1)
    lo = lane < dh

    slope = slope_ref[0]
    sub = lax.broadcasted_iota(I32, (tile, 2 * dh), 0)
    off = lane & (dh - 1)
    hi_lane = (off < 6) & ((off & 1) == 0)
    lo_lane = (off < 6) & ((off & 1) == 1)
    bias0 = jnp.where(hi_lane, ((sub >> 8) << 8).astype(F32),
                      jnp.where(lo_lane, (sub & 255).astype(F32), 0.0)) * slope
    bias_step = jnp.where(hi_lane, float(tile), 0.0) * slope
    ones_rows = jnp.where(lax.broadcasted_iota(I32, (BF16_SUBLANES, tile), 0) == 0, 1.0, 0.0)
    lo_t = lax.broadcasted_iota(I32, (2 * dh, tile), 0) < dh

    def prepare(c, carry):
        r0 = c * tile if isinstance(c, int) else pl.multiple_of(c * tile, tile)
        kn = _halves_rms(k_ref[pl.ds(r0, tile), :].astype(F32), lo, dh) * gk_ref[...]
        bias = bias0 + lax.convert_element_type(c, F32) * bias_step
        ka_ref[c] = jnp.where(lo, kn, bias).astype(BF16)
        kb_ref[c] = jnp.where(lo, bias, kn).astype(BF16)
        vt_ref[c, :dv, :] = v_ref[pl.ds(r0, tile), :].astype(F32).T.astype(BF16)
        vt_ref[c, dv:, :] = ones_rows.astype(BF16)
        qn = (_halves_rms(q_ref[pl.ds(r0, tile), :].astype(F32), lo, dh) * gq_ref[...]
              * (dh ** -0.5 * LOG2E))
        qn_t = qn.T
        qa_ref[c] = jnp.where(lo_t, qn_t, qca_ref[...]).astype(BF16)
        qb_ref[c] = jnp.where(lo_t, qcb_ref[...], qn_t).astype(BF16)
        return carry

    for c in range(min(2, nk)):
        prepare(c, 0)
    lam = (jnp.exp(jnp.sum(lq1_ref[...] * lk1_ref[...], axis=-1, keepdims=True))
           - jnp.exp(jnp.sum(lq2_ref[...] * lk2_ref[...], axis=-1, keepdims=True)) + LAMBDA_INIT)

    refs = (gate_ref, gsub_ref, o_ref, ka_ref, kb_ref, vt_ref, qa_ref, qb_ref,
            s0_ref, sa_ref, sb_ref, acc_ref, m_ref)
    _attn_scores(ka_ref, kb_ref, qa_ref, qb_ref, 0, 0, s0_ref, tile, diagonal=True)
    n_peeled = min(ATT_PEELED_TILES, nk)
    for qi in range(n_peeled):
        _attn_q_tile(qi, lam, prepare, *refs, tile=tile, dv=dv, nk=nk, peeled=True)

    def q_tile(qi, carry):
        _attn_q_tile(jnp.asarray(qi, I32), lam, prepare, *refs, tile=tile, dv=dv, nk=nk, peeled=False)
        return carry

    lax.fori_loop(n_peeled, nk, q_tile, 0)


def _attn_scores(ka_ref, kb_ref, qa_ref, qb_ref, jb, q, s_ref, tile, diagonal=False):
    h = tile // 2
    for idx, (k_ref, q_ref) in enumerate(((ka_ref, qa_ref), (kb_ref, qb_ref))):
        if diagonal:
            s_ref[idx, :h, :] = jnp.dot(k_ref[jb, :h, :], q_ref[q], preferred_element_type=F32)
            s_ref[idx, h:tile, h:] = jnp.dot(k_ref[jb, h:, :], q_ref[q, :, h:], preferred_element_type=F32)
        else:
            s = jnp.dot(k_ref[jb], q_ref[q], preferred_element_type=F32)
            s_ref[idx, :tile, :] = s
            s_ref[idx, tile:tile + 1, :] = jnp.max(s, axis=0, keepdims=True)


def _attn_q_tile(qi, lam, prepare, gate_ref, gsub_ref, o_ref, ka_ref, kb_ref, vt_ref, qa_ref, qb_ref,
                 s0_ref, sa_ref, sb_ref, acc_ref, m_ref, *, tile, dv, nk, peeled):
    def scores(jb, s_ref, q=qi, diagonal=False):
        _attn_scores(ka_ref, kb_ref, qa_ref, qb_ref, jb, q, s_ref, tile, diagonal)

    def absorb(jb, s_ref, masked):
        vt = vt_ref[jb]
        if masked:
            return absorb_diagonal(s_ref, vt)
        for idx in range(2):
            m_old = m_ref[idx]
            m_new = jnp.maximum(m_old, s_ref[idx, tile:tile + 1, :])
            m_ref[idx] = m_new
            p = jnp.exp2(s_ref[idx, :tile, :] - m_new).astype(BF16)
            acc_ref[idx] = (acc_ref[idx] * jnp.exp2(m_old - m_new)
                            + jnp.dot(vt, p, preferred_element_type=F32))

    def absorb_diagonal(s_ref, vt):
        h = tile // 2
        future = lax.broadcasted_iota(I32, (h, h), 0) > lax.broadcasted_iota(I32, (h, h), 1)
        for idx in range(2):
            early = s_ref[idx, :h, :]
            early = jnp.concatenate([jnp.where(future, MASK_VALUE, early[:, :h]), early[:, h:]], axis=1)
            late = jnp.where(future, MASK_VALUE, s_ref[idx, h:tile, h:])
            m_old = m_ref[idx]
            m_new = jnp.maximum(m_old, jnp.max(early, axis=0, keepdims=True))
            m_new = jnp.concatenate(
                [m_new[:, :h], jnp.maximum(m_new[:, h:], jnp.max(late, axis=0, keepdims=True))], axis=1)
            m_ref[idx] = m_new
            p_early = jnp.exp2(early - m_new).astype(BF16)
            p_late = jnp.exp2(late - m_new[:, h:]).astype(BF16)
            acc_ref[idx] = (acc_ref[idx] * jnp.exp2(m_old - m_new)
                            + jnp.dot(vt[:, :h], p_early, preferred_element_type=F32))
            acc_ref[idx, :, h:] = acc_ref[idx, :, h:] + jnp.dot(vt[:, h:], p_late,
                                                                preferred_element_type=F32)

    ahead = (lambda n: min(qi + n, nk - 1)) if peeled else (lambda n: jnp.minimum(qi + n, nk - 1))

    def diagonal_block_and_output(s_ref):
        if s_ref is not s0_ref:
            scores(0, s0_ref, q=ahead(1))
        absorb(qi, s_ref, True)
        if s_ref is s0_ref:
            scores(0, s0_ref, q=ahead(1))
        prepare(ahead(2), 0)
        a1, a2 = acc_ref[0], acc_ref[1]
        o_t = a1[:dv] * (1.0 / a1[dv:dv + 1]) - lam * (a2[:dv] * (1.0 / a2[dv:dv + 1]))
        o = _rms(o_t, axis=0).T * gsub_ref[...] * (1.0 - LAMBDA_INIT)
        rows = pl.ds(pl.multiple_of(qi * tile, tile), tile)
        o_ref[rows, :] = (o * _sigmoid(gate_ref[rows, :].astype(F32))).astype(o_ref.dtype)

    acc_ref[...] = jnp.zeros_like(acc_ref)
    m_ref[...] = jnp.full_like(m_ref, MASK_VALUE)
    if peeled and qi == 0:
        diagonal_block_and_output(s0_ref)
        return
    scores(1, sa_ref, diagonal=peeled and qi == 1)
    absorb(0, s0_ref, False)

    def pair(j2, carry):
        jb = 2 * j2 + 1
        scores(jb + 1, sb_ref)
        absorb(jb, sa_ref, False)
        scores(jb + 2, sa_ref)
        absorb(jb + 1, sb_ref, False)
        return carry

    def even_tail():
        scores(qi, sb_ref, diagonal=True)
        absorb(qi - 1, sa_ref, False)
        diagonal_block_and_output(sb_ref)

    def odd_tail():
        diagonal_block_and_output(sa_ref)

    if peeled:
        for j2 in range((qi - 1) // 2):
            pair(j2, 0)
        (even_tail if qi % 2 == 0 else odd_tail)()
    else:
        lax.fori_loop(0, (qi - 1) // 2, pair, 0)
        pl.when(qi % 2 == 0)(even_tail)
        pl.when(qi % 2 == 1)(odd_tail)


def _bf16_terms(x, n):
    terms, rest = [], jnp.asarray(x, F32)
    for _ in range(n):
        term = rest.astype(BF16).astype(F32)
        terms.append(term)
        rest = rest - term
    return terms


def _diff_attention(proj, g_q, g_k, g_sub, lq1, lk1, lq2, lk2, batch, seq, d):
    t = proj.shape[0]
    tile = ATT_T
    nq = seq // tile
    dh = d // (2 * DIFF_HEADS)
    w = 2 * dh
    assert w == LANES
    q_col, k_col, v_col, gate_col = 3 * d // w, 4 * d // w, 5 * d // w, 7 * d // w
    slopes = 2.0 ** -(jnp.arange(DIFF_HEADS, dtype=F32) + 1.0)
    slopes = jnp.broadcast_to(slopes[:, None, None], (DIFF_HEADS, 1, LANES))
    c1, c2, c3 = _bf16_terms(LOG2E, 3)
    qconst = jnp.zeros((dh,), F32).at[:6].set(jnp.stack([c1, c1, c2, c2, c3, c3]))
    zeros = jnp.zeros((dh,), F32)
    qconst_a = jnp.concatenate([zeros, qconst])[:, None]
    qconst_b = jnp.concatenate([qconst, zeros])[:, None]
    tile2 = lambda a: jnp.concatenate([a, a], axis=-1)
    assert tile % 256 == 0
    small = lambda n: pl.BlockSpec((1, n), lambda b, h: (0, 0))
    kernel = functools.partial(_attn_kernel, tile=tile, dh=dh, dv=w, nk=nq)
    return pl.pallas_call(
        kernel,
        out_shape=jax.ShapeDtypeStruct((t, d), BF16),
        grid=(batch, DIFF_HEADS),
        in_specs=[pl.BlockSpec((seq, w), lambda b, h: (b, q_col + h)),
                  pl.BlockSpec((seq, w), lambda b, h: (b, k_col + h)),
                  pl.BlockSpec((seq, w), lambda b, h: (b, v_col + h)),
                  pl.BlockSpec((seq, w), lambda b, h: (b, gate_col + h)),
                  pl.BlockSpec((1, 1, LANES), lambda b, h: (h, 0, 0)),
                  pl.BlockSpec((w, 1), lambda b, h: (0, 0)), pl.BlockSpec((w, 1), lambda b, h: (0, 0)),
                  small(w), small(w), small(w),
                  small(dh), small(dh), small(dh), small(dh)],
        out_specs=pl.BlockSpec((seq, w), lambda b, h: (b, h)),
        scratch_shapes=[pltpu.VMEM((nq, tile, w), BF16),
                        pltpu.VMEM((nq, tile, w), BF16),
                        pltpu.VMEM((nq, w + BF16_SUBLANES, tile), BF16),
                        pltpu.VMEM((nq, w, tile), BF16),
                        pltpu.VMEM((nq, w, tile), BF16),
                        pltpu.VMEM((2, tile + 8, tile), F32),
                        pltpu.VMEM((2, tile + 8, tile), F32),
                        pltpu.VMEM((2, tile + 8, tile), F32),
                        pltpu.VMEM((2, w + BF16_SUBLANES, tile), F32),
                        pltpu.VMEM((2, 1, tile), F32)],
        compiler_params=_params("arbitrary", "arbitrary"),
        name="diff_attn",
    )(proj, proj, proj, proj, slopes, qconst_a, qconst_b, tile2(g_q), tile2(g_k), g_sub,
      lq1, lk1, lq2, lk2)


def _out_kernel(x_ref, ret_ref, att_ref, mod_ref, wo_ref, g2_ref, wr_ref, br_ref,
                x1_ref, h2_ref, tw_ref, ld_ref, cnt_ref, *, tm, n_exp):
    m = mod_ref[0]
    merged = (ret_ref[...].astype(F32) + att_ref[...].astype(F32)).astype(BF16)
    x1 = x_ref[...] + m[2:3] * jnp.dot(merged, wo_ref[...], preferred_element_type=F32)
    x1_ref[...] = x1
    h2 = _rms(x1) * g2_ref[...] * (1.0 + m[4:5]) + m[3:4]
    hi = h2.astype(BF16)
    h2_ref[...] = hi

    lo = (h2 - hi.astype(F32)).astype(BF16)
    both = jnp.dot(hi, wr_ref[...], preferred_element_type=F32)
    logits = (both[:, :LANES] + both[:, LANES:]
              + jnp.dot(lo, wr_ref[:, :LANES], preferred_element_type=F32) + br_ref[...])
    lt = logits.T[:n_exp]

    erow = lax.broadcasted_iota(I32, (n_exp, tm), 0)
    vals, sels = [], []
    for k in range(TOP_K):
        mx = jnp.max(lt, axis=0, keepdims=True)
        idx = jnp.min(jnp.where(lt == mx, erow, n_exp), axis=0, keepdims=True)
        sel = erow == idx
        vals.append(mx)
        sels.append(sel)
        lt = jnp.where(sel, MASK_VALUE, lt)
    exps = [jnp.exp(v - vals[0]) for v in vals]
    inv = 1.0 / (exps[0] + exps[1] + exps[2] + exps[3])
    for k in range(TOP_K):
        tw_ref[k:k + 1, :] = exps[k] * inv

    chosen = jnp.where(sels[0] | sels[1] | sels[2] | sels[3], 1.0, 0.0)
    before = (lax.broadcasted_iota(I32, (tm, tm), 0) < lax.broadcasted_iota(I32, (tm, tm), 1))
    prefix = jnp.dot(chosen.astype(BF16), jnp.where(before, 1.0, 0.0).astype(BF16),
                     preferred_element_type=F32)
    count = jnp.sum(chosen, axis=1, keepdims=True)
    cnt_ref[...] = count
    padded = jnp.ceil(count * (1.0 / RUN_ALIGN)) * RUN_ALIGN
    below = (lax.broadcasted_iota(I32, (n_exp, n_exp), 1) < lax.broadcasted_iota(I32, (n_exp, n_exp), 0))
    run_start = jnp.dot(jnp.where(below, 1.0, 0.0).astype(BF16),
                        jnp.broadcast_to(padded, (n_exp, LANES)).astype(BF16),
                        preferred_element_type=F32)[:, :1]
    pos = prefix + run_start
    for k in range(TOP_K):
        ld_ref[k:k + 1, :] = jnp.sum(jnp.where(sels[k], pos, 0.0), axis=0, keepdims=True).astype(I32)


def _out_router(x2, ret, att, mod3, w_out_bf16, g2, wr_split, br_pad, seq, n_exp):
    t, d = x2.shape
    tm = TOK_TM
    row = lambda i: (i, 0)
    const = lambda i: (0, 0)
    kernel = functools.partial(_out_kernel, tm=tm, n_exp=n_exp)
    return pl.pallas_call(
        kernel,
        out_shape=(jax.ShapeDtypeStruct((t, d), F32),
                   jax.ShapeDtypeStruct((t, d), BF16),
                   jax.ShapeDtypeStruct((TOP_K, t), F32),
                   jax.ShapeDtypeStruct((TOP_K, t), I32),
                   jax.ShapeDtypeStruct((t // tm, n_exp, 1), F32)),
        grid=(t // tm,),
        in_specs=[pl.BlockSpec((tm, d), row), pl.BlockSpec((tm, d), row), pl.BlockSpec((tm, d), row),
                  pl.BlockSpec((1, 6, d), lambda i: (i * tm // seq, 0, 0)),
                  pl.BlockSpec((d, d), const), pl.BlockSpec((1, d), const),
                  pl.BlockSpec((d, 2 * LANES), const), pl.BlockSpec((1, LANES), const)],
        out_specs=(pl.BlockSpec((tm, d), row), pl.BlockSpec((tm, d), row),
                   pl.BlockSpec((TOP_K, tm), lambda i: (0, i)),
                   pl.BlockSpec((TOP_K, tm), lambda i: (0, i)),
                   pl.BlockSpec((None, n_exp, 1), lambda i: (i, 0, 0))),
        compiler_params=_params("arbitrary"),
        name="out_router",
    )(x2, ret, att, mod3, w_out_bf16, g2, wr_split, br_pad)


def _meta_stride(n_exp):
    return 2 * len(SLAB_SIZES) * n_exp + len(SLAB_SIZES)


def _run_slabs(meta_ref, tile, n_exp, local_ref, global_ref, sem, *, to_global, wait):
    n_sizes = len(SLAB_SIZES)
    base = tile * _meta_stride(n_exp)
    for c, size in enumerate(SLAB_SIZES):
        def one(p, carry, c=c, size=size):
            if wait:
                local = glob = 0
            else:
                local = pl.multiple_of(meta_ref[base + c * n_exp + p], RUN_ALIGN)
                glob = pl.multiple_of(meta_ref[base + (n_sizes + c) * n_exp + p], RUN_ALIGN)
            loc = local_ref.at[pl.ds(local, size)]
            glo = global_ref.at[pl.ds(glob, size)]
            cp = pltpu.make_async_copy(loc, glo, sem) if to_global else pltpu.make_async_copy(glo, loc, sem)
            if wait:
                cp.wait()
            else:
                cp.start()
            return carry

        lax.fori_loop(0, meta_ref[base + 2 * n_sizes * n_exp + c], one, 0)


def _dispatch_kernel(meta_ref, tail_ref, ld_ref, h_ref, xs_ref, stage_ref, zero_ref, sems, tail_sem,
                     *, tm, n_exp, n_tiles, n_blocks, stage_rows):
    i = pl.program_id(0)
    slot = i & 1
    stage = stage_ref.at[slot]

    def tails(wait):
        def unused_block(b, carry):
            cp = pltpu.make_async_copy(
                zero_ref, xs_ref.at[pl.ds(pl.multiple_of(b * EXP_BLK, EXP_BLK), EXP_BLK)], tail_sem)
            if wait:
                cp.wait()
            else:
                cp.start()
            return carry

        lax.fori_loop(tail_ref[2 * n_exp] // EXP_BLK, n_blocks, unused_block, 0)

        def per_expert(e, carry):
            start, n = tail_ref[e], tail_ref[n_exp + e]
            off = 0
            for size in TAIL_SIZES:
                take = n & size

                @pl.when(take != 0)
                def _(off=off, size=size):
                    cp = pltpu.make_async_copy(
                        zero_ref.at[pl.ds(0, size)],
                        xs_ref.at[pl.ds(pl.multiple_of(start + off, RUN_ALIGN), size)], tail_sem)
                    if wait:
                        cp.wait()
                    else:
                        cp.start()

                off = off + take
            return carry

        lax.fori_loop(0, n_exp, per_expert, 0)

    @pl.when(i == 0)
    def _():
        zero_ref[...] = jnp.zeros_like(zero_ref)
        tails(False)
        tails(True)

    ld = ld_ref[...]
    h = h_ref[...]
    for c in range(stage_rows // GATHER_RC):
        r = lax.broadcasted_iota(I32, (GATHER_RC, tm), 0) + c * GATHER_RC
        hit = jnp.where(r == ld[0:1], 1.0, jnp.where(r == ld[1:2], 1.0,
                        jnp.where(r == ld[2:3], 1.0, jnp.where(r == ld[3:4], 1.0, 0.0))))
        stage[c * GATHER_RC:(c + 1) * GATHER_RC, :] = jnp.dot(
            hit.astype(BF16), h, preferred_element_type=F32).astype(BF16)

    _run_slabs(meta_ref, i, n_exp, stage, xs_ref, sems.at[slot], to_global=True, wait=False)

    @pl.when(i > 0)
    def _():
        _run_slabs(meta_ref, i - 1, n_exp, stage_ref.at[1 - slot], xs_ref, sems.at[1 - slot],
                   to_global=True, wait=True)

    @pl.when(i == n_tiles - 1)
    def _():
        _run_slabs(meta_ref, i, n_exp, stage, xs_ref, sems.at[slot], to_global=True, wait=True)


def _dispatch(meta, tail, ldest, h2, n_rows, n_exp):
    t, d = h2.shape
    tm = TOK_TM
    n_tiles = t // tm
    stage_rows = _stage_rows(n_exp)
    kernel = functools.partial(_dispatch_kernel, tm=tm, n_exp=n_exp, n_tiles=n_tiles,
                               n_blocks=n_rows // EXP_BLK, stage_rows=stage_rows)
    grid_spec = pltpu.PrefetchScalarGridSpec(
        num_scalar_prefetch=2,
        grid=(n_tiles,),
        in_specs=[pl.BlockSpec((TOP_K, tm), lambda i, m, tl: (0, i)),
                  pl.BlockSpec((tm, d), lambda i, m, tl: (i, 0))],
        out_specs=pl.BlockSpec(memory_space=pl.ANY),
        scratch_shapes=[pltpu.VMEM((2, stage_rows, d), BF16),
                        pltpu.VMEM((EXP_BLK, d), BF16),
                        pltpu.SemaphoreType.DMA((2,)),
                        pltpu.SemaphoreType.DMA],
    )
    return pl.pallas_call(
        kernel,
        out_shape=jax.ShapeDtypeStruct((n_rows, d), BF16),
        grid_spec=grid_spec,
        compiler_params=_params("arbitrary"),
        name="dispatch",
    )(meta, tail, ldest, h2)


def _expert_kernel(region_ref, xs_ref, w1_ref, b1_ref, w2_ref, b2_ref, eo_ref,
                   w1f_ref, w2f_ref, w1b_ref, w2b_ref, x_buf, o_buf, w_sems, x_sems, o_sems,
                   *, f, blk, n_exp, n_blocks):
    e = pl.program_id(0)
    first, count = region_ref[e], region_ref[n_exp + e]
    w_slot = e & 1
    d = w1f_ref.shape[1]
    r1, r2 = d // W_CHUNKS, f // W_CHUNKS

    def rows(j):
        return pl.ds(pl.multiple_of((first + j) * blk, blk), blk)

    def x_copy(j, slot):
        return pltpu.make_async_copy(xs_ref.at[rows(j)], x_buf.at[slot], x_sems.at[slot])

    def o_copy(j, slot):
        return pltpu.make_async_copy(o_buf.at[slot], eo_ref.at[rows(j)], o_sems.at[slot])

    def w_chunk(expert, c, slot, wait):
        for src, dst, r in ((w1_ref, w1f_ref, r1), (w2_ref, w2f_ref, r2)):
            piece = pl.ds(c * r if isinstance(c, int) else pl.multiple_of(c * r, r), r)
            cp = pltpu.make_async_copy(src.at[expert, piece], dst.at[slot, piece], w_sems.at[slot])
            if wait:
                cp.wait()
            else:
                cp.start()

    @pl.when(e == 0)
    def _():
        for c in range(W_CHUNKS):
            w_chunk(0, c, 0, False)

    @pl.when(count > 0)
    def _():
        x_copy(0, 0).start()

    for c in range(W_CHUNKS):
        w_chunk(e, c, w_slot, True)

    @pl.when(count > 0)
    def _():
        w1b_ref[...] = w1f_ref[w_slot].astype(BF16)
        w2b_ref[...] = w2f_ref[w_slot].astype(BF16)

    def next_chunk(c):
        @pl.when(e + 1 < n_exp)
        def _():
            w_chunk(e + 1, c, 1 - w_slot, False)

    def block(j, carry):
        slot = j & 1
        x_copy(j, slot).wait()

        @pl.when(j + 1 < count)
        def _():
            x_copy(j + 1, 1 - slot).start()

        @pl.when(j < W_CHUNKS)
        def _():
            next_chunk(j)

        @pl.when(j >= 2)
        def _():
            o_copy(j - 2, slot).wait()

        gu = jnp.dot(x_buf[slot], w1b_ref[...], preferred_element_type=F32) + b1_ref[0]
        g = jnp.minimum(gu[:, :f], SWIGLU_LIMIT)
        u = jnp.clip(gu[:, f:], -SWIGLU_LIMIT, SWIGLU_LIMIT)
        a = g * _sigmoid(SWIGLU_ALPHA * g) * (u + 1.0)
        o_buf[slot] = (jnp.dot(a.astype(BF16), w2b_ref[...], preferred_element_type=F32)
                       + b2_ref[0]).astype(o_buf.dtype)
        o_copy(j, slot).start()
        return carry

    lax.fori_loop(0, count, block, 0)
    lax.fori_loop(jnp.minimum(count, W_CHUNKS), W_CHUNKS, lambda c, carry: (next_chunk(c), carry)[1], 0)

    @pl.when(count >= 2)
    def _():
        o_copy(count - 2, count & 1).wait()

    @pl.when(count >= 1)
    def _():
        o_copy(count - 1, (count - 1) & 1).wait()

    @pl.when(e == n_exp - 1)
    def _():
        o_buf[0] = jnp.zeros(o_buf.shape[1:], o_buf.dtype)
        used = first + count

        def fill(j, wait):
            cp = pltpu.make_async_copy(
                o_buf.at[0], eo_ref.at[pl.ds(pl.multiple_of(j * blk, blk), blk)], o_sems.at[0])
            if wait:
                cp.wait()
            else:
                cp.start()

        lax.fori_loop(used, n_blocks, lambda j, c: (fill(j, False), c)[1], 0)
        lax.fori_loop(used, n_blocks, lambda j, c: (fill(j, True), c)[1], 0)


def _experts(region, xs, w1, b1, w2, b2):
    n_rows, d = xs.shape
    n_exp, _, f2 = w1.shape
    f = f2 // 2
    blk = EXP_BLK
    kernel = functools.partial(_expert_kernel, f=f, blk=blk, n_exp=n_exp, n_blocks=n_rows // blk)
    grid_spec = pltpu.PrefetchScalarGridSpec(
        num_scalar_prefetch=1,
        grid=(n_exp,),
        in_specs=[pl.BlockSpec(memory_space=pl.ANY),
                  pl.BlockSpec(memory_space=pl.ANY),
                  pl.BlockSpec((1, 1, f2), lambda e, r: (e, 0, 0)),
                  pl.BlockSpec(memory_space=pl.ANY),
                  pl.BlockSpec((1, 1, d), lambda e, r: (e, 0, 0))],
        out_specs=pl.BlockSpec(memory_space=pl.ANY),
        scratch_shapes=[pltpu.VMEM((2, d, f2), F32), pltpu.VMEM((2, f, d), F32),
                        pltpu.VMEM((d, f2), BF16), pltpu.VMEM((f, d), BF16),
                        pltpu.VMEM((2, blk, d), BF16), pltpu.VMEM((2, blk, d), BF16),
                        pltpu.SemaphoreType.DMA((2,)), pltpu.SemaphoreType.DMA((2,)),
                        pltpu.SemaphoreType.DMA((2,))],
    )
    return pl.pallas_call(
        kernel,
        out_shape=jax.ShapeDtypeStruct((n_rows, d), BF16),
        grid_spec=grid_spec,
        compiler_params=_params("arbitrary"),
        name="experts",
    )(region, xs, w1, b1, w2, b2)


def _combine_kernel(meta_ref, eo_ref, ld_ref, w_ref, x1_ref, mod_ref, o_ref, stage_ref, sems,
                    *, tm, n_exp, n_tiles, stage_rows):
    i = pl.program_id(0)
    slot = i & 1

    @pl.when(i == 0)
    def _():
        stage_ref[...] = jnp.zeros_like(stage_ref)
        _run_slabs(meta_ref, 0, n_exp, stage_ref.at[0], eo_ref, sems.at[0], to_global=False, wait=False)

    @pl.when(i + 1 < n_tiles)
    def _():
        _run_slabs(meta_ref, i + 1, n_exp, stage_ref.at[1 - slot], eo_ref, sems.at[1 - slot],
                   to_global=False, wait=False)

    ld = ld_ref[...]
    w = w_ref[...]

    _run_slabs(meta_ref, i, n_exp, stage_ref.at[slot], eo_ref, sems.at[slot], to_global=False, wait=True)
    stage = stage_ref.at[slot]
    y = None
    for c in range(stage_rows // COMBINE_KC):
        r0 = c * COMBINE_KC
        r = lax.broadcasted_iota(I32, (COMBINE_KC, tm), 0) + r0
        sel = jnp.where(r == ld[0:1], w[0:1], jnp.where(r == ld[1:2], w[1:2],
                        jnp.where(r == ld[2:3], w[2:3], jnp.where(r == ld[3:4], w[3:4], 0.0))))
        part = lax.dot_general(sel.astype(BF16), stage[r0:r0 + COMBINE_KC, :], (((0,), (0,)), ((), ())),
                               preferred_element_type=F32)
        y = part if y is None else y + part
    o_ref[...] = x1_ref[...] + mod_ref[0][5:6] * y


def _combine(meta, eo, ld_tok, w_tok, x1, mod3, seq, n_exp):
    t, d = x1.shape
    tm = TOK_TM
    n_tiles = t // tm
    stage_rows = _stage_rows(n_exp)
    kernel = functools.partial(_combine_kernel, tm=tm, n_exp=n_exp, n_tiles=n_tiles, stage_rows=stage_rows)
    grid_spec = pltpu.PrefetchScalarGridSpec(
        num_scalar_prefetch=1,
        grid=(n_tiles,),
        in_specs=[pl.BlockSpec(memory_space=pl.ANY),
                  pl.BlockSpec((TOP_K, tm), lambda i, m: (0, i)),
                  pl.BlockSpec((TOP_K, tm), lambda i, m: (0, i)),
                  pl.BlockSpec((tm, d), lambda i, m: (i, 0)),
                  pl.BlockSpec((1, 6, d), lambda i, m: (i * tm // seq, 0, 0))],
        out_specs=pl.BlockSpec((tm, d), lambda i, m: (i, 0)),
        scratch_shapes=[pltpu.VMEM((2, stage_rows, d), BF16),
                        pltpu.SemaphoreType.DMA((2,))],
    )
    return pl.pallas_call(
        kernel,
        out_shape=jax.ShapeDtypeStruct((t, d), F32),
        grid_spec=grid_spec,
        compiler_params=_params("arbitrary"),
        name="combine",
    )(meta, eo, ld_tok, w_tok, x1, mod3)


def _routing_tables(tile_cnt, n_tokens):
    n_tiles, n_exp = tile_cnt.shape
    blk = EXP_BLK
    run = _round_up(tile_cnt, RUN_ALIGN)
    local = jnp.cumsum(run, axis=1) - run
    size = jnp.sum(run, axis=0)
    region = _round_up(size, blk)
    region_end = jnp.cumsum(region)
    region_start = region_end - region
    glob = region_start[None, :] + jnp.cumsum(run, axis=0) - run
    sizes = jnp.asarray(SLAB_SIZES, I32)
    has = (run[:, :, None] & sizes) != 0
    before = run[:, :, None] & ~(2 * sizes - 1)
    place = jnp.cumsum(has, axis=1) - has
    hit = has[..., None] & (place[..., None] == jnp.arange(n_exp, dtype=I32))
    listed = lambda start: jnp.sum(
        jnp.where(hit, (start[:, :, None] + before)[..., None], 0), axis=1).reshape(n_tiles, -1)
    meta = jnp.concatenate([listed(local), listed(glob), jnp.sum(has, axis=1)],
                           axis=1).reshape(-1).astype(I32)
    tail = jnp.concatenate([region_start + size, region - size, region_end[-1:]]).astype(I32)
    n_rows = _round_up(n_tokens * TOP_K + n_tiles * n_exp * RUN_ALIGN + n_exp * blk, blk)
    region_blocks = jnp.concatenate([region_start // blk, region // blk]).astype(I32)
    return meta, tail, region_blocks, n_rows


def kernel(x, c, w_ada, b_ada, g_norm1, w_in, g_ret_gn, b_ret_gn, g_qnorm, g_knorm,
           lambda_q1, lambda_k1, lambda_q2, lambda_k2, g_diff_subln, w_out, g_norm2,
           w_router, b_router, w_expert_in, b_expert_in, w_expert_out, b_expert_out):
    batch, seq, d = x.shape
    depth = w_ada.shape[0]
    assert depth == 1
    t = batch * seq
    n_exp = w_router.shape[-1]
    l = 0

    mod = _adaln_mod(c, w_ada[l], b_ada[l][None, :])
    mod3 = mod.reshape(batch, 6, d)
    x2 = x.reshape(t, d)

    proj = _in_proj(x2, mod3, g_norm1[l][None, :], w_in[l].astype(BF16), seq)
    ret = _retention(proj, g_ret_gn[l][None, :], b_ret_gn[l][None, :], batch, seq, d)
    att = _diff_attention(proj, g_qnorm[l][None, :], g_knorm[l][None, :], g_diff_subln[l][None, :],
                          lambda_q1[l][None, :], lambda_k1[l][None, :],
                          lambda_q2[l][None, :], lambda_k2[l][None, :], batch, seq, d)

    wr = w_router[l]
    wr_hi = wr.astype(BF16)
    wr_lo = (wr - wr_hi.astype(F32)).astype(BF16)
    pad = lambda a: jnp.pad(a, ((0, 0), (0, LANES - n_exp)))
    wr_split = jnp.concatenate([pad(wr_hi), pad(wr_lo)], axis=1)
    br_pad = jnp.pad(b_router[l][None, :], ((0, 0), (0, LANES - n_exp)), constant_values=MASK_VALUE)
    x1, h2, top_w, ldest, tile_cnt = _out_router(
        x2, ret, att, mod3, w_out[l].astype(BF16), g_norm2[l][None, :], wr_split, br_pad, seq, n_exp)

    meta, tail, region_blocks, n_rows = _routing_tables(tile_cnt[:, :, 0].astype(I32), t)
    xs = _dispatch(meta, tail, ldest, h2, n_rows, n_exp)
    eo = _experts(region_blocks, xs, w_expert_in[l], b_expert_in[l][:, None, :],
                  w_expert_out[l], b_expert_out[l][:, None, :])
    out = _combine(meta, eo, ldest, top_w, x1, mod3, seq, n_exp)
    return out.reshape(batch, seq, d)
```

```python
import functools
import itertools
import math

import jax
import jax.numpy as jnp
from jax import lax
from jax.experimental import pallas as pl
from jax.experimental.pallas import tpu as pltpu

F32 = jnp.float32
BF16 = jnp.bfloat16
I32 = jnp.int32

EPS = 1e-6
LOG2E = 1.4426950216293335
MASK_VALUE = -1e30
LANES = 128
BF16_SUBLANES = 16
VMEM_LIMIT_BYTES = 48 * 1024 * 1024

RET_HEADS = 4
DIFF_HEADS = 8
TOP_K = 4
SWIGLU_LIMIT = 7.0
SWIGLU_ALPHA = 1.702
LAMBDA_INIT = 0.8 - 0.6 * math.exp(-0.3 * 0)

IN_TM, IN_TN = 1024, 2048
RET_TC = 256
RET_CHUNKS_PER_STEP = 2
ATT_T = 512
ATT_PEELED_TILES = 8
TOK_TM = 512
EXP_BLK = 512
RUN_ALIGN = BF16_SUBLANES
GATHER_RC = 256
COMBINE_KC = 512
W_CHUNKS = 8


def _round_up(n, m):
    return (n + m - 1) // m * m


def _stage_rows(n_exp):
    return _round_up(TOK_TM * TOP_K + n_exp * (RUN_ALIGN - 1), max(GATHER_RC, COMBINE_KC))


SLAB_SIZES = tuple(TOK_TM >> s for s in range(TOK_TM.bit_length()) if TOK_TM >> s >= RUN_ALIGN)
TAIL_SIZES = tuple(s for s in SLAB_SIZES if s < EXP_BLK)


def _params(*sem):
    return pltpu.CompilerParams(dimension_semantics=sem, vmem_limit_bytes=VMEM_LIMIT_BYTES)


def _sigmoid(x):
    return 0.5 * jnp.tanh(0.5 * x) + 0.5


def _rms(x, axis=-1):
    return x * lax.rsqrt(jnp.mean(x * x, axis=axis, keepdims=True) + EPS)


def _mod_kernel(c_ref, w_ref, b_ref, o_ref):
    c = c_ref[...]
    s = c * _sigmoid(c)
    o_ref[...] = jnp.dot(s, w_ref[...], preferred_element_type=F32,
                         precision=lax.Precision.HIGHEST) + b_ref[...]


def _adaln_mod(c, w, b):
    bn, d = c.shape
    n = w.shape[1]
    tn = d
    return pl.pallas_call(
        _mod_kernel,
        out_shape=jax.ShapeDtypeStruct((bn, n), F32),
        grid=(n // tn,),
        in_specs=[pl.BlockSpec((bn, d), lambda j: (0, 0)),
                  pl.BlockSpec((d, tn), lambda j: (0, j)),
                  pl.BlockSpec((1, tn), lambda j: (0, j))],
        out_specs=pl.BlockSpec((bn, tn), lambda j: (0, j)),
        compiler_params=_params("arbitrary"),
        name="adaln_mod",
    )(c, w, b)


def _inproj_kernel(x_ref, mod_ref, g_ref, w_ref, o_ref, h_ref):
    @pl.when(pl.program_id(1) == 0)
    def _():
        m = mod_ref[0]
        h = _rms(x_ref[...]) * g_ref[...] * (1.0 + m[1:2]) + m[0:1]
        h_ref[...] = h.astype(BF16)

    o_ref[...] = jnp.dot(h_ref[...], w_ref[...], preferred_element_type=F32).astype(o_ref.dtype)


def _in_proj(x2, mod3, g1, w_in_bf16, seq):
    t, d = x2.shape
    n = w_in_bf16.shape[1]
    tm, tn = IN_TM, IN_TN
    return pl.pallas_call(
        _inproj_kernel,
        out_shape=jax.ShapeDtypeStruct((t, n), BF16),
        grid=(t // tm, n // tn),
        in_specs=[pl.BlockSpec((tm, d), lambda i, j: (i, 0)),
                  pl.BlockSpec((1, 6, d), lambda i, j: (i * tm // seq, 0, 0)),
                  pl.BlockSpec((1, d), lambda i, j: (0, 0)),
                  pl.BlockSpec((d, tn), lambda i, j: (0, j))],
        out_specs=pl.BlockSpec((tm, tn), lambda i, j: (i, j)),
        scratch_shapes=[pltpu.VMEM((tm, d), BF16)],
        compiler_params=_params("arbitrary", "arbitrary"),
        name="in_proj",
    )(x2, mod3, g1, w_in_bf16)


def _ret_kernel(q_ref, k_ref, v_ref, rg_ref, mg_ref, gng_ref, gnb_ref, o_ref, r_ref, decay_ref,
                *, dk, dv, tc):
    scale = dk ** -0.5
    log_gs = [math.log(1.0 - 2.0 ** (-5.0 - h)) for h in range(RET_HEADS)]

    @pl.when(pl.program_id(1) == 0)
    def _():
        r_ref[...] = jnp.zeros_like(r_ref)
        row = lax.broadcasted_iota(I32, (tc, tc), 0)
        col = lax.broadcasted_iota(I32, (tc, tc), 1)
        rel = (row - col).astype(F32)
        for h, log_g in enumerate(log_gs):
            decay_ref[h] = jnp.where(rel >= 0, jnp.exp(log_g * jnp.maximum(rel, 0.0)), 0.0) * scale

    pos = lax.broadcasted_iota(I32, (tc, 1), 0).astype(F32)
    for sub, (h, log_g) in itertools.product(range(RET_CHUNKS_PER_STEP), enumerate(log_gs)):
        rows = slice(sub * tc, (sub + 1) * tc)
        decay = decay_ref[h]
        xi = jnp.exp(log_g * (pos + 1.0))
        zeta = jnp.exp(log_g * (tc - 1.0 - pos))
        g_chunk = math.exp(log_g * tc)
        q = q_ref[rows, h * dk:(h + 1) * dk]
        k = k_ref[rows, h * dk:(h + 1) * dk]
        v = v_ref[rows, h * dv:(h + 1) * dv]
        s = lax.dot_general(q, k, (((1,), (1,)), ((), ())), preferred_element_type=F32)
        y = jnp.dot((s * decay).astype(BF16), v, preferred_element_type=F32)
        state = r_ref[h]
        y = y + jnp.dot(q, state.astype(BF16), preferred_element_type=F32) * xi
        vz = (v.astype(F32) * zeta).astype(BF16)
        kv = lax.dot_general(k, vz, (((0,), (0,)), ((), ())), preferred_element_type=F32)
        r_ref[h] = state * g_chunk + kv * scale

        mu = jnp.mean(y, axis=-1, keepdims=True)
        yc = y - mu
        var = jnp.mean(yc * yc, axis=-1, keepdims=True)
        sl = slice(h * dv, (h + 1) * dv)
        yn = yc * lax.rsqrt(var + EPS) * gng_ref[:, sl] + gnb_ref[:, sl]
        rg = rg_ref[rows, sl].astype(F32)
        gate = _sigmoid(mg_ref[rows, sl].astype(F32))
        o_ref[rows, sl] = (rg * _sigmoid(rg) * yn * gate).astype(o_ref.dtype)


def _retention(proj, gn_g, gn_b, batch, seq, d):
    t = proj.shape[0]
    tc = RET_TC
    tr = tc * RET_CHUNKS_PER_STEP
    nc = seq // tr
    dk = d // (2 * RET_HEADS)
    dv = d // RET_HEADS
    qk_w = RET_HEADS * dk
    row = lambda b, c: b * nc + c
    kernel = functools.partial(_ret_kernel, dk=dk, dv=dv, tc=tc)
    return pl.pallas_call(
        kernel,
        out_shape=jax.ShapeDtypeStruct((t, d), BF16),
        grid=(batch, nc),
        in_specs=[pl.BlockSpec((tr, qk_w), lambda b, c: (row(b, c), 0)),
                  pl.BlockSpec((tr, qk_w), lambda b, c: (row(b, c), 1)),
                  pl.BlockSpec((tr, d), lambda b, c: (row(b, c), 1)),
                  pl.BlockSpec((tr, d), lambda b, c: (row(b, c), 2)),
                  pl.BlockSpec((tr, d), lambda b, c: (row(b, c), 6)),
                  pl.BlockSpec((1, d), lambda b, c: (0, 0)),
                  pl.BlockSpec((1, d), lambda b, c: (0, 0))],
        out_specs=pl.BlockSpec((tr, d), lambda b, c: (row(b, c), 0)),
        scratch_shapes=[pltpu.VMEM((RET_HEADS, dk, dv), F32), pltpu.VMEM((RET_HEADS, tc, tc), F32)],
        compiler_params=_params("arbitrary", "arbitrary"),
        name="retention",
    )(proj, proj, proj, proj, proj, gn_g, gn_b)


def _halves_rms(x, lo, dh):
    sq = x * x
    s_lo = jnp.sum(jnp.where(lo, sq, 0.0), axis=-1, keepdims=True)
    s_hi = jnp.sum(jnp.where(lo, 0.0, sq), axis=-1, keepdims=True)
    inv = jnp.where(lo, lax.rsqrt(s_lo / dh + EPS), lax.rsqrt(s_hi / dh + EPS))
    return x * inv


def _attn_kernel(q_ref, k_ref, v_ref, gate_ref, slope_ref, qca_ref, qcb_ref, gq_ref, gk_ref, gsub_ref,
                 lq1_ref, lk1_ref, lq2_ref, lk2_ref,
                 o_ref, ka_ref, kb_ref, vt_ref, qa_ref, qb_ref, s0_ref, sa_ref, sb_ref, acc_ref, m_ref,
                 *, tile, dh, dv, nk):
    lane = lax.broadcasted_iota(I32, (tile, 2 * dh), 1)
    lo = lane < dh

    slope = slope_ref[0]
    sub = lax.broadcasted_iota(I32, (tile, 2 * dh), 0)
    off = lane & (dh - 1)
    hi_lane = (off < 6) & ((off & 1) == 0)
    lo_lane = (off < 6) & ((off & 1) == 1)
    bias0 = jnp.where(hi_lane, ((sub >> 8) << 8).astype(F32),
                      jnp.where(lo_lane, (sub & 255).astype(F32), 0.0)) * slope
    bias_step = jnp.where(hi_lane, float(tile), 0.0) * slope
    ones_rows = jnp.where(lax.broadcasted_iota(I32, (BF16_SUBLANES, tile), 0) == 0, 1.0, 0.0)
    lo_t = lax.broadcasted_iota(I32, (2 * dh, tile), 0) < dh

    def prepare(c, carry):
        r0 = c * tile if isinstance(c, int) else pl.multiple_of(c * tile, tile)
        kn = _halves_rms(k_ref[pl.ds(r0, tile), :].astype(F32), lo, dh) * gk_ref[...]
        bias = bias0 + lax.convert_element_type(c, F32) * bias_step
        ka_ref[c] = jnp.where(lo, kn, bias).astype(BF16)
        kb_ref[c] = jnp.where(lo, bias, kn).astype(BF16)
        vt_ref[c, :dv, :] = v_ref[pl.ds(r0, tile), :].astype(F32).T.astype(BF16)
        vt_ref[c, dv:, :] = ones_rows.astype(BF16)
        qn = (_halves_rms(q_ref[pl.ds(r0, tile), :].astype(F32), lo, dh) * gq_ref[...]
              * (dh ** -0.5 * LOG2E))
        qn_t = qn.T
        qa_ref[c] = jnp.where(lo_t, qn_t, qca_ref[...]).astype(BF16)
        qb_ref[c] = jnp.where(lo_t, qcb_ref[...], qn_t).astype(BF16)
        return carry

    for c in range(min(2, nk)):
        prepare(c, 0)
    lam = (jnp.exp(jnp.sum(lq1_ref[...] * lk1_ref[...], axis=-1, keepdims=True))
           - jnp.exp(jnp.sum(lq2_ref[...] * lk2_ref[...], axis=-1, keepdims=True)) + LAMBDA_INIT)

    refs = (gate_ref, gsub_ref, o_ref, ka_ref, kb_ref, vt_ref, qa_ref, qb_ref,
            s0_ref, sa_ref, sb_ref, acc_ref, m_ref)
    _attn_scores(ka_ref, kb_ref, qa_ref, qb_ref, 0, 0, s0_ref, tile, diagonal=True)
    n_peeled = min(ATT_PEELED_TILES, nk)
    for qi in range(n_peeled):
        _attn_q_tile(qi, lam, prepare, *refs, tile=tile, dv=dv, nk=nk, peeled=True)

    def q_tile(qi, carry):
        _attn_q_tile(jnp.asarray(qi, I32), lam, prepare, *refs, tile=tile, dv=dv, nk=nk, peeled=False)
        return carry

    lax.fori_loop(n_peeled, nk, q_tile, 0)


def _attn_scores(ka_ref, kb_ref, qa_ref, qb_ref, jb, q, s_ref, tile, diagonal=False):
    h = tile // 2
    for idx, (k_ref, q_ref) in enumerate(((ka_ref, qa_ref), (kb_ref, qb_ref))):
        if diagonal:
            s_ref[idx, :h, :] = jnp.dot(k_ref[jb, :h, :], q_ref[q], preferred_element_type=F32)
            s_ref[idx, h:tile, h:] = jnp.dot(k_ref[jb, h:, :], q_ref[q, :, h:], preferred_element_type=F32)
        else:
            s = jnp.dot(k_ref[jb], q_ref[q], preferred_element_type=F32)
            s_ref[idx, :tile, :] = s
            s_ref[idx, tile:tile + 1, :] = jnp.max(s, axis=0, keepdims=True)


def _attn_q_tile(qi, lam, prepare, gate_ref, gsub_ref, o_ref, ka_ref, kb_ref, vt_ref, qa_ref, qb_ref,
                 s0_ref, sa_ref, sb_ref, acc_ref, m_ref, *, tile, dv, nk, peeled):
    def scores(jb, s_ref, q=qi, diagonal=False):
        _attn_scores(ka_ref, kb_ref, qa_ref, qb_ref, jb, q, s_ref, tile, diagonal)

    def absorb(jb, s_ref, masked):
        vt = vt_ref[jb]
        if masked:
            return absorb_diagonal(s_ref, vt)
        for idx in range(2):
            m_old = m_ref[idx]
            m_new = jnp.maximum(m_old, s_ref[idx, tile:tile + 1, :])
            m_ref[idx] = m_new
            p = jnp.exp2(s_ref[idx, :tile, :] - m_new).astype(BF16)
            acc_ref[idx] = (acc_ref[idx] * jnp.exp2(m_old - m_new)
                            + jnp.dot(vt, p, preferred_element_type=F32))

    def absorb_diagonal(s_ref, vt):
        h = tile // 2
        future = lax.broadcasted_iota(I32, (h, h), 0) > lax.broadcasted_iota(I32, (h, h), 1)
        for idx in range(2):
            early = s_ref[idx, :h, :]
            early = jnp.concatenate([jnp.where(future, MASK_VALUE, early[:, :h]), early[:, h:]], axis=1)
            late = jnp.where(future, MASK_VALUE, s_ref[idx, h:tile, h:])
            m_old = m_ref[idx]
            m_new = jnp.maximum(m_old, jnp.max(early, axis=0, keepdims=True))
            m_new = jnp.concatenate(
                [m_new[:, :h], jnp.maximum(m_new[:, h:], jnp.max(late, axis=0, keepdims=True))], axis=1)
            m_ref[idx] = m_new
            p_early = jnp.exp2(early - m_new).astype(BF16)
            p_late = jnp.exp2(late - m_new[:, h:]).astype(BF16)
            acc_ref[idx] = (acc_ref[idx] * jnp.exp2(m_old - m_new)
                            + jnp.dot(vt[:, :h], p_early, preferred_element_type=F32))
            acc_ref[idx, :, h:] = acc_ref[idx, :, h:] + jnp.dot(vt[:, h:], p_late,
                                                                preferred_element_type=F32)

    ahead = (lambda n: min(qi + n, nk - 1)) if peeled else (lambda n: jnp.minimum(qi + n, nk - 1))

    def diagonal_block_and_output(s_ref):
        has_next = not peeled or qi + 1 < nk
        if has_next and s_ref is not s0_ref:
            scores(0, s0_ref, q=ahead(1))
        absorb(qi, s_ref, True)
        if has_next and s_ref is s0_ref:
            scores(0, s0_ref, q=ahead(1))
        if not peeled or qi + 2 < nk:
            prepare(ahead(2), 0)
        a1, a2 = acc_ref[0], acc_ref[1]
        o_t = a1[:dv] * (1.0 / a1[dv:dv + 1]) - lam * (a2[:dv] * (1.0 / a2[dv:dv + 1]))
        o = _rms(o_t, axis=0).T * gsub_ref[...] * (1.0 - LAMBDA_INIT)
        rows = pl.ds(pl.multiple_of(qi * tile, tile), tile)
        o_ref[rows, :] = (o * _sigmoid(gate_ref[rows, :].astype(F32))).astype(o_ref.dtype)

    acc_ref[...] = jnp.zeros_like(acc_ref)
    m_ref[...] = jnp.full_like(m_ref, MASK_VALUE)
    if peeled and qi == 0:
        diagonal_block_and_output(s0_ref)
        return
    scores(1, sa_ref, diagonal=peeled and qi == 1)
    absorb(0, s0_ref, False)

    def pair(j2, carry):
        jb = 2 * j2 + 1
        scores(jb + 1, sb_ref)
        absorb(jb, sa_ref, False)
        scores(jb + 2, sa_ref)
        absorb(jb + 1, sb_ref, False)
        return carry

    def even_tail():
        scores(qi, sb_ref, diagonal=True)
        absorb(qi - 1, sa_ref, False)
        diagonal_block_and_output(sb_ref)

    def odd_tail():
        diagonal_block_and_output(sa_ref)

    if peeled:
        for j2 in range((qi - 1) // 2):
            pair(j2, 0)
        (even_tail if qi % 2 == 0 else odd_tail)()
    else:
        lax.fori_loop(0, (qi - 1) // 2, pair, 0)
        pl.when(qi % 2 == 0)(even_tail)
        pl.when(qi % 2 == 1)(odd_tail)


def _bf16_terms(x, n):
    terms, rest = [], jnp.asarray(x, F32)
    for _ in range(n):
        term = rest.astype(BF16).astype(F32)
        terms.append(term)
        rest = rest - term
    return terms


def _diff_attention(proj, g_q, g_k, g_sub, lq1, lk1, lq2, lk2, batch, seq, d):
    t = proj.shape[0]
    tile = ATT_T
    nq = seq // tile
    dh = d // (2 * DIFF_HEADS)
    w = 2 * dh
    assert w == LANES
    q_col, k_col, v_col, gate_col = 3 * d // w, 4 * d // w, 5 * d // w, 7 * d // w
    slopes = 2.0 ** -(jnp.arange(DIFF_HEADS, dtype=F32) + 1.0)
    slopes = jnp.broadcast_to(slopes[:, None, None], (DIFF_HEADS, 1, LANES))
    c1, c2, c3 = _bf16_terms(LOG2E, 3)
    qconst = jnp.zeros((dh,), F32).at[:6].set(jnp.stack([c1, c1, c2, c2, c3, c3]))
    zeros = jnp.zeros((dh,), F32)
    qconst_a = jnp.concatenate([zeros, qconst])[:, None]
    qconst_b = jnp.concatenate([qconst, zeros])[:, None]
    tile2 = lambda a: jnp.concatenate([a, a], axis=-1)
    assert tile % 256 == 0
    small = lambda n: pl.BlockSpec((1, n), lambda b, h: (0, 0))
    kernel = functools.partial(_attn_kernel, tile=tile, dh=dh, dv=w, nk=nq)
    return pl.pallas_call(
        kernel,
        out_shape=jax.ShapeDtypeStruct((t, d), BF16),
        grid=(batch, DIFF_HEADS),
        in_specs=[pl.BlockSpec((seq, w), lambda b, h: (b, q_col + h)),
                  pl.BlockSpec((seq, w), lambda b, h: (b, k_col + h)),
                  pl.BlockSpec((seq, w), lambda b, h: (b, v_col + h)),
                  pl.BlockSpec((seq, w), lambda b, h: (b, gate_col + h)),
                  pl.BlockSpec((1, 1, LANES), lambda b, h: (h, 0, 0)),
                  pl.BlockSpec((w, 1), lambda b, h: (0, 0)), pl.BlockSpec((w, 1), lambda b, h: (0, 0)),
                  small(w), small(w), small(w),
                  small(dh), small(dh), small(dh), small(dh)],
        out_specs=pl.BlockSpec((seq, w), lambda b, h: (b, h)),
        scratch_shapes=[pltpu.VMEM((nq, tile, w), BF16),
                        pltpu.VMEM((nq, tile, w), BF16),
                        pltpu.VMEM((nq, w + BF16_SUBLANES, tile), BF16),
                        pltpu.VMEM((nq, w, tile), BF16),
                        pltpu.VMEM((nq, w, tile), BF16),
                        pltpu.VMEM((2, tile + 8, tile), F32),
                        pltpu.VMEM((2, tile + 8, tile), F32),
                        pltpu.VMEM((2, tile + 8, tile), F32),
                        pltpu.VMEM((2, w + BF16_SUBLANES, tile), F32),
                        pltpu.VMEM((2, 1, tile), F32)],
        compiler_params=_params("arbitrary", "arbitrary"),
        name="diff_attn",
    )(proj, proj, proj, proj, slopes, qconst_a, qconst_b, tile2(g_q), tile2(g_k), g_sub,
      lq1, lk1, lq2, lk2)


def _out_kernel(x_ref, ret_ref, att_ref, mod_ref, wo_ref, g2_ref, wr_ref, br_ref,
                x1_ref, h2_ref, tw_ref, ld_ref, cnt_ref, *, tm, n_exp):
    m = mod_ref[0]
    merged = (ret_ref[...].astype(F32) + att_ref[...].astype(F32)).astype(BF16)
    x1 = x_ref[...] + m[2:3] * jnp.dot(merged, wo_ref[...], preferred_element_type=F32)
    x1_ref[...] = x1
    h2 = _rms(x1) * g2_ref[...] * (1.0 + m[4:5]) + m[3:4]
    hi = h2.astype(BF16)
    h2_ref[...] = hi

    lo = (h2 - hi.astype(F32)).astype(BF16)
    both = jnp.dot(hi, wr_ref[...], preferred_element_type=F32)
    logits = (both[:, :LANES] + both[:, LANES:]
              + jnp.dot(lo, wr_ref[:, :LANES], preferred_element_type=F32) + br_ref[...])
    lt = logits.T[:n_exp]

    erow = lax.broadcasted_iota(I32, (n_exp, tm), 0)
    vals, sels = [], []
    for k in range(TOP_K):
        mx = jnp.max(lt, axis=0, keepdims=True)
        idx = jnp.min(jnp.where(lt == mx, erow, n_exp), axis=0, keepdims=True)
        sel = erow == idx
        vals.append(mx)
        sels.append(sel)
        lt = jnp.where(sel, MASK_VALUE, lt)
    exps = [jnp.exp(v - vals[0]) for v in vals]
    inv = 1.0 / (exps[0] + exps[1] + exps[2] + exps[3])
    for k in range(TOP_K):
        tw_ref[k:k + 1, :] = exps[k] * inv

    chosen = jnp.where(sels[0] | sels[1] | sels[2] | sels[3], 1.0, 0.0)
    before = (lax.broadcasted_iota(I32, (tm, tm), 0) < lax.broadcasted_iota(I32, (tm, tm), 1))
    prefix = jnp.dot(chosen.astype(BF16), jnp.where(before, 1.0, 0.0).astype(BF16),
                     preferred_element_type=F32)
    count = jnp.sum(chosen, axis=1, keepdims=True)
    cnt_ref[...] = count
    padded = jnp.ceil(count * (1.0 / RUN_ALIGN)) * RUN_ALIGN
    below = (lax.broadcasted_iota(I32, (n_exp, n_exp), 1) < lax.broadcasted_iota(I32, (n_exp, n_exp), 0))
    run_start = jnp.dot(jnp.where(below, 1.0, 0.0).astype(BF16),
                        jnp.broadcast_to(padded, (n_exp, LANES)).astype(BF16),
                        preferred_element_type=F32)[:, :1]
    pos = prefix + run_start
    for k in range(TOP_K):
        ld_ref[k:k + 1, :] = jnp.sum(jnp.where(sels[k], pos, 0.0), axis=0, keepdims=True).astype(I32)


def _out_router(x2, ret, att, mod3, w_out_bf16, g2, wr_split, br_pad, seq, n_exp):
    t, d = x2.shape
    tm = TOK_TM
    row = lambda i: (i, 0)
    const = lambda i: (0, 0)
    kernel = functools.partial(_out_kernel, tm=tm, n_exp=n_exp)
    return pl.pallas_call(
        kernel,
        out_shape=(jax.ShapeDtypeStruct((t, d), F32),
                   jax.ShapeDtypeStruct((t, d), BF16),
                   jax.ShapeDtypeStruct((TOP_K, t), F32),
                   jax.ShapeDtypeStruct((TOP_K, t), I32),
                   jax.ShapeDtypeStruct((t // tm, n_exp, 1), F32)),
        grid=(t // tm,),
        in_specs=[pl.BlockSpec((tm, d), row), pl.BlockSpec((tm, d), row), pl.BlockSpec((tm, d), row),
                  pl.BlockSpec((1, 6, d), lambda i: (i * tm // seq, 0, 0)),
                  pl.BlockSpec((d, d), const), pl.BlockSpec((1, d), const),
                  pl.BlockSpec((d, 2 * LANES), const), pl.BlockSpec((1, LANES), const)],
        out_specs=(pl.BlockSpec((tm, d), row), pl.BlockSpec((tm, d), row),
                   pl.BlockSpec((TOP_K, tm), lambda i: (0, i)),
                   pl.BlockSpec((TOP_K, tm), lambda i: (0, i)),
                   pl.BlockSpec((None, n_exp, 1), lambda i: (i, 0, 0))),
        compiler_params=_params("arbitrary"),
        name="out_router",
    )(x2, ret, att, mod3, w_out_bf16, g2, wr_split, br_pad)


def _meta_stride(n_exp):
    return 2 * len(SLAB_SIZES) * n_exp + len(SLAB_SIZES)


def _run_slabs(meta_ref, tile, n_exp, local_ref, global_ref, sem, *, to_global, wait):
    n_sizes = len(SLAB_SIZES)
    base = tile * _meta_stride(n_exp)
    for c, size in enumerate(SLAB_SIZES):
        def one(p, carry, c=c, size=size):
            if wait:
                local = glob = 0
            else:
                local = pl.multiple_of(meta_ref[base + c * n_exp + p], RUN_ALIGN)
                glob = pl.multiple_of(meta_ref[base + (n_sizes + c) * n_exp + p], RUN_ALIGN)
            loc = local_ref.at[pl.ds(local, size)]
            glo = global_ref.at[pl.ds(glob, size)]
            cp = pltpu.make_async_copy(loc, glo, sem) if to_global else pltpu.make_async_copy(glo, loc, sem)
            if wait:
                cp.wait()
            else:
                cp.start()
            return carry

        lax.fori_loop(0, meta_ref[base + 2 * n_sizes * n_exp + c], one, 0)


def _dispatch_kernel(meta_ref, tail_ref, ld_ref, h_ref, xs_ref, stage_ref, zero_ref, sems, tail_sem,
                     *, tm, n_exp, n_tiles, n_blocks, stage_rows):
    i = pl.program_id(0)
    slot = i & 1
    stage = stage_ref.at[slot]

    def tails(wait):
        def unused_block(b, carry):
            cp = pltpu.make_async_copy(
                zero_ref, xs_ref.at[pl.ds(pl.multiple_of(b * EXP_BLK, EXP_BLK), EXP_BLK)], tail_sem)
            if wait:
                cp.wait()
            else:
                cp.start()
            return carry

        lax.fori_loop(tail_ref[2 * n_exp] // EXP_BLK, n_blocks, unused_block, 0)

        def per_expert(e, carry):
            start, n = tail_ref[e], tail_ref[n_exp + e]
            off = 0
            for size in TAIL_SIZES:
                take = n & size

                @pl.when(take != 0)
                def _(off=off, size=size):
                    cp = pltpu.make_async_copy(
                        zero_ref.at[pl.ds(0, size)],
                        xs_ref.at[pl.ds(pl.multiple_of(start + off, RUN_ALIGN), size)], tail_sem)
                    if wait:
                        cp.wait()
                    else:
                        cp.start()

                off = off + take
            return carry

        lax.fori_loop(0, n_exp, per_expert, 0)

    @pl.when(i == 0)
    def _():
        zero_ref[...] = jnp.zeros_like(zero_ref)
        tails(False)
        tails(True)

    ld = ld_ref[...]
    h = h_ref[...]
    for c in range(stage_rows // GATHER_RC):
        r = lax.broadcasted_iota(I32, (GATHER_RC, tm), 0) + c * GATHER_RC
        hit = jnp.where(r == ld[0:1], 1.0, jnp.where(r == ld[1:2], 1.0,
                        jnp.where(r == ld[2:3], 1.0, jnp.where(r == ld[3:4], 1.0, 0.0))))
        stage[c * GATHER_RC:(c + 1) * GATHER_RC, :] = jnp.dot(
            hit.astype(BF16), h, preferred_element_type=F32).astype(BF16)

    _run_slabs(meta_ref, i, n_exp, stage, xs_ref, sems.at[slot], to_global=True, wait=False)

    @pl.when(i > 0)
    def _():
        _run_slabs(meta_ref, i - 1, n_exp, stage_ref.at[1 - slot], xs_ref, sems.at[1 - slot],
                   to_global=True, wait=True)

    @pl.when(i == n_tiles - 1)
    def _():
        _run_slabs(meta_ref, i, n_exp, stage, xs_ref, sems.at[slot], to_global=True, wait=True)


def _dispatch(meta, tail, ldest, h2, n_rows, n_exp):
    t, d = h2.shape
    tm = TOK_TM
    n_tiles = t // tm
    stage_rows = _stage_rows(n_exp)
    kernel = functools.partial(_dispatch_kernel, tm=tm, n_exp=n_exp, n_tiles=n_tiles,
                               n_blocks=n_rows // EXP_BLK, stage_rows=stage_rows)
    grid_spec = pltpu.PrefetchScalarGridSpec(
        num_scalar_prefetch=2,
        grid=(n_tiles,),
        in_specs=[pl.BlockSpec((TOP_K, tm), lambda i, m, tl: (0, i)),
                  pl.BlockSpec((tm, d), lambda i, m, tl: (i, 0))],
        out_specs=pl.BlockSpec(memory_space=pl.ANY),
        scratch_shapes=[pltpu.VMEM((2, stage_rows, d), BF16),
                        pltpu.VMEM((EXP_BLK, d), BF16),
                        pltpu.SemaphoreType.DMA((2,)),
                        pltpu.SemaphoreType.DMA],
    )
    return pl.pallas_call(
        kernel,
        out_shape=jax.ShapeDtypeStruct((n_rows, d), BF16),
        grid_spec=grid_spec,
        compiler_params=_params("arbitrary"),
        name="dispatch",
    )(meta, tail, ldest, h2)


def _expert_kernel(region_ref, xs_ref, w1_ref, b1_ref, w2_ref, b2_ref, eo_ref,
                   w1f_ref, w2f_ref, w1b_ref, w2b_ref, x_buf, o_buf, w_sems, x_sems, o_sems,
                   *, f, blk, n_exp, n_blocks):
    e = pl.program_id(0)
    first, count = region_ref[e], region_ref[n_exp + e]
    w_slot = e & 1
    d = w1f_ref.shape[1]
    r1, r2 = d // W_CHUNKS, f // W_CHUNKS

    def rows(j):
        return pl.ds(pl.multiple_of((first + j) * blk, blk), blk)

    def x_copy(j, slot):
        return pltpu.make_async_copy(xs_ref.at[rows(j)], x_buf.at[slot], x_sems.at[slot])

    def o_copy(j, slot):
        return pltpu.make_async_copy(o_buf.at[slot], eo_ref.at[rows(j)], o_sems.at[slot])

    def w_chunk(expert, c, slot, wait):
        for src, dst, r in ((w1_ref, w1f_ref, r1), (w2_ref, w2f_ref, r2)):
            piece = pl.ds(c * r if isinstance(c, int) else pl.multiple_of(c * r, r), r)
            cp = pltpu.make_async_copy(src.at[expert, piece], dst.at[slot, piece], w_sems.at[slot])
            if wait:
                cp.wait()
            else:
                cp.start()

    @pl.when(e == 0)
    def _():
        for c in range(W_CHUNKS):
            w_chunk(0, c, 0, False)

    @pl.when(count > 0)
    def _():
        x_copy(0, 0).start()

    for c in range(W_CHUNKS):
        w_chunk(e, c, w_slot, True)

    @pl.when(count > 0)
    def _():
        w1b_ref[...] = w1f_ref[w_slot].astype(BF16)
        w2b_ref[...] = w2f_ref[w_slot].astype(BF16)

    def next_chunk(c):
        @pl.when(e + 1 < n_exp)
        def _():
            w_chunk(e + 1, c, 1 - w_slot, False)

    def block(j, carry):
        slot = j & 1
        x_copy(j, slot).wait()

        @pl.when(j + 1 < count)
        def _():
            x_copy(j + 1, 1 - slot).start()

        @pl.when(j < W_CHUNKS)
        def _():
            next_chunk(j)

        @pl.when(j >= 2)
        def _():
            o_copy(j - 2, slot).wait()

        gu = jnp.dot(x_buf[slot], w1b_ref[...], preferred_element_type=F32) + b1_ref[0]
        g = jnp.minimum(gu[:, :f], SWIGLU_LIMIT)
        u = jnp.clip(gu[:, f:], -SWIGLU_LIMIT, SWIGLU_LIMIT)
        a = g * _sigmoid(SWIGLU_ALPHA * g) * (u + 1.0)
        o_buf[slot] = (jnp.dot(a.astype(BF16), w2b_ref[...], preferred_element_type=F32)
                       + b2_ref[0]).astype(o_buf.dtype)
        o_copy(j, slot).start()
        return carry

    lax.fori_loop(0, count, block, 0)
    lax.fori_loop(jnp.minimum(count, W_CHUNKS), W_CHUNKS, lambda c, carry: (next_chunk(c), carry)[1], 0)

    @pl.when(count >= 2)
    def _():
        o_copy(count - 2, count & 1).wait()

    @pl.when(count >= 1)
    def _():
        o_copy(count - 1, (count - 1) & 1).wait()

    @pl.when(e == n_exp - 1)
    def _():
        o_buf[0] = jnp.zeros(o_buf.shape[1:], o_buf.dtype)
        used = first + count

        def fill(j, wait):
            cp = pltpu.make_async_copy(
                o_buf.at[0], eo_ref.at[pl.ds(pl.multiple_of(j * blk, blk), blk)], o_sems.at[0])
            if wait:
                cp.wait()
            else:
                cp.start()

        lax.fori_loop(used, n_blocks, lambda j, c: (fill(j, False), c)[1], 0)
        lax.fori_loop(used, n_blocks, lambda j, c: (fill(j, True), c)[1], 0)


def _experts(region, xs, w1, b1, w2, b2):
    n_rows, d = xs.shape
    n_exp, _, f2 = w1.shape
    f = f2 // 2
    blk = EXP_BLK
    kernel = functools.partial(_expert_kernel, f=f, blk=blk, n_exp=n_exp, n_blocks=n_rows // blk)
    grid_spec = pltpu.PrefetchScalarGridSpec(
        num_scalar_prefetch=1,
        grid=(n_exp,),
        in_specs=[pl.BlockSpec(memory_space=pl.ANY),
                  pl.BlockSpec(memory_space=pl.ANY),
                  pl.BlockSpec((1, 1, f2), lambda e, r: (e, 0, 0)),
                  pl.BlockSpec(memory_space=pl.ANY),
                  pl.BlockSpec((1, 1, d), lambda e, r: (e, 0, 0))],
        out_specs=pl.BlockSpec(memory_space=pl.ANY),
        scratch_shapes=[pltpu.VMEM((2, d, f2), F32), pltpu.VMEM((2, f, d), F32),
                        pltpu.VMEM((d, f2), BF16), pltpu.VMEM((f, d), BF16),
                        pltpu.VMEM((2, blk, d), BF16), pltpu.VMEM((2, blk, d), BF16),
                        pltpu.SemaphoreType.DMA((2,)), pltpu.SemaphoreType.DMA((2,)),
                        pltpu.SemaphoreType.DMA((2,))],
    )
    return pl.pallas_call(
        kernel,
        out_shape=jax.ShapeDtypeStruct((n_rows, d), BF16),
        grid_spec=grid_spec,
        compiler_params=_params("arbitrary"),
        name="experts",
    )(region, xs, w1, b1, w2, b2)


def _combine_kernel(meta_ref, eo_ref, ld_ref, w_ref, x1_ref, mod_ref, o_ref, stage_ref, sems,
                    *, tm, n_exp, n_tiles, stage_rows):
    i = pl.program_id(0)
    slot = i & 1

    @pl.when(i == 0)
    def _():
        stage_ref[...] = jnp.zeros_like(stage_ref)
        _run_slabs(meta_ref, 0, n_exp, stage_ref.at[0], eo_ref, sems.at[0], to_global=False, wait=False)

    @pl.when(i + 1 < n_tiles)
    def _():
        _run_slabs(meta_ref, i + 1, n_exp, stage_ref.at[1 - slot], eo_ref, sems.at[1 - slot],
                   to_global=False, wait=False)

    ld = ld_ref[...]
    w = w_ref[...]

    _run_slabs(meta_ref, i, n_exp, stage_ref.at[slot], eo_ref, sems.at[slot], to_global=False, wait=True)
    stage = stage_ref.at[slot]
    y = None
    for c in range(stage_rows // COMBINE_KC):
        r0 = c * COMBINE_KC
        r = lax.broadcasted_iota(I32, (COMBINE_KC, tm), 0) + r0
        sel = jnp.where(r == ld[0:1], w[0:1], jnp.where(r == ld[1:2], w[1:2],
                        jnp.where(r == ld[2:3], w[2:3], jnp.where(r == ld[3:4], w[3:4], 0.0))))
        part = lax.dot_general(sel.astype(BF16), stage[r0:r0 + COMBINE_KC, :], (((0,), (0,)), ((), ())),
                               preferred_element_type=F32)
        y = part if y is None else y + part
    o_ref[...] = x1_ref[...] + mod_ref[0][5:6] * y


def _combine(meta, eo, ld_tok, w_tok, x1, mod3, seq, n_exp):
    t, d = x1.shape
    tm = TOK_TM
    n_tiles = t // tm
    stage_rows = _stage_rows(n_exp)
    kernel = functools.partial(_combine_kernel, tm=tm, n_exp=n_exp, n_tiles=n_tiles, stage_rows=stage_rows)
    grid_spec = pltpu.PrefetchScalarGridSpec(
        num_scalar_prefetch=1,
        grid=(n_tiles,),
        in_specs=[pl.BlockSpec(memory_space=pl.ANY),
                  pl.BlockSpec((TOP_K, tm), lambda i, m: (0, i)),
                  pl.BlockSpec((TOP_K, tm), lambda i, m: (0, i)),
                  pl.BlockSpec((tm, d), lambda i, m: (i, 0)),
                  pl.BlockSpec((1, 6, d), lambda i, m: (i * tm // seq, 0, 0))],
        out_specs=pl.BlockSpec((tm, d), lambda i, m: (i, 0)),
        scratch_shapes=[pltpu.VMEM((2, stage_rows, d), BF16),
                        pltpu.SemaphoreType.DMA((2,))],
    )
    return pl.pallas_call(
        kernel,
        out_shape=jax.ShapeDtypeStruct((t, d), F32),
        grid_spec=grid_spec,
        compiler_params=_params("arbitrary"),
        name="combine",
    )(meta, eo, ld_tok, w_tok, x1, mod3)


def _routing_tables(tile_cnt, n_tokens):
    n_tiles, n_exp = tile_cnt.shape
    blk = EXP_BLK
    run = _round_up(tile_cnt, RUN_ALIGN)
    local = jnp.cumsum(run, axis=1) - run
    size = jnp.sum(run, axis=0)
    region = _round_up(size, blk)
    region_end = jnp.cumsum(region)
    region_start = region_end - region
    glob = region_start[None, :] + jnp.cumsum(run, axis=0) - run
    sizes = jnp.asarray(SLAB_SIZES, I32)
    has = (run[:, :, None] & sizes) != 0
    before = run[:, :, None] & ~(2 * sizes - 1)
    place = jnp.cumsum(has, axis=1) - has
    hit = has[..., None] & (place[..., None] == jnp.arange(n_exp, dtype=I32))
    listed = lambda start: jnp.sum(
        jnp.where(hit, (start[:, :, None] + before)[..., None], 0), axis=1).reshape(n_tiles, -1)
    meta = jnp.concatenate([listed(local), listed(glob), jnp.sum(has, axis=1)],
                           axis=1).reshape(-1).astype(I32)
    tail = jnp.concatenate([region_start + size, region - size, region_end[-1:]]).astype(I32)
    n_rows = _round_up(n_tokens * TOP_K + n_tiles * n_exp * RUN_ALIGN + n_exp * blk, blk)
    region_blocks = jnp.concatenate([region_start // blk, region // blk]).astype(I32)
    return meta, tail, region_blocks, n_rows


def kernel(x, c, w_ada, b_ada, g_norm1, w_in, g_ret_gn, b_ret_gn, g_qnorm, g_knorm,
           lambda_q1, lambda_k1, lambda_q2, lambda_k2, g_diff_subln, w_out, g_norm2,
           w_router, b_router, w_expert_in, b_expert_in, w_expert_out, b_expert_out):
    batch, seq, d = x.shape
    depth = w_ada.shape[0]
    assert depth == 1
    t = batch * seq
    n_exp = w_router.shape[-1]
    l = 0

    mod = _adaln_mod(c, w_ada[l], b_ada[l][None, :])
    mod3 = mod.reshape(batch, 6, d)
    x2 = x.reshape(t, d)

    proj = _in_proj(x2, mod3, g_norm1[l][None, :], w_in[l].astype(BF16), seq)
    ret = _retention(proj, g_ret_gn[l][None, :], b_ret_gn[l][None, :], batch, seq, d)
    att = _diff_attention(proj, g_qnorm[l][None, :], g_knorm[l][None, :], g_diff_subln[l][None, :],
                          lambda_q1[l][None, :], lambda_k1[l][None, :],
                          lambda_q2[l][None, :], lambda_k2[l][None, :], batch, seq, d)

    wr = w_router[l]
    wr_hi = wr.astype(BF16)
    wr_lo = (wr - wr_hi.astype(F32)).astype(BF16)
    pad = lambda a: jnp.pad(a, ((0, 0), (0, LANES - n_exp)))
    wr_split = jnp.concatenate([pad(wr_hi), pad(wr_lo)], axis=1)
    br_pad = jnp.pad(b_router[l][None, :], ((0, 0), (0, LANES - n_exp)), constant_values=MASK_VALUE)
    x1, h2, top_w, ldest, tile_cnt = _out_router(
        x2, ret, att, mod3, w_out[l].astype(BF16), g_norm2[l][None, :], wr_split, br_pad, seq, n_exp)

    meta, tail, region_blocks, n_rows = _routing_tables(tile_cnt[:, :, 0].astype(I32), t)
    xs = _dispatch(meta, tail, ldest, h2, n_rows, n_exp)
    eo = _experts(region_blocks, xs, w_expert_in[l], b_expert_in[l][:, None, :],
                  w_expert_out[l], b_expert_out[l][:, None, :])
    out = _combine(meta, eo, ldest, top_w, x1, mod3, seq, n_exp)
    return out.reshape(batch, seq, d)
```

```python
import functools
import itertools
import math

import jax
import jax.numpy as jnp
from jax import lax
from jax.experimental import pallas as pl
from jax.experimental.pallas import tpu as pltpu

F32 = jnp.float32
BF16 = jnp.bfloat16
I32 = jnp.int32

EPS = 1e-6
LOG2E = 1.4426950216293335
MASK_VALUE = -1e30
LANES = 128
BF16_SUBLANES = 16
VMEM_LIMIT_BYTES = 48 * 1024 * 1024

RET_HEADS = 4
DIFF_HEADS = 8
TOP_K = 4
SWIGLU_LIMIT = 7.0
SWIGLU_ALPHA = 1.702
LAMBDA_INIT = 0.8 - 0.6 * math.exp(-0.3 * 0)

IN_TM, IN_TN = 1024, 2048
RET_TC = 256
RET_CHUNKS_PER_STEP = 2
ATT_T = 512
ATT_PEELED_TILES = 8
TOK_TM = 512
EXP_BLK = 512
RUN_ALIGN = BF16_SUBLANES
GATHER_RC = 256
COMBINE_KC = 512
W_CHUNKS = 8


def _round_up(n, m):
    return (n + m - 1) // m * m


def _stage_rows(n_exp):
    return _round_up(TOK_TM * TOP_K + n_exp * (RUN_ALIGN - 1), max(GATHER_RC, COMBINE_KC))


SLAB_SIZES = tuple(TOK_TM >> s for s in range(TOK_TM.bit_length()) if TOK_TM >> s >= RUN_ALIGN)
TAIL_SIZES = tuple(s for s in SLAB_SIZES if s < EXP_BLK)


def _params(*sem):
    return pltpu.CompilerParams(dimension_semantics=sem, vmem_limit_bytes=VMEM_LIMIT_BYTES)


def _sigmoid(x):
    return 0.5 * jnp.tanh(0.5 * x) + 0.5


def _rms(x, axis=-1):
    return x * lax.rsqrt(jnp.mean(x * x, axis=axis, keepdims=True) + EPS)


def _mod_kernel(c_ref, w_ref, b_ref, o_ref):
    c = c_ref[...]
    s = c * _sigmoid(c)
    o_ref[...] = jnp.dot(s, w_ref[...], preferred_element_type=F32,
                         precision=lax.Precision.HIGHEST) + b_ref[...]


def _adaln_mod(c, w, b):
    bn, d = c.shape
    n = w.shape[1]
    tn = d
    return pl.pallas_call(
        _mod_kernel,
        out_shape=jax.ShapeDtypeStruct((bn, n), F32),
        grid=(n // tn,),
        in_specs=[pl.BlockSpec((bn, d), lambda j: (0, 0)),
                  pl.BlockSpec((d, tn), lambda j: (0, j)),
                  pl.BlockSpec((1, tn), lambda j: (0, j))],
        out_specs=pl.BlockSpec((bn, tn), lambda j: (0, j)),
        compiler_params=_params("arbitrary"),
        name="adaln_mod",
    )(c, w, b)


def _inproj_kernel(x_ref, mod_ref, g_ref, w_ref, o_ref, h_ref):
    @pl.when(pl.program_id(1) == 0)
    def _():
        m = mod_ref[0]
        h = _rms(x_ref[...]) * g_ref[...] * (1.0 + m[1:2]) + m[0:1]
        h_ref[...] = h.astype(BF16)

    o_ref[...] = jnp.dot(h_ref[...], w_ref[...], preferred_element_type=F32).astype(o_ref.dtype)


def _in_proj(x2, mod3, g1, w_in_bf16, seq):
    t, d = x2.shape
    n = w_in_bf16.shape[1]
    tm, tn = IN_TM, IN_TN
    return pl.pallas_call(
        _inproj_kernel,
        out_shape=jax.ShapeDtypeStruct((t, n), BF16),
        grid=(t // tm, n // tn),
        in_specs=[pl.BlockSpec((tm, d), lambda i, j: (i, 0)),
                  pl.BlockSpec((1, 6, d), lambda i, j: (i * tm // seq, 0, 0)),
                  pl.BlockSpec((1, d), lambda i, j: (0, 0)),
                  pl.BlockSpec((d, tn), lambda i, j: (0, j))],
        out_specs=pl.BlockSpec((tm, tn), lambda i, j: (i, j)),
        scratch_shapes=[pltpu.VMEM((tm, d), BF16)],
        compiler_params=_params("arbitrary", "arbitrary"),
        name="in_proj",
    )(x2, mod3, g1, w_in_bf16)


def _ret_kernel(q_ref, k_ref, v_ref, rg_ref, mg_ref, gng_ref, gnb_ref, o_ref, r_ref, decay_ref,
                *, dk, dv, tc):
    scale = dk ** -0.5
    log_gs = [math.log(1.0 - 2.0 ** (-5.0 - h)) for h in range(RET_HEADS)]

    @pl.when(pl.program_id(1) == 0)
    def _():
        r_ref[...] = jnp.zeros_like(r_ref)
        row = lax.broadcasted_iota(I32, (tc, tc), 0)
        col = lax.broadcasted_iota(I32, (tc, tc), 1)
        rel = (row - col).astype(F32)
        for h, log_g in enumerate(log_gs):
            decay_ref[h] = jnp.where(rel >= 0, jnp.exp(log_g * jnp.maximum(rel, 0.0)), 0.0) * scale

    pos = lax.broadcasted_iota(I32, (tc, 1), 0).astype(F32)
    for sub, (h, log_g) in itertools.product(range(RET_CHUNKS_PER_STEP), enumerate(log_gs)):
        rows = slice(sub * tc, (sub + 1) * tc)
        decay = decay_ref[h]
        xi = jnp.exp(log_g * (pos + 1.0))
        zeta = jnp.exp(log_g * (tc - 1.0 - pos))
        g_chunk = math.exp(log_g * tc)
        q = q_ref[rows, h * dk:(h + 1) * dk]
        k = k_ref[rows, h * dk:(h + 1) * dk]
        v = v_ref[rows, h * dv:(h + 1) * dv]
        s = lax.dot_general(q, k, (((1,), (1,)), ((), ())), preferred_element_type=F32)
        y = jnp.dot((s * decay).astype(BF16), v, preferred_element_type=F32)
        state = r_ref[h]
        y = y + jnp.dot(q, state.astype(BF16), preferred_element_type=F32) * xi
        vz = (v.astype(F32) * zeta).astype(BF16)
        kv = lax.dot_general(k, vz, (((0,), (0,)), ((), ())), preferred_element_type=F32)
        r_ref[h] = state * g_chunk + kv * scale

        mu = jnp.mean(y, axis=-1, keepdims=True)
        yc = y - mu
        var = jnp.mean(yc * yc, axis=-1, keepdims=True)
        sl = slice(h * dv, (h + 1) * dv)
        yn = yc * lax.rsqrt(var + EPS) * gng_ref[:, sl] + gnb_ref[:, sl]
        rg = rg_ref[rows, sl].astype(F32)
        gate = _sigmoid(mg_ref[rows, sl].astype(F32))
        o_ref[rows, sl] = (rg * _sigmoid(rg) * yn * gate).astype(o_ref.dtype)


def _retention(proj, gn_g, gn_b, batch, seq, d):
    t = proj.shape[0]
    tc = RET_TC
    tr = tc * RET_CHUNKS_PER_STEP
    nc = seq // tr
    dk = d // (2 * RET_HEADS)
    dv = d // RET_HEADS
    qk_w = RET_HEADS * dk
    row = lambda b, c: b * nc + c
    kernel = functools.partial(_ret_kernel, dk=dk, dv=dv, tc=tc)
    return pl.pallas_call(
        kernel,
        out_shape=jax.ShapeDtypeStruct((t, d), BF16),
        grid=(batch, nc),
        in_specs=[pl.BlockSpec((tr, qk_w), lambda b, c: (row(b, c), 0)),
                  pl.BlockSpec((tr, qk_w), lambda b, c: (row(b, c), 1)),
                  pl.BlockSpec((tr, d), lambda b, c: (row(b, c), 1)),
                  pl.BlockSpec((tr, d), lambda b, c: (row(b, c), 2)),
                  pl.BlockSpec((tr, d), lambda b, c: (row(b, c), 6)),
                  pl.BlockSpec((1, d), lambda b, c: (0, 0)),
                  pl.BlockSpec((1, d), lambda b, c: (0, 0))],
        out_specs=pl.BlockSpec((tr, d), lambda b, c: (row(b, c), 0)),
        scratch_shapes=[pltpu.VMEM((RET_HEADS, dk, dv), F32), pltpu.VMEM((RET_HEADS, tc, tc), F32)],
        compiler_params=_params("arbitrary", "arbitrary"),
        name="retention",
    )(proj, proj, proj, proj, proj, gn_g, gn_b)


def _halves_rms(x, lo, dh):
    sq = x * x
    s_lo = jnp.sum(jnp.where(lo, sq, 0.0), axis=-1, keepdims=True)
    s_hi = jnp.sum(jnp.where(lo, 0.0, sq), axis=-1, keepdims=True)
    inv = jnp.where(lo, lax.rsqrt(s_lo / dh + EPS), lax.rsqrt(s_hi / dh + EPS))
    return x * inv


def _attn_kernel(q_ref, k_ref, v_ref, gate_ref, slope_ref, qca_ref, qcb_ref, gq_ref, gk_ref, gsub_ref,
                 lq1_ref, lk1_ref, lq2_ref, lk2_ref,
                 o_ref, ka_ref, kb_ref, vt_ref, qa_ref, qb_ref, s0_ref, sa_ref, sb_ref, acc_ref, m_ref,
                 *, tile, dh, dv, nk):
    lane = lax.broadcasted_iota(I32, (tile, 2 * dh), 1)
    lo = lane < dh

    slope = slope_ref[0]
    sub = lax.broadcasted_iota(I32, (tile, 2 * dh), 0)
    off = lane & (dh - 1)
    hi_lane = (off < 6) & ((off & 1) == 0)
    lo_lane = (off < 6) & ((off & 1) == 1)
    bias0 = jnp.where(hi_lane, ((sub >> 8) << 8).astype(F32),
                      jnp.where(lo_lane, (sub & 255).astype(F32), 0.0)) * slope
    bias_step = jnp.where(hi_lane, float(tile), 0.0) * slope
    ones_rows = jnp.where(lax.broadcasted_iota(I32, (BF16_SUBLANES, tile), 0) == 0, 1.0, 0.0)
    lo_t = lax.broadcasted_iota(I32, (2 * dh, tile), 0) < dh

    def prepare(c, carry):
        r0 = c * tile if isinstance(c, int) else pl.multiple_of(c * tile, tile)
        kn = _halves_rms(k_ref[pl.ds(r0, tile), :].astype(F32), lo, dh) * gk_ref[...]
        bias = bias0 + lax.convert_element_type(c, F32) * bias_step
        ka_ref[c] = jnp.where(lo, kn, bias).astype(BF16)
        kb_ref[c] = jnp.where(lo, bias, kn).astype(BF16)
        vt_ref[c, :dv, :] = v_ref[pl.ds(r0, tile), :].astype(F32).T.astype(BF16)
        vt_ref[c, dv:, :] = ones_rows.astype(BF16)
        qn = (_halves_rms(q_ref[pl.ds(r0, tile), :].astype(F32), lo, dh) * gq_ref[...]
              * (dh ** -0.5 * LOG2E))
        qn_t = qn.T
        qa_ref[c] = jnp.where(lo_t, qn_t, qca_ref[...]).astype(BF16)
        qb_ref[c] = jnp.where(lo_t, qcb_ref[...], qn_t).astype(BF16)
        return carry

    for c in range(min(2, nk)):
        prepare(c, 0)
    lam = (jnp.exp(jnp.sum(lq1_ref[...] * lk1_ref[...], axis=-1, keepdims=True))
           - jnp.exp(jnp.sum(lq2_ref[...] * lk2_ref[...], axis=-1, keepdims=True)) + LAMBDA_INIT)

    refs = (gate_ref, gsub_ref, o_ref, ka_ref, kb_ref, vt_ref, qa_ref, qb_ref,
            s0_ref, sa_ref, sb_ref, acc_ref, m_ref)
    _attn_scores(ka_ref, kb_ref, qa_ref, qb_ref, 0, 0, s0_ref, tile, diagonal=True)
    n_peeled = min(ATT_PEELED_TILES, nk)
    for qi in range(n_peeled):
        _attn_q_tile(qi, lam, prepare, *refs, tile=tile, dv=dv, nk=nk, peeled=True)

    def q_tile(qi, carry):
        _attn_q_tile(jnp.asarray(qi, I32), lam, prepare, *refs, tile=tile, dv=dv, nk=nk, peeled=False)
        return carry

    lax.fori_loop(n_peeled, nk, q_tile, 0)


def _attn_scores(ka_ref, kb_ref, qa_ref, qb_ref, jb, q, s_ref, tile, diagonal=False):
    h = tile // 2
    for idx, (k_ref, q_ref) in enumerate(((ka_ref, qa_ref), (kb_ref, qb_ref))):
        if diagonal:
            s_ref[idx, :h, :] = jnp.dot(k_ref[jb, :h, :], q_ref[q], preferred_element_type=F32)
            s_ref[idx, h:tile, h:] = jnp.dot(k_ref[jb, h:, :], q_ref[q, :, h:], preferred_element_type=F32)
        else:
            s = jnp.dot(k_ref[jb], q_ref[q], preferred_element_type=F32)
            s_ref[idx, :tile, :] = s
            s_ref[idx, tile:tile + 1, :] = jnp.max(s, axis=0, keepdims=True)


def _attn_q_tile(qi, lam, prepare, gate_ref, gsub_ref, o_ref, ka_ref, kb_ref, vt_ref, qa_ref, qb_ref,
                 s0_ref, sa_ref, sb_ref, acc_ref, m_ref, *, tile, dv, nk, peeled):
    def scores(jb, s_ref, q=qi, diagonal=False):
        _attn_scores(ka_ref, kb_ref, qa_ref, qb_ref, jb, q, s_ref, tile, diagonal)

    def absorb(jb, s_ref, masked):
        vt = vt_ref[jb]
        if masked:
            return absorb_diagonal(s_ref, vt)
        for idx in range(2):
            m_old = m_ref[idx]
            m_new = jnp.maximum(m_old, s_ref[idx, tile:tile + 1, :])
            m_ref[idx] = m_new
            p = jnp.exp2(s_ref[idx, :tile, :] - m_new).astype(BF16)
            acc_ref[idx] = (acc_ref[idx] * jnp.exp2(m_old - m_new)
                            + jnp.dot(vt, p, preferred_element_type=F32))

    def absorb_diagonal(s_ref, vt):
        h = tile // 2
        future = lax.broadcasted_iota(I32, (h, h), 0) > lax.broadcasted_iota(I32, (h, h), 1)
        for idx in range(2):
            early = s_ref[idx, :h, :]
            early = jnp.concatenate([jnp.where(future, MASK_VALUE, early[:, :h]), early[:, h:]], axis=1)
            late = jnp.where(future, MASK_VALUE, s_ref[idx, h:tile, h:])
            m_old = m_ref[idx]
            m_new = jnp.maximum(m_old, jnp.max(early, axis=0, keepdims=True))
            m_new = jnp.concatenate(
                [m_new[:, :h], jnp.maximum(m_new[:, h:], jnp.max(late, axis=0, keepdims=True))], axis=1)
            m_ref[idx] = m_new
            p_early = jnp.exp2(early - m_new).astype(BF16)
            p_late = jnp.exp2(late - m_new[:, h:]).astype(BF16)
            acc_ref[idx] = (acc_ref[idx] * jnp.exp2(m_old - m_new)
                            + jnp.dot(vt[:, :h], p_early, preferred_element_type=F32))
            acc_ref[idx, :, h:] = acc_ref[idx, :, h:] + jnp.dot(vt[:, h:], p_late,
                                                                preferred_element_type=F32)

    ahead = (lambda n: min(qi + n, nk - 1)) if peeled else (lambda n: jnp.minimum(qi + n, nk - 1))

    def diagonal_block_and_output(s_ref):
        has_next = not peeled or qi + 1 < nk
        if has_next and s_ref is not s0_ref:
            scores(0, s0_ref, q=ahead(1))
        absorb(qi, s_ref, True)
        if has_next and s_ref is s0_ref:
            scores(0, s0_ref, q=ahead(1))
        if not peeled or qi + 2 < nk:
            prepare(ahead(2), 0)
        a1, a2 = acc_ref[0], acc_ref[1]
        o_t = a1[:dv] * (1.0 / a1[dv:dv + 1]) - lam * (a2[:dv] * (1.0 / a2[dv:dv + 1]))
        o = _rms(o_t, axis=0).T * gsub_ref[...] * (1.0 - LAMBDA_INIT)
        rows = pl.ds(pl.multiple_of(qi * tile, tile), tile)
        o_ref[rows, :] = (o * _sigmoid(gate_ref[rows, :].astype(F32))).astype(o_ref.dtype)

    acc_ref[...] = jnp.zeros_like(acc_ref)
    m_ref[...] = jnp.full_like(m_ref, MASK_VALUE)
    if peeled and qi == 0:
        diagonal_block_and_output(s0_ref)
        return
    scores(1, sa_ref, diagonal=peeled and qi == 1)
    absorb(0, s0_ref, False)

    def pair(j2, carry):
        jb = 2 * j2 + 1
        scores(jb + 1, sb_ref)
        absorb(jb, sa_ref, False)
        scores(jb + 2, sa_ref, diagonal=peeled and jb + 2 == qi)
        absorb(jb + 1, sb_ref, False)
        return carry

    def even_tail():
        scores(qi, sb_ref, diagonal=True)
        absorb(qi - 1, sa_ref, False)
        diagonal_block_and_output(sb_ref)

    def odd_tail():
        diagonal_block_and_output(sa_ref)

    if peeled:
        for j2 in range((qi - 1) // 2):
            pair(j2, 0)
        (even_tail if qi % 2 == 0 else odd_tail)()
    else:
        lax.fori_loop(0, (qi - 1) // 2, pair, 0)
        pl.when(qi % 2 == 0)(even_tail)
        pl.when(qi % 2 == 1)(odd_tail)


def _bf16_terms(x, n):
    terms, rest = [], jnp.asarray(x, F32)
    for _ in range(n):
        term = rest.astype(BF16).astype(F32)
        terms.append(term)
        rest = rest - term
    return terms


def _diff_attention(proj, g_q, g_k, g_sub, lq1, lk1, lq2, lk2, batch, seq, d):
    t = proj.shape[0]
    tile = ATT_T
    nq = seq // tile
    dh = d // (2 * DIFF_HEADS)
    w = 2 * dh
    assert w == LANES
    q_col, k_col, v_col, gate_col = 3 * d // w, 4 * d // w, 5 * d // w, 7 * d // w
    slopes = 2.0 ** -(jnp.arange(DIFF_HEADS, dtype=F32) + 1.0)
    slopes = jnp.broadcast_to(slopes[:, None, None], (DIFF_HEADS, 1, LANES))
    c1, c2, c3 = _bf16_terms(LOG2E, 3)
    qconst = jnp.zeros((dh,), F32).at[:6].set(jnp.stack([c1, c1, c2, c2, c3, c3]))
    zeros = jnp.zeros((dh,), F32)
    qconst_a = jnp.concatenate([zeros, qconst])[:, None]
    qconst_b = jnp.concatenate([qconst, zeros])[:, None]
    tile2 = lambda a: jnp.concatenate([a, a], axis=-1)
    assert tile % 256 == 0
    small = lambda n: pl.BlockSpec((1, n), lambda b, h: (0, 0))
    kernel = functools.partial(_attn_kernel, tile=tile, dh=dh, dv=w, nk=nq)
    return pl.pallas_call(
        kernel,
        out_shape=jax.ShapeDtypeStruct((t, d), BF16),
        grid=(batch, DIFF_HEADS),
        in_specs=[pl.BlockSpec((seq, w), lambda b, h: (b, q_col + h)),
                  pl.BlockSpec((seq, w), lambda b, h: (b, k_col + h)),
                  pl.BlockSpec((seq, w), lambda b, h: (b, v_col + h)),
                  pl.BlockSpec((seq, w), lambda b, h: (b, gate_col + h)),
                  pl.BlockSpec((1, 1, LANES), lambda b, h: (h, 0, 0)),
                  pl.BlockSpec((w, 1), lambda b, h: (0, 0)), pl.BlockSpec((w, 1), lambda b, h: (0, 0)),
                  small(w), small(w), small(w),
                  small(dh), small(dh), small(dh), small(dh)],
        out_specs=pl.BlockSpec((seq, w), lambda b, h: (b, h)),
        scratch_shapes=[pltpu.VMEM((nq, tile, w), BF16),
                        pltpu.VMEM((nq, tile, w), BF16),
                        pltpu.VMEM((nq, w + BF16_SUBLANES, tile), BF16),
                        pltpu.VMEM((nq, w, tile), BF16),
                        pltpu.VMEM((nq, w, tile), BF16),
                        pltpu.VMEM((2, tile + 8, tile), F32),
                        pltpu.VMEM((2, tile + 8, tile), F32),
                        pltpu.VMEM((2, tile + 8, tile), F32),
                        pltpu.VMEM((2, w + BF16_SUBLANES, tile), F32),
                        pltpu.VMEM((2, 1, tile), F32)],
        compiler_params=_params("arbitrary", "arbitrary"),
        name="diff_attn",
    )(proj, proj, proj, proj, slopes, qconst_a, qconst_b, tile2(g_q), tile2(g_k), g_sub,
      lq1, lk1, lq2, lk2)


def _out_kernel(x_ref, ret_ref, att_ref, mod_ref, wo_ref, g2_ref, wr_ref, br_ref,
                x1_ref, h2_ref, tw_ref, ld_ref, cnt_ref, *, tm, n_exp):
    m = mod_ref[0]
    merged = (ret_ref[...].astype(F32) + att_ref[...].astype(F32)).astype(BF16)
    x1 = x_ref[...] + m[2:3] * jnp.dot(merged, wo_ref[...], preferred_element_type=F32)
    x1_ref[...] = x1
    h2 = _rms(x1) * g2_ref[...] * (1.0 + m[4:5]) + m[3:4]
    hi = h2.astype(BF16)
    h2_ref[...] = hi

    lo = (h2 - hi.astype(F32)).astype(BF16)
    both = jnp.dot(hi, wr_ref[...], preferred_element_type=F32)
    logits = (both[:, :LANES] + both[:, LANES:]
              + jnp.dot(lo, wr_ref[:, :LANES], preferred_element_type=F32) + br_ref[...])
    lt = logits.T[:n_exp]

    erow = lax.broadcasted_iota(I32, (n_exp, tm), 0)
    vals, sels = [], []
    for k in range(TOP_K):
        mx = jnp.max(lt, axis=0, keepdims=True)
        idx = jnp.min(jnp.where(lt == mx, erow, n_exp), axis=0, keepdims=True)
        sel = erow == idx
        vals.append(mx)
        sels.append(sel)
        lt = jnp.where(sel, MASK_VALUE, lt)
    exps = [jnp.exp(v - vals[0]) for v in vals]
    inv = 1.0 / (exps[0] + exps[1] + exps[2] + exps[3])
    for k in range(TOP_K):
        tw_ref[k:k + 1, :] = exps[k] * inv

    chosen = jnp.where(sels[0] | sels[1] | sels[2] | sels[3], 1.0, 0.0)
    before = (lax.broadcasted_iota(I32, (tm, tm), 0) < lax.broadcasted_iota(I32, (tm, tm), 1))
    prefix = jnp.dot(chosen.astype(BF16), jnp.where(before, 1.0, 0.0).astype(BF16),
                     preferred_element_type=F32)
    count = jnp.sum(chosen, axis=1, keepdims=True)
    cnt_ref[...] = count
    padded = jnp.ceil(count * (1.0 / RUN_ALIGN)) * RUN_ALIGN
    below = (lax.broadcasted_iota(I32, (n_exp, n_exp), 1) < lax.broadcasted_iota(I32, (n_exp, n_exp), 0))
    run_start = jnp.dot(jnp.where(below, 1.0, 0.0).astype(BF16),
                        jnp.broadcast_to(padded, (n_exp, LANES)).astype(BF16),
                        preferred_element_type=F32)[:, :1]
    pos = prefix + run_start
    for k in range(TOP_K):
        ld_ref[k:k + 1, :] = jnp.sum(jnp.where(sels[k], pos, 0.0), axis=0, keepdims=True).astype(I32)


def _out_router(x2, ret, att, mod3, w_out_bf16, g2, wr_split, br_pad, seq, n_exp):
    t, d = x2.shape
    tm = TOK_TM
    row = lambda i: (i, 0)
    const = lambda i: (0, 0)
    kernel = functools.partial(_out_kernel, tm=tm, n_exp=n_exp)
    return pl.pallas_call(
        kernel,
        out_shape=(jax.ShapeDtypeStruct((t, d), F32),
                   jax.ShapeDtypeStruct((t, d), BF16),
                   jax.ShapeDtypeStruct((TOP_K, t), F32),
                   jax.ShapeDtypeStruct((TOP_K, t), I32),
                   jax.ShapeDtypeStruct((t // tm, n_exp, 1), F32)),
        grid=(t // tm,),
        in_specs=[pl.BlockSpec((tm, d), row), pl.BlockSpec((tm, d), row), pl.BlockSpec((tm, d), row),
                  pl.BlockSpec((1, 6, d), lambda i: (i * tm // seq, 0, 0)),
                  pl.BlockSpec((d, d), const), pl.BlockSpec((1, d), const),
                  pl.BlockSpec((d, 2 * LANES), const), pl.BlockSpec((1, LANES), const)],
        out_specs=(pl.BlockSpec((tm, d), row), pl.BlockSpec((tm, d), row),
                   pl.BlockSpec((TOP_K, tm), lambda i: (0, i)),
                   pl.BlockSpec((TOP_K, tm), lambda i: (0, i)),
                   pl.BlockSpec((None, n_exp, 1), lambda i: (i, 0, 0))),
        compiler_params=_params("arbitrary"),
        name="out_router",
    )(x2, ret, att, mod3, w_out_bf16, g2, wr_split, br_pad)


def _meta_stride(n_exp):
    return 2 * len(SLAB_SIZES) * n_exp + len(SLAB_SIZES)


def _run_slabs(meta_ref, tile, n_exp, local_ref, global_ref, sem, *, to_global, wait):
    n_sizes = len(SLAB_SIZES)
    base = tile * _meta_stride(n_exp)
    for c, size in enumerate(SLAB_SIZES):
        def one(p, carry, c=c, size=size):
            if wait:
                local = glob = 0
            else:
                local = pl.multiple_of(meta_ref[base + c * n_exp + p], RUN_ALIGN)
                glob = pl.multiple_of(meta_ref[base + (n_sizes + c) * n_exp + p], RUN_ALIGN)
            loc = local_ref.at[pl.ds(local, size)]
            glo = global_ref.at[pl.ds(glob, size)]
            cp = pltpu.make_async_copy(loc, glo, sem) if to_global else pltpu.make_async_copy(glo, loc, sem)
            if wait:
                cp.wait()
            else:
                cp.start()
            return carry

        lax.fori_loop(0, meta_ref[base + 2 * n_sizes * n_exp + c], one, 0)


def _dispatch_kernel(meta_ref, tail_ref, ld_ref, h_ref, xs_ref, stage_ref, zero_ref, sems, tail_sem,
                     *, tm, n_exp, n_tiles, n_blocks, stage_rows):
    i = pl.program_id(0)
    slot = i & 1
    stage = stage_ref.at[slot]

    def tails(wait):
        def unused_block(b, carry):
            cp = pltpu.make_async_copy(
                zero_ref, xs_ref.at[pl.ds(pl.multiple_of(b * EXP_BLK, EXP_BLK), EXP_BLK)], tail_sem)
            if wait:
                cp.wait()
            else:
                cp.start()
            return carry

        lax.fori_loop(tail_ref[2 * n_exp] // EXP_BLK, n_blocks, unused_block, 0)

        def per_expert(e, carry):
            start, n = tail_ref[e], tail_ref[n_exp + e]
            off = 0
            for size in TAIL_SIZES:
                take = n & size

                @pl.when(take != 0)
                def _(off=off, size=size):
                    cp = pltpu.make_async_copy(
                        zero_ref.at[pl.ds(0, size)],
                        xs_ref.at[pl.ds(pl.multiple_of(start + off, RUN_ALIGN), size)], tail_sem)
                    if wait:
                        cp.wait()
                    else:
                        cp.start()

                off = off + take
            return carry

        lax.fori_loop(0, n_exp, per_expert, 0)

    @pl.when(i == 0)
    def _():
        zero_ref[...] = jnp.zeros_like(zero_ref)
        tails(False)
        tails(True)

    ld = ld_ref[...]
    h = h_ref[...]
    for c in range(stage_rows // GATHER_RC):
        r = lax.broadcasted_iota(I32, (GATHER_RC, tm), 0) + c * GATHER_RC
        hit = jnp.where(r == ld[0:1], 1.0, jnp.where(r == ld[1:2], 1.0,
                        jnp.where(r == ld[2:3], 1.0, jnp.where(r == ld[3:4], 1.0, 0.0))))
        stage[c * GATHER_RC:(c + 1) * GATHER_RC, :] = jnp.dot(
            hit.astype(BF16), h, preferred_element_type=F32).astype(BF16)

    _run_slabs(meta_ref, i, n_exp, stage, xs_ref, sems.at[slot], to_global=True, wait=False)

    @pl.when(i > 0)
    def _():
        _run_slabs(meta_ref, i - 1, n_exp, stage_ref.at[1 - slot], xs_ref, sems.at[1 - slot],
                   to_global=True, wait=True)

    @pl.when(i == n_tiles - 1)
    def _():
        _run_slabs(meta_ref, i, n_exp, stage, xs_ref, sems.at[slot], to_global=True, wait=True)


def _dispatch(meta, tail, ldest, h2, n_rows, n_exp):
    t, d = h2.shape
    tm = TOK_TM
    n_tiles = t // tm
    stage_rows = _stage_rows(n_exp)
    kernel = functools.partial(_dispatch_kernel, tm=tm, n_exp=n_exp, n_tiles=n_tiles,
                               n_blocks=n_rows // EXP_BLK, stage_rows=stage_rows)
    grid_spec = pltpu.PrefetchScalarGridSpec(
        num_scalar_prefetch=2,
        grid=(n_tiles,),
        in_specs=[pl.BlockSpec((TOP_K, tm), lambda i, m, tl: (0, i)),
                  pl.BlockSpec((tm, d), lambda i, m, tl: (i, 0))],
        out_specs=pl.BlockSpec(memory_space=pl.ANY),
        scratch_shapes=[pltpu.VMEM((2, stage_rows, d), BF16),
                        pltpu.VMEM((EXP_BLK, d), BF16),
                        pltpu.SemaphoreType.DMA((2,)),
                        pltpu.SemaphoreType.DMA],
    )
    return pl.pallas_call(
        kernel,
        out_shape=jax.ShapeDtypeStruct((n_rows, d), BF16),
        grid_spec=grid_spec,
        compiler_params=_params("arbitrary"),
        name="dispatch",
    )(meta, tail, ldest, h2)


def _expert_kernel(region_ref, xs_ref, w1_ref, b1_ref, w2_ref, b2_ref, eo_ref,
                   w1f_ref, w2f_ref, w1b_ref, w2b_ref, x_buf, o_buf, w_sems, x_sems, o_sems,
                   *, f, blk, n_exp, n_blocks):
    e = pl.program_id(0)
    first, count = region_ref[e], region_ref[n_exp + e]
    w_slot = e & 1
    d = w1f_ref.shape[1]
    r1, r2 = d // W_CHUNKS, f // W_CHUNKS

    def rows(j):
        return pl.ds(pl.multiple_of((first + j) * blk, blk), blk)

    def x_copy(j, slot):
        return pltpu.make_async_copy(xs_ref.at[rows(j)], x_buf.at[slot], x_sems.at[slot])

    def o_copy(j, slot):
        return pltpu.make_async_copy(o_buf.at[slot], eo_ref.at[rows(j)], o_sems.at[slot])

    def w_chunk(expert, c, slot, wait):
        for src, dst, r in ((w1_ref, w1f_ref, r1), (w2_ref, w2f_ref, r2)):
            piece = pl.ds(c * r if isinstance(c, int) else pl.multiple_of(c * r, r), r)
            cp = pltpu.make_async_copy(src.at[expert, piece], dst.at[slot, piece], w_sems.at[slot])
            if wait:
                cp.wait()
            else:
                cp.start()

    @pl.when(e == 0)
    def _():
        for c in range(W_CHUNKS):
            w_chunk(0, c, 0, False)

    @pl.when(count > 0)
    def _():
        x_copy(0, 0).start()

    for c in range(W_CHUNKS):
        w_chunk(e, c, w_slot, True)

    @pl.when(count > 0)
    def _():
        w1b_ref[...] = w1f_ref[w_slot].astype(BF16)
        w2b_ref[...] = w2f_ref[w_slot].astype(BF16)

    def next_chunk(c):
        @pl.when(e + 1 < n_exp)
        def _():
            w_chunk(e + 1, c, 1 - w_slot, False)

    def block(j, carry):
        slot = j & 1
        x_copy(j, slot).wait()

        @pl.when(j + 1 < count)
        def _():
            x_copy(j + 1, 1 - slot).start()

        @pl.when(j < W_CHUNKS)
        def _():
            next_chunk(j)

        @pl.when(j >= 2)
        def _():
            o_copy(j - 2, slot).wait()

        gu = jnp.dot(x_buf[slot], w1b_ref[...], preferred_element_type=F32) + b1_ref[0]
        g = jnp.minimum(gu[:, :f], SWIGLU_LIMIT)
        u = jnp.clip(gu[:, f:], -SWIGLU_LIMIT, SWIGLU_LIMIT)
        a = g * _sigmoid(SWIGLU_ALPHA * g) * (u + 1.0)
        o_buf[slot] = (jnp.dot(a.astype(BF16), w2b_ref[...], preferred_element_type=F32)
                       + b2_ref[0]).astype(o_buf.dtype)
        o_copy(j, slot).start()
        return carry

    lax.fori_loop(0, count, block, 0)
    lax.fori_loop(jnp.minimum(count, W_CHUNKS), W_CHUNKS, lambda c, carry: (next_chunk(c), carry)[1], 0)

    @pl.when(count >= 2)
    def _():
        o_copy(count - 2, count & 1).wait()

    @pl.when(count >= 1)
    def _():
        o_copy(count - 1, (count - 1) & 1).wait()

    @pl.when(e == n_exp - 1)
    def _():
        o_buf[0] = jnp.zeros(o_buf.shape[1:], o_buf.dtype)
        used = first + count

        def fill(j, wait):
            cp = pltpu.make_async_copy(
                o_buf.at[0], eo_ref.at[pl.ds(pl.multiple_of(j * blk, blk), blk)], o_sems.at[0])
            if wait:
                cp.wait()
            else:
                cp.start()

        lax.fori_loop(used, n_blocks, lambda j, c: (fill(j, False), c)[1], 0)
        lax.fori_loop(used, n_blocks, lambda j, c: (fill(j, True), c)[1], 0)


def _experts(region, xs, w1, b1, w2, b2):
    n_rows, d = xs.shape
    n_exp, _, f2 = w1.shape
    f = f2 // 2
    blk = EXP_BLK
    kernel = functools.partial(_expert_kernel, f=f, blk=blk, n_exp=n_exp, n_blocks=n_rows // blk)
    grid_spec = pltpu.PrefetchScalarGridSpec(
        num_scalar_prefetch=1,
        grid=(n_exp,),
        in_specs=[pl.BlockSpec(memory_space=pl.ANY),
                  pl.BlockSpec(memory_space=pl.ANY),
                  pl.BlockSpec((1, 1, f2), lambda e, r: (e, 0, 0)),
                  pl.BlockSpec(memory_space=pl.ANY),
                  pl.BlockSpec((1, 1, d), lambda e, r: (e, 0, 0))],
        out_specs=pl.BlockSpec(memory_space=pl.ANY),
        scratch_shapes=[pltpu.VMEM((2, d, f2), F32), pltpu.VMEM((2, f, d), F32),
                        pltpu.VMEM((d, f2), BF16), pltpu.VMEM((f, d), BF16),
                        pltpu.VMEM((2, blk, d), BF16), pltpu.VMEM((2, blk, d), BF16),
                        pltpu.SemaphoreType.DMA((2,)), pltpu.SemaphoreType.DMA((2,)),
                        pltpu.SemaphoreType.DMA((2,))],
    )
    return pl.pallas_call(
        kernel,
        out_shape=jax.ShapeDtypeStruct((n_rows, d), BF16),
        grid_spec=grid_spec,
        compiler_params=_params("arbitrary"),
        name="experts",
    )(region, xs, w1, b1, w2, b2)


def _combine_kernel(meta_ref, eo_ref, ld_ref, w_ref, x1_ref, mod_ref, o_ref, stage_ref, sems,
                    *, tm, n_exp, n_tiles, stage_rows):
    i = pl.program_id(0)
    slot = i & 1

    @pl.when(i == 0)
    def _():
        stage_ref[...] = jnp.zeros_like(stage_ref)
        _run_slabs(meta_ref, 0, n_exp, stage_ref.at[0], eo_ref, sems.at[0], to_global=False, wait=False)

    @pl.when(i + 1 < n_tiles)
    def _():
        _run_slabs(meta_ref, i + 1, n_exp, stage_ref.at[1 - slot], eo_ref, sems.at[1 - slot],
                   to_global=False, wait=False)

    ld = ld_ref[...]
    w = w_ref[...]

    _run_slabs(meta_ref, i, n_exp, stage_ref.at[slot], eo_ref, sems.at[slot], to_global=False, wait=True)
    stage = stage_ref.at[slot]
    y = None
    for c in range(stage_rows // COMBINE_KC):
        r0 = c * COMBINE_KC
        r = lax.broadcasted_iota(I32, (COMBINE_KC, tm), 0) + r0
        sel = jnp.where(r == ld[0:1], w[0:1], jnp.where(r == ld[1:2], w[1:2],
                        jnp.where(r == ld[2:3], w[2:3], jnp.where(r == ld[3:4], w[3:4], 0.0))))
        part = lax.dot_general(sel.astype(BF16), stage[r0:r0 + COMBINE_KC, :], (((0,), (0,)), ((), ())),
                               preferred_element_type=F32)
        y = part if y is None else y + part
    o_ref[...] = x1_ref[...] + mod_ref[0][5:6] * y


def _combine(meta, eo, ld_tok, w_tok, x1, mod3, seq, n_exp):
    t, d = x1.shape
    tm = TOK_TM
    n_tiles = t // tm
    stage_rows = _stage_rows(n_exp)
    kernel = functools.partial(_combine_kernel, tm=tm, n_exp=n_exp, n_tiles=n_tiles, stage_rows=stage_rows)
    grid_spec = pltpu.PrefetchScalarGridSpec(
        num_scalar_prefetch=1,
        grid=(n_tiles,),
        in_specs=[pl.BlockSpec(memory_space=pl.ANY),
                  pl.BlockSpec((TOP_K, tm), lambda i, m: (0, i)),
                  pl.BlockSpec((TOP_K, tm), lambda i, m: (0, i)),
                  pl.BlockSpec((tm, d), lambda i, m: (i, 0)),
                  pl.BlockSpec((1, 6, d), lambda i, m: (i * tm // seq, 0, 0))],
        out_specs=pl.BlockSpec((tm, d), lambda i, m: (i, 0)),
        scratch_shapes=[pltpu.VMEM((2, stage_rows, d), BF16),
                        pltpu.SemaphoreType.DMA((2,))],
    )
    return pl.pallas_call(
        kernel,
        out_shape=jax.ShapeDtypeStruct((t, d), F32),
        grid_spec=grid_spec,
        compiler_params=_params("arbitrary"),
        name="combine",
    )(meta, eo, ld_tok, w_tok, x1, mod3)


def _routing_tables(tile_cnt, n_tokens):
    n_tiles, n_exp = tile_cnt.shape
    blk = EXP_BLK
    run = _round_up(tile_cnt, RUN_ALIGN)
    local = jnp.cumsum(run, axis=1) - run
    size = jnp.sum(run, axis=0)
    region = _round_up(size, blk)
    region_end = jnp.cumsum(region)
    region_start = region_end - region
    glob = region_start[None, :] + jnp.cumsum(run, axis=0) - run
    sizes = jnp.asarray(SLAB_SIZES, I32)
    has = (run[:, :, None] & sizes) != 0
    before = run[:, :, None] & ~(2 * sizes - 1)
    place = jnp.cumsum(has, axis=1) - has
    hit = has[..., None] & (place[..., None] == jnp.arange(n_exp, dtype=I32))
    listed = lambda start: jnp.sum(
        jnp.where(hit, (start[:, :, None] + before)[..., None], 0), axis=1).reshape(n_tiles, -1)
    meta = jnp.concatenate([listed(local), listed(glob), jnp.sum(has, axis=1)],
                           axis=1).reshape(-1).astype(I32)
    tail = jnp.concatenate([region_start + size, region - size, region_end[-1:]]).astype(I32)
    n_rows = _round_up(n_tokens * TOP_K + n_tiles * n_exp * RUN_ALIGN + n_exp * blk, blk)
    region_blocks = jnp.concatenate([region_start // blk, region // blk]).astype(I32)
    return meta, tail, region_blocks, n_rows


def kernel(x, c, w_ada, b_ada, g_norm1, w_in, g_ret_gn, b_ret_gn, g_qnorm, g_knorm,
           lambda_q1, lambda_k1, lambda_q2, lambda_k2, g_diff_subln, w_out, g_norm2,
           w_router, b_router, w_expert_in, b_expert_in, w_expert_out, b_expert_out):
    batch, seq, d = x.shape
    depth = w_ada.shape[0]
    assert depth == 1
    t = batch * seq
    n_exp = w_router.shape[-1]
    l = 0

    mod = _adaln_mod(c, w_ada[l], b_ada[l][None, :])
    mod3 = mod.reshape(batch, 6, d)
    x2 = x.reshape(t, d)

    proj = _in_proj(x2, mod3, g_norm1[l][None, :], w_in[l].astype(BF16), seq)
    ret = _retention(proj, g_ret_gn[l][None, :], b_ret_gn[l][None, :], batch, seq, d)
    att = _diff_attention(proj, g_qnorm[l][None, :], g_knorm[l][None, :], g_diff_subln[l][None, :],
                          lambda_q1[l][None, :], lambda_k1[l][None, :],
                          lambda_q2[l][None, :], lambda_k2[l][None, :], batch, seq, d)

    wr = w_router[l]
    wr_hi = wr.astype(BF16)
    wr_lo = (wr - wr_hi.astype(F32)).astype(BF16)
    pad = lambda a: jnp.pad(a, ((0, 0), (0, LANES - n_exp)))
    wr_split = jnp.concatenate([pad(wr_hi), pad(wr_lo)], axis=1)
    br_pad = jnp.pad(b_router[l][None, :], ((0, 0), (0, LANES - n_exp)), constant_values=MASK_VALUE)
    x1, h2, top_w, ldest, tile_cnt = _out_router(
        x2, ret, att, mod3, w_out[l].astype(BF16), g_norm2[l][None, :], wr_split, br_pad, seq, n_exp)

    meta, tail, region_blocks, n_rows = _routing_tables(tile_cnt[:, :, 0].astype(I32), t)
    xs = _dispatch(meta, tail, ldest, h2, n_rows, n_exp)
    eo = _experts(region_blocks, xs, w_expert_in[l], b_expert_in[l][:, None, :],
                  w_expert_out[l], b_expert_out[l][:, None, :])
    out = _combine(meta, eo, ldest, top_w, x1, mod3, seq, n_exp)
    return out.reshape(batch, seq, d)
```

```python
import functools
import itertools
import math

import jax
import jax.numpy as jnp
from jax import lax
from jax.experimental import pallas as pl
from jax.experimental.pallas import tpu as pltpu

F32 = jnp.float32
BF16 = jnp.bfloat16
I32 = jnp.int32

EPS = 1e-6
LOG2E = 1.4426950216293335
MASK_VALUE = -1e30
LANES = 128
BF16_SUBLANES = 16
VMEM_LIMIT_BYTES = 48 * 1024 * 1024

RET_HEADS = 4
DIFF_HEADS = 8
TOP_K = 4
SWIGLU_LIMIT = 7.0
SWIGLU_ALPHA = 1.702
LAMBDA_INIT = 0.8 - 0.6 * math.exp(-0.3 * 0)

IN_TM, IN_TN = 512, 2048
RET_TC = 256
RET_CHUNKS_PER_STEP = 2
ATT_T = 512
ATT_PEELED_TILES = 8
TOK_TM = 512
EXP_BLK = 512
RUN_ALIGN = BF16_SUBLANES
GATHER_RC = 256
COMBINE_KC = 512
W_CHUNKS = 8


def _round_up(n, m):
    return (n + m - 1) // m * m


def _stage_rows(n_exp):
    return _round_up(TOK_TM * TOP_K + n_exp * (RUN_ALIGN - 1), max(GATHER_RC, COMBINE_KC))


SLAB_SIZES = tuple(TOK_TM >> s for s in range(TOK_TM.bit_length()) if TOK_TM >> s >= RUN_ALIGN)
TAIL_SIZES = tuple(s for s in SLAB_SIZES if s < EXP_BLK)


def _params(*sem):
    return pltpu.CompilerParams(dimension_semantics=sem, vmem_limit_bytes=VMEM_LIMIT_BYTES)


def _sigmoid(x):
    return 0.5 * jnp.tanh(0.5 * x) + 0.5


def _rms(x, axis=-1):
    return x * lax.rsqrt(jnp.mean(x * x, axis=axis, keepdims=True) + EPS)


def _mod_kernel(c_ref, w_ref, b_ref, o_ref):
    c = c_ref[...]
    s = c * _sigmoid(c)
    o_ref[...] = jnp.dot(s, w_ref[...], preferred_element_type=F32,
                         precision=lax.Precision.HIGHEST) + b_ref[...]


def _adaln_mod(c, w, b):
    bn, d = c.shape
    n = w.shape[1]
    tn = d
    return pl.pallas_call(
        _mod_kernel,
        out_shape=jax.ShapeDtypeStruct((bn, n), F32),
        grid=(n // tn,),
        in_specs=[pl.BlockSpec((bn, d), lambda j: (0, 0)),
                  pl.BlockSpec((d, tn), lambda j: (0, j)),
                  pl.BlockSpec((1, tn), lambda j: (0, j))],
        out_specs=pl.BlockSpec((bn, tn), lambda j: (0, j)),
        compiler_params=_params("arbitrary"),
        name="adaln_mod",
    )(c, w, b)


def _inproj_kernel(x_ref, mod_ref, g_ref, w_ref, o_ref, *, tn):
    m = mod_ref[0]
    h = (_rms(x_ref[...]) * g_ref[...] * (1.0 + m[1:2]) + m[0:1]).astype(BF16)
    for c in range(0, w_ref.shape[1], tn):
        o_ref[:, c:c + tn] = jnp.dot(h, w_ref[:, c:c + tn], preferred_element_type=F32).astype(o_ref.dtype)


def _in_proj(x2, mod3, g1, w_in_bf16, seq):
    t, d = x2.shape
    n = w_in_bf16.shape[1]
    tm, tn = IN_TM, IN_TN
    return pl.pallas_call(
        functools.partial(_inproj_kernel, tn=tn),
        out_shape=jax.ShapeDtypeStruct((t, n), BF16),
        grid=(t // tm,),
        in_specs=[pl.BlockSpec((tm, d), lambda i: (i, 0)),
                  pl.BlockSpec((1, 6, d), lambda i: (i * tm // seq, 0, 0)),
                  pl.BlockSpec((1, d), lambda i: (0, 0)),
                  pl.BlockSpec((d, n), lambda i: (0, 0), pipeline_mode=pl.Buffered(1))],
        out_specs=pl.BlockSpec((tm, n), lambda i: (i, 0)),
        compiler_params=_params("arbitrary"),
        name="in_proj",
    )(x2, mod3, g1, w_in_bf16)


def _ret_kernel(q_ref, k_ref, v_ref, rg_ref, mg_ref, gng_ref, gnb_ref, o_ref, r_ref, decay_ref,
                *, dk, dv, tc):
    scale = dk ** -0.5
    log_gs = [math.log(1.0 - 2.0 ** (-5.0 - h)) for h in range(RET_HEADS)]

    @pl.when(pl.program_id(1) == 0)
    def _():
        r_ref[...] = jnp.zeros_like(r_ref)
        row = lax.broadcasted_iota(I32, (tc, tc), 0)
        col = lax.broadcasted_iota(I32, (tc, tc), 1)
        rel = (row - col).astype(F32)
        for h, log_g in enumerate(log_gs):
            decay_ref[h] = jnp.where(rel >= 0, jnp.exp(log_g * jnp.maximum(rel, 0.0)), 0.0) * scale

    pos = lax.broadcasted_iota(I32, (tc, 1), 0).astype(F32)
    for sub, (h, log_g) in itertools.product(range(RET_CHUNKS_PER_STEP), enumerate(log_gs)):
        rows = slice(sub * tc, (sub + 1) * tc)
        decay = decay_ref[h]
        xi = jnp.exp(log_g * (pos + 1.0))
        zeta = jnp.exp(log_g * (tc - 1.0 - pos))
        g_chunk = math.exp(log_g * tc)
        q = q_ref[rows, h * dk:(h + 1) * dk]
        k = k_ref[rows, h * dk:(h + 1) * dk]
        v = v_ref[rows, h * dv:(h + 1) * dv]
        s = lax.dot_general(q, k, (((1,), (1,)), ((), ())), preferred_element_type=F32)
        y = jnp.dot((s * decay).astype(BF16), v, preferred_element_type=F32)
        state = r_ref[h]
        y = y + jnp.dot(q, state.astype(BF16), preferred_element_type=F32) * xi
        vz = (v.astype(F32) * zeta).astype(BF16)
        kv = lax.dot_general(k, vz, (((0,), (0,)), ((), ())), preferred_element_type=F32)
        r_ref[h] = state * g_chunk + kv * scale

        mu = jnp.mean(y, axis=-1, keepdims=True)
        yc = y - mu
        var = jnp.mean(yc * yc, axis=-1, keepdims=True)
        sl = slice(h * dv, (h + 1) * dv)
        yn = yc * lax.rsqrt(var + EPS) * gng_ref[:, sl] + gnb_ref[:, sl]
        rg = rg_ref[rows, sl].astype(F32)
        gate = _sigmoid(mg_ref[rows, sl].astype(F32))
        o_ref[rows, sl] = (rg * _sigmoid(rg) * yn * gate).astype(o_ref.dtype)


def _retention(proj, gn_g, gn_b, batch, seq, d):
    t = proj.shape[0]
    tc = RET_TC
    tr = tc * RET_CHUNKS_PER_STEP
    nc = seq // tr
    dk = d // (2 * RET_HEADS)
    dv = d // RET_HEADS
    qk_w = RET_HEADS * dk
    row = lambda b, c: b * nc + c
    kernel = functools.partial(_ret_kernel, dk=dk, dv=dv, tc=tc)
    return pl.pallas_call(
        kernel,
        out_shape=jax.ShapeDtypeStruct((t, d), BF16),
        grid=(batch, nc),
        in_specs=[pl.BlockSpec((tr, qk_w), lambda b, c: (row(b, c), 0)),
                  pl.BlockSpec((tr, qk_w), lambda b, c: (row(b, c), 1)),
                  pl.BlockSpec((tr, d), lambda b, c: (row(b, c), 1)),
                  pl.BlockSpec((tr, d), lambda b, c: (row(b, c), 2)),
                  pl.BlockSpec((tr, d), lambda b, c: (row(b, c), 6)),
                  pl.BlockSpec((1, d), lambda b, c: (0, 0)),
                  pl.BlockSpec((1, d), lambda b, c: (0, 0))],
        out_specs=pl.BlockSpec((tr, d), lambda b, c: (row(b, c), 0)),
        scratch_shapes=[pltpu.VMEM((RET_HEADS, dk, dv), F32), pltpu.VMEM((RET_HEADS, tc, tc), F32)],
        compiler_params=_params("arbitrary", "arbitrary"),
        name="retention",
    )(proj, proj, proj, proj, proj, gn_g, gn_b)


def _halves_rms(x, lo, dh):
    sq = x * x
    s_lo = jnp.sum(jnp.where(lo, sq, 0.0), axis=-1, keepdims=True)
    s_hi = jnp.sum(jnp.where(lo, 0.0, sq), axis=-1, keepdims=True)
    inv = jnp.where(lo, lax.rsqrt(s_lo / dh + EPS), lax.rsqrt(s_hi / dh + EPS))
    return x * inv


def _attn_kernel(q_ref, k_ref, v_ref, gate_ref, slope_ref, qca_ref, qcb_ref, gq_ref, gk_ref, gsub_ref,
                 lq1_ref, lk1_ref, lq2_ref, lk2_ref,
                 o_ref, ka_ref, kb_ref, vt_ref, qa_ref, qb_ref, s0_ref, sa_ref, sb_ref, acc_ref, m_ref,
                 *, tile, dh, dv, nk):
    lane = lax.broadcasted_iota(I32, (tile, 2 * dh), 1)
    lo = lane < dh

    slope = slope_ref[0]
    sub = lax.broadcasted_iota(I32, (tile, 2 * dh), 0)
    off = lane & (dh - 1)
    hi_lane = (off < 6) & ((off & 1) == 0)
    lo_lane = (off < 6) & ((off & 1) == 1)
    bias0 = jnp.where(hi_lane, ((sub >> 8) << 8).astype(F32),
                      jnp.where(lo_lane, (sub & 255).astype(F32), 0.0)) * slope
    bias_step = jnp.where(hi_lane, float(tile), 0.0) * slope
    ones_rows = jnp.where(lax.broadcasted_iota(I32, (BF16_SUBLANES, tile), 0) == 0, 1.0, 0.0)
    lo_t = lax.broadcasted_iota(I32, (2 * dh, tile), 0) < dh

    def prepare(c, carry):
        r0 = c * tile if isinstance(c, int) else pl.multiple_of(c * tile, tile)
        kn = _halves_rms(k_ref[pl.ds(r0, tile), :].astype(F32), lo, dh) * gk_ref[...]
        bias = bias0 + lax.convert_element_type(c, F32) * bias_step
        ka_ref[c] = jnp.where(lo, kn, bias).astype(BF16)
        kb_ref[c] = jnp.where(lo, bias, kn).astype(BF16)
        vt_ref[c, :dv, :] = v_ref[pl.ds(r0, tile), :].astype(F32).T.astype(BF16)
        vt_ref[c, dv:, :] = ones_rows.astype(BF16)
        qn = (_halves_rms(q_ref[pl.ds(r0, tile), :].astype(F32), lo, dh) * gq_ref[...]
              * (dh ** -0.5 * LOG2E))
        qn_t = qn.T
        qa_ref[c] = jnp.where(lo_t, qn_t, qca_ref[...]).astype(BF16)
        qb_ref[c] = jnp.where(lo_t, qcb_ref[...], qn_t).astype(BF16)
        return carry

    for c in range(min(2, nk)):
        prepare(c, 0)
    lam = (jnp.exp(jnp.sum(lq1_ref[...] * lk1_ref[...], axis=-1, keepdims=True))
           - jnp.exp(jnp.sum(lq2_ref[...] * lk2_ref[...], axis=-1, keepdims=True)) + LAMBDA_INIT)

    refs = (gate_ref, gsub_ref, o_ref, ka_ref, kb_ref, vt_ref, qa_ref, qb_ref,
            s0_ref, sa_ref, sb_ref, acc_ref, m_ref)
    _attn_scores(ka_ref, kb_ref, qa_ref, qb_ref, 0, 0, s0_ref, tile, diagonal=True)
    n_peeled = min(ATT_PEELED_TILES, nk)
    for qi in range(n_peeled):
        _attn_q_tile(qi, lam, prepare, *refs, tile=tile, dv=dv, nk=nk, peeled=True)

    def q_tile(qi, carry):
        _attn_q_tile(jnp.asarray(qi, I32), lam, prepare, *refs, tile=tile, dv=dv, nk=nk, peeled=False)
        return carry

    lax.fori_loop(n_peeled, nk, q_tile, 0)


def _attn_scores(ka_ref, kb_ref, qa_ref, qb_ref, jb, q, s_ref, tile, diagonal=False):
    h = tile // 2
    for idx, (k_ref, q_ref) in enumerate(((ka_ref, qa_ref), (kb_ref, qb_ref))):
        if diagonal:
            s_ref[idx, :h, :] = jnp.dot(k_ref[jb, :h, :], q_ref[q], preferred_element_type=F32)
            s_ref[idx, h:tile, h:] = jnp.dot(k_ref[jb, h:, :], q_ref[q, :, h:], preferred_element_type=F32)
        else:
            s = jnp.dot(k_ref[jb], q_ref[q], preferred_element_type=F32)
            s_ref[idx, :tile, :] = s
            s_ref[idx, tile:tile + 1, :] = jnp.max(s, axis=0, keepdims=True)


def _attn_q_tile(qi, lam, prepare, gate_ref, gsub_ref, o_ref, ka_ref, kb_ref, vt_ref, qa_ref, qb_ref,
                 s0_ref, sa_ref, sb_ref, acc_ref, m_ref, *, tile, dv, nk, peeled):
    def scores(jb, s_ref, q=qi, diagonal=False):
        _attn_scores(ka_ref, kb_ref, qa_ref, qb_ref, jb, q, s_ref, tile, diagonal)

    def absorb(jb, s_ref, masked):
        vt = vt_ref[jb]
        if masked:
            return absorb_diagonal(s_ref, vt)
        for idx in range(2):
            m_old = m_ref[idx]
            m_new = jnp.maximum(m_old, s_ref[idx, tile:tile + 1, :])
            m_ref[idx] = m_new
            p = jnp.exp2(s_ref[idx, :tile, :] - m_new).astype(BF16)
            acc_ref[idx] = (acc_ref[idx] * jnp.exp2(m_old - m_new)
                            + jnp.dot(vt, p, preferred_element_type=F32))

    def absorb_diagonal(s_ref, vt):
        h = tile // 2
        future = lax.broadcasted_iota(I32, (h, h), 0) > lax.broadcasted_iota(I32, (h, h), 1)
        for idx in range(2):
            early = s_ref[idx, :h, :]
            early = jnp.concatenate([jnp.where(future, MASK_VALUE, early[:, :h]), early[:, h:]], axis=1)
            late = jnp.where(future, MASK_VALUE, s_ref[idx, h:tile, h:])
            m_old = m_ref[idx]
            m_new = jnp.maximum(m_old, jnp.max(early, axis=0, keepdims=True))
            m_new = jnp.concatenate(
                [m_new[:, :h], jnp.maximum(m_new[:, h:], jnp.max(late, axis=0, keepdims=True))], axis=1)
            m_ref[idx] = m_new
            p_early = jnp.exp2(early - m_new).astype(BF16)
            p_late = jnp.exp2(late - m_new[:, h:]).astype(BF16)
            acc_ref[idx] = (acc_ref[idx] * jnp.exp2(m_old - m_new)
                            + jnp.dot(vt[:, :h], p_early, preferred_element_type=F32))
            acc_ref[idx, :, h:] = acc_ref[idx, :, h:] + jnp.dot(vt[:, h:], p_late,
                                                                preferred_element_type=F32)

    ahead = (lambda n: min(qi + n, nk - 1)) if peeled else (lambda n: jnp.minimum(qi + n, nk - 1))

    def diagonal_block_and_output(s_ref):
        has_next = not peeled or qi + 1 < nk
        if has_next and s_ref is not s0_ref:
            scores(0, s0_ref, q=ahead(1))
        absorb(qi, s_ref, True)
        if has_next and s_ref is s0_ref:
            scores(0, s0_ref, q=ahead(1))
        if not peeled or qi + 2 < nk:
            prepare(ahead(2), 0)
        a1, a2 = acc_ref[0], acc_ref[1]
        o_t = a1[:dv] * (1.0 / a1[dv:dv + 1]) - lam * (a2[:dv] * (1.0 / a2[dv:dv + 1]))
        o = _rms(o_t, axis=0).T * gsub_ref[...] * (1.0 - LAMBDA_INIT)
        rows = pl.ds(pl.multiple_of(qi * tile, tile), tile)
        o_ref[rows, :] = (o * _sigmoid(gate_ref[rows, :].astype(F32))).astype(o_ref.dtype)

    acc_ref[...] = jnp.zeros_like(acc_ref)
    m_ref[...] = jnp.full_like(m_ref, MASK_VALUE)
    if peeled and qi == 0:
        diagonal_block_and_output(s0_ref)
        return
    scores(1, sa_ref, diagonal=peeled and qi == 1)
    absorb(0, s0_ref, False)

    def pair(j2, carry):
        jb = 2 * j2 + 1
        scores(jb + 1, sb_ref)
        absorb(jb, sa_ref, False)
        scores(jb + 2, sa_ref)
        absorb(jb + 1, sb_ref, False)
        return carry

    def even_tail():
        scores(qi, sb_ref, diagonal=True)
        absorb(qi - 1, sa_ref, False)
        diagonal_block_and_output(sb_ref)

    def odd_tail():
        diagonal_block_and_output(sa_ref)

    if peeled:
        for j2 in range((qi - 1) // 2):
            pair(j2, 0)
        (even_tail if qi % 2 == 0 else odd_tail)()
    else:
        lax.fori_loop(0, (qi - 1) // 2, pair, 0)
        pl.when(qi % 2 == 0)(even_tail)
        pl.when(qi % 2 == 1)(odd_tail)


def _bf16_terms(x, n):
    terms, rest = [], jnp.asarray(x, F32)
    for _ in range(n):
        term = rest.astype(BF16).astype(F32)
        terms.append(term)
        rest = rest - term
    return terms


def _diff_attention(proj, g_q, g_k, g_sub, lq1, lk1, lq2, lk2, batch, seq, d):
    t = proj.shape[0]
    tile = ATT_T
    nq = seq // tile
    dh = d // (2 * DIFF_HEADS)
    w = 2 * dh
    assert w == LANES
    q_col, k_col, v_col, gate_col = 3 * d // w, 4 * d // w, 5 * d // w, 7 * d // w
    slopes = 2.0 ** -(jnp.arange(DIFF_HEADS, dtype=F32) + 1.0)
    slopes = jnp.broadcast_to(slopes[:, None, None], (DIFF_HEADS, 1, LANES))
    c1, c2, c3 = _bf16_terms(LOG2E, 3)
    qconst = jnp.zeros((dh,), F32).at[:6].set(jnp.stack([c1, c1, c2, c2, c3, c3]))
    zeros = jnp.zeros((dh,), F32)
    qconst_a = jnp.concatenate([zeros, qconst])[:, None]
    qconst_b = jnp.concatenate([qconst, zeros])[:, None]
    tile2 = lambda a: jnp.concatenate([a, a], axis=-1)
    assert tile % 256 == 0
    small = lambda n: pl.BlockSpec((1, n), lambda b, h: (0, 0))
    kernel = functools.partial(_attn_kernel, tile=tile, dh=dh, dv=w, nk=nq)
    return pl.pallas_call(
        kernel,
        out_shape=jax.ShapeDtypeStruct((t, d), BF16),
        grid=(batch, DIFF_HEADS),
        in_specs=[pl.BlockSpec((seq, w), lambda b, h: (b, q_col + h)),
                  pl.BlockSpec((seq, w), lambda b, h: (b, k_col + h)),
                  pl.BlockSpec((seq, w), lambda b, h: (b, v_col + h)),
                  pl.BlockSpec((seq, w), lambda b, h: (b, gate_col + h)),
                  pl.BlockSpec((1, 1, LANES), lambda b, h: (h, 0, 0)),
                  pl.BlockSpec((w, 1), lambda b, h: (0, 0)), pl.BlockSpec((w, 1), lambda b, h: (0, 0)),
                  small(w), small(w), small(w),
                  small(dh), small(dh), small(dh), small(dh)],
        out_specs=pl.BlockSpec((seq, w), lambda b, h: (b, h)),
        scratch_shapes=[pltpu.VMEM((nq, tile, w), BF16),
                        pltpu.VMEM((nq, tile, w), BF16),
                        pltpu.VMEM((nq, w + BF16_SUBLANES, tile), BF16),
                        pltpu.VMEM((nq, w, tile), BF16),
                        pltpu.VMEM((nq, w, tile), BF16),
                        pltpu.VMEM((2, tile + 8, tile), F32),
                        pltpu.VMEM((2, tile + 8, tile), F32),
                        pltpu.VMEM((2, tile + 8, tile), F32),
                        pltpu.VMEM((2, w + BF16_SUBLANES, tile), F32),
                        pltpu.VMEM((2, 1, tile), F32)],
        compiler_params=_params("arbitrary", "arbitrary"),
        name="diff_attn",
    )(proj, proj, proj, proj, slopes, qconst_a, qconst_b, tile2(g_q), tile2(g_k), g_sub,
      lq1, lk1, lq2, lk2)


def _out_kernel(x_ref, ret_ref, att_ref, mod_ref, wo_ref, g2_ref, wr_ref, br_ref,
                x1_ref, h2_ref, tw_ref, ld_ref, cnt_ref, *, tm, n_exp):
    m = mod_ref[0]
    merged = (ret_ref[...].astype(F32) + att_ref[...].astype(F32)).astype(BF16)
    x1 = x_ref[...] + m[2:3] * jnp.dot(merged, wo_ref[...], preferred_element_type=F32)
    x1_ref[...] = x1
    h2 = _rms(x1) * g2_ref[...] * (1.0 + m[4:5]) + m[3:4]
    hi = h2.astype(BF16)
    h2_ref[...] = hi

    lo = (h2 - hi.astype(F32)).astype(BF16)
    both = jnp.dot(hi, wr_ref[...], preferred_element_type=F32)
    logits = (both[:, :LANES] + both[:, LANES:]
              + jnp.dot(lo, wr_ref[:, :LANES], preferred_element_type=F32) + br_ref[...])
    lt = logits.T[:n_exp]

    erow = lax.broadcasted_iota(I32, (n_exp, tm), 0)
    vals, sels = [], []
    for k in range(TOP_K):
        mx = jnp.max(lt, axis=0, keepdims=True)
        idx = jnp.min(jnp.where(lt == mx, erow, n_exp), axis=0, keepdims=True)
        sel = erow == idx
        vals.append(mx)
        sels.append(sel)
        lt = jnp.where(sel, MASK_VALUE, lt)
    exps = [jnp.exp(v - vals[0]) for v in vals]
    inv = 1.0 / (exps[0] + exps[1] + exps[2] + exps[3])
    for k in range(TOP_K):
        tw_ref[k:k + 1, :] = exps[k] * inv

    chosen = jnp.where(sels[0] | sels[1] | sels[2] | sels[3], 1.0, 0.0)
    before = (lax.broadcasted_iota(I32, (tm, tm), 0) < lax.broadcasted_iota(I32, (tm, tm), 1))
    prefix = jnp.dot(chosen.astype(BF16), jnp.where(before, 1.0, 0.0).astype(BF16),
                     preferred_element_type=F32)
    count = jnp.sum(chosen, axis=1, keepdims=True)
    cnt_ref[...] = count
    padded = jnp.ceil(count * (1.0 / RUN_ALIGN)) * RUN_ALIGN
    below = (lax.broadcasted_iota(I32, (n_exp, n_exp), 1) < lax.broadcasted_iota(I32, (n_exp, n_exp), 0))
    run_start = jnp.dot(jnp.where(below, 1.0, 0.0).astype(BF16),
                        jnp.broadcast_to(padded, (n_exp, LANES)).astype(BF16),
                        preferred_element_type=F32)[:, :1]
    pos = prefix + run_start
    for k in range(TOP_K):
        ld_ref[k:k + 1, :] = jnp.sum(jnp.where(sels[k], pos, 0.0), axis=0, keepdims=True).astype(I32)


def _out_router(x2, ret, att, mod3, w_out_bf16, g2, wr_split, br_pad, seq, n_exp):
    t, d = x2.shape
    tm = TOK_TM
    row = lambda i: (i, 0)
    const = lambda i: (0, 0)
    kernel = functools.partial(_out_kernel, tm=tm, n_exp=n_exp)
    return pl.pallas_call(
        kernel,
        out_shape=(jax.ShapeDtypeStruct((t, d), F32),
                   jax.ShapeDtypeStruct((t, d), BF16),
                   jax.ShapeDtypeStruct((TOP_K, t), F32),
                   jax.ShapeDtypeStruct((TOP_K, t), I32),
                   jax.ShapeDtypeStruct((t // tm, n_exp, 1), F32)),
        grid=(t // tm,),
        in_specs=[pl.BlockSpec((tm, d), row), pl.BlockSpec((tm, d), row), pl.BlockSpec((tm, d), row),
                  pl.BlockSpec((1, 6, d), lambda i: (i * tm // seq, 0, 0)),
                  pl.BlockSpec((d, d), const), pl.BlockSpec((1, d), const),
                  pl.BlockSpec((d, 2 * LANES), const), pl.BlockSpec((1, LANES), const)],
        out_specs=(pl.BlockSpec((tm, d), row), pl.BlockSpec((tm, d), row),
                   pl.BlockSpec((TOP_K, tm), lambda i: (0, i)),
                   pl.BlockSpec((TOP_K, tm), lambda i: (0, i)),
                   pl.BlockSpec((None, n_exp, 1), lambda i: (i, 0, 0))),
        compiler_params=_params("arbitrary"),
        name="out_router",
    )(x2, ret, att, mod3, w_out_bf16, g2, wr_split, br_pad)


def _meta_stride(n_exp):
    return 2 * len(SLAB_SIZES) * n_exp + len(SLAB_SIZES)


def _run_slabs(meta_ref, tile, n_exp, local_ref, global_ref, sem, *, to_global, wait):
    n_sizes = len(SLAB_SIZES)
    base = tile * _meta_stride(n_exp)
    for c, size in enumerate(SLAB_SIZES):
        def one(p, carry, c=c, size=size):
            if wait:
                local = glob = 0
            else:
                local = pl.multiple_of(meta_ref[base + c * n_exp + p], RUN_ALIGN)
                glob = pl.multiple_of(meta_ref[base + (n_sizes + c) * n_exp + p], RUN_ALIGN)
            loc = local_ref.at[pl.ds(local, size)]
            glo = global_ref.at[pl.ds(glob, size)]
            cp = pltpu.make_async_copy(loc, glo, sem) if to_global else pltpu.make_async_copy(glo, loc, sem)
            if wait:
                cp.wait()
            else:
                cp.start()
            return carry

        lax.fori_loop(0, meta_ref[base + 2 * n_sizes * n_exp + c], one, 0)


def _dispatch_kernel(meta_ref, tail_ref, ld_ref, h_ref, xs_ref, stage_ref, zero_ref, sems, tail_sem,
                     *, tm, n_exp, n_tiles, n_blocks, stage_rows):
    i = pl.program_id(0)
    slot = i & 1
    stage = stage_ref.at[slot]

    def tails(wait):
        def unused_block(b, carry):
            cp = pltpu.make_async_copy(
                zero_ref, xs_ref.at[pl.ds(pl.multiple_of(b * EXP_BLK, EXP_BLK), EXP_BLK)], tail_sem)
            if wait:
                cp.wait()
            else:
                cp.start()
            return carry

        lax.fori_loop(tail_ref[2 * n_exp] // EXP_BLK, n_blocks, unused_block, 0)

        def per_expert(e, carry):
            start, n = tail_ref[e], tail_ref[n_exp + e]
            off = 0
            for size in TAIL_SIZES:
                take = n & size

                @pl.when(take != 0)
                def _(off=off, size=size):
                    cp = pltpu.make_async_copy(
                        zero_ref.at[pl.ds(0, size)],
                        xs_ref.at[pl.ds(pl.multiple_of(start + off, RUN_ALIGN), size)], tail_sem)
                    if wait:
                        cp.wait()
                    else:
                        cp.start()

                off = off + take
            return carry

        lax.fori_loop(0, n_exp, per_expert, 0)

    @pl.when(i == 0)
    def _():
        zero_ref[...] = jnp.zeros_like(zero_ref)
        tails(False)
        tails(True)

    ld = ld_ref[...]
    h = h_ref[...]
    for c in range(stage_rows // GATHER_RC):
        r = lax.broadcasted_iota(I32, (GATHER_RC, tm), 0) + c * GATHER_RC
        hit = jnp.where(r == ld[0:1], 1.0, jnp.where(r == ld[1:2], 1.0,
                        jnp.where(r == ld[2:3], 1.0, jnp.where(r == ld[3:4], 1.0, 0.0))))
        stage[c * GATHER_RC:(c + 1) * GATHER_RC, :] = jnp.dot(
            hit.astype(BF16), h, preferred_element_type=F32).astype(BF16)

    _run_slabs(meta_ref, i, n_exp, stage, xs_ref, sems.at[slot], to_global=True, wait=False)

    @pl.when(i > 0)
    def _():
        _run_slabs(meta_ref, i - 1, n_exp, stage_ref.at[1 - slot], xs_ref, sems.at[1 - slot],
                   to_global=True, wait=True)

    @pl.when(i == n_tiles - 1)
    def _():
        _run_slabs(meta_ref, i, n_exp, stage, xs_ref, sems.at[slot], to_global=True, wait=True)


def _dispatch(meta, tail, ldest, h2, n_rows, n_exp):
    t, d = h2.shape
    tm = TOK_TM
    n_tiles = t // tm
    stage_rows = _stage_rows(n_exp)
    kernel = functools.partial(_dispatch_kernel, tm=tm, n_exp=n_exp, n_tiles=n_tiles,
                               n_blocks=n_rows // EXP_BLK, stage_rows=stage_rows)
    grid_spec = pltpu.PrefetchScalarGridSpec(
        num_scalar_prefetch=2,
        grid=(n_tiles,),
        in_specs=[pl.BlockSpec((TOP_K, tm), lambda i, m, tl: (0, i)),
                  pl.BlockSpec((tm, d), lambda i, m, tl: (i, 0))],
        out_specs=pl.BlockSpec(memory_space=pl.ANY),
        scratch_shapes=[pltpu.VMEM((2, stage_rows, d), BF16),
                        pltpu.VMEM((EXP_BLK, d), BF16),
                        pltpu.SemaphoreType.DMA((2,)),
                        pltpu.SemaphoreType.DMA],
    )
    return pl.pallas_call(
        kernel,
        out_shape=jax.ShapeDtypeStruct((n_rows, d), BF16),
        grid_spec=grid_spec,
        compiler_params=_params("arbitrary"),
        name="dispatch",
    )(meta, tail, ldest, h2)


def _expert_kernel(region_ref, xs_ref, w1_ref, b1_ref, w2_ref, b2_ref, eo_ref,
                   w1f_ref, w2f_ref, w1b_ref, w2b_ref, x_buf, o_buf, w_sems, x_sems, o_sems,
                   *, f, blk, n_exp, n_blocks):
    e = pl.program_id(0)
    first, count = region_ref[e], region_ref[n_exp + e]
    w_slot = e & 1
    d = w1f_ref.shape[1]
    r1, r2 = d // W_CHUNKS, f // W_CHUNKS

    def rows(j):
        return pl.ds(pl.multiple_of((first + j) * blk, blk), blk)

    def x_copy(j, slot):
        return pltpu.make_async_copy(xs_ref.at[rows(j)], x_buf.at[slot], x_sems.at[slot])

    def o_copy(j, slot):
        return pltpu.make_async_copy(o_buf.at[slot], eo_ref.at[rows(j)], o_sems.at[slot])

    def w_chunk(expert, c, slot, wait):
        for src, dst, r in ((w1_ref, w1f_ref, r1), (w2_ref, w2f_ref, r2)):
            piece = pl.ds(c * r if isinstance(c, int) else pl.multiple_of(c * r, r), r)
            cp = pltpu.make_async_copy(src.at[expert, piece], dst.at[slot, piece], w_sems.at[slot])
            if wait:
                cp.wait()
            else:
                cp.start()

    @pl.when(e == 0)
    def _():
        for c in range(W_CHUNKS):
            w_chunk(0, c, 0, False)

    @pl.when(count > 0)
    def _():
        x_copy(0, 0).start()

    for c in range(W_CHUNKS):
        w_chunk(e, c, w_slot, True)

    @pl.when(count > 0)
    def _():
        w1b_ref[...] = w1f_ref[w_slot].astype(BF16)
        w2b_ref[...] = w2f_ref[w_slot].astype(BF16)

    def next_chunk(c):
        @pl.when(e + 1 < n_exp)
        def _():
            w_chunk(e + 1, c, 1 - w_slot, False)

    def block(j, carry):
        slot = j & 1
        x_copy(j, slot).wait()

        @pl.when(j + 1 < count)
        def _():
            x_copy(j + 1, 1 - slot).start()

        @pl.when(j < W_CHUNKS)
        def _():
            next_chunk(j)

        @pl.when(j >= 2)
        def _():
            o_copy(j - 2, slot).wait()

        gu = jnp.dot(x_buf[slot], w1b_ref[...], preferred_element_type=F32) + b1_ref[0]
        g = jnp.minimum(gu[:, :f], SWIGLU_LIMIT)
        u = jnp.clip(gu[:, f:], -SWIGLU_LIMIT, SWIGLU_LIMIT)
        a = g * _sigmoid(SWIGLU_ALPHA * g) * (u + 1.0)
        o_buf[slot] = (jnp.dot(a.astype(BF16), w2b_ref[...], preferred_element_type=F32)
                       + b2_ref[0]).astype(o_buf.dtype)
        o_copy(j, slot).start()
        return carry

    lax.fori_loop(0, count, block, 0)
    lax.fori_loop(jnp.minimum(count, W_CHUNKS), W_CHUNKS, lambda c, carry: (next_chunk(c), carry)[1], 0)

    @pl.when(count >= 2)
    def _():
        o_copy(count - 2, count & 1).wait()

    @pl.when(count >= 1)
    def _():
        o_copy(count - 1, (count - 1) & 1).wait()

    @pl.when(e == n_exp - 1)
    def _():
        o_buf[0] = jnp.zeros(o_buf.shape[1:], o_buf.dtype)
        used = first + count

        def fill(j, wait):
            cp = pltpu.make_async_copy(
                o_buf.at[0], eo_ref.at[pl.ds(pl.multiple_of(j * blk, blk), blk)], o_sems.at[0])
            if wait:
                cp.wait()
            else:
                cp.start()

        lax.fori_loop(used, n_blocks, lambda j, c: (fill(j, False), c)[1], 0)
        lax.fori_loop(used, n_blocks, lambda j, c: (fill(j, True), c)[1], 0)


def _experts(region, xs, w1, b1, w2, b2):
    n_rows, d = xs.shape
    n_exp, _, f2 = w1.shape
    f = f2 // 2
    blk = EXP_BLK
    kernel = functools.partial(_expert_kernel, f=f, blk=blk, n_exp=n_exp, n_blocks=n_rows // blk)
    grid_spec = pltpu.PrefetchScalarGridSpec(
        num_scalar_prefetch=1,
        grid=(n_exp,),
        in_specs=[pl.BlockSpec(memory_space=pl.ANY),
                  pl.BlockSpec(memory_space=pl.ANY),
                  pl.BlockSpec((1, 1, f2), lambda e, r: (e, 0, 0)),
                  pl.BlockSpec(memory_space=pl.ANY),
                  pl.BlockSpec((1, 1, d), lambda e, r: (e, 0, 0))],
        out_specs=pl.BlockSpec(memory_space=pl.ANY),
        scratch_shapes=[pltpu.VMEM((2, d, f2), F32), pltpu.VMEM((2, f, d), F32),
                        pltpu.VMEM((d, f2), BF16), pltpu.VMEM((f, d), BF16),
                        pltpu.VMEM((2, blk, d), BF16), pltpu.VMEM((2, blk, d), BF16),
                        pltpu.SemaphoreType.DMA((2,)), pltpu.SemaphoreType.DMA((2,)),
                        pltpu.SemaphoreType.DMA((2,))],
    )
    return pl.pallas_call(
        kernel,
        out_shape=jax.ShapeDtypeStruct((n_rows, d), BF16),
        grid_spec=grid_spec,
        compiler_params=_params("arbitrary"),
        name="experts",
    )(region, xs, w1, b1, w2, b2)


def _combine_kernel(meta_ref, eo_ref, ld_ref, w_ref, x1_ref, mod_ref, o_ref, stage_ref, sems,
                    *, tm, n_exp, n_tiles, stage_rows):
    i = pl.program_id(0)
    slot = i & 1

    @pl.when(i == 0)
    def _():
        stage_ref[...] = jnp.zeros_like(stage_ref)
        _run_slabs(meta_ref, 0, n_exp, stage_ref.at[0], eo_ref, sems.at[0], to_global=False, wait=False)

    @pl.when(i + 1 < n_tiles)
    def _():
        _run_slabs(meta_ref, i + 1, n_exp, stage_ref.at[1 - slot], eo_ref, sems.at[1 - slot],
                   to_global=False, wait=False)

    ld = ld_ref[...]
    w = w_ref[...]

    _run_slabs(meta_ref, i, n_exp, stage_ref.at[slot], eo_ref, sems.at[slot], to_global=False, wait=True)
    stage = stage_ref.at[slot]
    y = None
    for c in range(stage_rows // COMBINE_KC):
        r0 = c * COMBINE_KC
        r = lax.broadcasted_iota(I32, (COMBINE_KC, tm), 0) + r0
        sel = jnp.where(r == ld[0:1], w[0:1], jnp.where(r == ld[1:2], w[1:2],
                        jnp.where(r == ld[2:3], w[2:3], jnp.where(r == ld[3:4], w[3:4], 0.0))))
        part = lax.dot_general(sel.astype(BF16), stage[r0:r0 + COMBINE_KC, :], (((0,), (0,)), ((), ())),
                               preferred_element_type=F32)
        y = part if y is None else y + part
    o_ref[...] = x1_ref[...] + mod_ref[0][5:6] * y


def _combine(meta, eo, ld_tok, w_tok, x1, mod3, seq, n_exp):
    t, d = x1.shape
    tm = TOK_TM
    n_tiles = t // tm
    stage_rows = _stage_rows(n_exp)
    kernel = functools.partial(_combine_kernel, tm=tm, n_exp=n_exp, n_tiles=n_tiles, stage_rows=stage_rows)
    grid_spec = pltpu.PrefetchScalarGridSpec(
        num_scalar_prefetch=1,
        grid=(n_tiles,),
        in_specs=[pl.BlockSpec(memory_space=pl.ANY),
                  pl.BlockSpec((TOP_K, tm), lambda i, m: (0, i)),
                  pl.BlockSpec((TOP_K, tm), lambda i, m: (0, i)),
                  pl.BlockSpec((tm, d), lambda i, m: (i, 0)),
                  pl.BlockSpec((1, 6, d), lambda i, m: (i * tm // seq, 0, 0))],
        out_specs=pl.BlockSpec((tm, d), lambda i, m: (i, 0)),
        scratch_shapes=[pltpu.VMEM((2, stage_rows, d), BF16),
                        pltpu.SemaphoreType.DMA((2,))],
    )
    return pl.pallas_call(
        kernel,
        out_shape=jax.ShapeDtypeStruct((t, d), F32),
        grid_spec=grid_spec,
        compiler_params=_params("arbitrary"),
        name="combine",
    )(meta, eo, ld_tok, w_tok, x1, mod3)


def _routing_tables(tile_cnt, n_tokens):
    n_tiles, n_exp = tile_cnt.shape
    blk = EXP_BLK
    run = _round_up(tile_cnt, RUN_ALIGN)
    local = jnp.cumsum(run, axis=1) - run
    size = jnp.sum(run, axis=0)
    region = _round_up(size, blk)
    region_end = jnp.cumsum(region)
    region_start = region_end - region
    glob = region_start[None, :] + jnp.cumsum(run, axis=0) - run
    sizes = jnp.asarray(SLAB_SIZES, I32)
    has = (run[:, :, None] & sizes) != 0
    before = run[:, :, None] & ~(2 * sizes - 1)
    place = jnp.cumsum(has, axis=1) - has
    hit = has[..., None] & (place[..., None] == jnp.arange(n_exp, dtype=I32))
    listed = lambda start: jnp.sum(
        jnp.where(hit, (start[:, :, None] + before)[..., None], 0), axis=1).reshape(n_tiles, -1)
    meta = jnp.concatenate([listed(local), listed(glob), jnp.sum(has, axis=1)],
                           axis=1).reshape(-1).astype(I32)
    tail = jnp.concatenate([region_start + size, region - size, region_end[-1:]]).astype(I32)
    n_rows = _round_up(n_tokens * TOP_K + n_tiles * n_exp * RUN_ALIGN + n_exp * blk, blk)
    region_blocks = jnp.concatenate([region_start // blk, region // blk]).astype(I32)
    return meta, tail, region_blocks, n_rows


def kernel(x, c, w_ada, b_ada, g_norm1, w_in, g_ret_gn, b_ret_gn, g_qnorm, g_knorm,
           lambda_q1, lambda_k1, lambda_q2, lambda_k2, g_diff_subln, w_out, g_norm2,
           w_router, b_router, w_expert_in, b_expert_in, w_expert_out, b_expert_out):
    batch, seq, d = x.shape
    depth = w_ada.shape[0]
    assert depth == 1
    t = batch * seq
    n_exp = w_router.shape[-1]
    l = 0

    mod = _adaln_mod(c, w_ada[l], b_ada[l][None, :])
    mod3 = mod.reshape(batch, 6, d)
    x2 = x.reshape(t, d)

    proj = _in_proj(x2, mod3, g_norm1[l][None, :], w_in[l].astype(BF16), seq)
    ret = _retention(proj, g_ret_gn[l][None, :], b_ret_gn[l][None, :], batch, seq, d)
    att = _diff_attention(proj, g_qnorm[l][None, :], g_knorm[l][None, :], g_diff_subln[l][None, :],
                          lambda_q1[l][None, :], lambda_k1[l][None, :],
                          lambda_q2[l][None, :], lambda_k2[l][None, :], batch, seq, d)

    wr = w_router[l]
    wr_hi = wr.astype(BF16)
    wr_lo = (wr - wr_hi.astype(F32)).astype(BF16)
    pad = lambda a: jnp.pad(a, ((0, 0), (0, LANES - n_exp)))
    wr_split = jnp.concatenate([pad(wr_hi), pad(wr_lo)], axis=1)
    br_pad = jnp.pad(b_router[l][None, :], ((0, 0), (0, LANES - n_exp)), constant_values=MASK_VALUE)
    x1, h2, top_w, ldest, tile_cnt = _out_router(
        x2, ret, att, mod3, w_out[l].astype(BF16), g_norm2[l][None, :], wr_split, br_pad, seq, n_exp)

    meta, tail, region_blocks, n_rows = _routing_tables(tile_cnt[:, :, 0].astype(I32), t)
    xs = _dispatch(meta, tail, ldest, h2, n_rows, n_exp)
    eo = _experts(region_blocks, xs, w_expert_in[l], b_expert_in[l][:, None, :],
                  w_expert_out[l], b_expert_out[l][:, None, :])
    out = _combine(meta, eo, ldest, top_w, x1, mod3, seq, n_exp)
    return out.reshape(batch, seq, d)
```
